```python
import jax
import jax.numpy as jnp
from jax import lax
import numpy as np

D_MODEL = 1024
BATCH = 4
SEQ = 4096
DEPTH = 2

N_MIXERS = 4
GROUP_WIDTH = D_MODEL // N_MIXERS
N_HEADS = 4
HEAD_DIM = GROUP_WIDTH // N_HEADS
CHUNK = 64
CONV_WIDTH = 4
GLA_RANK = 16
GLA_TAU = 16.0
ROPE_BASE = 10000.0
N_GROUPS = 4
EXPERTS_PER_GROUP = 4
N_EXPERTS = N_GROUPS * EXPERTS_PER_GROUP
TOP_K = 2
D_EXPERT = D_MODEL // 2
ALPHA = (2 * DEPTH) ** 0.25
BETA = (8 * DEPTH) ** -0.25
LN_EPS = 1e-5
NORM_EPS = 1e-6
W = GROUP_WIDTH
IN_SPLITS = (W, W, W, W, W, W, W, W, N_HEADS, N_HEADS, W, W, W, W, GLA_RANK, W, W, W, W)
IN_DIM = 16 * W + 2 * N_HEADS + GLA_RANK

kernel_name = "hybrid_parallel_mixer_moe_block"


def layer_norm(x, g, b):
    xf = x.astype(jnp.float32)
    mu = xf.mean(-1, keepdims=True)
    var = jnp.mean(jnp.square(xf - mu), axis=-1, keepdims=True)
    return ((xf - mu) * lax.rsqrt(var + LN_EPS)).astype(x.dtype) * g + b


def head_norm(x, gain, center):
    B, S, H, Dh = x.shape
    xf = x.astype(jnp.float32)
    if center:
        xf = xf - xf.mean(-1, keepdims=True)
    y = xf * lax.rsqrt(jnp.mean(xf * xf, axis=-1, keepdims=True) + NORM_EPS)
    return y.reshape(B, S, H * Dh).astype(x.dtype) * gain


def to_heads(t):
    return t.reshape(t.shape[0], t.shape[1], N_HEADS, -1)


def to_chunks(t):
    B, S, H, D = t.shape
    return t.reshape(B, S // CHUNK, CHUNK, H, D).transpose(1, 0, 3, 2, 4)


def from_chunks(t):
    N, B, H, C, D = t.shape
    return t.transpose(1, 0, 3, 2, 4).reshape(B, N * C, H, D)


def gate_chunks(t):
    B, S, H = t.shape
    return t.reshape(B, S // CHUNK, CHUNK, H).transpose(1, 0, 3, 2)


def rotary(x):
    S, Dh = x.shape[1], x.shape[-1]
    inv = ROPE_BASE ** (-jnp.arange(0, Dh, 2, dtype=jnp.float32) / Dh)
    ang = jnp.arange(S, dtype=jnp.float32)[:, None] * inv[None, :]
    cos = jnp.cos(ang)[None, :, None, :].astype(x.dtype)
    sin = jnp.sin(ang)[None, :, None, :].astype(x.dtype)
    x1, x2 = x[..., : Dh // 2], x[..., Dh // 2:]
    return jnp.concatenate([x1 * cos - x2 * sin, x1 * sin + x2 * cos], axis=-1)


def causal_conv(x, w):
    K = w.shape[0]
    return lax.conv_general_dilated(
        x, w[:, None, :].astype(x.dtype), window_strides=(1,), padding=[(K - 1, 0)],
        dimension_numbers=("NWC", "WIO", "NWC"), feature_group_count=x.shape[-1])


def retention_chunkwise(q, k, v):
    dt = v.dtype
    qc, kc, vc = (to_chunks(t.astype(jnp.float32)) for t in (q, k, v))
    _, B, H, C, Dk = qc.shape
    Dv = vc.shape[-1]
    log_gamma = jnp.log1p(-jnp.exp2(-5.0 - jnp.arange(H, dtype=jnp.float32)))
    pos = jnp.arange(1, C + 1, dtype=jnp.float32)
    rel = pos[:, None] - pos[None, :]
    decay_intra = jnp.where(rel >= 0, jnp.exp(log_gamma[:, None, None] * jnp.maximum(rel, 0.0)), 0.0)
    decay_q = jnp.exp(log_gamma[:, None] * pos[None, :])[:, :, None]
    decay_k = jnp.exp(log_gamma[:, None] * (C - pos)[None, :])[:, :, None]
    decay_chunk = jnp.exp(log_gamma * C)[:, None, None]

    def step(state, inp):
        qi, ki, vi = inp
        scores = jnp.einsum("bhid,bhjd->bhij", qi, ki) * decay_intra
        out = jnp.einsum("bhij,bhjv->bhiv", scores, vi) + jnp.einsum("bhid,bhdv->bhiv", qi, state) * decay_q
        state = state * decay_chunk + jnp.einsum("bhjd,bhjv->bhdv", ki * decay_k, vi)
        return state, out

    init = jnp.zeros((B, H, Dk, Dv), jnp.float32)
    _, out = lax.scan(step, init, (qc, kc, vc))
    return from_chunks(out).astype(dt)


def mlstm_chunkwise(q, k, v, i_pre, log_f):
    dt = v.dtype
    qc, kc, vc = (to_chunks(t.astype(jnp.float32)) for t in (q, k, v))
    ic, fc = (gate_chunks(t.astype(jnp.float32)) for t in (i_pre, log_f))
    _, B, H, C, Dk = qc.shape
    Dv = vc.shape[-1]
    causal = jnp.tril(jnp.ones((C, C), dtype=bool))

    def step(carry, inp):
        mem, nvec, m_prev = carry
        qi, ki, vi, ii, fi = inp
        b = jnp.cumsum(fi, axis=-1)
        log_w = jnp.where(causal, b[..., :, None] - b[..., None, :] + ii[..., None, :], -jnp.inf)
        log_inter = b + m_prev[..., None]
        m = jnp.maximum(log_inter, log_w.max(-1))
        s = jnp.einsum("bhid,bhjd->bhij", qi, ki) * jnp.exp(log_w - m[..., None])
        w_inter = jnp.exp(log_inter - m)
        num = jnp.einsum("bhij,bhjv->bhiv", s, vi) + w_inter[..., None] * jnp.einsum("bhid,bhdv->bhiv", qi, mem)
        den = s.sum(-1) + w_inter * jnp.einsum("bhid,bhd->bhi", qi, nvec)
        h = num / jnp.maximum(jnp.abs(den), jnp.exp(-m))[..., None]
        log_end_inter = b[..., -1] + m_prev
        log_end = b[..., -1:] - b + ii
        m_new = jnp.maximum(log_end_inter, log_end.max(-1))
        wk = jnp.exp(log_end - m_new[..., None])
        decay = jnp.exp(log_end_inter - m_new)
        mem = decay[..., None, None] * mem + jnp.einsum("bhjd,bhjv->bhdv", ki * wk[..., None], vi)
        nvec = decay[..., None] * nvec + jnp.einsum("bhj,bhjd->bhd", wk, ki)
        return (mem, nvec, m_new), h

    init = (jnp.zeros((B, H, Dk, Dv), jnp.float32), jnp.zeros((B, H, Dk), jnp.float32),
            jnp.zeros((B, H), jnp.float32))
    _, out = lax.scan(step, init, (qc, kc, vc, ic, fc))
    return from_chunks(out).astype(dt)


def chunked_gated_linear_attention(q, k, v, log_f):
    dt = v.dtype
    qc, kc, vc, gc = (to_chunks(t.astype(jnp.float32)) for t in (q, k, v, log_f))
    _, B, H, C, Dk = qc.shape
    Dv = vc.shape[-1]
    causal = jnp.tril(jnp.ones((C, C), dtype=bool))[:, :, None]

    def step(state, inp):
        qi, ki, vi, gi = inp
        G = jnp.cumsum(gi, axis=2)
        w = jnp.exp(jnp.where(causal, G[:, :, :, None, :] - G[:, :, None, :, :], -jnp.inf))
        scores = jnp.einsum("bhid,bhjd,bhijd->bhij", qi, ki, w)
        out = jnp.einsum("bhij,bhjv->bhiv", scores, vi) + jnp.einsum("bhid,bhdv->bhiv", qi * jnp.exp(G), state)
        G_end = G[:, :, -1:, :]
        state = state * jnp.exp(G_end)[:, :, 0, :, None] + jnp.einsum("bhjd,bhjv->bhdv", ki * jnp.exp(G_end - G), vi)
        return state, out

    init = jnp.zeros((B, H, Dk, Dv), jnp.float32)
    _, out = lax.scan(step, init, (qc, kc, vc, gc))
    return from_chunks(out).astype(dt)


def hybrid_mixer(h, w_in, conv_w, gate_b, gla_w2, gla_b2, lb, ret_g, mlstm_g, gla_g, hgrn_g, w_out):
    proj = h @ w_in
    (r_q, r_k, r_v, r_g,
     m_q, m_k, m_v, m_o, m_i, m_f,
     g_q, g_k, g_v, g_g, g_lr,
     h_q, h_f, h_i, h_g) = jnp.split(proj, np.cumsum(IN_SPLITS)[:-1].tolist(), axis=-1)
    scale = HEAD_DIM ** -0.5

    ro = retention_chunkwise(rotary(to_heads(r_q)) * scale, rotary(to_heads(r_k)), to_heads(r_v))
    y_ret = head_norm(ro, ret_g, True) * jax.nn.silu(r_g)

    qk = jax.nn.silu(causal_conv(jnp.concatenate([m_q, m_k], axis=-1), conv_w))
    mq, mk = jnp.split(qk, 2, axis=-1)
    i_pre = m_i + gate_b[:N_HEADS]
    log_fm = jax.nn.log_sigmoid(m_f + gate_b[N_HEADS:])
    mh = mlstm_chunkwise(to_heads(mq), to_heads(mk) * scale, to_heads(m_v), i_pre, log_fm)
    y_mlstm = head_norm(mh * to_heads(jax.nn.sigmoid(m_o)), mlstm_g, True)

    log_a = jax.nn.log_sigmoid(g_lr @ gla_w2 + gla_b2) / GLA_TAU
    go = chunked_gated_linear_attention(to_heads(g_q), to_heads(g_k) * scale, to_heads(g_v), to_heads(log_a))
    y_gla = head_norm(go, gla_g, False) * jax.nn.silu(g_g)

    z = h_f.astype(jnp.float32)
    log_fh = jnp.logaddexp(jnp.log(lb), jnp.log1p(-lb) + jax.nn.log_sigmoid(z))
    k_h = (1.0 - lb) * jax.nn.sigmoid(-z)
    ho = chunked_gated_linear_attention(to_heads(jax.nn.silu(h_q)), to_heads(k_h), to_heads(h_i), to_heads(log_fh))
    y_hgrn = head_norm(ho * to_heads(jax.nn.sigmoid(h_g)), hgrn_g, False)

    return jnp.concatenate([y_ret, y_mlstm, y_gla, y_hgrn], axis=-1) @ w_out


def group_limited_moe(h, router_w, router_b, w1, w3, w2):
    B, S, D = h.shape
    T = B * S
    ht = h.reshape(T, D)
    probs = jax.nn.softmax((ht @ router_w).astype(jnp.float32), axis=-1)
    sel = (probs + router_b.astype(jnp.float32)).reshape(T, N_GROUPS, EXPERTS_PER_GROUP)
    group_score = lax.top_k(sel, TOP_K)[0].sum(-1)
    in_group = jax.nn.one_hot(jnp.argmax(group_score, axis=-1), N_GROUPS, dtype=bool)
    masked = jnp.where(in_group[:, :, None], sel, -jnp.inf).reshape(T, N_EXPERTS)
    _, idx = lax.top_k(masked, TOP_K)
    wts = jnp.take_along_axis(probs, idx, axis=-1)
    wts = wts / wts.sum(-1, keepdims=True)
    gates = jnp.einsum("tk,tke->te", wts, jax.nn.one_hot(idx, N_EXPERTS, dtype=jnp.float32)).astype(h.dtype)
    y = jnp.zeros_like(ht)
    for e in range(N_EXPERTS):
        a = jax.nn.silu(ht @ w1[e]) * (ht @ w3[e])
        y = y + gates[:, e:e + 1] * (a @ w2[e])
    return y.reshape(B, S, D)


def setup_inputs(seed: int = 0) -> dict:
    key = jax.random.key(seed)
    ks = jax.random.split(key, 26)
    D = D_MODEL

    def nrm(k, shape, s):
        return jax.random.normal(k, shape, jnp.float32) * s

    gate_b = jnp.concatenate(
        [nrm(ks[6], (DEPTH, N_HEADS), 0.1),
         jnp.linspace(3.0, 6.0, N_HEADS, dtype=jnp.float32)[None, :] + nrm(ks[7], (DEPTH, N_HEADS), 0.1)], axis=-1)
    return {
        "x": nrm(ks[0], (BATCH, SEQ, D), 1.0),
        "c": nrm(ks[1], (BATCH, D), 1.0),
        "ada_w": nrm(ks[2], (DEPTH, D, 6 * D), 0.5 * D ** -0.5),
        "ada_b": nrm(ks[3], (DEPTH, 6 * D), 0.02),
        "w_in": nrm(ks[4], (DEPTH, D, IN_DIM), D ** -0.5),
        "mlstm_conv": nrm(ks[5], (DEPTH, CONV_WIDTH, 2 * W), CONV_WIDTH ** -0.5),
        "mlstm_gate_b": gate_b,
        "gla_w2": nrm(ks[8], (DEPTH, GLA_RANK, W), GLA_RANK ** -0.5),
        "gla_b2": nrm(ks[9], (DEPTH, W), 0.02),
        "hgrn_lb": nrm(ks[10], (DEPTH, W), 0.1),
        "ret_norm": 1.0 + nrm(ks[11], (DEPTH, W), 0.02),
        "mlstm_norm": 1.0 + nrm(ks[12], (DEPTH, W), 0.02),
        "gla_norm": 1.0 + nrm(ks[13], (DEPTH, W), 0.02),
        "hgrn_norm": 1.0 + nrm(ks[14], (DEPTH, W), 0.02),
        "w_out": nrm(ks[15], (DEPTH, D, D), BETA * D ** -0.5),
        "ln1_g": 1.0 + nrm(ks[16], (DEPTH, D), 0.02),
        "ln1_b": nrm(ks[17], (DEPTH, D), 0.02),
        "router_w": nrm(ks[18], (D, N_EXPERTS), D ** -0.5),
        "router_b": nrm(ks[19], (N_EXPERTS,), 0.01),
        "exp_w1": nrm(ks[20], (DEPTH, N_EXPERTS, D, D_EXPERT), D ** -0.5),
        "exp_w3": nrm(ks[21], (DEPTH, N_EXPERTS, D, D_EXPERT), D ** -0.5),
        "exp_w2": nrm(ks[22], (DEPTH, N_EXPERTS, D_EXPERT, D), BETA * D_EXPERT ** -0.5),
        "ln2_g": 1.0 + nrm(ks[23], (DEPTH, D), 0.02),
        "ln2_b": nrm(ks[24], (DEPTH, D), 0.02),
    }


def reference(x, c, ada_w, ada_b, w_in, mlstm_conv, mlstm_gate_b, gla_w2, gla_b2, hgrn_lb,
              ret_norm, mlstm_norm, gla_norm, hgrn_norm, w_out, ln1_g, ln1_b,
              router_w, router_b, exp_w1, exp_w3, exp_w2, ln2_g, ln2_b):
    lb_all = jnp.cumsum(jax.nn.softmax(hgrn_lb.astype(jnp.float32), axis=0), axis=0)
    lb_all = lb_all - lb_all[:1]
    cond = jax.nn.silu(c)
    for l in range(DEPTH):
        mod = (cond @ ada_w[l] + ada_b[l])[:, None, :]
        sh1, sc1, g1, sh2, sc2, g2 = jnp.split(mod, 6, axis=-1)
        h = x * (1 + sc1) + sh1
        mix = hybrid_mixer(h, w_in[l], mlstm_conv[l], mlstm_gate_b[l], gla_w2[l], gla_b2[l], lb_all[l],
                           ret_norm[l], mlstm_norm[l], gla_norm[l], hgrn_norm[l], w_out[l])
        x = layer_norm(ALPHA * x + g1 * mix, ln1_g[l], ln1_b[l])
        h = x * (1 + sc2) + sh2
        ffn = group_limited_moe(h, router_w, router_b, exp_w1[l], exp_w3[l], exp_w2[l])
        x = layer_norm(ALPHA * x + g2 * ffn, ln2_g[l], ln2_b[l])
    return x
```

```python
import functools

import numpy as np
import jax
import jax.numpy as jnp
from jax import lax
from jax.experimental import pallas as pl
from jax.experimental.pallas import tpu as pltpu

F32 = jnp.float32
BF16 = jnp.bfloat16

D_MODEL = 1024
N_MIXERS = 4
GW = D_MODEL // N_MIXERS
N_HEADS = 4
HEAD_DIM = GW // N_HEADS
CONV_WIDTH = 4
GLA_RANK = 16
GLA_TAU = 16.0
ROPE_BASE = 10000.0
N_GROUPS = 4
EXPERTS_PER_GROUP = 4
N_EXPERTS = 16
D_EXPERT = D_MODEL // 2
LN_EPS = 1e-5
NORM_EPS = 1e-6
QK_SCALE = HEAD_DIM ** -0.5

LANES = 128
CHUNK = 128
SUB = 16
CONV_PAD = 8
NPROJ = 16 * GW + LANES
EXTRA = 16 * GW
VMEM_LIMIT = 56 * 1024 * 1024

(R_Q, R_K, R_V, R_G, M_Q, M_K, M_V, M_O, G_Q, G_K, G_V, G_G, H_Q, H_F, H_I, H_G) = (
    GW * i for i in range(16))


def _bf(x):
    return x.astype(BF16)


def _dot(a, b):
    return jnp.dot(a, b, preferred_element_type=F32)


def _dot_nt(a, b):
    return lax.dot_general(a, b, (((1,), (1,)), ((), ())), preferred_element_type=F32)


def _split2(x):
    hi = _bf(x)
    lo = _bf(x - hi.astype(F32))
    return hi, lo


def _split3(x):
    hi = _bf(x)
    r = x - hi.astype(F32)
    mid = _bf(r)
    lo = _bf(r - mid.astype(F32))
    return hi, mid, lo


def _dot_f32(a, b):
    ah, al = _split2(a)
    bh, bl = _split2(b)
    return _dot(ah, bh) + _dot(ah, bl) + _dot(al, bh)


def _sigmoid(x):
    return 1.0 / (1.0 + jnp.exp(-x))


def _silu(x):
    return x * _sigmoid(x)


def _log_sigmoid(x):
    return jnp.minimum(x, 0.0) - jnp.log1p(jnp.exp(-jnp.abs(x)))


def _layer_norm(v, g, b):
    mu = jnp.mean(v, axis=-1, keepdims=True)
    d = v - mu
    var = jnp.mean(d * d, axis=-1, keepdims=True)
    return d * lax.rsqrt(var + LN_EPS) * g + b


def _adaln_kernel(c_ref, w_ref, b_ref, o_ref):
    cond = _silu(c_ref[...])
    o_ref[0] = _dot_f32(cond, w_ref[0]) + b_ref[0]


def _adaln(c, ada_w, ada_b):
    depth, d, n = ada_w.shape
    bsz = c.shape[0]
    tn = 1536
    return pl.pallas_call(
        _adaln_kernel,
        grid=(depth, n // tn),
        in_specs=[
            pl.BlockSpec((bsz, d), lambda l, j: (0, 0)),
            pl.BlockSpec((1, d, tn), lambda l, j: (l, 0, j)),
            pl.BlockSpec((1, 1, tn), lambda l, j: (l, 0, j)),
        ],
        out_specs=pl.BlockSpec((1, bsz, tn), lambda l, j: (l, 0, j)),
        out_shape=jax.ShapeDtypeStruct((depth, bsz, n), F32),
        compiler_params=pltpu.CompilerParams(
            dimension_semantics=("arbitrary", "arbitrary"), vmem_limit_bytes=VMEM_LIMIT),
        name="adaln",
    )(c, ada_w, ada_b.reshape(depth, 1, n))


def _proj_kernel(x_ref, mod_ref, w_ref, o_ref):
    sh = mod_ref[0, 0:1, :]
    sc = mod_ref[0, 1:2, :]
    h = x_ref[...] * (1.0 + sc) + sh
    o_ref[...] = _dot(_bf(h), w_ref[...])


def _proj(xt, modl, w_in_p, seq):
    t, d = xt.shape
    tm = 256
    per_batch = seq // tm
    return pl.pallas_call(
        _proj_kernel,
        grid=(t // tm,),
        in_specs=[
            pl.BlockSpec((tm, d), lambda i: (i, 0)),
            pl.BlockSpec((1, 6, d), lambda i: (i // per_batch, 0, 0)),
            pl.BlockSpec((d, NPROJ), lambda i: (0, 0)),
        ],
        out_specs=pl.BlockSpec((tm, NPROJ), lambda i: (i, 0)),
        out_shape=jax.ShapeDtypeStruct((t, NPROJ), F32),
        compiler_params=pltpu.CompilerParams(
            dimension_semantics=("arbitrary",), vmem_limit_bytes=VMEM_LIMIT),
        name="proj",
    )(xt, modl, w_in_p)


def _mixer_consts():
    c = CHUNK
    i = np.arange(c)[:, None]
    j = np.arange(c)[None, :]
    rel = i - j
    log_gamma = np.log1p(-np.exp2(-5.0 - np.arange(N_HEADS, dtype=np.float64)))
    dret = np.concatenate(
        [np.where(rel >= 0, np.exp(log_gamma[h] * np.maximum(rel, 0)), 0.0) for h in range(N_HEADS)], axis=1)
    lane_head = np.arange(GW) // HEAD_DIM
    dq = np.exp(log_gamma[lane_head][None, :] * (np.arange(c)[:, None] + 1.0))
    dk = np.exp(log_gamma[lane_head][None, :] * (c - 1.0 - np.arange(c)[:, None]))
    gc = np.exp(log_gamma[lane_head] * c)[None, :]
    lv = np.zeros((c, c), np.int32)
    lv[(rel >= 0) & (i // SUB == j // SUB)] = 1
    code, b = 2, SUB
    while b < c:
        lv[(i // (2 * b) == j // (2 * b)) & ((i // b) % 2 == 1) & ((j // b) % 2 == 0)] = code
        code, b = code + 1, b * 2
    lv = np.tile(lv, (1, N_HEADS))
    hh = (lane_head[:, None] == lane_head[None, :]).astype(np.float32)
    ones_bd = np.repeat(np.eye(N_HEADS, dtype=np.float32), c, axis=0)
    ones_bd = np.repeat(ones_bd, HEAD_DIM, axis=1)
    tri = (rel >= 0).astype(np.float32)
    return dict(
        dret=jnp.asarray(dret, F32), dq=jnp.asarray(dq, F32), dk=jnp.asarray(dk, F32),
        gc=jnp.asarray(gc, F32), lv=jnp.asarray(lv), hh=jnp.asarray(hh, BF16),
        bdf=jnp.asarray(hh, F32), ones_bd=jnp.asarray(ones_bd, BF16), tri=jnp.asarray(tri, BF16))


def _lane_cumsum(x):
    lane = lax.broadcasted_iota(jnp.int32, x.shape, 1)
    s = 1
    while s < x.shape[1]:
        x = x + jnp.where(lane >= s, pltpu.roll(x, s, 1), 0.0)
        s *= 2
    return x


def _lane_cummax(x):
    lane = lax.broadcasted_iota(jnp.int32, x.shape, 1)
    s = 1
    while s < x.shape[1]:
        x = jnp.maximum(x, jnp.where(lane >= s, pltpu.roll(x, s, 1), -jnp.inf))
        s *= 2
    return x


def _mixer_kernel(layer,
                  proj_ref, cos_ref, sin_ref, conv_ref, gb_ref, w2_ref, b2_ref, lb_ref,
                  retg_ref, mlg_ref, glag_ref, hgg_ref,
                  dret_ref, dq_ref, dk_ref, gc_ref, lv_ref, hh_ref, bdf_ref, onesbd_ref, tri_ref,
                  y_ref,
                  s_ret, s_cn, s_gla, s_hg, m_ml, conv_buf, g_s, q_s, k_s, v_s, d_s):
    c = CHUNK
    n = pl.program_id(1)

    @pl.when(n == 0)
    def _():
        s_ret[...] = jnp.zeros_like(s_ret)
        s_cn[...] = jnp.zeros_like(s_cn)
        s_gla[...] = jnp.zeros_like(s_gla)
        s_hg[...] = jnp.zeros_like(s_hg)
        m_ml[...] = jnp.zeros_like(m_ml)
        conv_buf[0:CONV_PAD, :] = jnp.zeros((CONV_PAD, 2 * GW), F32)

    lane_head = lax.broadcasted_iota(jnp.int32, (1, GW), 1) // HEAD_DIM
    head_rows = [(lane_head == h).astype(BF16) for h in range(N_HEADS)]
    hh = hh_ref[...]
    bdf = bdf_ref[...]
    lv = lv_ref[...]
    causal = lv >= 1

    def col(off, width=GW):
        return proj_ref[:, off:off + width]

    def stackmask(xb):
        return jnp.concatenate([xb * head_rows[h] for h in range(N_HEADS)], axis=0)

    def expand_heads(cols, first):
        out = jnp.zeros((cols.shape[0], GW), F32)
        for h in range(N_HEADS):
            out = jnp.where(lane_head == h, cols[:, first + h:first + h + 1], out)
        return out

    def head_mean(v):
        hi, lo = _split2(v)
        return (_dot(hi, hh) + _dot(lo, hh)) * (1.0 / HEAD_DIM)

    def head_norm(v, gain, center):
        if center:
            v = v - head_mean(v)
        return v * lax.rsqrt(head_mean(v * v) + NORM_EPS) * gain

    cosf = cos_ref[...]
    sins = sin_ref[...]
    half = lax.broadcasted_iota(jnp.int32, (1, GW), 1) % HEAD_DIM < HEAD_DIM // 2

    def rotary(v):
        swapped = jnp.where(half, pltpu.roll(v, GW - HEAD_DIM // 2, 1), pltpu.roll(v, HEAD_DIM // 2, 1))
        return v * cosf + swapped * sins

    rq = _bf(rotary(col(R_Q)) * QK_SCALE)
    rk = rotary(col(R_K))
    rv = col(R_V)
    sc = _dot_nt(rq, stackmask(_bf(rk))) * dret_ref[...]
    ro = _dot(_bf(sc), stackmask(_bf(rv))) + _dot(rq, _bf(s_ret[...])) * dq_ref[...]
    s_ret[...] = s_ret[...] * gc_ref[...] + _dot(_bf((rk * dk_ref[...]).T), _bf(rv)) * bdf
    y_ref[:, 0:GW] = head_norm(ro, retg_ref[...], True) * _silu(col(R_G))

    conv_buf[CONV_PAD:CONV_PAD + c, :] = col(M_Q, 2 * GW)
    qk = jnp.zeros((c, 2 * GW), F32)
    for t in range(CONV_WIDTH):
        s = CONV_WIDTH - 1 - t
        qk = qk + conv_buf[CONV_PAD - s:CONV_PAD - s + c, :] * conv_ref[t:t + 1, :]
    conv_buf[0:CONV_PAD, :] = conv_buf[c:c + CONV_PAD, :]
    qk = _silu(qk)
    mq = _bf(qk[:, 0:GW])
    mk = qk[:, GW:2 * GW] * QK_SCALE
    mv = col(M_V)

    ext = col(EXTRA, LANES)
    pre = ext.T[0:8, :] + gb_ref[...]
    row8 = lax.broadcasted_iota(jnp.int32, (8, c), 0)
    valid = row8 >= N_HEADS
    lf = jnp.where(valid, _log_sigmoid(pre), 0.0)
    gi = jnp.where(valid, pltpu.roll(pre, N_HEADS, 0), 0.0)
    bcum = _lane_cumsum(lf)
    a = gi - bcum
    m_prev = m_ml[...]
    mrow = jnp.maximum(m_prev, _lane_cummax(a))
    m_last = jnp.broadcast_to(mrow[:, c - 1:c], (8, c))
    b_last = jnp.broadcast_to(bcum[:, c - 1:c], (8, c))
    wi = jnp.exp(m_prev - mrow)
    em = jnp.exp(-(bcum + mrow))
    wk = jnp.exp(a - m_last)
    dec = jnp.exp(m_prev - m_last)
    m_ml[...] = jnp.where(valid, b_last + m_last, 0.0)
    cols = jnp.concatenate([mrow, wi, em, wk, jnp.zeros((c - 32, c), F32)], axis=0).T
    mexp = jnp.concatenate(
        [jnp.broadcast_to(cols[:, 4 + h:5 + h], (c, c)) for h in range(N_HEADS)], axis=1)
    a_row = jnp.concatenate([a[4 + h:5 + h, :] for h in range(N_HEADS)], axis=1)
    wmat = jnp.exp(jnp.where(causal, a_row - mexp, -jnp.inf))
    sm = _bf(_dot_nt(mq, stackmask(_bf(mk))) * wmat)
    v_ext = jnp.concatenate([stackmask(_bf(mv)), onesbd_ref[...]], axis=1)
    wi_l = expand_heads(cols, 12)
    nd = _dot(sm, v_ext) + jnp.concatenate([wi_l, wi_l], axis=1) * _dot(mq, _bf(s_cn[...]))
    mh = nd[:, 0:GW] / jnp.maximum(jnp.abs(nd[:, GW:2 * GW]), expand_heads(cols, 20))
    kt = mk * expand_heads(cols, 28)
    dec2 = jnp.concatenate([dec, dec], axis=1)
    dec_l = jnp.zeros((1, GW), F32)
    for h in range(N_HEADS):
        dec_l = jnp.where(lane_head == h, dec2[4 + h:5 + h, :], dec_l)
    upd = _dot(_bf(kt.T), jnp.concatenate([_bf(mv), jnp.ones((c, GW), BF16)], axis=1))
    s_cn[...] = (s_cn[...] * jnp.concatenate([dec_l, dec_l], axis=1)
                 + upd * jnp.concatenate([bdf, bdf], axis=1))
    y_ref[:, GW:2 * GW] = head_norm(mh * _sigmoid(col(M_O)), mlg_ref[...], True)

    tri = tri_ref[...]
    row_c = lax.broadcasted_iota(jnp.int32, (c, 1), 0)
    row_s = lax.broadcasted_iota(jnp.int32, (SUB, GW), 0)

    def decay_attention(q, k, v, g, st_ref):
        g3 = _split3(g)
        gcum = _dot(tri, g3[0]) + _dot(tri, g3[1]) + _dot(tri, g3[2])
        g_s[...] = gcum
        q_s[...] = q
        k_s[...] = k
        v_s[...] = v
        out = _dot_nt(_bf(q * jnp.exp(gcum)), _bf(st_ref[...]))
        scores = jnp.zeros((c, N_HEADS * c), F32)
        code, b = 2, SUB
        while b < c:
            ref = jnp.concatenate(
                [jnp.broadcast_to(g_s[blk * 2 * b + b - 1:blk * 2 * b + b, :], (2 * b, GW))
                 for blk in range(c // (2 * b))], axis=0)
            right = (row_c // b) % 2 == 1
            e = jnp.exp(jnp.where(right, gcum - ref, ref - gcum))
            qb = jnp.where(right, q * e, 0.0)
            kb = jnp.where(right, 0.0, k * e)
            scores = jnp.where(lv == code, _dot_nt(_bf(qb), stackmask(_bf(kb))), scores)
            code, b = code + 1, b * 2
        out = out + _dot(_bf(scores), stackmask(_bf(v)))

        def diag_block(blk, carry):
            r0 = pl.multiple_of(blk * SUB, SUB)
            gb = g_s[pl.ds(r0, SUB), :]
            qb = q_s[pl.ds(r0, SUB), :]
            ps = []
            for j in range(SUB):
                gj = g_s[pl.ds(r0 + j, 1), :]
                kj = k_s[pl.ds(r0 + j, 1), :]
                e = jnp.exp(jnp.where(row_s >= j, gb - gj, -jnp.inf))
                ps.append(_bf(qb * kj * e))
            r = _dot(jnp.concatenate(ps, axis=0), hh)
            o = jnp.zeros((SUB, GW), F32)
            for j in range(SUB):
                o = o + r[j * SUB:(j + 1) * SUB, :] * v_s[pl.ds(r0 + j, 1), :]
            d_s[pl.ds(r0, SUB), :] = o
            return carry

        lax.fori_loop(0, c // SUB, diag_block, 0)
        out = out + d_s[...]
        g_end = g_s[c - 1:c, :]
        st_ref[...] = (st_ref[...] * jnp.exp(g_end)
                       + _dot(_bf(v.T), _bf(k * jnp.exp(g_end - gcum))) * bdf)
        return out

    x_lr = _dot_f32(ext, w2_ref[...]) + b2_ref[...]
    log_a = _log_sigmoid(x_lr) * (1.0 / GLA_TAU)
    go = decay_attention(col(G_Q), col(G_K) * QK_SCALE, col(G_V), log_a, s_gla)
    y_ref[:, 2 * GW:3 * GW] = head_norm(go, glag_ref[...], False) * _silu(col(G_G))

    lb_all = lb_ref[...]
    lb_e = jnp.exp(lb_all - jnp.max(lb_all, axis=0, keepdims=True))
    lb_p = lb_e / jnp.sum(lb_e, axis=0, keepdims=True)
    lb = jnp.zeros((1, GW), F32)
    for l in range(1, layer + 1):
        lb = lb + lb_p[l:l + 1, :]
    z = col(H_F)
    x1 = jnp.log(lb)
    x2 = jnp.log1p(-lb) + _log_sigmoid(z)
    log_f = jnp.maximum(x1, x2) + jnp.log1p(jnp.exp(-jnp.abs(x1 - x2)))
    k_h = (1.0 - lb) * _sigmoid(-z)
    ho = decay_attention(_silu(col(H_Q)), k_h, col(H_I), log_f, s_hg)
    y_ref[:, 3 * GW:4 * GW] = head_norm(ho * _sigmoid(col(H_G)), hgg_ref[...], False)


def _mixer(proj, cosf, sins, conv_w, gate_rows, w2_pad, b2, lb, ret_g, ml_g, gla_g, hg_g, consts,
           layer, bsz, seq):
    c = CHUNK
    nc = seq // c
    t = bsz * seq
    row = lambda a: a.reshape(1, -1)
    full = lambda a: pl.BlockSpec(a.shape, lambda b, n: (0,) * a.ndim)
    cs = consts
    args = [proj, cosf, sins, conv_w, gate_rows, w2_pad, row(b2), lb,
            row(ret_g), row(ml_g), row(gla_g), row(hg_g),
            cs["dret"], cs["dq"], cs["dk"], cs["gc"], cs["lv"], cs["hh"], cs["bdf"], cs["ones_bd"], cs["tri"]]
    in_specs = [
        pl.BlockSpec((c, NPROJ), lambda b, n: (b * nc + n, 0)),
        pl.BlockSpec((c, GW), lambda b, n: (n, 0)),
        pl.BlockSpec((c, GW), lambda b, n: (n, 0)),
    ] + [full(a) for a in args[3:]]
    return pl.pallas_call(
        functools.partial(_mixer_kernel, layer),
        grid=(bsz, nc),
        in_specs=in_specs,
        out_specs=pl.BlockSpec((c, D_MODEL), lambda b, n: (b * nc + n, 0)),
        out_shape=jax.ShapeDtypeStruct((t, D_MODEL), F32),
        scratch_shapes=[
            pltpu.VMEM((GW, GW), F32),
            pltpu.VMEM((GW, 2 * GW), F32),
            pltpu.VMEM((GW, GW), F32),
            pltpu.VMEM((GW, GW), F32),
            pltpu.VMEM((8, c), F32),
            pltpu.VMEM((c + CONV_PAD, 2 * GW), F32),
            pltpu.VMEM((c, GW), F32),
            pltpu.VMEM((c, GW), F32),
            pltpu.VMEM((c, GW), F32),
            pltpu.VMEM((c, GW), F32),
            pltpu.VMEM((c, GW), F32),
        ],
        compiler_params=pltpu.CompilerParams(
            dimension_semantics=("arbitrary", "arbitrary"), vmem_limit_bytes=VMEM_LIMIT),
        name="mixer",
    )(*args)


def _group_shift(v, pos, k):
    return jnp.where(pos < EXPERTS_PER_GROUP - k,
                     pltpu.roll(v, LANES - k, 1), pltpu.roll(v, EXPERTS_PER_GROUP - k, 1))


def _post_kernel(alpha, y_ref, x_ref, mod_ref, wout_ref, lng_ref, lnb_ref, rw_ref, rb_ref,
                 x1_ref, h2_ref, gates_ref):
    g1 = mod_ref[0, 2:3, :]
    sh2 = mod_ref[0, 3:4, :]
    sc2 = mod_ref[0, 4:5, :]
    mix = _dot(_bf(y_ref[...]), wout_ref[...])
    x1 = _layer_norm(alpha * x_ref[...] + g1 * mix, lng_ref[...], lnb_ref[...])
    x1_ref[...] = x1
    h = x1 * (1.0 + sc2) + sh2
    h2_ref[...] = _bf(h)

    lane = lax.broadcasted_iota(jnp.int32, (1, LANES), 1)
    real = lane < N_EXPERTS
    logits = jnp.where(real, _dot_f32(h, rw_ref[...]), -jnp.inf)
    ex = jnp.exp(logits - jnp.max(logits, axis=-1, keepdims=True))
    probs = ex / jnp.sum(ex, axis=-1, keepdims=True)
    sel = jnp.where(real, probs + rb_ref[...], -jnp.inf)
    pos = lane % EXPERTS_PER_GROUP
    gid = lane // EXPERTS_PER_GROUP
    r1 = _group_shift(sel, pos, 1)
    r2 = _group_shift(sel, pos, 2)
    r3 = _group_shift(sel, pos, 3)
    pair = jnp.maximum(jnp.maximum(jnp.maximum(sel + r1, sel + r2), jnp.maximum(sel + r3, r1 + r2)),
                       jnp.maximum(r1 + r3, r2 + r3))
    pair = jnp.where(real, pair, -jnp.inf)
    best = jnp.max(pair, axis=-1, keepdims=True)
    first = jnp.min(jnp.where(pair == best, gid, N_GROUPS), axis=-1, keepdims=True)
    in_group = gid == first
    rank = jnp.zeros(sel.shape, jnp.int32)
    for k, r in ((1, r1), (2, r2), (3, r3)):
        beats = (r > sel) | ((r == sel) & (pos + k >= EXPERTS_PER_GROUP))
        rank = rank + beats.astype(jnp.int32)
    chosen = in_group & (rank < 2)
    w = jnp.where(chosen, probs, 0.0)
    gates_ref[...] = w / jnp.sum(w, axis=-1, keepdims=True)


def _post(y, xt, modl, w_out, ln_g, ln_b, rw_pad, rb_pad, seq, alpha):
    t, d = xt.shape
    tm = 256
    per_batch = seq // tm
    row = lambda a: a.reshape(1, -1)
    tile = pl.BlockSpec((tm, d), lambda i: (i, 0))
    full = lambda a: pl.BlockSpec(a.shape, lambda i: (0,) * a.ndim)
    args = [y, xt, modl, w_out, row(ln_g), row(ln_b), rw_pad, rb_pad]
    return pl.pallas_call(
        functools.partial(_post_kernel, alpha),
        grid=(t // tm,),
        in_specs=[tile, tile, pl.BlockSpec((1, 6, d), lambda i: (i // per_batch, 0, 0))]
        + [full(a) for a in args[3:]],
        out_specs=[tile, tile, pl.BlockSpec((tm, LANES), lambda i: (i, 0))],
        out_shape=[jax.ShapeDtypeStruct((t, d), F32), jax.ShapeDtypeStruct((t, d), BF16),
                   jax.ShapeDtypeStruct((t, LANES), F32)],
        compiler_params=pltpu.CompilerParams(
            dimension_semantics=("arbitrary",), vmem_limit_bytes=VMEM_LIMIT),
        name="post",
    )(*args)


def _moe_kernel(alpha, h_ref, gates_ref, x1_ref, mod_ref, w1_ref, w3_ref, w2_ref, lng_ref, lnb_ref,
                o_ref, acc_ref):
    e =pl.program_id(1)

    @pl.when(e == 0)
    def _():
        acc_ref[...] = jnp.zeros_like(acc_ref)

    h = h_ref[...]
    lane = lax.broadcasted_iota(jnp.int32, (1, LANES), 1)
    ge = jnp.sum(jnp.where(lane == e, gates_ref[...], 0.0), axis=-1, keepdims=True)
    a = _silu(_dot(h, w1_ref[0])) * _dot(h, w3_ref[0])
    acc_ref[...] += _dot(_bf(a * ge), w2_ref[0])

    @pl.when(e == N_EXPERTS - 1)
    def _():
        g2 = mod_ref[0, 5:6, :]
        o_ref[...] = _layer_norm(alpha * x1_ref[...] + g2 * acc_ref[...], lng_ref[...], lnb_ref[...])


def _moe(h2, gates, x1, modl, w1, w3, w2, ln_g, ln_b, seq, alpha):
    t, d = x1.shape
    tm = 512
    per_batch = seq // tm
    row = lambda a: a.reshape(1, -1)
    tile = pl.BlockSpec((tm, d), lambda i, e: (i, 0))
    return pl.pallas_call(
        functools.partial(_moe_kernel, alpha),
        grid=(t // tm, N_EXPERTS),
        in_specs=[
            tile,
            pl.BlockSpec((tm, LANES), lambda i, e: (i, 0)),
            tile,
            pl.BlockSpec((1, 6, d), lambda i, e: (i // per_batch, 0, 0)),
            pl.BlockSpec((1, d, D_EXPERT), lambda i, e: (e, 0, 0)),
            pl.BlockSpec((1, d, D_EXPERT), lambda i, e: (e, 0, 0)),
            pl.BlockSpec((1, D_EXPERT, d), lambda i, e: (e, 0, 0)),
            pl.BlockSpec((1, d), lambda i, e: (0, 0)),
            pl.BlockSpec((1, d), lambda i, e: (0, 0)),
        ],
        out_specs=tile,
        out_shape=jax.ShapeDtypeStruct((t, d), F32),
        scratch_shapes=[pltpu.VMEM((tm, d), F32)],
        compiler_params=pltpu.CompilerParams(
            dimension_semantics=("arbitrary", "arbitrary"), vmem_limit_bytes=VMEM_LIMIT),
        name="moe",
    )(h2, gates, x1, modl, w1, w3, w2, row(ln_g), row(ln_b))


def _regroup_w_in(w):
    a = 8 * GW
    b = a + 2 * N_HEADS
    cc = b + 4 * GW
    dd = cc + GLA_RANK
    pad = jnp.zeros((w.shape[0], LANES - 2 * N_HEADS - GLA_RANK), w.dtype)
    return jnp.concatenate([w[:, :a], w[:, b:cc], w[:, dd:], w[:, a:b], w[:, cc:dd], pad], axis=1)


def _rotary_tables(seq):
    inv = ROPE_BASE ** (-jnp.arange(0, HEAD_DIM, 2, dtype=F32) / HEAD_DIM)
    ang = jnp.arange(seq, dtype=F32)[:, None] * inv[None, :]
    cos = jnp.cos(ang)
    sin = jnp.sin(ang)
    cosf = jnp.tile(jnp.concatenate([cos, cos], axis=1), (1, N_HEADS))
    sins = jnp.tile(jnp.concatenate([-sin, sin], axis=1), (1, N_HEADS))
    return cosf, sins


def kernel(x, c, ada_w, ada_b, w_in, mlstm_conv, mlstm_gate_b, gla_w2, gla_b2, hgrn_lb, ret_norm, mlstm_norm, gla_norm, hgrn_norm, w_out, ln1_g, ln1_b, router_w, router_b, exp_w1, exp_w3, exp_w2, ln2_g, ln2_b):
    bsz, seq, d = x.shape
    depth = ada_w.shape[0]
    assert d == D_MODEL and seq % 512 == 0 and bsz <= 16
    t = bsz * seq
    alpha = (2.0 * depth) ** 0.25
    consts = _mixer_consts()
    cosf, sins = _rotary_tables(seq)
    c_rows = jnp.pad(c, ((0, 16 - bsz), (0, 0)))
    mod = _adaln(c_rows, ada_w, ada_b)[:, :bsz]
    rw_pad = jnp.pad(router_w, ((0, 0), (0, LANES - N_EXPERTS)))
    rb_pad = jnp.pad(router_b, (0, LANES - N_EXPERTS)).reshape(1, LANES)
    xt = x.reshape(t, d)
    for l in range(depth):
        modl = mod[l].reshape(bsz, 6, d)
        proj = _proj(xt, modl, _bf(_regroup_w_in(w_in[l])), seq)
        gate_rows = jnp.broadcast_to(mlstm_gate_b[l][:, None], (2 * N_HEADS, CHUNK))
        w2_pad = jnp.zeros((LANES, GW), F32).at[2 * N_HEADS:2 * N_HEADS + GLA_RANK].set(gla_w2[l])
        y = _mixer(proj, cosf, sins, mlstm_conv[l], gate_rows, w2_pad, gla_b2[l], hgrn_lb,
                   ret_norm[l], mlstm_norm[l], gla_norm[l], hgrn_norm[l], consts, l, bsz, seq)
        x1, h2, gates = _post(y, xt, modl, _bf(w_out[l]), ln1_g[l], ln1_b[l], rw_pad, rb_pad, seq, alpha)
        xt = _moe(h2, gates, x1, modl, _bf(exp_w1[l]), _bf(exp_w3[l]), _bf(exp_w2[l]),
                  ln2_g[l], ln2_b[l], seq, alpha)
    return xt.reshape(bsz, seq, d)
```

```python
import functools

import numpy as np
import jax
import jax.numpy as jnp
from jax import lax
from jax.experimental import pallas as pl
from jax.experimental.pallas import tpu as pltpu

F32 = jnp.float32
BF16 = jnp.bfloat16

D_MODEL = 1024
N_MIXERS = 4
GW = D_MODEL // N_MIXERS
N_HEADS = 4
HEAD_DIM = GW // N_HEADS
CONV_WIDTH = 4
GLA_RANK = 16
GLA_TAU = 16.0
ROPE_BASE = 10000.0
N_GROUPS = 4
EXPERTS_PER_GROUP = 4
N_EXPERTS = 16
D_EXPERT = D_MODEL // 2
LN_EPS = 1e-5
NORM_EPS = 1e-6
QK_SCALE = HEAD_DIM ** -0.5

LANES = 128
CHUNK = 128
LEVELS = 7
CONV_PAD = 8
NPROJ = 16 * GW + LANES
EXTRA = 16 * GW
VMEM_LIMIT = 56 * 1024 * 1024

(R_Q, R_K, R_V, R_G, M_Q, M_K, M_V, M_O, G_Q, G_K, G_V, G_G, H_Q, H_F, H_I, H_G) = (
    GW * i for i in range(16))


def _bf(x):
    return x.astype(BF16)


def _dot(a, b):
    return jnp.dot(a, b, preferred_element_type=F32)


def _dot_nt(a, b):
    return lax.dot_general(a, b, (((1,), (1,)), ((), ())), preferred_element_type=F32)


def _split2(x):
    hi = _bf(x)
    lo = _bf(x - hi.astype(F32))
    return hi, lo


def _split3(x):
    hi = _bf(x)
    r = x - hi.astype(F32)
    mid = _bf(r)
    lo = _bf(r - mid.astype(F32))
    return hi, mid, lo


def _dot_f32(a, b):
    ah, al = _split2(a)
    bh, bl = _split2(b)
    return _dot(ah, bh) + _dot(ah, bl) + _dot(al, bh)


def _sigmoid(x):
    return 0.5 * jnp.tanh(0.5 * x) + 0.5


def _silu(x):
    return x * _sigmoid(x)


def _log_sigmoid(x):
    return jnp.minimum(x, 0.0) - jnp.log1p(jnp.exp(-jnp.abs(x)))


def _layer_norm(v, g, b):
    mu = jnp.mean(v, axis=-1, keepdims=True)
    d = v - mu
    var = jnp.mean(d * d, axis=-1, keepdims=True)
    return d * lax.rsqrt(var + LN_EPS) * g + b


def _adaln_kernel(c_ref, w_ref, b_ref, o_ref):
    cond = _silu(c_ref[...])
    o_ref[0] = _dot_f32(cond, w_ref[0]) + b_ref[0]


def _adaln(c, ada_w, ada_b):
    depth, d, n = ada_w.shape
    bsz = c.shape[0]
    tn = 1536
    return pl.pallas_call(
        _adaln_kernel,
        grid=(depth, n // tn),
        in_specs=[
            pl.BlockSpec((bsz, d), lambda l, j: (0, 0)),
            pl.BlockSpec((1, d, tn), lambda l, j: (l, 0, j)),
            pl.BlockSpec((1, 1, tn), lambda l, j: (l, 0, j)),
        ],
        out_specs=pl.BlockSpec((1, bsz, tn), lambda l, j: (l, 0, j)),
        out_shape=jax.ShapeDtypeStruct((depth, bsz, n), F32),
        compiler_params=pltpu.CompilerParams(
            dimension_semantics=("arbitrary", "arbitrary"), vmem_limit_bytes=VMEM_LIMIT),
        name="adaln",
    )(c, ada_w, ada_b.reshape(depth, 1, n))


def _proj_kernel(x_ref, mod_ref, w_ref, o_ref):
    sh = mod_ref[0, 0:1, :]
    sc = mod_ref[0, 1:2, :]
    h = x_ref[...] * (1.0 + sc) + sh
    o_ref[...] = _dot(_bf(h), w_ref[...])


def _proj(xt, modl, w_in_p, seq):
    t, d = xt.shape
    tm = 256
    per_batch = seq // tm
    return pl.pallas_call(
        _proj_kernel,
        grid=(t // tm,),
        in_specs=[
            pl.BlockSpec((tm, d), lambda i: (i, 0)),
            pl.BlockSpec((1, 6, d), lambda i: (i // per_batch, 0, 0)),
            pl.BlockSpec((d, NPROJ), lambda i: (0, 0)),
        ],
        out_specs=pl.BlockSpec((tm, NPROJ), lambda i: (i, 0)),
        out_shape=jax.ShapeDtypeStruct((t, NPROJ), F32),
        compiler_params=pltpu.CompilerParams(
            dimension_semantics=("arbitrary",), vmem_limit_bytes=VMEM_LIMIT),
        name="proj",
    )(xt, modl, w_in_p)


def _mixer_consts():
    c = CHUNK
    i = np.arange(c)[:, None]
    j = np.arange(c)[None, :]
    rel = i - j
    log_gamma = np.log1p(-np.exp2(-5.0 - np.arange(N_HEADS, dtype=np.float64)))
    dret = np.concatenate(
        [np.where(rel >= 0, np.exp(log_gamma[h] * np.maximum(rel, 0)), 0.0) for h in range(N_HEADS)], axis=1)
    lane_head = np.arange(GW) // HEAD_DIM
    dq = np.exp(log_gamma[lane_head][None, :] * (np.arange(c)[:, None] + 1.0))
    dk = np.exp(log_gamma[lane_head][None, :] * (c - 1.0 - np.arange(c)[:, None]))
    gc = np.exp(log_gamma[lane_head] * c)[None, :]
    lv = np.zeros((c, c), np.int32)
    lv[rel == 0] = 1
    spans = [(rel >= 0), (rel < 0)]
    code, b = 2, 1
    while b < c:
        lv[(i // (2 * b) == j // (2 * b)) & ((i // b) % 2 == 1) & ((j // b) % 2 == 0)] = code
        ref = (i // (2 * b)) * 2 * b + b - 1
        right = (i // b) % 2 == 1
        spans.append(np.where(right, (j > ref) & (j <= i), (j > i) & (j <= ref)))
        code, b = code + 1, b * 2
    lv = np.tile(lv, (1, N_HEADS))
    span = np.concatenate(spans, axis=0).astype(np.float32)
    span = np.tile(span, (1, 3))
    hh = (lane_head[:, None] == lane_head[None, :]).astype(np.float32)
    ones_bd = np.repeat(np.eye(N_HEADS, dtype=np.float32), c, axis=0)
    ones_bd = np.repeat(ones_bd, HEAD_DIM, axis=1)
    return dict(
        dret=jnp.asarray(dret, F32), dq=jnp.asarray(dq, F32), dk=jnp.asarray(dk, F32),
        gc=jnp.asarray(gc, F32), lv=jnp.asarray(lv), hh=jnp.asarray(hh, BF16),
        bdf=jnp.asarray(hh, F32), ones_bd=jnp.asarray(ones_bd, BF16), span=jnp.asarray(span, BF16))


def _lane_cumsum(x):
    lane = lax.broadcasted_iota(jnp.int32, x.shape, 1)
    s = 1
    while s < x.shape[1]:
        x = x + jnp.where(lane >= s, pltpu.roll(x, s, 1), 0.0)
        s *= 2
    return x


def _lane_cummax(x):
    lane = lax.broadcasted_iota(jnp.int32, x.shape, 1)
    s = 1
    while s < x.shape[1]:
        x = jnp.maximum(x, jnp.where(lane >= s, pltpu.roll(x, s, 1), -jnp.inf))
        s *= 2
    return x


def _mixer_kernel(layer,
                  proj_ref, cos_ref, sin_ref, conv_ref, gb_ref, w2_ref, b2_ref, lb_ref,
                  retg_ref, mlg_ref, glag_ref, hgg_ref,
                  dret_ref, dq_ref, dk_ref, gc_ref, lv_ref, hh_ref, bdf_ref, onesbd_ref, span_ref,
                  y_ref,
                  s_ret, s_cn, s_gla, s_hg, m_ml, conv_buf):
    c = CHUNK
    n = pl.program_id(1)

    @pl.when(n == 0)
    def _():
        s_ret[...] = jnp.zeros_like(s_ret)
        s_cn[...] = jnp.zeros_like(s_cn)
        s_gla[...] = jnp.zeros_like(s_gla)
        s_hg[...] = jnp.zeros_like(s_hg)
        m_ml[...] = jnp.zeros_like(m_ml)
        conv_buf[0:CONV_PAD, :] = jnp.zeros((CONV_PAD, 2 * GW), F32)

    lane_head = lax.broadcasted_iota(jnp.int32, (1, GW), 1) // HEAD_DIM
    head_rows = [(lane_head == h).astype(BF16) for h in range(N_HEADS)]
    hh = hh_ref[...]
    bdf = bdf_ref[...]
    lv = lv_ref[...]
    causal = lv >= 1

    def col(off, width=GW):
        return proj_ref[:, off:off + width]

    def stackmask(xb):
        return jnp.concatenate([xb * head_rows[h] for h in range(N_HEADS)], axis=0)

    def expand_heads(cols, first):
        out = jnp.zeros((cols.shape[0], GW), F32)
        for h in range(N_HEADS):
            out = jnp.where(lane_head == h, cols[:, first + h:first + h + 1], out)
        return out

    def head_mean(v):
        hi, lo = _split2(v)
        return (_dot(hi, hh) + _dot(lo, hh)) * (1.0 / HEAD_DIM)

    def head_norm(v, gain, center):
        if center:
            v = v - head_mean(v)
        return v * lax.rsqrt(head_mean(v * v) + NORM_EPS) * gain

    cosf = cos_ref[...]
    sins = sin_ref[...]
    half = lax.broadcasted_iota(jnp.int32, (1, GW), 1) % HEAD_DIM < HEAD_DIM // 2

    def rotary(v):
        swapped = jnp.where(half, pltpu.roll(v, GW - HEAD_DIM // 2, 1), pltpu.roll(v, HEAD_DIM // 2, 1))
        return v * cosf + swapped * sins

    rq = _bf(rotary(col(R_Q)) * QK_SCALE)
    rk = rotary(col(R_K))
    rv = col(R_V)
    sc = _dot_nt(rq, stackmask(_bf(rk))) * dret_ref[...]
    ro = _dot(_bf(sc), stackmask(_bf(rv))) + _dot(rq, _bf(s_ret[...])) * dq_ref[...]
    s_ret[...] = s_ret[...] * gc_ref[...] + _dot(_bf((rk * dk_ref[...]).T), _bf(rv)) * bdf
    y_ref[:, 0:GW] = head_norm(ro, retg_ref[...], True) * _silu(col(R_G))

    conv_buf[CONV_PAD:CONV_PAD + c, :] = col(M_Q, 2 * GW)
    qk = jnp.zeros((c, 2 * GW), F32)
    for t in range(CONV_WIDTH):
        s = CONV_WIDTH - 1 - t
        qk = qk + conv_buf[CONV_PAD - s:CONV_PAD - s + c, :] * conv_ref[t:t + 1, :]
    conv_buf[0:CONV_PAD, :] = conv_buf[c:c + CONV_PAD, :]
    qk = _silu(qk)
    mq = _bf(qk[:, 0:GW])
    mk = qk[:, GW:2 * GW] * QK_SCALE
    mv = col(M_V)

    ext = col(EXTRA, LANES)
    pre = ext.T[0:8, :] + gb_ref[...]
    row8 = lax.broadcasted_iota(jnp.int32, (8, c), 0)
    valid = row8 >= N_HEADS
    lf = jnp.where(valid, _log_sigmoid(pre), 0.0)
    gi = jnp.where(valid, pltpu.roll(pre, N_HEADS, 0), 0.0)
    bcum = _lane_cumsum(lf)
    a = gi - bcum
    m_prev = m_ml[...]
    mrow = jnp.maximum(m_prev, _lane_cummax(a))
    m_last = jnp.broadcast_to(mrow[:, c - 1:c], (8, c))
    b_last = jnp.broadcast_to(bcum[:, c - 1:c], (8, c))
    wi = jnp.exp(m_prev - mrow)
    em = jnp.exp(-(bcum + mrow))
    wk = jnp.exp(a - m_last)
    dec = jnp.exp(m_prev - m_last)
    m_ml[...] = jnp.where(valid, b_last + m_last, 0.0)
    cols = jnp.concatenate([mrow, wi, em, wk, jnp.zeros((c - 32, c), F32)], axis=0).T
    mexp = jnp.concatenate(
        [jnp.broadcast_to(cols[:, 4 + h:5 + h], (c, c)) for h in range(N_HEADS)], axis=1)
    a_row = jnp.concatenate([a[4 + h:5 + h, :] for h in range(N_HEADS)], axis=1)
    wmat = jnp.exp(jnp.where(causal, a_row - mexp, -jnp.inf))
    sm = _bf(_dot_nt(mq, stackmask(_bf(mk))) * wmat)
    v_ext = jnp.concatenate([stackmask(_bf(mv)), onesbd_ref[...]], axis=1)
    wi_l = expand_heads(cols, 12)
    nd = _dot(sm, v_ext) + jnp.concatenate([wi_l, wi_l], axis=1) * _dot(mq, _bf(s_cn[...]))
    mh = nd[:, 0:GW] / jnp.maximum(jnp.abs(nd[:, GW:2 * GW]), expand_heads(cols, 20))
    kt = mk * expand_heads(cols, 28)
    dec2 = jnp.concatenate([dec, dec], axis=1)
    dec_l = jnp.zeros((1, GW), F32)
    for h in range(N_HEADS):
        dec_l = jnp.where(lane_head == h, dec2[4 + h:5 + h, :], dec_l)
    upd = _dot(_bf(kt.T), jnp.concatenate([_bf(mv), jnp.ones((c, GW), BF16)], axis=1))
    s_cn[...] = (s_cn[...] * jnp.concatenate([dec_l, dec_l], axis=1)
                 + upd * jnp.concatenate([bdf, bdf], axis=1))
    y_ref[:, GW:2 * GW] = head_norm(mh * _sigmoid(col(M_O)), mlg_ref[...], True)

    span = span_ref[...]
    level_masks = [lv == code for code in range(2, 2 + LEVELS)]

    def decay_attention(q, k, v, g, st_ref):
        dec = jnp.exp(_dot(span, jnp.concatenate(_split3(g), axis=0)))
        e_cum = dec[0:c]
        e_end = dec[c:2 * c]
        out = _dot_nt(_bf(q * e_cum), _bf(st_ref[...]))
        out = out + _dot(_bf(q * k), hh) * v
        scores = jnp.zeros((c, N_HEADS * c), F32)
        for lvl in range(LEVELS):
            e = dec[(2 + lvl) * c:(3 + lvl) * c]
            scores = jnp.where(level_masks[lvl], _dot_nt(_bf(q * e), stackmask(_bf(k * e))), scores)
        out = out + _dot(_bf(scores), stackmask(_bf(v)))
        st_ref[...] = st_ref[...] * e_cum[c - 1:c, :] + _dot(_bf(v.T), _bf(k * e_end)) * bdf
        return out

    x_lr = _dot_f32(ext, w2_ref[...]) + b2_ref[...]
    log_a = _log_sigmoid(x_lr) * (1.0 / GLA_TAU)
    go = decay_attention(col(G_Q), col(G_K) * QK_SCALE, col(G_V), log_a, s_gla)
    y_ref[:, 2 * GW:3 * GW] = head_norm(go, glag_ref[...], False) * _silu(col(G_G))

    lb_all = lb_ref[...]
    lb_e = jnp.exp(lb_all - jnp.max(lb_all, axis=0, keepdims=True))
    lb_p = lb_e / jnp.sum(lb_e, axis=0, keepdims=True)
    lb = jnp.zeros((1, GW), F32)
    for l in range(1, layer + 1):
        lb = lb + lb_p[l:l + 1, :]
    z = col(H_F)
    x1 = jnp.log(lb)
    x2 = jnp.log1p(-lb) + _log_sigmoid(z)
    log_f = jnp.maximum(x1, x2) + jnp.log1p(jnp.exp(-jnp.abs(x1 - x2)))
    k_h = (1.0 - lb) * _sigmoid(-z)
    ho = decay_attention(_silu(col(H_Q)), k_h, col(H_I), log_f, s_hg)
    y_ref[:, 3 * GW:4 * GW] = head_norm(ho * _sigmoid(col(H_G)), hgg_ref[...], False)


def _mixer(proj, cosf, sins, conv_w, gate_rows, w2_pad, b2, lb, ret_g, ml_g, gla_g, hg_g, consts,
           layer, bsz, seq):
    c = CHUNK
    nc = seq // c
    t = bsz * seq
    row = lambda a: a.reshape(1, -1)
    full = lambda a: pl.BlockSpec(a.shape, lambda b, n: (0,) * a.ndim)
    cs = consts
    args = [proj, cosf, sins, conv_w, gate_rows, w2_pad, row(b2), lb,
            row(ret_g), row(ml_g), row(gla_g), row(hg_g),
            cs["dret"], cs["dq"], cs["dk"], cs["gc"], cs["lv"], cs["hh"], cs["bdf"], cs["ones_bd"], cs["span"]]
    in_specs = [
        pl.BlockSpec((c, NPROJ), lambda b, n: (b * nc + n, 0)),
        pl.BlockSpec((c, GW), lambda b, n: (n, 0)),
        pl.BlockSpec((c, GW), lambda b, n: (n, 0)),
    ] + [full(a) for a in args[3:]]
    return pl.pallas_call(
        functools.partial(_mixer_kernel, layer),
        grid=(bsz, nc),
        in_specs=in_specs,
        out_specs=pl.BlockSpec((c, D_MODEL), lambda b, n: (b * nc + n, 0)),
        out_shape=jax.ShapeDtypeStruct((t, D_MODEL), F32),
        scratch_shapes=[
            pltpu.VMEM((GW, GW), F32),
            pltpu.VMEM((GW, 2 * GW), F32),
            pltpu.VMEM((GW, GW), F32),
            pltpu.VMEM((GW, GW), F32),
            pltpu.VMEM((8, c), F32),
            pltpu.VMEM((c + CONV_PAD, 2 * GW), F32),
        ],
        compiler_params=pltpu.CompilerParams(
            dimension_semantics=("arbitrary", "arbitrary"), vmem_limit_bytes=VMEM_LIMIT),
        name="mixer",
    )(*args)


def _group_shift(v, pos, k):
    return jnp.where(pos < EXPERTS_PER_GROUP - k,
                     pltpu.roll(v, LANES - k, 1), pltpu.roll(v, EXPERTS_PER_GROUP - k, 1))


def _post_kernel(alpha, y_ref, x_ref, mod_ref, wout_ref, lng_ref, lnb_ref, rw_ref, rb_ref,
                 x1_ref, h2_ref, gates_ref):
    g1 = mod_ref[0, 2:3, :]
    sh2 = mod_ref[0, 3:4, :]
    sc2 = mod_ref[0, 4:5, :]
    mix = _dot(_bf(y_ref[...]), wout_ref[...])
    x1 = _layer_norm(alpha * x_ref[...] + g1 * mix, lng_ref[...], lnb_ref[...])
    x1_ref[...] = x1
    h = x1 * (1.0 + sc2) + sh2
    h2_ref[...] = _bf(h)

    lane = lax.broadcasted_iota(jnp.int32, (1, LANES), 1)
    real = lane < N_EXPERTS
    logits = jnp.where(real, _dot_f32(h, rw_ref[...]), -jnp.inf)
    ex = jnp.exp(logits - jnp.max(logits, axis=-1, keepdims=True))
    probs = ex / jnp.sum(ex, axis=-1, keepdims=True)
    sel = jnp.where(real, probs + rb_ref[...], -jnp.inf)
    pos = lane % EXPERTS_PER_GROUP
    gid = lane // EXPERTS_PER_GROUP
    r1 = _group_shift(sel, pos, 1)
    r2 = _group_shift(sel, pos, 2)
    r3 = _group_shift(sel, pos, 3)
    pair = jnp.maximum(jnp.maximum(jnp.maximum(sel + r1, sel + r2), jnp.maximum(sel + r3, r1 + r2)),
                       jnp.maximum(r1 + r3, r2 + r3))
    pair = jnp.where(real, pair, -jnp.inf)
    best = jnp.max(pair, axis=-1, keepdims=True)
    first = jnp.min(jnp.where(pair == best, gid, N_GROUPS), axis=-1, keepdims=True)
    in_group = gid == first
    rank = jnp.zeros(sel.shape, jnp.int32)
    for k, r in ((1, r1), (2, r2), (3, r3)):
        beats = (r > sel) | ((r == sel) & (pos + k >= EXPERTS_PER_GROUP))
        rank = rank + beats.astype(jnp.int32)
    chosen = in_group & (rank < 2)
    w = jnp.where(chosen, probs, 0.0)
    gates_ref[...] = w / jnp.sum(w, axis=-1, keepdims=True)


def _post(y, xt, modl, w_out, ln_g, ln_b, rw_pad, rb_pad, seq, alpha):
    t, d = xt.shape
    tm = 256
    per_batch = seq // tm
    row = lambda a: a.reshape(1, -1)
    tile = pl.BlockSpec((tm, d), lambda i: (i, 0))
    full = lambda a: pl.BlockSpec(a.shape, lambda i: (0,) * a.ndim)
    args = [y, xt, modl, w_out, row(ln_g), row(ln_b), rw_pad, rb_pad]
    return pl.pallas_call(
        functools.partial(_post_kernel, alpha),
        grid=(t // tm,),
        in_specs=[tile, tile, pl.BlockSpec((1, 6, d), lambda i: (i // per_batch, 0, 0))]
        + [full(a) for a in args[3:]],
        out_specs=[tile, tile, pl.BlockSpec((tm, LANES), lambda i: (i, 0))],
        out_shape=[jax.ShapeDtypeStruct((t, d), F32), jax.ShapeDtypeStruct((t, d), BF16),
                   jax.ShapeDtypeStruct((t, LANES), F32)],
        compiler_params=pltpu.CompilerParams(
            dimension_semantics=("arbitrary",), vmem_limit_bytes=VMEM_LIMIT),
        name="post",
    )(*args)


def _moe_kernel(alpha, h_ref, gates_ref, x1_ref, mod_ref, w1_ref, w3_ref, w2_ref, lng_ref, lnb_ref,
                o_ref, acc_ref):
    e =pl.program_id(1)

    @pl.when(e == 0)
    def _():
        acc_ref[...] = jnp.zeros_like(acc_ref)

    h = h_ref[...]
    lane = lax.broadcasted_iota(jnp.int32, (1, LANES), 1)
    ge = jnp.sum(jnp.where(lane == e, gates_ref[...], 0.0), axis=-1, keepdims=True)
    a = _silu(_dot(h, w1_ref[0])) * _dot(h, w3_ref[0])
    acc_ref[...] += _dot(_bf(a * ge), w2_ref[0])

    @pl.when(e == N_EXPERTS - 1)
    def _():
        g2 = mod_ref[0, 5:6, :]
        o_ref[...] = _layer_norm(alpha * x1_ref[...] + g2 * acc_ref[...], lng_ref[...], lnb_ref[...])


def _moe(h2, gates, x1, modl, w1, w3, w2, ln_g, ln_b, seq, alpha):
    t, d = x1.shape
    tm = 512
    per_batch = seq // tm
    row = lambda a: a.reshape(1, -1)
    tile = pl.BlockSpec((tm, d), lambda i, e: (i, 0))
    return pl.pallas_call(
        functools.partial(_moe_kernel, alpha),
        grid=(t // tm, N_EXPERTS),
        in_specs=[
            tile,
            pl.BlockSpec((tm, LANES), lambda i, e: (i, 0)),
            tile,
            pl.BlockSpec((1, 6, d), lambda i, e: (i // per_batch, 0, 0)),
            pl.BlockSpec((1, d, D_EXPERT), lambda i, e: (e, 0, 0)),
            pl.BlockSpec((1, d, D_EXPERT), lambda i, e: (e, 0, 0)),
            pl.BlockSpec((1, D_EXPERT, d), lambda i, e: (e, 0, 0)),
            pl.BlockSpec((1, d), lambda i, e: (0, 0)),
            pl.BlockSpec((1, d), lambda i, e: (0, 0)),
        ],
        out_specs=tile,
        out_shape=jax.ShapeDtypeStruct((t, d), F32),
        scratch_shapes=[pltpu.VMEM((tm, d), F32)],
        compiler_params=pltpu.CompilerParams(
            dimension_semantics=("arbitrary", "arbitrary"), vmem_limit_bytes=VMEM_LIMIT),
        name="moe",
    )(h2, gates, x1, modl, w1, w3, w2, row(ln_g), row(ln_b))


def _regroup_w_in(w):
    a = 8 * GW
    b = a + 2 * N_HEADS
    cc = b + 4 * GW
    dd = cc + GLA_RANK
    pad = jnp.zeros((w.shape[0], LANES - 2 * N_HEADS - GLA_RANK), w.dtype)
    return jnp.concatenate([w[:, :a], w[:, b:cc], w[:, dd:], w[:, a:b], w[:, cc:dd], pad], axis=1)


def _rotary_tables(seq):
    inv = ROPE_BASE ** (-jnp.arange(0, HEAD_DIM, 2, dtype=F32) / HEAD_DIM)
    ang = jnp.arange(seq, dtype=F32)[:, None] * inv[None, :]
    cos = jnp.cos(ang)
    sin = jnp.sin(ang)
    cosf = jnp.tile(jnp.concatenate([cos, cos], axis=1), (1, N_HEADS))
    sins = jnp.tile(jnp.concatenate([-sin, sin], axis=1), (1, N_HEADS))
    return cosf, sins


def kernel(x, c, ada_w, ada_b, w_in, mlstm_conv, mlstm_gate_b, gla_w2, gla_b2, hgrn_lb, ret_norm, mlstm_norm, gla_norm, hgrn_norm, w_out, ln1_g, ln1_b, router_w, router_b, exp_w1, exp_w3, exp_w2, ln2_g, ln2_b):
    bsz, seq, d = x.shape
    depth = ada_w.shape[0]
    assert d == D_MODEL and seq % 512 == 0 and bsz <= 16
    t = bsz * seq
    alpha = (2.0 * depth) ** 0.25
    consts = _mixer_consts()
    cosf, sins = _rotary_tables(seq)
    c_rows = jnp.pad(c, ((0, 16 - bsz), (0, 0)))
    mod = _adaln(c_rows, ada_w, ada_b)[:, :bsz]
    rw_pad = jnp.pad(router_w, ((0, 0), (0, LANES - N_EXPERTS)))
    rb_pad = jnp.pad(router_b, (0, LANES - N_EXPERTS)).reshape(1, LANES)
    xt = x.reshape(t, d)
    for l in range(depth):
        modl = mod[l].reshape(bsz, 6, d)
        proj = _proj(xt, modl, _bf(_regroup_w_in(w_in[l])), seq)
        gate_rows = jnp.broadcast_to(mlstm_gate_b[l][:, None], (2 * N_HEADS, CHUNK))
        w2_pad = jnp.zeros((LANES, GW), F32).at[2 * N_HEADS:2 * N_HEADS + GLA_RANK].set(gla_w2[l])
        y = _mixer(proj, cosf, sins, mlstm_conv[l], gate_rows, w2_pad, gla_b2[l], hgrn_lb,
                   ret_norm[l], mlstm_norm[l], gla_norm[l], hgrn_norm[l], consts, l, bsz, seq)
        x1, h2, gates = _post(y, xt, modl, _bf(w_out[l]), ln1_g[l], ln1_b[l], rw_pad, rb_pad, seq, alpha)
        xt = _moe(h2, gates, x1, modl, _bf(exp_w1[l]), _bf(exp_w3[l]), _bf(exp_w2[l]),
                  ln2_g[l], ln2_b[l], seq, alpha)
    return xt.reshape(bsz, seq, d)
```

```python
import functools

import numpy as np
import jax
import jax.numpy as jnp
from jax import lax
from jax.experimental import pallas as pl
from jax.experimental.pallas import tpu as pltpu

F32 = jnp.float32
BF16 = jnp.bfloat16

D_MODEL = 1024
N_MIXERS = 4
GW = D_MODEL // N_MIXERS
N_HEADS = 4
HEAD_DIM = GW // N_HEADS
CONV_WIDTH = 4
GLA_RANK = 16
GLA_TAU = 16.0
ROPE_BASE = 10000.0
N_GROUPS = 4
EXPERTS_PER_GROUP = 4
N_EXPERTS = 16
D_EXPERT = D_MODEL // 2
LN_EPS = 1e-5
NORM_EPS = 1e-6
QK_SCALE = HEAD_DIM ** -0.5

LANES = 128
CHUNK = 128
LEVELS = 7
CONV_PAD = 8
NPROJ = 16 * GW + LANES
EXTRA = 16 * GW
VMEM_LIMIT = 56 * 1024 * 1024

(R_Q, R_K, R_V, R_G, M_Q, M_K, M_V, M_O, G_Q, G_K, G_V, G_G, H_Q, H_F, H_I, H_G) = (
    GW * i for i in range(16))


def _bf(x):
    return x.astype(BF16)


def _dot(a, b):
    return jnp.dot(a, b, preferred_element_type=F32)


def _dot_nt(a, b):
    return lax.dot_general(a, b, (((1,), (1,)), ((), ())), preferred_element_type=F32)


def _split2(x):
    hi = _bf(x)
    lo = _bf(x - hi.astype(F32))
    return hi, lo


def _split3(x):
    hi = _bf(x)
    r = x - hi.astype(F32)
    mid = _bf(r)
    lo = _bf(r - mid.astype(F32))
    return hi, mid, lo


def _dot_f32(a, b):
    ah, al = _split2(a)
    bh, bl = _split2(b)
    return _dot(ah, bh) + _dot(ah, bl) + _dot(al, bh)


def _sigmoid(x):
    return 0.5 * jnp.tanh(0.5 * x) + 0.5


def _silu(x):
    return x * _sigmoid(x)


def _log_sigmoid(x):
    return jnp.minimum(x, 0.0) - jnp.log1p(jnp.exp(-jnp.abs(x)))


def _layer_norm(v, g, b):
    mu = jnp.mean(v, axis=-1, keepdims=True)
    d = v - mu
    var = jnp.mean(d * d, axis=-1, keepdims=True)
    return d * lax.rsqrt(var + LN_EPS) * g + b


def _adaln_kernel(c_ref, w_ref, b_ref, o_ref):
    cond = _silu(c_ref[...])
    o_ref[0] = _dot_f32(cond, w_ref[0]) + b_ref[0]


def _adaln(c, ada_w, ada_b):
    depth, d, n = ada_w.shape
    bsz = c.shape[0]
    tn = 1536
    return pl.pallas_call(
        _adaln_kernel,
        grid=(depth, n // tn),
        in_specs=[
            pl.BlockSpec((bsz, d), lambda l, j: (0, 0)),
            pl.BlockSpec((1, d, tn), lambda l, j: (l, 0, j)),
            pl.BlockSpec((1, 1, tn), lambda l, j: (l, 0, j)),
        ],
        out_specs=pl.BlockSpec((1, bsz, tn), lambda l, j: (l, 0, j)),
        out_shape=jax.ShapeDtypeStruct((depth, bsz, n), F32),
        compiler_params=pltpu.CompilerParams(
            dimension_semantics=("arbitrary", "arbitrary"), vmem_limit_bytes=VMEM_LIMIT),
        name="adaln",
    )(c, ada_w, ada_b.reshape(depth, 1, n))


def _proj_kernel(x_ref, mod_ref, w_ref, o_ref):
    sh = mod_ref[0, 0:1, :]
    sc = mod_ref[0, 1:2, :]
    h = x_ref[...] * (1.0 + sc) + sh
    o_ref[...] = _dot(_bf(h), w_ref[...])


def _proj(xt, modl, w_in_p, seq):
    t, d = xt.shape
    tm = 256
    per_batch = seq // tm
    return pl.pallas_call(
        _proj_kernel,
        grid=(t // tm,),
        in_specs=[
            pl.BlockSpec((tm, d), lambda i: (i, 0)),
            pl.BlockSpec((1, 6, d), lambda i: (i // per_batch, 0, 0)),
            pl.BlockSpec((d, NPROJ), lambda i: (0, 0)),
        ],
        out_specs=pl.BlockSpec((tm, NPROJ), lambda i: (i, 0)),
        out_shape=jax.ShapeDtypeStruct((t, NPROJ), F32),
        compiler_params=pltpu.CompilerParams(
            dimension_semantics=("arbitrary",), vmem_limit_bytes=VMEM_LIMIT),
        name="proj",
    )(xt, modl, w_in_p)


def _mixer_consts():
    c = CHUNK
    i = np.arange(c)[:, None]
    j = np.arange(c)[None, :]
    rel = i - j
    log_gamma = np.log1p(-np.exp2(-5.0 - np.arange(N_HEADS, dtype=np.float64)))
    dret = np.concatenate(
        [np.where(rel >= 0, np.exp(log_gamma[h] * np.maximum(rel, 0)), 0.0) for h in range(N_HEADS)], axis=1)
    lane_head = np.arange(GW) // HEAD_DIM
    dq = np.exp(log_gamma[lane_head][None, :] * (np.arange(c)[:, None] + 1.0))
    dk = np.exp(log_gamma[lane_head][None, :] * (c - 1.0 - np.arange(c)[:, None]))
    gc = np.exp(log_gamma[lane_head] * c)[None, :]
    lv = np.zeros((c, c), np.int32)
    lv[rel == 0] = 1
    spans = [(rel >= 0), (rel < 0)]
    code, b = 2, 1
    while b < c:
        lv[(i // (2 * b) == j // (2 * b)) & ((i // b) % 2 == 1) & ((j // b) % 2 == 0)] = code
        ref = (i // (2 * b)) * 2 * b + b - 1
        right = (i // b) % 2 == 1
        spans.append(np.where(right, (j > ref) & (j <= i), (j > i) & (j <= ref)))
        code, b = code + 1, b * 2
    lv = np.tile(lv, (1, N_HEADS))
    span = np.concatenate(spans, axis=0).astype(np.float32)
    span = np.tile(span, (1, 3))
    hh = (lane_head[:, None] == lane_head[None, :]).astype(np.float32)
    ones_bd = np.repeat(np.eye(N_HEADS, dtype=np.float32), c, axis=0)
    ones_bd = np.repeat(ones_bd, HEAD_DIM, axis=1)
    return dict(
        dret=jnp.asarray(dret, F32), dq=jnp.asarray(dq, F32), dk=jnp.asarray(dk, F32),
        gc=jnp.asarray(gc, F32), lv=jnp.asarray(lv), hh=jnp.asarray(hh, BF16),
        bdf=jnp.asarray(hh, F32), ones_bd=jnp.asarray(ones_bd, BF16), span=jnp.asarray(span, BF16))


def _lane_cumsum(x):
    lane = lax.broadcasted_iota(jnp.int32, x.shape, 1)
    s = 1
    while s < x.shape[1]:
        x = x + jnp.where(lane >= s, pltpu.roll(x, s, 1), 0.0)
        s *= 2
    return x


def _lane_cummax(x):
    lane = lax.broadcasted_iota(jnp.int32, x.shape, 1)
    s = 1
    while s < x.shape[1]:
        x = jnp.maximum(x, jnp.where(lane >= s, pltpu.roll(x, s, 1), -jnp.inf))
        s *= 2
    return x


def _mixer_kernel(layer,
                  proj_ref, cos_ref, sin_ref, conv_ref, gb_ref, w2_ref, b2_ref, lb_ref,
                  retg_ref, mlg_ref, glag_ref, hgg_ref,
                  dret_ref, dq_ref, dk_ref, gc_ref, lv_ref, hh_ref, bdf_ref, onesbd_ref, span_ref,
                  y_ref,
                  s_ret, s_cn, s_gla, s_hg, m_ml, conv_buf):
    c = CHUNK
    n = pl.program_id(1)

    @pl.when(n == 0)
    def _():
        s_ret[...] = jnp.zeros_like(s_ret)
        s_cn[...] = jnp.zeros_like(s_cn)
        s_gla[...] = jnp.zeros_like(s_gla)
        s_hg[...] = jnp.zeros_like(s_hg)
        m_ml[...] = jnp.zeros_like(m_ml)
        conv_buf[0:CONV_PAD, :] = jnp.zeros((CONV_PAD, 2 * GW), F32)

    lane_head = lax.broadcasted_iota(jnp.int32, (1, GW), 1) // HEAD_DIM
    head_rows = [(lane_head == h).astype(BF16) for h in range(N_HEADS)]
    hh = hh_ref[...]
    bdf = bdf_ref[...]
    lv = lv_ref[...]
    causal = lv >= 1

    def col(off, width=GW):
        return proj_ref[:, off:off + width]

    def stackmask(xb):
        return jnp.concatenate([xb * head_rows[h] for h in range(N_HEADS)], axis=0)

    def expand_heads(cols, first):
        out = jnp.zeros((cols.shape[0], GW), F32)
        for h in range(N_HEADS):
            out = jnp.where(lane_head == h, cols[:, first + h:first + h + 1], out)
        return out

    def head_mean(v):
        hi, lo = _split2(v)
        return (_dot(hi, hh) + _dot(lo, hh)) * (1.0 / HEAD_DIM)

    def head_norm(v, gain, center):
        if center:
            v = v - head_mean(v)
        return v * lax.rsqrt(head_mean(v * v) + NORM_EPS) * gain

    cosf = cos_ref[...]
    sins = sin_ref[...]
    half = lax.broadcasted_iota(jnp.int32, (1, GW), 1) % HEAD_DIM < HEAD_DIM // 2

    def rotary(v):
        swapped = jnp.where(half, pltpu.roll(v, GW - HEAD_DIM // 2, 1), pltpu.roll(v, HEAD_DIM // 2, 1))
        return v * cosf + swapped * sins

    rq = _bf(rotary(col(R_Q)) * QK_SCALE)
    rk = rotary(col(R_K))
    rv = col(R_V)
    sc = _dot_nt(rq, stackmask(_bf(rk))) * dret_ref[...]
    ro = _dot(_bf(sc), stackmask(_bf(rv))) + _dot(rq, _bf(s_ret[...])) * dq_ref[...]
    s_ret[...] = s_ret[...] * gc_ref[...] + _dot(_bf((rk * dk_ref[...]).T), _bf(rv)) * bdf
    y_ref[:, 0:GW] = head_norm(ro, retg_ref[...], True) * _silu(col(R_G))

    conv_buf[CONV_PAD:CONV_PAD + c, :] = col(M_Q, 2 * GW)
    qk = jnp.zeros((c, 2 * GW), F32)
    for t in range(CONV_WIDTH):
        s = CONV_WIDTH - 1 - t
        qk = qk + conv_buf[CONV_PAD - s:CONV_PAD - s + c, :] * conv_ref[t:t + 1, :]
    conv_buf[0:CONV_PAD, :] = conv_buf[c:c + CONV_PAD, :]
    qk = _silu(qk)
    mq = _bf(qk[:, 0:GW])
    mk = qk[:, GW:2 * GW] * QK_SCALE
    mv = col(M_V)

    ext = col(EXTRA, LANES)
    pre = ext.T[0:8, :] + gb_ref[...]
    row8 = lax.broadcasted_iota(jnp.int32, (8, c), 0)
    valid = row8 >= N_HEADS
    lf = jnp.where(valid, _log_sigmoid(pre), 0.0)
    gi = jnp.where(valid, pltpu.roll(pre, N_HEADS, 0), 0.0)
    bcum = _lane_cumsum(lf)
    a = gi - bcum
    m_prev = m_ml[...]
    mrow = jnp.maximum(m_prev, _lane_cummax(a))
    m_last = jnp.broadcast_to(mrow[:, c - 1:c], (8, c))
    b_last = jnp.broadcast_to(bcum[:, c - 1:c], (8, c))
    wi = jnp.exp(m_prev - mrow)
    em = jnp.exp(-(bcum + mrow))
    wk = jnp.exp(a - m_last)
    dec = jnp.exp(m_prev - m_last)
    m_ml[...] = jnp.where(valid, b_last + m_last, 0.0)
    cols = jnp.concatenate([mrow, wi, em, wk, jnp.zeros((c - 32, c), F32)], axis=0).T
    mexp = jnp.concatenate(
        [jnp.broadcast_to(cols[:, 4 + h:5 + h], (c, c)) for h in range(N_HEADS)], axis=1)
    a_row = jnp.concatenate([a[4 + h:5 + h, :] for h in range(N_HEADS)], axis=1)
    wmat = jnp.exp(jnp.where(causal, a_row - mexp, -jnp.inf))
    sm = _bf(_dot_nt(mq, stackmask(_bf(mk))) * wmat)
    v_ext = jnp.concatenate([stackmask(_bf(mv)), onesbd_ref[...]], axis=1)
    wi_l = expand_heads(cols, 12)
    nd = _dot(sm, v_ext) + jnp.concatenate([wi_l, wi_l], axis=1) * _dot(mq, _bf(s_cn[...]))
    mh = nd[:, 0:GW] / jnp.maximum(jnp.abs(nd[:, GW:2 * GW]), expand_heads(cols, 20))
    kt = mk * expand_heads(cols, 28)
    dec2 = jnp.concatenate([dec, dec], axis=1)
    dec_l = jnp.zeros((1, GW), F32)
    for h in range(N_HEADS):
        dec_l = jnp.where(lane_head == h, dec2[4 + h:5 + h, :], dec_l)
    upd = _dot(_bf(kt.T), jnp.concatenate([_bf(mv), jnp.ones((c, GW), BF16)], axis=1))
    s_cn[...] = (s_cn[...] * jnp.concatenate([dec_l, dec_l], axis=1)
                 + upd * jnp.concatenate([bdf, bdf], axis=1))
    y_ref[:, GW:2 * GW] = head_norm(mh * _sigmoid(col(M_O)), mlg_ref[...], True)

    span = span_ref[...]
    level_masks = [lv == code for code in range(2, 2 + LEVELS)]

    def decay_attention(q, k, v, g, st_ref):
        dec = jnp.exp(_dot(span, jnp.concatenate(_split3(g), axis=0)))
        e_cum = dec[0:c]
        e_end = dec[c:2 * c]
        out = _dot_nt(_bf(q * e_cum), _bf(st_ref[...]))
        out = out + _dot(_bf(q * k), hh) * v
        scores = jnp.zeros((c, N_HEADS * c), F32)
        for lvl in range(LEVELS):
            e = dec[(2 + lvl) * c:(3 + lvl) * c]
            scores = jnp.where(level_masks[lvl], _dot_nt(_bf(q * e), stackmask(_bf(k * e))), scores)
        out = out + _dot(_bf(scores), stackmask(_bf(v)))
        st_ref[...] = st_ref[...] * e_cum[c - 1:c, :] + _dot(_bf(v.T), _bf(k * e_end)) * bdf
        return out

    x_lr = _dot_f32(ext, w2_ref[...]) + b2_ref[...]
    log_a = _log_sigmoid(x_lr) * (1.0 / GLA_TAU)
    go = decay_attention(col(G_Q), col(G_K) * QK_SCALE, col(G_V), log_a, s_gla)
    y_ref[:, 2 * GW:3 * GW] = head_norm(go, glag_ref[...], False) * _silu(col(G_G))

    lb_all = lb_ref[...]
    lb_e = jnp.exp(lb_all - jnp.max(lb_all, axis=0, keepdims=True))
    lb_p = lb_e / jnp.sum(lb_e, axis=0, keepdims=True)
    lb = jnp.zeros((1, GW), F32)
    for l in range(1, layer + 1):
        lb = lb + lb_p[l:l + 1, :]
    z = col(H_F)
    x1 = jnp.log(lb)
    x2 = jnp.log1p(-lb) + _log_sigmoid(z)
    log_f = jnp.maximum(x1, x2) + jnp.log1p(jnp.exp(-jnp.abs(x1 - x2)))
    k_h = (1.0 - lb) * _sigmoid(-z)
    ho = decay_attention(_silu(col(H_Q)), k_h, col(H_I), log_f, s_hg)
    y_ref[:, 3 * GW:4 * GW] = head_norm(ho * _sigmoid(col(H_G)), hgg_ref[...], False)


def _mixer(proj, cosf, sins, conv_w, gate_rows, w2_pad, b2, lb, ret_g, ml_g, gla_g, hg_g, consts,
           layer, bsz, seq):
    c = CHUNK
    nc = seq // c
    t = bsz * seq
    row = lambda a: a.reshape(1, -1)
    full = lambda a: pl.BlockSpec(a.shape, lambda b, n: (0,) * a.ndim)
    cs = consts
    args = [proj, cosf, sins, conv_w, gate_rows, w2_pad, row(b2), lb,
            row(ret_g), row(ml_g), row(gla_g), row(hg_g),
            cs["dret"], cs["dq"], cs["dk"], cs["gc"], cs["lv"], cs["hh"], cs["bdf"], cs["ones_bd"], cs["span"]]
    in_specs = [
        pl.BlockSpec((c, NPROJ), lambda b, n: (b * nc + n, 0)),
        pl.BlockSpec((c, GW), lambda b, n: (n, 0)),
        pl.BlockSpec((c, GW), lambda b, n: (n, 0)),
    ] + [full(a) for a in args[3:]]
    return pl.pallas_call(
        functools.partial(_mixer_kernel, layer),
        grid=(bsz, nc),
        in_specs=in_specs,
        out_specs=pl.BlockSpec((c, D_MODEL), lambda b, n: (b * nc + n, 0)),
        out_shape=jax.ShapeDtypeStruct((t, D_MODEL), F32),
        scratch_shapes=[
            pltpu.VMEM((GW, GW), F32),
            pltpu.VMEM((GW, 2 * GW), F32),
            pltpu.VMEM((GW, GW), F32),
            pltpu.VMEM((GW, GW), F32),
            pltpu.VMEM((8, c), F32),
            pltpu.VMEM((c + CONV_PAD, 2 * GW), F32),
        ],
        compiler_params=pltpu.CompilerParams(
            dimension_semantics=("arbitrary", "arbitrary"), vmem_limit_bytes=VMEM_LIMIT),
        name="mixer",
    )(*args)


def _group_shift(v, pos, k):
    return jnp.where(pos < EXPERTS_PER_GROUP - k,
                     pltpu.roll(v, LANES - k, 1), pltpu.roll(v, EXPERTS_PER_GROUP - k, 1))


def _router(h, rw, rb):
    lane = lax.broadcasted_iota(jnp.int32, (1, LANES), 1)
    real = lane < N_EXPERTS
    logits = jnp.where(real, _dot_f32(h, rw), -jnp.inf)
    ex = jnp.exp(logits - jnp.max(logits, axis=-1, keepdims=True))
    probs = ex / jnp.sum(ex, axis=-1, keepdims=True)
    sel = jnp.where(real, probs + rb, -jnp.inf)
    pos = lane % EXPERTS_PER_GROUP
    gid = lane // EXPERTS_PER_GROUP
    r1 = _group_shift(sel, pos, 1)
    r2 = _group_shift(sel, pos, 2)
    r3 = _group_shift(sel, pos, 3)
    pair = jnp.maximum(jnp.maximum(jnp.maximum(sel + r1, sel + r2), jnp.maximum(sel + r3, r1 + r2)),
                       jnp.maximum(r1 + r3, r2 + r3))
    pair = jnp.where(real, pair, -jnp.inf)
    best = jnp.max(pair, axis=-1, keepdims=True)
    first = jnp.min(jnp.where(pair == best, gid, N_GROUPS), axis=-1, keepdims=True)
    rank = jnp.zeros(sel.shape, jnp.int32)
    for k, r in ((1, r1), (2, r2), (3, r3)):
        beats = (r > sel) | ((r == sel) & (pos + k >= EXPERTS_PER_GROUP))
        rank = rank + beats.astype(jnp.int32)
    return probs, first, rank


def _post_kernel(alpha, y_ref, x_ref, mod_ref, wout_ref, lng_ref, lnb_ref, rw_ref, rb_ref,
                 stri_ref, x1_ref, h2_ref, slots_ref, counts_ref, carry_ref):
    tm = y_ref.shape[0]

    @pl.when(pl.program_id(0) == 0)
    def _():
        carry_ref[...] = jnp.zeros_like(carry_ref)

    g1 = mod_ref[0, 2:3, :]
    sh2 = mod_ref[0, 3:4, :]
    sc2 = mod_ref[0, 4:5, :]
    mix = _dot(_bf(y_ref[...]), wout_ref[...])
    x1 = _layer_norm(alpha * x_ref[...] + g1 * mix, lng_ref[...], lnb_ref[...])
    x1_ref[...] = x1
    h = x1 * (1.0 + sc2) + sh2
    h2_ref[...] = h

    _, first, _ = _router(h, rw_ref[...], rb_ref[...])
    lane = lax.broadcasted_iota(jnp.int32, (1, LANES), 1)
    onehot = jnp.where(lane == first, 1.0, 0.0)
    before = _dot(stri_ref[...], _bf(onehot)) + carry_ref[...]
    code = first.astype(F32) * float(1 << POS_BITS) + jnp.sum(onehot * before, axis=-1, keepdims=True)
    slots_ref[0] = jnp.broadcast_to(code, (tm, LANES)).T[0:1, :].astype(jnp.int32)
    carry_ref[...] += jnp.sum(onehot, axis=0, keepdims=True)
    counts_ref[...] = carry_ref[...].astype(jnp.int32)


def _post(y, xt, modl, w_out, ln_g, ln_b, rw_pad, rb_pad, seq, alpha):
    t, d = xt.shape
    tm = 256
    per_batch = seq // tm
    row = lambda a: a.reshape(1, -1)
    tile = pl.BlockSpec((tm, d), lambda i: (i, 0))
    full = lambda a: pl.BlockSpec(a.shape, lambda i: (0,) * a.ndim)
    stri = jnp.asarray(np.tril(np.ones((tm, tm), np.float32), -1), BF16)
    args = [y, xt, modl, w_out, row(ln_g), row(ln_b), rw_pad, rb_pad, stri]
    return pl.pallas_call(
        functools.partial(_post_kernel, alpha),
        grid=(t // tm,),
        in_specs=[tile, tile, pl.BlockSpec((1, 6, d), lambda i: (i // per_batch, 0, 0))]
        + [full(a) for a in args[3:]],
        out_specs=[tile, tile, pl.BlockSpec((1, 1, tm), lambda i: (i, 0, 0)),
                   pl.BlockSpec((1, LANES), lambda i: (0, 0))],
        out_shape=[jax.ShapeDtypeStruct((t, d), F32), jax.ShapeDtypeStruct((t, d), F32),
                   jax.ShapeDtypeStruct((t // tm, 1, tm), jnp.int32),
                   jax.ShapeDtypeStruct((1, LANES), jnp.int32)],
        scratch_shapes=[pltpu.VMEM((1, LANES), F32)],
        compiler_params=pltpu.CompilerParams(
            dimension_semantics=("arbitrary",), vmem_limit_bytes=VMEM_LIMIT),
        name="post",
    )(*args)


MOE_TILE = 512
DMA_WINDOW = 64


POS_BITS = 15


def _slot(code, row_start_ref):
    return row_start_ref[code >> POS_BITS] + (code & ((1 << POS_BITS) - 1))


def _dispatch_kernel(n_rows, codes_ref, row_start_ref, zero_tiles_ref, h_hbm, hs_hbm, zero_buf, sem, zsem):
    zero_buf[...] = jnp.zeros_like(zero_buf)
    for k in range(zero_tiles_ref.shape[0]):
        blk = zero_tiles_ref[k]

        @pl.when(blk >= 0)
        def _():
            dst = hs_hbm.at[pl.ds(pl.multiple_of(blk * MOE_TILE, MOE_TILE), MOE_TILE)]
            pltpu.make_async_copy(zero_buf, dst, zsem).start()
            pltpu.make_async_copy(zero_buf, dst, zsem).wait()

    def window_wait():
        pltpu.make_async_copy(h_hbm.at[pl.ds(0, DMA_WINDOW)], hs_hbm.at[pl.ds(0, DMA_WINDOW)], sem).wait()

    def issue(w, carry):
        for r in range(DMA_WINDOW):
            t = w * DMA_WINDOW + r
            s = _slot(codes_ref[t], row_start_ref)
            pltpu.make_async_copy(h_hbm.at[pl.ds(t, 1)], hs_hbm.at[pl.ds(s, 1)], sem).start()

        @pl.when(w > 0)
        def _():
            window_wait()
        return carry

    lax.fori_loop(0, n_rows // DMA_WINDOW, issue, 0)
    window_wait()


def _dispatch(codes, row_start, zero_tiles, h2, n_tiles_max):
    t, d = h2.shape
    return pl.pallas_call(
        functools.partial(_dispatch_kernel, t),
        grid_spec=pltpu.PrefetchScalarGridSpec(
            num_scalar_prefetch=3,
            grid=(1,),
            in_specs=[pl.BlockSpec(memory_space=pl.ANY)],
            out_specs=pl.BlockSpec(memory_space=pl.ANY),
            scratch_shapes=[pltpu.VMEM((MOE_TILE, d), F32), pltpu.SemaphoreType.DMA(()),
                            pltpu.SemaphoreType.DMA(())],
        ),
        out_shape=jax.ShapeDtypeStruct((n_tiles_max * MOE_TILE, d), F32),
        compiler_params=pltpu.CompilerParams(
            dimension_semantics=("arbitrary",), vmem_limit_bytes=VMEM_LIMIT),
        name="dispatch",
    )(codes, row_start, zero_tiles, h2)


def _moe_kernel(tile_group_ref, n_tiles_ref, hs_ref, rw_ref, rb_ref, w1_ref, w3_ref, w2_ref, ys_ref):
    i = pl.program_id(0)

    @pl.when(i >= n_tiles_ref[0])
    def _():
        ys_ref[...] = jnp.zeros_like(ys_ref)

    @pl.when(i < n_tiles_ref[0])
    def _():
        group = tile_group_ref[i]
        h = hs_ref[...]
        probs, _, rank = _router(h, rw_ref[...], rb_ref[...])
        lane = lax.broadcasted_iota(jnp.int32, (1, LANES), 1)
        chosen = (lane // EXPERTS_PER_GROUP == group) & (rank < 2)
        w = jnp.where(chosen, probs, 0.0)
        gates = w / jnp.sum(w, axis=-1, keepdims=True)
        hb = _bf(h)
        acc = jnp.zeros(ys_ref.shape, F32)
        for e in range(EXPERTS_PER_GROUP):
            ge = jnp.sum(jnp.where(lane == group * EXPERTS_PER_GROUP + e, gates, 0.0),
                         axis=-1, keepdims=True)
            a = _silu(_dot(hb, w1_ref[e])) * _dot(hb, w3_ref[e])
            acc = acc + _dot(_bf(a * ge), w2_ref[e])
        ys_ref[...] = acc


def _moe(tile_group, n_tiles, hs, rw_pad, rb_pad, w1, w3, w2):
    rows, d = hs.shape
    tile = pl.BlockSpec((MOE_TILE, d), lambda i, tg, nt: (i, 0))
    full = lambda a: pl.BlockSpec(a.shape, lambda i, tg, nt: (0,) * a.ndim)
    up = pl.BlockSpec((EXPERTS_PER_GROUP, d, D_EXPERT), lambda i, tg, nt: (tg[i], 0, 0))
    down = pl.BlockSpec((EXPERTS_PER_GROUP, D_EXPERT, d), lambda i, tg, nt: (tg[i], 0, 0))
    return pl.pallas_call(
        _moe_kernel,
        grid_spec=pltpu.PrefetchScalarGridSpec(
            num_scalar_prefetch=2,
            grid=(rows // MOE_TILE,),
            in_specs=[tile, full(rw_pad), full(rb_pad), up, up, down],
            out_specs=tile,
        ),
        out_shape=jax.ShapeDtypeStruct((rows, d), F32),
        compiler_params=pltpu.CompilerParams(
            dimension_semantics=("arbitrary",), vmem_limit_bytes=VMEM_LIMIT),
        name="moe",
    )(tile_group, n_tiles, hs, rw_pad, rb_pad, w1, w3, w2)


def _tile_table(counts, n_tiles_max):
    per_group = (counts + MOE_TILE - 1) // MOE_TILE
    ends = jnp.cumsum(per_group)
    starts = ends - per_group
    total = ends[-1]
    i = jnp.arange(n_tiles_max, dtype=jnp.int32)
    group = jnp.sum((jnp.minimum(i, total - 1)[:, None] >= ends[None, :]).astype(jnp.int32), axis=1)
    spare = i[n_tiles_max - N_GROUPS:]
    zero_tiles = jnp.concatenate([jnp.where(per_group > 0, ends - 1, -1), jnp.where(spare >= total, spare, -1)])
    as_i32 = lambda a: a.astype(jnp.int32)
    return as_i32(group), as_i32(total.reshape(1)), as_i32(starts * MOE_TILE), as_i32(zero_tiles)


def _combine_kernel(alpha, codes_ref, row_start_ref, ys_hbm, x1_ref, mod_ref, lng_ref, lnb_ref,
                    o_ref, buf, sem):
    tm = o_ref.shape[0]
    i = pl.program_id(0)
    n = pl.num_programs(0)

    def gather(tile, b):
        def body(w, carry):
            for r in range(8):
                row = w * 8 + r
                s = _slot(codes_ref[tile * tm + row], row_start_ref)
                pltpu.make_async_copy(ys_hbm.at[pl.ds(s, 1)], buf.at[b, pl.ds(row, 1)], sem.at[b]).start()
            return carry
        lax.fori_loop(0, tm // 8, body, 0)

    @pl.when(i == 0)
    def _():
        gather(0, 0)

    @pl.when(i + 1 < n)
    def _():
        gather(i + 1, (i + 1) % 2)

    cur = i % 2
    pltpu.make_async_copy(ys_hbm.at[pl.ds(0, tm)], buf.at[cur], sem.at[cur]).wait()
    g2 = mod_ref[0, 5:6, :]
    o_ref[...] = _layer_norm(alpha * x1_ref[...] + g2 * buf[cur], lng_ref[...], lnb_ref[...])


def _combine(codes, row_start, ys, x1, modl, ln_g, ln_b, seq, alpha):
    t, d = x1.shape
    tm = 256
    per_batch = seq // tm
    row = lambda a: a.reshape(1, -1)
    tile = pl.BlockSpec((tm, d), lambda i, s, r: (i, 0))
    vec = pl.BlockSpec((1, d), lambda i, s, r: (0, 0))
    return pl.pallas_call(
        functools.partial(_combine_kernel, alpha),
        grid_spec=pltpu.PrefetchScalarGridSpec(
            num_scalar_prefetch=2,
            grid=(t // tm,),
            in_specs=[pl.BlockSpec(memory_space=pl.ANY), tile,
                      pl.BlockSpec((1, 6, d), lambda i, s, r: (i // per_batch, 0, 0)), vec, vec],
            out_specs=tile,
            scratch_shapes=[pltpu.VMEM((2, tm, d), F32), pltpu.SemaphoreType.DMA((2,))],
        ),
        out_shape=jax.ShapeDtypeStruct((t, d), F32),
        compiler_params=pltpu.CompilerParams(
            dimension_semantics=("arbitrary",), vmem_limit_bytes=VMEM_LIMIT),
        name="combine",
    )(codes, row_start, ys, x1, modl, row(ln_g), row(ln_b))


def _regroup_w_in(w):
    a = 8 * GW
    b = a + 2 * N_HEADS
    cc = b + 4 * GW
    dd = cc + GLA_RANK
    pad = jnp.zeros((w.shape[0], LANES - 2 * N_HEADS - GLA_RANK), w.dtype)
    return jnp.concatenate([w[:, :a], w[:, b:cc], w[:, dd:], w[:, a:b], w[:, cc:dd], pad], axis=1)


def _rotary_tables(seq):
    inv = ROPE_BASE ** (-jnp.arange(0, HEAD_DIM, 2, dtype=F32) / HEAD_DIM)
    ang = jnp.arange(seq, dtype=F32)[:, None] * inv[None, :]
    cos = jnp.cos(ang)
    sin = jnp.sin(ang)
    cosf = jnp.tile(jnp.concatenate([cos, cos], axis=1), (1, N_HEADS))
    sins = jnp.tile(jnp.concatenate([-sin, sin], axis=1), (1, N_HEADS))
    return cosf, sins


def kernel(x, c, ada_w, ada_b, w_in, mlstm_conv, mlstm_gate_b, gla_w2, gla_b2, hgrn_lb, ret_norm, mlstm_norm, gla_norm, hgrn_norm, w_out, ln1_g, ln1_b, router_w, router_b, exp_w1, exp_w3, exp_w2, ln2_g, ln2_b):
    bsz, seq, d = x.shape
    depth = ada_w.shape[0]
    assert d == D_MODEL and seq % 512 == 0 and bsz <= 16
    t = bsz * seq
    alpha = (2.0 * depth) ** 0.25
    assert t % MOE_TILE == 0 and t < (1 << POS_BITS)
    n_tiles_max = t // MOE_TILE + N_GROUPS
    consts = _mixer_consts()
    cosf, sins = _rotary_tables(seq)
    c_rows = jnp.pad(c, ((0, 16 - bsz), (0, 0)))
    mod = _adaln(c_rows, ada_w, ada_b)[:, :bsz]
    rw_pad = jnp.pad(router_w, ((0, 0), (0, LANES - N_EXPERTS)))
    rb_pad = jnp.pad(router_b, (0, LANES - N_EXPERTS)).reshape(1, LANES)
    xt = x.reshape(t, d)
    for l in range(depth):
        modl = mod[l].reshape(bsz, 6, d)
        proj = _proj(xt, modl, _bf(_regroup_w_in(w_in[l])), seq)
        gate_rows = jnp.broadcast_to(mlstm_gate_b[l][:, None], (2 * N_HEADS, CHUNK))
        w2_pad = jnp.zeros((LANES, GW), F32).at[2 * N_HEADS:2 * N_HEADS + GLA_RANK].set(gla_w2[l])
        y = _mixer(proj, cosf, sins, mlstm_conv[l], gate_rows, w2_pad, gla_b2[l], hgrn_lb,
                   ret_norm[l], mlstm_norm[l], gla_norm[l], hgrn_norm[l], consts, l, bsz, seq)
        x1, h2, codes, counts = _post(y, xt, modl, _bf(w_out[l]), ln1_g[l], ln1_b[l], rw_pad, rb_pad,
                                      seq, alpha)
        codes = codes.reshape(t)
        tile_group, n_tiles, row_start, zero_tiles = _tile_table(counts[0, :N_GROUPS], n_tiles_max)
        hs = _dispatch(codes, row_start, zero_tiles, h2, n_tiles_max)
        ys = _moe(tile_group, n_tiles, hs, rw_pad, rb_pad, _bf(exp_w1[l]), _bf(exp_w3[l]), _bf(exp_w2[l]))
        xt = _combine(codes, row_start, ys, x1, modl, ln2_g[l], ln2_b[l], seq, alpha)
    return xt.reshape(bsz, seq, d)
```

```python
import functools

import numpy as np
import jax
import jax.numpy as jnp
from jax import lax
from jax.experimental import pallas as pl
from jax.experimental.pallas import tpu as pltpu

F32 = jnp.float32
BF16 = jnp.bfloat16

D_MODEL = 1024
N_MIXERS = 4
GW = D_MODEL // N_MIXERS
N_HEADS = 4
HEAD_DIM = GW // N_HEADS
CONV_WIDTH = 4
GLA_RANK = 16
GLA_TAU = 16.0
ROPE_BASE = 10000.0
N_GROUPS = 4
EXPERTS_PER_GROUP = 4
N_EXPERTS = 16
D_EXPERT = D_MODEL // 2
LN_EPS = 1e-5
NORM_EPS = 1e-6
QK_SCALE = HEAD_DIM ** -0.5

LANES = 128
CHUNK = 128
LEVELS = 7
CONV_PAD = 8
NPROJ = 16 * GW + LANES
EXTRA = 16 * GW
VMEM_LIMIT = 56 * 1024 * 1024

(R_Q, R_K, R_V, R_G, M_Q, M_K, M_V, M_O, G_Q, G_K, G_V, G_G, H_Q, H_F, H_I, H_G) = (
    GW * i for i in range(16))


def _bf(x):
    return x.astype(BF16)


def _dot(a, b):
    return jnp.dot(a, b, preferred_element_type=F32)


def _dot_nt(a, b):
    return lax.dot_general(a, b, (((1,), (1,)), ((), ())), preferred_element_type=F32)


def _split2(x):
    hi = _bf(x)
    lo = _bf(x - hi.astype(F32))
    return hi, lo


def _split3(x):
    hi = _bf(x)
    r = x - hi.astype(F32)
    mid = _bf(r)
    lo = _bf(r - mid.astype(F32))
    return hi, mid, lo


def _dot_f32(a, b):
    ah, al = _split2(a)
    bh, bl = _split2(b)
    return _dot(ah, bh) + _dot(ah, bl) + _dot(al, bh)


def _sigmoid(x):
    return 0.5 * jnp.tanh(0.5 * x) + 0.5


def _silu(x):
    return x * _sigmoid(x)


def _log_sigmoid(x):
    return jnp.minimum(x, 0.0) - jnp.log1p(jnp.exp(-jnp.abs(x)))


def _layer_norm(v, g, b):
    mu = jnp.mean(v, axis=-1, keepdims=True)
    d = v - mu
    var = jnp.mean(d * d, axis=-1, keepdims=True)
    return d * lax.rsqrt(var + LN_EPS) * g + b


def _adaln_kernel(c_ref, w_ref, b_ref, o_ref):
    cond = _silu(c_ref[...])
    o_ref[0] = _dot_f32(cond, w_ref[0]) + b_ref[0]


def _adaln(c, ada_w, ada_b):
    depth, d, n = ada_w.shape
    bsz = c.shape[0]
    tn = 1536
    return pl.pallas_call(
        _adaln_kernel,
        grid=(depth, n // tn),
        in_specs=[
            pl.BlockSpec((bsz, d), lambda l, j: (0, 0)),
            pl.BlockSpec((1, d, tn), lambda l, j: (l, 0, j)),
            pl.BlockSpec((1, 1, tn), lambda l, j: (l, 0, j)),
        ],
        out_specs=pl.BlockSpec((1, bsz, tn), lambda l, j: (l, 0, j)),
        out_shape=jax.ShapeDtypeStruct((depth, bsz, n), F32),
        compiler_params=pltpu.CompilerParams(
            dimension_semantics=("arbitrary", "arbitrary"), vmem_limit_bytes=VMEM_LIMIT),
        name="adaln",
    )(c, ada_w, ada_b.reshape(depth, 1, n))


def _proj_kernel(x_ref, mod_ref, w_ref, o_ref):
    sh = mod_ref[0, 0:1, :]
    sc = mod_ref[0, 1:2, :]
    h = x_ref[...] * (1.0 + sc) + sh
    o_ref[...] = _dot(_bf(h), w_ref[...])


def _proj(xt, modl, w_in_p, seq):
    t, d = xt.shape
    tm = 256
    per_batch = seq // tm
    return pl.pallas_call(
        _proj_kernel,
        grid=(t // tm,),
        in_specs=[
            pl.BlockSpec((tm, d), lambda i: (i, 0)),
            pl.BlockSpec((1, 6, d), lambda i: (i // per_batch, 0, 0)),
            pl.BlockSpec((d, NPROJ), lambda i: (0, 0)),
        ],
        out_specs=pl.BlockSpec((tm, NPROJ), lambda i: (i, 0)),
        out_shape=jax.ShapeDtypeStruct((t, NPROJ), F32),
        compiler_params=pltpu.CompilerParams(
            dimension_semantics=("arbitrary",), vmem_limit_bytes=VMEM_LIMIT),
        name="proj",
    )(xt, modl, w_in_p)


def _mixer_consts():
    c = CHUNK
    i = np.arange(c)[:, None]
    j = np.arange(c)[None, :]
    rel = i - j
    log_gamma = np.log1p(-np.exp2(-5.0 - np.arange(N_HEADS, dtype=np.float64)))
    dret = np.concatenate(
        [np.where(rel >= 0, np.exp(log_gamma[h] * np.maximum(rel, 0)), 0.0) for h in range(N_HEADS)], axis=1)
    lane_head = np.arange(GW) // HEAD_DIM
    dq = np.exp(log_gamma[lane_head][None, :] * (np.arange(c)[:, None] + 1.0))
    dk = np.exp(log_gamma[lane_head][None, :] * (c - 1.0 - np.arange(c)[:, None]))
    gc = np.exp(log_gamma[lane_head] * c)[None, :]
    lv = np.zeros((c, c), np.int32)
    lv[rel == 0] = 1
    spans = [(rel >= 0), (rel < 0)]
    code, b = 2, 1
    while b < c:
        lv[(i // (2 * b) == j // (2 * b)) & ((i // b) % 2 == 1) & ((j // b) % 2 == 0)] = code
        ref = (i // (2 * b)) * 2 * b + b - 1
        right = (i // b) % 2 == 1
        spans.append(np.where(right, (j > ref) & (j <= i), (j > i) & (j <= ref)))
        code, b = code + 1, b * 2
    lv = np.tile(lv, (1, N_HEADS))
    span = np.concatenate(spans, axis=0).astype(np.float32)
    span = np.tile(span, (1, 3))
    hh = (lane_head[:, None] == lane_head[None, :]).astype(np.float32)
    ones_bd = np.repeat(np.eye(N_HEADS, dtype=np.float32), c, axis=0)
    ones_bd = np.repeat(ones_bd, HEAD_DIM, axis=1)
    return dict(
        dret=jnp.asarray(dret, F32), dq=jnp.asarray(dq, F32), dk=jnp.asarray(dk, F32),
        gc=jnp.asarray(gc, F32), lv=jnp.asarray(lv), hh=jnp.asarray(hh, BF16),
        bdf=jnp.asarray(hh, F32), ones_bd=jnp.asarray(ones_bd, BF16), span=jnp.asarray(span, BF16))


def _lane_cumsum(x):
    lane = lax.broadcasted_iota(jnp.int32, x.shape, 1)
    s = 1
    while s < x.shape[1]:
        x = x + jnp.where(lane >= s, pltpu.roll(x, s, 1), 0.0)
        s *= 2
    return x


def _lane_cummax(x):
    lane = lax.broadcasted_iota(jnp.int32, x.shape, 1)
    s = 1
    while s < x.shape[1]:
        x = jnp.maximum(x, jnp.where(lane >= s, pltpu.roll(x, s, 1), -jnp.inf))
        s *= 2
    return x


def _mixer_kernel(layer,
                  proj_ref, cos_ref, sin_ref, conv_ref, gb_ref, w2_ref, b2_ref, lb_ref,
                  retg_ref, mlg_ref, glag_ref, hgg_ref,
                  dret_ref, dq_ref, dk_ref, gc_ref, lv_ref, hh_ref, bdf_ref, onesbd_ref, span_ref,
                  y_ref,
                  s_ret, s_cn, s_gla, s_hg, m_ml, conv_buf):
    c = CHUNK
    n = pl.program_id(1)

    @pl.when(n == 0)
    def _():
        s_ret[...] = jnp.zeros_like(s_ret)
        s_cn[...] = jnp.zeros_like(s_cn)
        s_gla[...] = jnp.zeros_like(s_gla)
        s_hg[...] = jnp.zeros_like(s_hg)
        m_ml[...] = jnp.zeros_like(m_ml)
        conv_buf[0:CONV_PAD, :] = jnp.zeros((CONV_PAD, 2 * GW), F32)

    lane_head = lax.broadcasted_iota(jnp.int32, (1, GW), 1) // HEAD_DIM
    head_rows = [(lane_head == h).astype(BF16) for h in range(N_HEADS)]
    hh = hh_ref[...]
    bdf = bdf_ref[...]
    lv = lv_ref[...]
    causal = lv >= 1

    def col(off, width=GW):
        return proj_ref[:, off:off + width]

    def stackmask(xb):
        return jnp.concatenate([xb * head_rows[h] for h in range(N_HEADS)], axis=0)

    def expand_heads(cols, first):
        out = jnp.zeros((cols.shape[0], GW), F32)
        for h in range(N_HEADS):
            out = jnp.where(lane_head == h, cols[:, first + h:first + h + 1], out)
        return out

    def head_mean(v):
        hi, lo = _split2(v)
        return (_dot(hi, hh) + _dot(lo, hh)) * (1.0 / HEAD_DIM)

    def head_norm(v, gain, center):
        if center:
            v = v - head_mean(v)
        return v * lax.rsqrt(head_mean(v * v) + NORM_EPS) * gain

    cosf = cos_ref[...]
    sins = sin_ref[...]
    half = lax.broadcasted_iota(jnp.int32, (1, GW), 1) % HEAD_DIM < HEAD_DIM // 2

    def rotary(v):
        swapped = jnp.where(half, pltpu.roll(v, GW - HEAD_DIM // 2, 1), pltpu.roll(v, HEAD_DIM // 2, 1))
        return v * cosf + swapped * sins

    rq = _bf(rotary(col(R_Q)) * QK_SCALE)
    rk = rotary(col(R_K))
    rv = col(R_V)
    sc = _dot_nt(rq, stackmask(_bf(rk))) * dret_ref[...]
    ro = _dot(_bf(sc), stackmask(_bf(rv))) + _dot(rq, _bf(s_ret[...])) * dq_ref[...]
    s_ret[...] = s_ret[...] * gc_ref[...] + _dot(_bf((rk * dk_ref[...]).T), _bf(rv)) * bdf
    y_ref[:, 0:GW] = head_norm(ro, retg_ref[...], True) * _silu(col(R_G))

    conv_buf[CONV_PAD:CONV_PAD + c, :] = col(M_Q, 2 * GW)
    qk = jnp.zeros((c, 2 * GW), F32)
    for t in range(CONV_WIDTH):
        s = CONV_WIDTH - 1 - t
        qk = qk + conv_buf[CONV_PAD - s:CONV_PAD - s + c, :] * conv_ref[t:t + 1, :]
    conv_buf[0:CONV_PAD, :] = conv_buf[c:c + CONV_PAD, :]
    qk = _silu(qk)
    mq = _bf(qk[:, 0:GW])
    mk = qk[:, GW:2 * GW] * QK_SCALE
    mv = col(M_V)

    ext = col(EXTRA, LANES)
    pre = ext.T[0:8, :] + gb_ref[...]
    row8 = lax.broadcasted_iota(jnp.int32, (8, c), 0)
    valid = row8 >= N_HEADS
    lf = jnp.where(valid, _log_sigmoid(pre), 0.0)
    gi = jnp.where(valid, pltpu.roll(pre, N_HEADS, 0), 0.0)
    bcum = _lane_cumsum(lf)
    a = gi - bcum
    m_prev = m_ml[...]
    mrow = jnp.maximum(m_prev, _lane_cummax(a))
    m_last = jnp.broadcast_to(mrow[:, c - 1:c], (8, c))
    b_last = jnp.broadcast_to(bcum[:, c - 1:c], (8, c))
    wi = jnp.exp(m_prev - mrow)
    em = jnp.exp(-(bcum + mrow))
    wk = jnp.exp(a - m_last)
    dec = jnp.exp(m_prev - m_last)
    m_ml[...] = jnp.where(valid, b_last + m_last, 0.0)
    cols = jnp.concatenate([mrow, wi, em, wk, jnp.zeros((c - 32, c), F32)], axis=0).T
    mexp = jnp.concatenate(
        [jnp.broadcast_to(cols[:, 4 + h:5 + h], (c, c)) for h in range(N_HEADS)], axis=1)
    a_row = jnp.concatenate([a[4 + h:5 + h, :] for h in range(N_HEADS)], axis=1)
    wmat = jnp.exp(jnp.where(causal, a_row - mexp, -jnp.inf))
    sm = _bf(_dot_nt(mq, stackmask(_bf(mk))) * wmat)
    v_ext = jnp.concatenate([stackmask(_bf(mv)), onesbd_ref[...]], axis=1)
    wi_l = expand_heads(cols, 12)
    nd = _dot(sm, v_ext) + jnp.concatenate([wi_l, wi_l], axis=1) * _dot(mq, _bf(s_cn[...]))
    mh = nd[:, 0:GW] / jnp.maximum(jnp.abs(nd[:, GW:2 * GW]), expand_heads(cols, 20))
    kt = mk * expand_heads(cols, 28)
    dec2 = jnp.concatenate([dec, dec], axis=1)
    dec_l = jnp.zeros((1, GW), F32)
    for h in range(N_HEADS):
        dec_l = jnp.where(lane_head == h, dec2[4 + h:5 + h, :], dec_l)
    upd = _dot(_bf(kt.T), jnp.concatenate([_bf(mv), jnp.ones((c, GW), BF16)], axis=1))
    s_cn[...] = (s_cn[...] * jnp.concatenate([dec_l, dec_l], axis=1)
                 + upd * jnp.concatenate([bdf, bdf], axis=1))
    y_ref[:, GW:2 * GW] = head_norm(mh * _sigmoid(col(M_O)), mlg_ref[...], True)

    span = span_ref[...]
    level_masks = [lv == code for code in range(2, 2 + LEVELS)]

    def decay_attention(q, k, v, g, st_ref):
        dec = jnp.exp(_dot(span, jnp.concatenate(_split3(g), axis=0)))
        e_cum = dec[0:c]
        e_end = dec[c:2 * c]
        out = _dot_nt(_bf(q * e_cum), _bf(st_ref[...]))
        out = out + _dot(_bf(q * k), hh) * v
        scores = jnp.zeros((c, N_HEADS * c), F32)
        for lvl in range(LEVELS):
            e = dec[(2 + lvl) * c:(3 + lvl) * c]
            scores = jnp.where(level_masks[lvl], _dot_nt(_bf(q * e), stackmask(_bf(k * e))), scores)
        out = out + _dot(_bf(scores), stackmask(_bf(v)))
        st_ref[...] = st_ref[...] * e_cum[c - 1:c, :] + _dot(_bf(v.T), _bf(k * e_end)) * bdf
        return out

    x_lr = _dot_f32(ext, w2_ref[...]) + b2_ref[...]
    log_a = _log_sigmoid(x_lr) * (1.0 / GLA_TAU)
    go = decay_attention(col(G_Q), col(G_K) * QK_SCALE, col(G_V), log_a, s_gla)
    y_ref[:, 2 * GW:3 * GW] = head_norm(go, glag_ref[...], False) * _silu(col(G_G))

    lb_all = lb_ref[...]
    lb_e = jnp.exp(lb_all - jnp.max(lb_all, axis=0, keepdims=True))
    lb_p = lb_e / jnp.sum(lb_e, axis=0, keepdims=True)
    lb = jnp.zeros((1, GW), F32)
    for l in range(1, layer + 1):
        lb = lb + lb_p[l:l + 1, :]
    z = col(H_F)
    x1 = jnp.log(lb)
    x2 = jnp.log1p(-lb) + _log_sigmoid(z)
    log_f = jnp.maximum(x1, x2) + jnp.log1p(jnp.exp(-jnp.abs(x1 - x2)))
    k_h = (1.0 - lb) * _sigmoid(-z)
    ho = decay_attention(_silu(col(H_Q)), k_h, col(H_I), log_f, s_hg)
    y_ref[:, 3 * GW:4 * GW] = head_norm(ho * _sigmoid(col(H_G)), hgg_ref[...], False)


def _mixer(proj, cosf, sins, conv_w, gate_rows, w2_pad, b2, lb, ret_g, ml_g, gla_g, hg_g, consts,
           layer, bsz, seq):
    c = CHUNK
    nc = seq // c
    t = bsz * seq
    row = lambda a: a.reshape(1, -1)
    full = lambda a: pl.BlockSpec(a.shape, lambda b, n: (0,) * a.ndim)
    cs = consts
    args = [proj, cosf, sins, conv_w, gate_rows, w2_pad, row(b2), lb,
            row(ret_g), row(ml_g), row(gla_g), row(hg_g),
            cs["dret"], cs["dq"], cs["dk"], cs["gc"], cs["lv"], cs["hh"], cs["bdf"], cs["ones_bd"], cs["span"]]
    in_specs = [
        pl.BlockSpec((c, NPROJ), lambda b, n: (b * nc + n, 0)),
        pl.BlockSpec((c, GW), lambda b, n: (n, 0)),
        pl.BlockSpec((c, GW), lambda b, n: (n, 0)),
    ] + [full(a) for a in args[3:]]
    return pl.pallas_call(
        functools.partial(_mixer_kernel, layer),
        grid=(bsz, nc),
        in_specs=in_specs,
        out_specs=pl.BlockSpec((c, D_MODEL), lambda b, n: (b * nc + n, 0)),
        out_shape=jax.ShapeDtypeStruct((t, D_MODEL), F32),
        scratch_shapes=[
            pltpu.VMEM((GW, GW), F32),
            pltpu.VMEM((GW, 2 * GW), F32),
            pltpu.VMEM((GW, GW), F32),
            pltpu.VMEM((GW, GW), F32),
            pltpu.VMEM((8, c), F32),
            pltpu.VMEM((c + CONV_PAD, 2 * GW), F32),
        ],
        compiler_params=pltpu.CompilerParams(
            dimension_semantics=("arbitrary", "arbitrary"), vmem_limit_bytes=VMEM_LIMIT),
        name="mixer",
    )(*args)


def _group_shift(v, pos, k):
    return jnp.where(pos < EXPERTS_PER_GROUP - k,
                     pltpu.roll(v, LANES - k, 1), pltpu.roll(v, EXPERTS_PER_GROUP - k, 1))


def _router(h, rw, rb):
    lane = lax.broadcasted_iota(jnp.int32, (1, LANES), 1)
    real = lane < N_EXPERTS
    logits = jnp.where(real, _dot_f32(h, rw), -jnp.inf)
    ex = jnp.exp(logits - jnp.max(logits, axis=-1, keepdims=True))
    probs = ex / jnp.sum(ex, axis=-1, keepdims=True)
    sel = jnp.where(real, probs + rb, -jnp.inf)
    pos = lane % EXPERTS_PER_GROUP
    gid = lane // EXPERTS_PER_GROUP
    r1 = _group_shift(sel, pos, 1)
    r2 = _group_shift(sel, pos, 2)
    r3 = _group_shift(sel, pos, 3)
    pair = jnp.maximum(jnp.maximum(jnp.maximum(sel + r1, sel + r2), jnp.maximum(sel + r3, r1 + r2)),
                       jnp.maximum(r1 + r3, r2 + r3))
    pair = jnp.where(real, pair, -jnp.inf)
    best = jnp.max(pair, axis=-1, keepdims=True)
    first = jnp.min(jnp.where(pair == best, gid, N_GROUPS), axis=-1, keepdims=True)
    rank = jnp.zeros(sel.shape, jnp.int32)
    for k, r in ((1, r1), (2, r2), (3, r3)):
        beats = (r > sel) | ((r == sel) & (pos + k >= EXPERTS_PER_GROUP))
        rank = rank + beats.astype(jnp.int32)
    return probs, first, rank


def _post_kernel(alpha, y_ref, x_ref, mod_ref, wout_ref, lng_ref, lnb_ref, rw_ref, rb_ref,
                 stri_ref, x1_ref, h2_ref, slots_ref, counts_ref, carry_ref):
    tm = y_ref.shape[0]

    @pl.when(pl.program_id(0) == 0)
    def _():
        carry_ref[...] = jnp.zeros_like(carry_ref)

    g1 = mod_ref[0, 2:3, :]
    sh2 = mod_ref[0, 3:4, :]
    sc2 = mod_ref[0, 4:5, :]
    mix = _dot(_bf(y_ref[...]), wout_ref[...])
    x1 = _layer_norm(alpha * x_ref[...] + g1 * mix, lng_ref[...], lnb_ref[...])
    x1_ref[...] = x1
    h = x1 * (1.0 + sc2) + sh2
    h2_ref[...] = h

    _, first, _ = _router(h, rw_ref[...], rb_ref[...])
    lane = lax.broadcasted_iota(jnp.int32, (1, LANES), 1)
    onehot = jnp.where(lane == first, 1.0, 0.0)
    before = _dot(stri_ref[...], _bf(onehot)) + carry_ref[...]
    code = first.astype(F32) * float(1 << POS_BITS) + jnp.sum(onehot * before, axis=-1, keepdims=True)
    slots_ref[0] = jnp.broadcast_to(code, (tm, LANES)).T[0:1, :].astype(jnp.int32)
    carry_ref[...] += jnp.sum(onehot, axis=0, keepdims=True)
    counts_ref[...] = carry_ref[...].astype(jnp.int32)


def _post(y, xt, modl, w_out, ln_g, ln_b, rw_pad, rb_pad, seq, alpha):
    t, d = xt.shape
    tm = 256
    per_batch = seq // tm
    row = lambda a: a.reshape(1, -1)
    tile = pl.BlockSpec((tm, d), lambda i: (i, 0))
    full = lambda a: pl.BlockSpec(a.shape, lambda i: (0,) * a.ndim)
    stri = jnp.asarray(np.tril(np.ones((tm, tm), np.float32), -1), BF16)
    args = [y, xt, modl, w_out, row(ln_g), row(ln_b), rw_pad, rb_pad, stri]
    return pl.pallas_call(
        functools.partial(_post_kernel, alpha),
        grid=(t // tm,),
        in_specs=[tile, tile, pl.BlockSpec((1, 6, d), lambda i: (i // per_batch, 0, 0))]
        + [full(a) for a in args[3:]],
        out_specs=[tile, tile, pl.BlockSpec((1, 1, tm), lambda i: (i, 0, 0)),
                   pl.BlockSpec((1, LANES), lambda i: (0, 0))],
        out_shape=[jax.ShapeDtypeStruct((t, d), F32), jax.ShapeDtypeStruct((t, d), F32),
                   jax.ShapeDtypeStruct((t // tm, 1, tm), jnp.int32),
                   jax.ShapeDtypeStruct((1, LANES), jnp.int32)],
        scratch_shapes=[pltpu.VMEM((1, LANES), F32)],
        compiler_params=pltpu.CompilerParams(
            dimension_semantics=("arbitrary",), vmem_limit_bytes=VMEM_LIMIT),
        name="post",
    )(*args)


MOE_TILE = 512


POS_BITS = 15


def _slot(code, row_start_ref):
    return row_start_ref[code >> POS_BITS] + (code & ((1 << POS_BITS) - 1))


def _dispatch_kernel(codes_ref, row_start_ref, zero_tiles_ref, h_ref, hs_hbm, zero_buf, sem, zsem):
    tm = h_ref.shape[0]
    i = pl.program_id(0)

    @pl.when(i == 0)
    def _():
        zero_buf[...] = jnp.zeros_like(zero_buf)
        for k in range(zero_tiles_ref.shape[0]):
            blk = zero_tiles_ref[k]

            @pl.when(blk >= 0)
            def _():
                dst = hs_hbm.at[pl.ds(pl.multiple_of(blk * MOE_TILE, MOE_TILE), MOE_TILE)]
                pltpu.make_async_copy(zero_buf, dst, zsem).start()
                pltpu.make_async_copy(zero_buf, dst, zsem).wait()

    def issue(w, carry):
        for r in range(8):
            row = w * 8 + r
            s = _slot(codes_ref[i * tm + row], row_start_ref)
            pltpu.make_async_copy(h_ref.at[pl.ds(row, 1)], hs_hbm.at[pl.ds(s, 1)], sem).start()
        return carry

    lax.fori_loop(0, tm // 8, issue, 0)
    pltpu.make_async_copy(h_ref, hs_hbm.at[pl.ds(0, tm)], sem).wait()


def _dispatch(codes, row_start, zero_tiles, h2, n_tiles_max):
    t, d = h2.shape
    tm = MOE_TILE
    return pl.pallas_call(
        _dispatch_kernel,
        grid_spec=pltpu.PrefetchScalarGridSpec(
            num_scalar_prefetch=3,
            grid=(t // tm,),
            in_specs=[pl.BlockSpec((tm, d), lambda i, c, r, z: (i, 0))],
            out_specs=pl.BlockSpec(memory_space=pl.ANY),
            scratch_shapes=[pltpu.VMEM((MOE_TILE, d), F32), pltpu.SemaphoreType.DMA(()),
                            pltpu.SemaphoreType.DMA(())],
        ),
        out_shape=jax.ShapeDtypeStruct((n_tiles_max * MOE_TILE, d), F32),
        compiler_params=pltpu.CompilerParams(
            dimension_semantics=("arbitrary",), vmem_limit_bytes=VMEM_LIMIT),
        name="dispatch",
    )(codes, row_start, zero_tiles, h2)


def _moe_kernel(tile_group_ref, n_tiles_ref, hs_ref, rw_ref, rb_ref, w1_ref, w3_ref, w2_ref, ys_ref):
    i = pl.program_id(0)

    @pl.when(i >= n_tiles_ref[0])
    def _():
        ys_ref[...] = jnp.zeros_like(ys_ref)

    @pl.when(i < n_tiles_ref[0])
    def _():
        group = tile_group_ref[i]
        h = hs_ref[...]
        probs, _, rank = _router(h, rw_ref[...], rb_ref[...])
        lane = lax.broadcasted_iota(jnp.int32, (1, LANES), 1)
        chosen = (lane // EXPERTS_PER_GROUP == group) & (rank < 2)
        w = jnp.where(chosen, probs, 0.0)
        gates = w / jnp.sum(w, axis=-1, keepdims=True)
        hb = _bf(h)
        acc = jnp.zeros(ys_ref.shape, F32)
        for e in range(EXPERTS_PER_GROUP):
            ge = jnp.sum(jnp.where(lane == group * EXPERTS_PER_GROUP + e, gates, 0.0),
                         axis=-1, keepdims=True)
            a = _silu(_dot(hb, w1_ref[e])) * _dot(hb, w3_ref[e])
            acc = acc + _dot(_bf(a * ge), w2_ref[e])
        ys_ref[...] = acc


def _moe(tile_group, n_tiles, hs, rw_pad, rb_pad, w1, w3, w2):
    rows, d = hs.shape
    tile = pl.BlockSpec((MOE_TILE, d), lambda i, tg, nt: (i, 0))
    full = lambda a: pl.BlockSpec(a.shape, lambda i, tg, nt: (0,) * a.ndim)
    up = pl.BlockSpec((EXPERTS_PER_GROUP, d, D_EXPERT), lambda i, tg, nt: (tg[i], 0, 0))
    down = pl.BlockSpec((EXPERTS_PER_GROUP, D_EXPERT, d), lambda i, tg, nt: (tg[i], 0, 0))
    return pl.pallas_call(
        _moe_kernel,
        grid_spec=pltpu.PrefetchScalarGridSpec(
            num_scalar_prefetch=2,
            grid=(rows // MOE_TILE,),
            in_specs=[tile, full(rw_pad), full(rb_pad), up, up, down],
            out_specs=tile,
        ),
        out_shape=jax.ShapeDtypeStruct((rows, d), F32),
        compiler_params=pltpu.CompilerParams(
            dimension_semantics=("arbitrary",), vmem_limit_bytes=VMEM_LIMIT),
        name="moe",
    )(tile_group, n_tiles, hs, rw_pad, rb_pad, w1, w3, w2)


def _tile_table(counts, n_tiles_max):
    per_group = (counts + MOE_TILE - 1) // MOE_TILE
    ends = jnp.cumsum(per_group)
    starts = ends - per_group
    total = ends[-1]
    i = jnp.arange(n_tiles_max, dtype=jnp.int32)
    group = jnp.sum((jnp.minimum(i, total - 1)[:, None] >= ends[None, :]).astype(jnp.int32), axis=1)
    spare = i[n_tiles_max - N_GROUPS:]
    zero_tiles = jnp.concatenate([jnp.where(per_group > 0, ends - 1, -1), jnp.where(spare >= total, spare, -1)])
    as_i32 = lambda a: a.astype(jnp.int32)
    return as_i32(group), as_i32(total.reshape(1)), as_i32(starts * MOE_TILE), as_i32(zero_tiles)


def _combine_kernel(alpha, codes_ref, row_start_ref, ys_hbm, x1_ref, mod_ref, lng_ref, lnb_ref,
                    o_ref, buf, sem):
    tm = o_ref.shape[0]
    i = pl.program_id(0)
    n = pl.num_programs(0)

    def gather(tile, b):
        def body(w, carry):
            for r in range(8):
                row = w * 8 + r
                s = _slot(codes_ref[tile * tm + row], row_start_ref)
                pltpu.make_async_copy(ys_hbm.at[pl.ds(s, 1)], buf.at[b, pl.ds(row, 1)], sem.at[b]).start()
            return carry
        lax.fori_loop(0, tm // 8, body, 0)

    @pl.when(i == 0)
    def _():
        gather(0, 0)

    @pl.when(i + 1 < n)
    def _():
        gather(i + 1, (i + 1) % 2)

    cur = i % 2
    pltpu.make_async_copy(ys_hbm.at[pl.ds(0, tm)], buf.at[cur], sem.at[cur]).wait()
    g2 = mod_ref[0, 5:6, :]
    o_ref[...] = _layer_norm(alpha * x1_ref[...] + g2 * buf[cur], lng_ref[...], lnb_ref[...])


def _combine(codes, row_start, ys, x1, modl, ln_g, ln_b, seq, alpha):
    t, d = x1.shape
    tm = 256
    per_batch = seq // tm
    row = lambda a: a.reshape(1, -1)
    tile = pl.BlockSpec((tm, d), lambda i, s, r: (i, 0))
    vec = pl.BlockSpec((1, d), lambda i, s, r: (0, 0))
    return pl.pallas_call(
        functools.partial(_combine_kernel, alpha),
        grid_spec=pltpu.PrefetchScalarGridSpec(
            num_scalar_prefetch=2,
            grid=(t // tm,),
            in_specs=[pl.BlockSpec(memory_space=pl.ANY), tile,
                      pl.BlockSpec((1, 6, d), lambda i, s, r: (i // per_batch, 0, 0)), vec, vec],
            out_specs=tile,
            scratch_shapes=[pltpu.VMEM((2, tm, d), F32), pltpu.SemaphoreType.DMA((2,))],
        ),
        out_shape=jax.ShapeDtypeStruct((t, d), F32),
        compiler_params=pltpu.CompilerParams(
            dimension_semantics=("arbitrary",), vmem_limit_bytes=VMEM_LIMIT),
        name="combine",
    )(codes, row_start, ys, x1, modl, row(ln_g), row(ln_b))


def _regroup_w_in(w):
    a = 8 * GW
    b = a + 2 * N_HEADS
    cc = b + 4 * GW
    dd = cc + GLA_RANK
    pad = jnp.zeros((w.shape[0], LANES - 2 * N_HEADS - GLA_RANK), w.dtype)
    return jnp.concatenate([w[:, :a], w[:, b:cc], w[:, dd:], w[:, a:b], w[:, cc:dd], pad], axis=1)


def _rotary_tables(seq):
    inv = ROPE_BASE ** (-jnp.arange(0, HEAD_DIM, 2, dtype=F32) / HEAD_DIM)
    ang = jnp.arange(seq, dtype=F32)[:, None] * inv[None, :]
    cos = jnp.cos(ang)
    sin = jnp.sin(ang)
    cosf = jnp.tile(jnp.concatenate([cos, cos], axis=1), (1, N_HEADS))
    sins = jnp.tile(jnp.concatenate([-sin, sin], axis=1), (1, N_HEADS))
    return cosf, sins


def kernel(x, c, ada_w, ada_b, w_in, mlstm_conv, mlstm_gate_b, gla_w2, gla_b2, hgrn_lb, ret_norm, mlstm_norm, gla_norm, hgrn_norm, w_out, ln1_g, ln1_b, router_w, router_b, exp_w1, exp_w3, exp_w2, ln2_g, ln2_b):
    bsz, seq, d = x.shape
    depth = ada_w.shape[0]
    assert d == D_MODEL and seq % 512 == 0 and bsz <= 16
    t = bsz * seq
    alpha = (2.0 * depth) ** 0.25
    assert t % MOE_TILE == 0 and t < (1 << POS_BITS)
    n_tiles_max = t // MOE_TILE + N_GROUPS
    consts = _mixer_consts()
    cosf, sins = _rotary_tables(seq)
    c_rows = jnp.pad(c, ((0, 16 - bsz), (0, 0)))
    mod = _adaln(c_rows, ada_w, ada_b)[:, :bsz]
    rw_pad = jnp.pad(router_w, ((0, 0), (0, LANES - N_EXPERTS)))
    rb_pad = jnp.pad(router_b, (0, LANES - N_EXPERTS)).reshape(1, LANES)
    xt = x.reshape(t, d)
    for l in range(depth):
        modl = mod[l].reshape(bsz, 6, d)
        proj = _proj(xt, modl, _bf(_regroup_w_in(w_in[l])), seq)
        gate_rows = jnp.broadcast_to(mlstm_gate_b[l][:, None], (2 * N_HEADS, CHUNK))
        w2_pad = jnp.zeros((LANES, GW), F32).at[2 * N_HEADS:2 * N_HEADS + GLA_RANK].set(gla_w2[l])
        y = _mixer(proj, cosf, sins, mlstm_conv[l], gate_rows, w2_pad, gla_b2[l], hgrn_lb,
                   ret_norm[l], mlstm_norm[l], gla_norm[l], hgrn_norm[l], consts, l, bsz, seq)
        x1, h2, codes, counts = _post(y, xt, modl, _bf(w_out[l]), ln1_g[l], ln1_b[l], rw_pad, rb_pad,
                                      seq, alpha)
        codes = codes.reshape(t)
        tile_group, n_tiles, row_start, zero_tiles = _tile_table(counts[0, :N_GROUPS], n_tiles_max)
        hs = _dispatch(codes, row_start, zero_tiles, h2, n_tiles_max)
        ys = _moe(tile_group, n_tiles, hs, rw_pad, rb_pad, _bf(exp_w1[l]), _bf(exp_w3[l]), _bf(exp_w2[l]))
        xt = _combine(codes, row_start, ys, x1, modl, ln2_g[l], ln2_b[l], seq, alpha)
    return xt.reshape(bsz, seq, d)
```

```python
import functools

import numpy as np
import jax
import jax.numpy as jnp
from jax import lax
from jax.experimental import pallas as pl
from jax.experimental.pallas import tpu as pltpu

F32 = jnp.float32
BF16 = jnp.bfloat16

D_MODEL = 1024
N_MIXERS = 4
GW = D_MODEL // N_MIXERS
N_HEADS = 4
HEAD_DIM = GW // N_HEADS
CONV_WIDTH = 4
GLA_RANK = 16
GLA_TAU = 16.0
ROPE_BASE = 10000.0
N_GROUPS = 4
EXPERTS_PER_GROUP = 4
N_EXPERTS = 16
D_EXPERT = D_MODEL // 2
LN_EPS = 1e-5
NORM_EPS = 1e-6
QK_SCALE = HEAD_DIM ** -0.5

LANES = 128
CHUNK = 128
LEVELS = 7
CONV_PAD = 8
NPROJ = 16 * GW + LANES
EXTRA = 16 * GW
VMEM_LIMIT = 56 * 1024 * 1024

(R_Q, R_K, R_V, R_G, M_Q, M_K, M_V, M_O, G_Q, G_K, G_V, G_G, H_Q, H_F, H_I, H_G) = (
    GW * i for i in range(16))


def _bf(x):
    return x.astype(BF16)


def _dot(a, b):
    return jnp.dot(a, b, preferred_element_type=F32)


def _dot_nt(a, b):
    return lax.dot_general(a, b, (((1,), (1,)), ((), ())), preferred_element_type=F32)


def _split2(x):
    hi = _bf(x)
    lo = _bf(x - hi.astype(F32))
    return hi, lo


def _split3(x):
    hi = _bf(x)
    r = x - hi.astype(F32)
    mid = _bf(r)
    lo = _bf(r - mid.astype(F32))
    return hi, mid, lo


def _dot_f32(a, b):
    ah, al = _split2(a)
    bh, bl = _split2(b)
    return _dot(ah, bh) + _dot(ah, bl) + _dot(al, bh)


def _sigmoid(x):
    return 0.5 * jnp.tanh(0.5 * x) + 0.5


def _silu(x):
    return x * _sigmoid(x)


def _log_sigmoid(x):
    return jnp.minimum(x, 0.0) - jnp.log1p(jnp.exp(-jnp.abs(x)))


def _layer_norm(v, g, b):
    mu = jnp.mean(v, axis=-1, keepdims=True)
    d = v - mu
    var = jnp.mean(d * d, axis=-1, keepdims=True)
    return d * lax.rsqrt(var + LN_EPS) * g + b


def _adaln_kernel(c_ref, w_ref, b_ref, o_ref):
    cond = _silu(c_ref[...])
    o_ref[0] = _dot_f32(cond, w_ref[0]) + b_ref[0]


def _adaln(c, ada_w, ada_b):
    depth, d, n = ada_w.shape
    bsz = c.shape[0]
    tn = 1536
    return pl.pallas_call(
        _adaln_kernel,
        grid=(depth, n // tn),
        in_specs=[
            pl.BlockSpec((bsz, d), lambda l, j: (0, 0)),
            pl.BlockSpec((1, d, tn), lambda l, j: (l, 0, j)),
            pl.BlockSpec((1, 1, tn), lambda l, j: (l, 0, j)),
        ],
        out_specs=pl.BlockSpec((1, bsz, tn), lambda l, j: (l, 0, j)),
        out_shape=jax.ShapeDtypeStruct((depth, bsz, n), F32),
        compiler_params=pltpu.CompilerParams(
            dimension_semantics=("arbitrary", "arbitrary"), vmem_limit_bytes=VMEM_LIMIT),
        name="adaln",
    )(c, ada_w, ada_b.reshape(depth, 1, n))


def _proj_kernel(x_ref, mod_ref, w_ref, o_ref):
    sh = mod_ref[0, 0:1, :]
    sc = mod_ref[0, 1:2, :]
    h = x_ref[...] * (1.0 + sc) + sh
    o_ref[...] = _dot(_bf(h), w_ref[...])


def _proj(xt, modl, w_in_p, seq):
    t, d = xt.shape
    tm = 256
    per_batch = seq // tm
    return pl.pallas_call(
        _proj_kernel,
        grid=(t // tm,),
        in_specs=[
            pl.BlockSpec((tm, d), lambda i: (i, 0)),
            pl.BlockSpec((1, 6, d), lambda i: (i // per_batch, 0, 0)),
            pl.BlockSpec((d, NPROJ), lambda i: (0, 0)),
        ],
        out_specs=pl.BlockSpec((tm, NPROJ), lambda i: (i, 0)),
        out_shape=jax.ShapeDtypeStruct((t, NPROJ), F32),
        compiler_params=pltpu.CompilerParams(
            dimension_semantics=("arbitrary",), vmem_limit_bytes=VMEM_LIMIT),
        name="proj",
    )(xt, modl, w_in_p)


def _mixer_consts():
    c = CHUNK
    i = np.arange(c)[:, None]
    j = np.arange(c)[None, :]
    rel = i - j
    log_gamma = np.log1p(-np.exp2(-5.0 - np.arange(N_HEADS, dtype=np.float64)))
    dret = np.concatenate(
        [np.where(rel >= 0, np.exp(log_gamma[h] * np.maximum(rel, 0)), 0.0) for h in range(N_HEADS)], axis=1)
    lane_head = np.arange(GW) // HEAD_DIM
    dq = np.exp(log_gamma[lane_head][None, :] * (np.arange(c)[:, None] + 1.0))
    dk = np.exp(log_gamma[lane_head][None, :] * (c - 1.0 - np.arange(c)[:, None]))
    gc = np.exp(log_gamma[lane_head] * c)[None, :]
    lv = np.zeros((c, c), np.int32)
    lv[rel == 0] = 1
    spans = [(rel >= 0), (rel < 0)]
    code, b = 2, 1
    while b < c:
        lv[(i // (2 * b) == j // (2 * b)) & ((i // b) % 2 == 1) & ((j // b) % 2 == 0)] = code
        ref = (i // (2 * b)) * 2 * b + b - 1
        right = (i // b) % 2 == 1
        spans.append(np.where(right, (j > ref) & (j <= i), (j > i) & (j <= ref)))
        code, b = code + 1, b * 2
    lv = np.tile(lv, (1, N_HEADS))
    span = np.concatenate(spans, axis=0).astype(np.float32)
    span = np.tile(span, (1, 3))
    hh = (lane_head[:, None] == lane_head[None, :]).astype(np.float32)
    ones_bd = np.repeat(np.eye(N_HEADS, dtype=np.float32), c, axis=0)
    ones_bd = np.repeat(ones_bd, HEAD_DIM, axis=1)
    return dict(
        dret=jnp.asarray(dret, F32), dq=jnp.asarray(dq, F32), dk=jnp.asarray(dk, F32),
        gc=jnp.asarray(gc, F32), lv=jnp.asarray(lv), hh=jnp.asarray(hh, BF16),
        bdf=jnp.asarray(hh, F32), ones_bd=jnp.asarray(ones_bd, BF16), span=jnp.asarray(span, BF16))


def _lane_cumsum(x):
    lane = lax.broadcasted_iota(jnp.int32, x.shape, 1)
    s = 1
    while s < x.shape[1]:
        x = x + jnp.where(lane >= s, pltpu.roll(x, s, 1), 0.0)
        s *= 2
    return x


def _lane_cummax(x):
    lane = lax.broadcasted_iota(jnp.int32, x.shape, 1)
    s = 1
    while s < x.shape[1]:
        x = jnp.maximum(x, jnp.where(lane >= s, pltpu.roll(x, s, 1), -jnp.inf))
        s *= 2
    return x


def _mixer_kernel(layer, nb,
                  proj_ref, cos_ref, sin_ref, conv_ref, gb_ref, w2_ref, b2_ref, lb_ref,
                  retg_ref, mlg_ref, glag_ref, hgg_ref,
                  dret_ref, dq_ref, dk_ref, gc_ref, lv_ref, hh_ref, bdf_ref, onesbd_ref, span_ref,
                  y_ref,
                  s_ret, s_cn, s_gla, s_hg, m_ml, conv_buf):
    c = CHUNK
    batches = range(nb)

    @pl.when(pl.program_id(1) == 0)
    def _():
        for s in (s_ret, s_cn, s_gla, s_hg, m_ml):
            s[...] = jnp.zeros_like(s)
        conv_buf[:, 0:CONV_PAD, :] = jnp.zeros((nb, CONV_PAD, 2 * GW), F32)

    lane_head = lax.broadcasted_iota(jnp.int32, (1, GW), 1) // HEAD_DIM
    head_rows = [(lane_head == h).astype(BF16) for h in range(N_HEADS)]
    hh = hh_ref[...]
    bdf = bdf_ref[...]
    lv = lv_ref[...]
    causal = lv >= 1

    def stack(xs):
        return jnp.concatenate(xs, axis=0)

    def tiled(x):
        return stack([x] * nb)

    def seq(x, b):
        return x[b * c:(b + 1) * c]

    def col(off, width=GW):
        return stack([proj_ref[b, :, off:off + width] for b in batches])

    def put(off, val):
        for b in batches:
            y_ref[b, :, off:off + GW] = seq(val, b)

    def stackmask(xb):
        return jnp.concatenate([xb * head_rows[h] for h in range(N_HEADS)], axis=0)

    def expand_heads(cols, first):
        out = jnp.zeros((cols.shape[0], GW), F32)
        for h in range(N_HEADS):
            out = jnp.where(lane_head == h, cols[:, first + h:first + h + 1], out)
        return out

    def head_mean(v):
        hi, lo = _split2(v)
        return (_dot(hi, hh) + _dot(lo, hh)) * (1.0 / HEAD_DIM)

    def head_norm(v, gain, center):
        if center:
            v = v - head_mean(v)
        return v * lax.rsqrt(head_mean(v * v) + NORM_EPS) * gain

    cosf = tiled(cos_ref[...])
    sins = tiled(sin_ref[...])
    half = lax.broadcasted_iota(jnp.int32, (1, GW), 1) % HEAD_DIM < HEAD_DIM // 2

    def rotary(v):
        swapped = jnp.where(half, pltpu.roll(v, GW - HEAD_DIM // 2, 1), pltpu.roll(v, HEAD_DIM // 2, 1))
        return v * cosf + swapped * sins

    rq = _bf(rotary(col(R_Q)) * QK_SCALE)
    rk = rotary(col(R_K))
    rkb = _bf(rk)
    rvb = _bf(col(R_V))
    rkd = rk * tiled(dk_ref[...])
    dret = dret_ref[...]
    dq = dq_ref[...]
    gc = gc_ref[...]
    sc = [_dot_nt(seq(rq, b), stackmask(seq(rkb, b))) for b in batches]
    inter = [_dot(seq(rq, b), _bf(s_ret[b])) for b in batches]
    scb = [_bf(s * dret) for s in sc]
    ro = stack([_dot(scb[b], stackmask(seq(rvb, b))) + inter[b] * dq for b in batches])
    upd = [_dot(_bf(seq(rkd, b).T), seq(rvb, b)) for b in batches]
    for b in batches:
        s_ret[b] = s_ret[b] * gc + upd[b] * bdf
    put(0, head_norm(ro, retg_ref[...], True) * _silu(col(R_G)))

    for b in batches:
        conv_buf[b, CONV_PAD:CONV_PAD + c, :] = proj_ref[b, :, M_Q:M_Q + 2 * GW]
    qk = jnp.zeros((nb * c, 2 * GW), F32)
    for t in range(CONV_WIDTH):
        s = CONV_WIDTH - 1 - t
        qk = qk + stack([conv_buf[b, CONV_PAD - s:CONV_PAD - s + c, :] for b in batches]) * conv_ref[t:t + 1, :]
    for b in batches:
        conv_buf[b, 0:CONV_PAD, :] = conv_buf[b, c:c + CONV_PAD, :]
    qk = _silu(qk)
    mq = _bf(qk[:, 0:GW])
    mk = qk[:, GW:2 * GW] * QK_SCALE
    mkb = _bf(mk)
    mvb = _bf(col(M_V))

    ext = col(EXTRA, LANES)
    pre = stack([seq(ext, b).T[0:8, :] for b in batches]) + tiled(gb_ref[...])
    row8 = lax.broadcasted_iota(jnp.int32, (8 * nb, c), 0)
    valid = row8 % 8 >= N_HEADS
    lf = jnp.where(valid, _log_sigmoid(pre), 0.0)
    gi = jnp.where(valid, pltpu.roll(pre, N_HEADS, 0), 0.0)
    bcum = _lane_cumsum(lf)
    a = gi - bcum
    m_prev = m_ml[...]
    mrow = jnp.maximum(m_prev, _lane_cummax(a))
    m_last = jnp.broadcast_to(mrow[:, c - 1:c], (8 * nb, c))
    b_last = jnp.broadcast_to(bcum[:, c - 1:c], (8 * nb, c))
    wi = jnp.exp(m_prev - mrow)
    em = jnp.exp(-(bcum + mrow))
    wk = jnp.exp(a - m_last)
    dec = jnp.exp(m_prev - m_last)
    m_ml[...] = jnp.where(valid, b_last + m_last, 0.0)
    kinds = [mrow, wi, em, wk]
    if 32 * nb < c:
        kinds.append(jnp.zeros((c - 32 * nb, c), F32))
    cols = stack(kinds).T

    def first_col(kind, b):
        return kind * 8 * nb + 8 * b + N_HEADS

    mexp = [jnp.concatenate([jnp.broadcast_to(cols[:, first_col(0, b) + h:first_col(0, b) + h + 1], (c, c))
                             for h in range(N_HEADS)], axis=1) for b in batches]
    a_row = [jnp.concatenate([a[8 * b + N_HEADS + h:8 * b + N_HEADS + h + 1, :] for h in range(N_HEADS)], axis=1)
             for b in batches]
    qkm = [_dot_nt(seq(mq, b), stackmask(seq(mkb, b))) for b in batches]
    inter = [_dot(seq(mq, b), _bf(s_cn[b])) for b in batches]
    sm = [_bf(qkm[b] * jnp.exp(jnp.where(causal, a_row[b] - mexp[b], -jnp.inf))) for b in batches]
    onesbd = onesbd_ref[...]
    wi_l = stack([expand_heads(cols, first_col(1, b)) for b in batches])
    em_l = stack([expand_heads(cols, first_col(2, b)) for b in batches])
    wk_l = stack([expand_heads(cols, first_col(3, b)) for b in batches])
    nd = (stack([_dot(sm[b], jnp.concatenate([stackmask(seq(mvb, b)), onesbd], axis=1)) for b in batches])
          + jnp.concatenate([wi_l, wi_l], axis=1) * stack(inter))
    mh = nd[:, 0:GW] / jnp.maximum(jnp.abs(nd[:, GW:2 * GW]), em_l)
    kt = mk * wk_l
    ones = jnp.ones((c, GW), BF16)
    upd = [_dot(_bf(seq(kt, b).T), jnp.concatenate([seq(mvb, b), ones], axis=1)) for b in batches]
    dec2 = jnp.concatenate([dec, dec], axis=1)
    bdf2 = jnp.concatenate([bdf, bdf], axis=1)
    for b in batches:
        dec_l = jnp.zeros((1, GW), F32)
        for h in range(N_HEADS):
            dec_l = jnp.where(lane_head == h, dec2[8 * b + N_HEADS + h:8 * b + N_HEADS + h + 1, :], dec_l)
        s_cn[b] = s_cn[b] * jnp.concatenate([dec_l, dec_l], axis=1) + upd[b] * bdf2
    put(GW, head_norm(mh * _sigmoid(col(M_O)), mlg_ref[...], True))

    span = span_ref[...]
    level_masks = [lv == code for code in range(2, 2 + LEVELS)]

    def decay_attention(q, k, v, g, st_ref):
        g3 = _split3(g)
        dec = [jnp.exp(_dot(span, stack([seq(part, b) for part in g3]))) for b in batches]
        e_cum = stack([d[0:c] for d in dec])
        e_end = stack([d[c:2 * c] for d in dec])
        qe = _bf(q * e_cum)
        out = stack([_dot_nt(seq(qe, b), _bf(st_ref[b])) for b in batches])
        out = out + _dot(_bf(q * k), hh) * v
        scores = [jnp.zeros((c, N_HEADS * c), F32) for _ in batches]
        for lvl in range(LEVELS):
            e = stack([d[(2 + lvl) * c:(3 + lvl) * c] for d in dec])
            qb = _bf(q * e)
            kb = _bf(k * e)
            level = [_dot_nt(seq(qb, b), stackmask(seq(kb, b))) for b in batches]
            scores = [jnp.where(level_masks[lvl], level[b], scores[b]) for b in batches]
        vb = _bf(v)
        out = out + stack([_dot(_bf(scores[b]), stackmask(seq(vb, b))) for b in batches])
        ke = _bf(k * e_end)
        upd = [_dot(_bf(seq(v, b).T), seq(ke, b)) for b in batches]
        for b in batches:
            st_ref[b] = st_ref[b] * seq(e_cum, b)[c - 1:c, :] + upd[b] * bdf
        return out

    x_lr = _dot_f32(ext, w2_ref[...]) + b2_ref[...]
    log_a = _log_sigmoid(x_lr) * (1.0 / GLA_TAU)
    go = decay_attention(col(G_Q), col(G_K) * QK_SCALE, col(G_V), log_a, s_gla)
    put(2 * GW, head_norm(go, glag_ref[...], False) * _silu(col(G_G)))

    lb_all = lb_ref[...]
    lb_e = jnp.exp(lb_all - jnp.max(lb_all, axis=0, keepdims=True))
    lb_p = lb_e / jnp.sum(lb_e, axis=0, keepdims=True)
    lb = jnp.zeros((1, GW), F32)
    for l in range(1, layer + 1):
        lb = lb + lb_p[l:l + 1, :]
    z = col(H_F)
    x1 = jnp.log(lb)
    x2 = jnp.log1p(-lb) + _log_sigmoid(z)
    log_f = jnp.maximum(x1, x2) + jnp.log1p(jnp.exp(-jnp.abs(x1 - x2)))
    k_h = (1.0 - lb) * _sigmoid(-z)
    ho = decay_attention(_silu(col(H_Q)), k_h, col(H_I), log_f, s_hg)
    put(3 * GW, head_norm(ho * _sigmoid(col(H_G)), hgg_ref[...], False))


MIXER_BATCH = 4


def _mixer(proj, cosf, sins, conv_w, gate_rows, w2_pad, b2, lb, ret_g, ml_g, gla_g, hg_g, consts,
           layer, bsz, seq):
    c = CHUNK
    nc = seq // c
    t = bsz * seq
    row = lambda a: a.reshape(1, -1)
    full = lambda a: pl.BlockSpec(a.shape, lambda b, n: (0,) * a.ndim)
    cs = consts
    args = [proj, cosf, sins, conv_w, gate_rows, w2_pad, row(b2), lb,
            row(ret_g), row(ml_g), row(gla_g), row(hg_g),
            cs["dret"], cs["dq"], cs["dk"], cs["gc"], cs["lv"], cs["hh"], cs["bdf"], cs["ones_bd"], cs["span"]]
    nb = MIXER_BATCH
    in_specs = [
        pl.BlockSpec((nb, c, NPROJ), lambda b, n: (b, n, 0)),
        pl.BlockSpec((c, GW), lambda b, n: (n, 0)),
        pl.BlockSpec((c, GW), lambda b, n: (n, 0)),
    ] + [full(a) for a in args[3:]]
    args[0] = proj.reshape(bsz, seq, NPROJ)
    y = pl.pallas_call(
        functools.partial(_mixer_kernel, layer, nb),
        grid=(bsz // nb, nc),
        in_specs=in_specs,
        out_specs=pl.BlockSpec((nb, c, D_MODEL), lambda b, n: (b, n, 0)),
        out_shape=jax.ShapeDtypeStruct((bsz, seq, D_MODEL), F32),
        scratch_shapes=[
            pltpu.VMEM((nb, GW, GW), F32),
            pltpu.VMEM((nb, GW, 2 * GW), F32),
            pltpu.VMEM((nb, GW, GW), F32),
            pltpu.VMEM((nb, GW, GW), F32),
            pltpu.VMEM((nb * 8, c), F32),
            pltpu.VMEM((nb, c + CONV_PAD, 2 * GW), F32),
        ],
        compiler_params=pltpu.CompilerParams(
            dimension_semantics=("arbitrary", "arbitrary"), vmem_limit_bytes=VMEM_LIMIT),
        name="mixer",
    )(*args)
    return y.reshape(t, D_MODEL)


def _group_shift(v, pos, k):
    return jnp.where(pos < EXPERTS_PER_GROUP - k,
                     pltpu.roll(v, LANES - k, 1), pltpu.roll(v, EXPERTS_PER_GROUP - k, 1))


def _router(h, rw, rb):
    lane = lax.broadcasted_iota(jnp.int32, (1, LANES), 1)
    real = lane < N_EXPERTS
    logits = jnp.where(real, _dot_f32(h, rw), -jnp.inf)
    ex = jnp.exp(logits - jnp.max(logits, axis=-1, keepdims=True))
    probs = ex / jnp.sum(ex, axis=-1, keepdims=True)
    sel = jnp.where(real, probs + rb, -jnp.inf)
    pos = lane % EXPERTS_PER_GROUP
    gid = lane // EXPERTS_PER_GROUP
    r1 = _group_shift(sel, pos, 1)
    r2 = _group_shift(sel, pos, 2)
    r3 = _group_shift(sel, pos, 3)
    pair = jnp.maximum(jnp.maximum(jnp.maximum(sel + r1, sel + r2), jnp.maximum(sel + r3, r1 + r2)),
                       jnp.maximum(r1 + r3, r2 + r3))
    pair = jnp.where(real, pair, -jnp.inf)
    best = jnp.max(pair, axis=-1, keepdims=True)
    first = jnp.min(jnp.where(pair == best, gid, N_GROUPS), axis=-1, keepdims=True)
    rank = jnp.zeros(sel.shape, jnp.int32)
    for k, r in ((1, r1), (2, r2), (3, r3)):
        beats = (r > sel) | ((r == sel) & (pos + k >= EXPERTS_PER_GROUP))
        rank = rank + beats.astype(jnp.int32)
    return probs, first, rank


def _post_kernel(alpha, y_ref, x_ref, mod_ref, wout_ref, lng_ref, lnb_ref, rw_ref, rb_ref,
                 stri_ref, x1_ref, h2_ref, slots_ref, counts_ref, carry_ref):
    tm = y_ref.shape[0]

    @pl.when(pl.program_id(0) == 0)
    def _():
        carry_ref[...] = jnp.zeros_like(carry_ref)

    g1 = mod_ref[0, 2:3, :]
    sh2 = mod_ref[0, 3:4, :]
    sc2 = mod_ref[0, 4:5, :]
    mix = _dot(_bf(y_ref[...]), wout_ref[...])
    x1 = _layer_norm(alpha * x_ref[...] + g1 * mix, lng_ref[...], lnb_ref[...])
    x1_ref[...] = x1
    h = x1 * (1.0 + sc2) + sh2
    h2_ref[...] = h

    _, first, _ = _router(h, rw_ref[...], rb_ref[...])
    lane = lax.broadcasted_iota(jnp.int32, (1, LANES), 1)
    onehot = jnp.where(lane == first, 1.0, 0.0)
    before = _dot(stri_ref[...], _bf(onehot)) + carry_ref[...]
    code = first.astype(F32) * float(1 << POS_BITS) + jnp.sum(onehot * before, axis=-1, keepdims=True)
    slots_ref[0] = jnp.broadcast_to(code, (tm, LANES)).T[0:1, :].astype(jnp.int32)
    carry_ref[...] += jnp.sum(onehot, axis=0, keepdims=True)
    counts_ref[...] = carry_ref[...].astype(jnp.int32)


def _post(y, xt, modl, w_out, ln_g, ln_b, rw_pad, rb_pad, seq, alpha):
    t, d = xt.shape
    tm = 256
    per_batch = seq // tm
    row = lambda a: a.reshape(1, -1)
    tile = pl.BlockSpec((tm, d), lambda i: (i, 0))
    full = lambda a: pl.BlockSpec(a.shape, lambda i: (0,) * a.ndim)
    stri = jnp.asarray(np.tril(np.ones((tm, tm), np.float32), -1), BF16)
    args = [y, xt, modl, w_out, row(ln_g), row(ln_b), rw_pad, rb_pad, stri]
    return pl.pallas_call(
        functools.partial(_post_kernel, alpha),
        grid=(t // tm,),
        in_specs=[tile, tile, pl.BlockSpec((1, 6, d), lambda i: (i // per_batch, 0, 0))]
        + [full(a) for a in args[3:]],
        out_specs=[tile, tile, pl.BlockSpec((1, 1, tm), lambda i: (i, 0, 0)),
                   pl.BlockSpec((1, LANES), lambda i: (0, 0))],
        out_shape=[jax.ShapeDtypeStruct((t, d), F32), jax.ShapeDtypeStruct((t, d), F32),
                   jax.ShapeDtypeStruct((t // tm, 1, tm), jnp.int32),
                   jax.ShapeDtypeStruct((1, LANES), jnp.int32)],
        scratch_shapes=[pltpu.VMEM((1, LANES), F32)],
        compiler_params=pltpu.CompilerParams(
            dimension_semantics=("arbitrary",), vmem_limit_bytes=VMEM_LIMIT),
        name="post",
    )(*args)


MOE_TILE = 512


POS_BITS = 15


def _slot(code, row_start_ref):
    return row_start_ref[code >> POS_BITS] + (code & ((1 << POS_BITS) - 1))


def _dispatch_kernel(codes_ref, row_start_ref, zero_tiles_ref, h_ref, hs_hbm, zero_buf, sem, zsem):
    tm = h_ref.shape[0]
    i = pl.program_id(0)

    @pl.when(i == 0)
    def _():
        zero_buf[...] = jnp.zeros_like(zero_buf)
        for k in range(zero_tiles_ref.shape[0]):
            blk = zero_tiles_ref[k]

            @pl.when(blk >= 0)
            def _():
                dst = hs_hbm.at[pl.ds(pl.multiple_of(blk * MOE_TILE, MOE_TILE), MOE_TILE)]
                pltpu.make_async_copy(zero_buf, dst, zsem).start()
                pltpu.make_async_copy(zero_buf, dst, zsem).wait()

    def issue(w, carry):
        for r in range(8):
            row = w * 8 + r
            s = _slot(codes_ref[i * tm + row], row_start_ref)
            pltpu.make_async_copy(h_ref.at[pl.ds(row, 1)], hs_hbm.at[pl.ds(s, 1)], sem).start()
        return carry

    lax.fori_loop(0, tm // 8, issue, 0)
    pltpu.make_async_copy(h_ref, hs_hbm.at[pl.ds(0, tm)], sem).wait()


def _dispatch(codes, row_start, zero_tiles, h2, n_tiles_max):
    t, d = h2.shape
    tm = MOE_TILE
    return pl.pallas_call(
        _dispatch_kernel,
        grid_spec=pltpu.PrefetchScalarGridSpec(
            num_scalar_prefetch=3,
            grid=(t // tm,),
            in_specs=[pl.BlockSpec((tm, d), lambda i, c, r, z: (i, 0))],
            out_specs=pl.BlockSpec(memory_space=pl.ANY),
            scratch_shapes=[pltpu.VMEM((MOE_TILE, d), F32), pltpu.SemaphoreType.DMA(()),
                            pltpu.SemaphoreType.DMA(())],
        ),
        out_shape=jax.ShapeDtypeStruct((n_tiles_max * MOE_TILE, d), F32),
        compiler_params=pltpu.CompilerParams(
            dimension_semantics=("arbitrary",), vmem_limit_bytes=VMEM_LIMIT),
        name="dispatch",
    )(codes, row_start, zero_tiles, h2)


def _moe_kernel(tile_group_ref, n_tiles_ref, hs_ref, rw_ref, rb_ref, w1_ref, w3_ref, w2_ref, ys_ref):
    i = pl.program_id(0)

    @pl.when(i >= n_tiles_ref[0])
    def _():
        ys_ref[...] = jnp.zeros_like(ys_ref)

    @pl.when(i < n_tiles_ref[0])
    def _():
        group = tile_group_ref[i]
        h = hs_ref[...]
        probs, _, rank = _router(h, rw_ref[...], rb_ref[...])
        lane = lax.broadcasted_iota(jnp.int32, (1, LANES), 1)
        chosen = (lane // EXPERTS_PER_GROUP == group) & (rank < 2)
        w = jnp.where(chosen, probs, 0.0)
        gates = w / jnp.sum(w, axis=-1, keepdims=True)
        hb = _bf(h)
        acc = jnp.zeros(ys_ref.shape, F32)
        for e in range(EXPERTS_PER_GROUP):
            ge = jnp.sum(jnp.where(lane == group * EXPERTS_PER_GROUP + e, gates, 0.0),
                         axis=-1, keepdims=True)
            a = _silu(_dot(hb, w1_ref[e])) * _dot(hb, w3_ref[e])
            acc = acc + _dot(_bf(a * ge), w2_ref[e])
        ys_ref[...] = acc


def _moe(tile_group, n_tiles, hs, rw_pad, rb_pad, w1, w3, w2):
    rows, d = hs.shape
    tile = pl.BlockSpec((MOE_TILE, d), lambda i, tg, nt: (i, 0))
    full = lambda a: pl.BlockSpec(a.shape, lambda i, tg, nt: (0,) * a.ndim)
    up = pl.BlockSpec((EXPERTS_PER_GROUP, d, D_EXPERT), lambda i, tg, nt: (tg[i], 0, 0))
    down = pl.BlockSpec((EXPERTS_PER_GROUP, D_EXPERT, d), lambda i, tg, nt: (tg[i], 0, 0))
    return pl.pallas_call(
        _moe_kernel,
        grid_spec=pltpu.PrefetchScalarGridSpec(
            num_scalar_prefetch=2,
            grid=(rows // MOE_TILE,),
            in_specs=[tile, full(rw_pad), full(rb_pad), up, up, down],
            out_specs=tile,
        ),
        out_shape=jax.ShapeDtypeStruct((rows, d), F32),
        compiler_params=pltpu.CompilerParams(
            dimension_semantics=("arbitrary",), vmem_limit_bytes=VMEM_LIMIT),
        name="moe",
    )(tile_group, n_tiles, hs, rw_pad, rb_pad, w1, w3, w2)


def _tile_table(counts, n_tiles_max):
    per_group = (counts + MOE_TILE - 1) // MOE_TILE
    ends = jnp.cumsum(per_group)
    starts = ends - per_group
    total = ends[-1]
    i = jnp.arange(n_tiles_max, dtype=jnp.int32)
    group = jnp.sum((jnp.minimum(i, total - 1)[:, None] >= ends[None, :]).astype(jnp.int32), axis=1)
    spare = i[n_tiles_max - N_GROUPS:]
    zero_tiles = jnp.concatenate([jnp.where(per_group > 0, ends - 1, -1), jnp.where(spare >= total, spare, -1)])
    as_i32 = lambda a: a.astype(jnp.int32)
    return as_i32(group), as_i32(total.reshape(1)), as_i32(starts * MOE_TILE), as_i32(zero_tiles)


def _combine_kernel(alpha, codes_ref, row_start_ref, ys_hbm, x1_ref, mod_ref, lng_ref, lnb_ref,
                    o_ref, buf, sem):
    tm = o_ref.shape[0]
    i = pl.program_id(0)
    n = pl.num_programs(0)

    def gather(tile, b):
        def body(w, carry):
            for r in range(8):
                row = w * 8 + r
                s = _slot(codes_ref[tile * tm + row], row_start_ref)
                pltpu.make_async_copy(ys_hbm.at[pl.ds(s, 1)], buf.at[b, pl.ds(row, 1)], sem.at[b]).start()
            return carry
        lax.fori_loop(0, tm // 8, body, 0)

    @pl.when(i == 0)
    def _():
        gather(0, 0)

    @pl.when(i + 1 < n)
    def _():
        gather(i + 1, (i + 1) % 2)

    cur = i % 2
    pltpu.make_async_copy(ys_hbm.at[pl.ds(0, tm)], buf.at[cur], sem.at[cur]).wait()
    g2 = mod_ref[0, 5:6, :]
    o_ref[...] = _layer_norm(alpha * x1_ref[...] + g2 * buf[cur], lng_ref[...], lnb_ref[...])


def _combine(codes, row_start, ys, x1, modl, ln_g, ln_b, seq, alpha):
    t, d = x1.shape
    tm = 256
    per_batch = seq // tm
    row = lambda a: a.reshape(1, -1)
    tile = pl.BlockSpec((tm, d), lambda i, s, r: (i, 0))
    vec = pl.BlockSpec((1, d), lambda i, s, r: (0, 0))
    return pl.pallas_call(
        functools.partial(_combine_kernel, alpha),
        grid_spec=pltpu.PrefetchScalarGridSpec(
            num_scalar_prefetch=2,
            grid=(t // tm,),
            in_specs=[pl.BlockSpec(memory_space=pl.ANY), tile,
                      pl.BlockSpec((1, 6, d), lambda i, s, r: (i // per_batch, 0, 0)), vec, vec],
            out_specs=tile,
            scratch_shapes=[pltpu.VMEM((2, tm, d), F32), pltpu.SemaphoreType.DMA((2,))],
        ),
        out_shape=jax.ShapeDtypeStruct((t, d), F32),
        compiler_params=pltpu.CompilerParams(
            dimension_semantics=("arbitrary",), vmem_limit_bytes=VMEM_LIMIT),
        name="combine",
    )(codes, row_start, ys, x1, modl, row(ln_g), row(ln_b))


def _regroup_w_in(w):
    a = 8 * GW
    b = a + 2 * N_HEADS
    cc = b + 4 * GW
    dd = cc + GLA_RANK
    pad = jnp.zeros((w.shape[0], LANES - 2 * N_HEADS - GLA_RANK), w.dtype)
    return jnp.concatenate([w[:, :a], w[:, b:cc], w[:, dd:], w[:, a:b], w[:, cc:dd], pad], axis=1)


def _rotary_tables(seq):
    inv = ROPE_BASE ** (-jnp.arange(0, HEAD_DIM, 2, dtype=F32) / HEAD_DIM)
    ang = jnp.arange(seq, dtype=F32)[:, None] * inv[None, :]
    cos = jnp.cos(ang)
    sin = jnp.sin(ang)
    cosf = jnp.tile(jnp.concatenate([cos, cos], axis=1), (1, N_HEADS))
    sins = jnp.tile(jnp.concatenate([-sin, sin], axis=1), (1, N_HEADS))
    return cosf, sins


def kernel(x, c, ada_w, ada_b, w_in, mlstm_conv, mlstm_gate_b, gla_w2, gla_b2, hgrn_lb, ret_norm, mlstm_norm, gla_norm, hgrn_norm, w_out, ln1_g, ln1_b, router_w, router_b, exp_w1, exp_w3, exp_w2, ln2_g, ln2_b):
    bsz, seq, d = x.shape
    depth = ada_w.shape[0]
    assert d == D_MODEL and seq % 512 == 0 and bsz <= 16
    t = bsz * seq
    alpha = (2.0 * depth) ** 0.25
    assert t % MOE_TILE == 0 and t < (1 << POS_BITS)
    n_tiles_max = t // MOE_TILE + N_GROUPS
    consts = _mixer_consts()
    cosf, sins = _rotary_tables(seq)
    c_rows = jnp.pad(c, ((0, 16 - bsz), (0, 0)))
    mod = _adaln(c_rows, ada_w, ada_b)[:, :bsz]
    rw_pad = jnp.pad(router_w, ((0, 0), (0, LANES - N_EXPERTS)))
    rb_pad = jnp.pad(router_b, (0, LANES - N_EXPERTS)).reshape(1, LANES)
    xt = x.reshape(t, d)
    for l in range(depth):
        modl = mod[l].reshape(bsz, 6, d)
        proj = _proj(xt, modl, _bf(_regroup_w_in(w_in[l])), seq)
        gate_rows = jnp.broadcast_to(mlstm_gate_b[l][:, None], (2 * N_HEADS, CHUNK))
        w2_pad = jnp.zeros((LANES, GW), F32).at[2 * N_HEADS:2 * N_HEADS + GLA_RANK].set(gla_w2[l])
        y = _mixer(proj, cosf, sins, mlstm_conv[l], gate_rows, w2_pad, gla_b2[l], hgrn_lb,
                   ret_norm[l], mlstm_norm[l], gla_norm[l], hgrn_norm[l], consts, l, bsz, seq)
        x1, h2, codes, counts = _post(y, xt, modl, _bf(w_out[l]), ln1_g[l], ln1_b[l], rw_pad, rb_pad,
                                      seq, alpha)
        codes = codes.reshape(t)
        tile_group, n_tiles, row_start, zero_tiles = _tile_table(counts[0, :N_GROUPS], n_tiles_max)
        hs = _dispatch(codes, row_start, zero_tiles, h2, n_tiles_max)
        ys = _moe(tile_group, n_tiles, hs, rw_pad, rb_pad, _bf(exp_w1[l]), _bf(exp_w3[l]), _bf(exp_w2[l]))
        xt = _combine(codes, row_start, ys, x1, modl, ln2_g[l], ln2_b[l], seq, alpha)
    return xt.reshape(bsz, seq, d)
```

```python
import functools

import numpy as np
import jax
import jax.numpy as jnp
from jax import lax
from jax.experimental import pallas as pl
from jax.experimental.pallas import tpu as pltpu

F32 = jnp.float32
BF16 = jnp.bfloat16

D_MODEL = 1024
N_MIXERS = 4
GW = D_MODEL // N_MIXERS
N_HEADS = 4
HEAD_DIM = GW // N_HEADS
CONV_WIDTH = 4
GLA_RANK = 16
GLA_TAU = 16.0
ROPE_BASE = 10000.0
N_GROUPS = 4
EXPERTS_PER_GROUP = 4
N_EXPERTS = 16
D_EXPERT = D_MODEL // 2
LN_EPS = 1e-5
NORM_EPS = 1e-6
QK_SCALE = HEAD_DIM ** -0.5

LANES = 128
CHUNK = 128
LEVELS = 7
CONV_PAD = 8
NPROJ = 16 * GW + LANES
EXTRA = 16 * GW
VMEM_LIMIT = 56 * 1024 * 1024

(R_Q, R_K, R_V, R_G, M_Q, M_K, M_V, M_O, G_Q, G_K, G_V, G_G, H_Q, H_F, H_I, H_G) = (
    GW * i for i in range(16))


def _bf(x):
    return x.astype(BF16)


def _dot(a, b):
    return jnp.dot(a, b, preferred_element_type=F32)


def _dot_nt(a, b):
    return lax.dot_general(a, b, (((1,), (1,)), ((), ())), preferred_element_type=F32)


def _split2(x):
    hi = _bf(x)
    lo = _bf(x - hi.astype(F32))
    return hi, lo


def _split3(x):
    hi = _bf(x)
    r = x - hi.astype(F32)
    mid = _bf(r)
    lo = _bf(r - mid.astype(F32))
    return hi, mid, lo


def _dot_f32(a, b):
    ah, al = _split2(a)
    bh, bl = _split2(b)
    return _dot(ah, bh) + _dot(ah, bl) + _dot(al, bh)


def _sigmoid(x):
    return 0.5 * jnp.tanh(0.5 * x) + 0.5


def _silu(x):
    return x * _sigmoid(x)


def _log_sigmoid(x):
    return jnp.minimum(x, 0.0) - jnp.log1p(jnp.exp(-jnp.abs(x)))


def _layer_norm(v, g, b):
    mu = jnp.mean(v, axis=-1, keepdims=True)
    d = v - mu
    var = jnp.mean(d * d, axis=-1, keepdims=True)
    return d * lax.rsqrt(var + LN_EPS) * g + b


def _adaln_kernel(c_ref, w_ref, b_ref, o_ref):
    cond = _silu(c_ref[...])
    o_ref[0] = _dot_f32(cond, w_ref[0]) + b_ref[0]


def _adaln(c, ada_w, ada_b):
    depth, d, n = ada_w.shape
    bsz = c.shape[0]
    tn = 1536
    return pl.pallas_call(
        _adaln_kernel,
        grid=(depth, n // tn),
        in_specs=[
            pl.BlockSpec((bsz, d), lambda l, j: (0, 0)),
            pl.BlockSpec((1, d, tn), lambda l, j: (l, 0, j)),
            pl.BlockSpec((1, 1, tn), lambda l, j: (l, 0, j)),
        ],
        out_specs=pl.BlockSpec((1, bsz, tn), lambda l, j: (l, 0, j)),
        out_shape=jax.ShapeDtypeStruct((depth, bsz, n), F32),
        compiler_params=pltpu.CompilerParams(
            dimension_semantics=("arbitrary", "arbitrary"), vmem_limit_bytes=VMEM_LIMIT),
        name="adaln",
    )(c, ada_w, ada_b.reshape(depth, 1, n))


def _mixer_consts():
    c = CHUNK
    i = np.arange(c)[:, None]
    j = np.arange(c)[None, :]
    rel = i - j
    log_gamma = np.log1p(-np.exp2(-5.0 - np.arange(N_HEADS, dtype=np.float64)))
    dret = np.concatenate(
        [np.where(rel >= 0, np.exp(log_gamma[h] * np.maximum(rel, 0)), 0.0) for h in range(N_HEADS)], axis=1)
    lane_head = np.arange(GW) // HEAD_DIM
    dq = np.exp(log_gamma[lane_head][None, :] * (np.arange(c)[:, None] + 1.0))
    dk = np.exp(log_gamma[lane_head][None, :] * (c - 1.0 - np.arange(c)[:, None]))
    gc = np.exp(log_gamma[lane_head] * c)[None, :]
    lv = np.zeros((c, c), np.int32)
    lv[rel == 0] = 1
    spans = [(rel >= 0), (rel < 0)]
    code, b = 2, 1
    while b < c:
        lv[(i // (2 * b) == j // (2 * b)) & ((i // b) % 2 == 1) & ((j // b) % 2 == 0)] = code
        ref = (i // (2 * b)) * 2 * b + b - 1
        right = (i // b) % 2 == 1
        spans.append(np.where(right, (j > ref) & (j <= i), (j > i) & (j <= ref)))
        code, b = code + 1, b * 2
    lv = np.tile(lv, (1, N_HEADS))
    span = np.concatenate(spans, axis=0).astype(np.float32)
    span = np.tile(span, (1, 3))
    hh = (lane_head[:, None] == lane_head[None, :]).astype(np.float32)
    ones_bd = np.repeat(np.eye(N_HEADS, dtype=np.float32), c, axis=0)
    ones_bd = np.repeat(ones_bd, HEAD_DIM, axis=1)
    return dict(
        dret=jnp.asarray(dret, F32), dq=jnp.asarray(dq, F32), dk=jnp.asarray(dk, F32),
        gc=jnp.asarray(gc, F32), lv=jnp.asarray(lv), hh=jnp.asarray(hh, BF16),
        bdf=jnp.asarray(hh, F32), ones_bd=jnp.asarray(ones_bd, BF16), span=jnp.asarray(span, BF16))


def _lane_cumsum(x):
    lane = lax.broadcasted_iota(jnp.int32, x.shape, 1)
    s = 1
    while s < x.shape[1]:
        x = x + jnp.where(lane >= s, pltpu.roll(x, s, 1), 0.0)
        s *= 2
    return x


def _lane_cummax(x):
    lane = lax.broadcasted_iota(jnp.int32, x.shape, 1)
    s = 1
    while s < x.shape[1]:
        x = jnp.maximum(x, jnp.where(lane >= s, pltpu.roll(x, s, 1), -jnp.inf))
        s *= 2
    return x


def _mixer_kernel(layer, nb,
                  x_ref, mod_ref, win_ref, cos_ref, sin_ref, conv_ref, gb_ref, w2_ref, b2_ref, lb_ref,
                  retg_ref, mlg_ref, glag_ref, hgg_ref,
                  dret_ref, dq_ref, dk_ref, gc_ref, lv_ref, hh_ref, bdf_ref, onesbd_ref, span_ref,
                  y_ref,
                  s_ret, s_cn, s_gla, s_hg, m_ml, conv_buf):
    c = CHUNK
    batches = range(nb)

    @pl.when(pl.program_id(1) == 0)
    def _():
        for s in (s_ret, s_cn, s_gla, s_hg, m_ml):
            s[...] = jnp.zeros_like(s)
        conv_buf[:, 0:CONV_PAD, :] = jnp.zeros((nb, CONV_PAD, 2 * GW), F32)

    lane_head = lax.broadcasted_iota(jnp.int32, (1, GW), 1) // HEAD_DIM
    head_rows = [(lane_head == h).astype(BF16) for h in range(N_HEADS)]
    hh = hh_ref[...]
    bdf = bdf_ref[...]
    lv = lv_ref[...]
    causal = lv >= 1

    def stack(xs):
        return jnp.concatenate(xs, axis=0)

    def tiled(x):
        return stack([x] * nb)

    def seq(x, b):
        return x[b * c:(b + 1) * c]

    hb = _bf(stack([x_ref[b] * (1.0 + mod_ref[b, 1:2, :]) + mod_ref[b, 0:1, :] for b in batches]))

    issued = {}

    def issue(first):
        issued[first] = _dot(hb, win_ref[:, first:first + N_MIXERS * GW])

    def col(off, width=GW):
        first = off // (N_MIXERS * GW) * (N_MIXERS * GW)
        return issued[first][:, off - first:off - first + width]

    issue(R_Q)
    issue(M_Q)
    ext = _dot(hb, win_ref[:, EXTRA:EXTRA + LANES])

    def put(off, val):
        for b in batches:
            y_ref[b, :, off:off + GW] = seq(val, b).astype(y_ref.dtype)

    def stackmask(xb):
        return jnp.concatenate([xb * head_rows[h] for h in range(N_HEADS)], axis=0)

    def expand_heads(cols, first):
        out = jnp.zeros((cols.shape[0], GW), F32)
        for h in range(N_HEADS):
            out = jnp.where(lane_head == h, cols[:, first + h:first + h + 1], out)
        return out

    def head_mean(v):
        hi, lo = _split2(v)
        return (_dot(hi, hh) + _dot(lo, hh)) * (1.0 / HEAD_DIM)

    def head_norm(v, gain, center):
        if center:
            v = v - head_mean(v)
        return v * lax.rsqrt(head_mean(v * v) + NORM_EPS) * gain

    cosf = tiled(cos_ref[...])
    sins = tiled(sin_ref[...])
    half = lax.broadcasted_iota(jnp.int32, (1, GW), 1) % HEAD_DIM < HEAD_DIM // 2

    def rotary(v):
        swapped = jnp.where(half, pltpu.roll(v, GW - HEAD_DIM // 2, 1), pltpu.roll(v, HEAD_DIM // 2, 1))
        return v * cosf + swapped * sins

    rq = _bf(rotary(col(R_Q)) * QK_SCALE)
    rk = rotary(col(R_K))
    rkb = _bf(rk)
    rvb = _bf(col(R_V))
    rkd = rk * tiled(dk_ref[...])
    dret = dret_ref[...]
    dq = dq_ref[...]
    gc = gc_ref[...]
    sc = [_dot_nt(seq(rq, b), stackmask(seq(rkb, b))) for b in batches]
    inter = [_dot(seq(rq, b), _bf(s_ret[b])) for b in batches]
    scb = [_bf(s * dret) for s in sc]
    ro = stack([_dot(scb[b], stackmask(seq(rvb, b))) + inter[b] * dq for b in batches])
    upd = [_dot(_bf(seq(rkd, b).T), seq(rvb, b)) for b in batches]
    for b in batches:
        s_ret[b] = s_ret[b] * gc + upd[b] * bdf
    put(0, head_norm(ro, retg_ref[...], True) * _silu(col(R_G)))

    issue(G_Q)
    mqk = col(M_Q, 2 * GW)
    for b in batches:
        conv_buf[b, CONV_PAD:CONV_PAD + c, :] = seq(mqk, b)
    qk = jnp.zeros((nb * c, 2 * GW), F32)
    for t in range(CONV_WIDTH):
        s = CONV_WIDTH - 1 - t
        qk = qk + stack([conv_buf[b, CONV_PAD - s:CONV_PAD - s + c, :] for b in batches]) * conv_ref[t:t + 1, :]
    for b in batches:
        conv_buf[b, 0:CONV_PAD, :] = conv_buf[b, c:c + CONV_PAD, :]
    qk = _silu(qk)
    mq = _bf(qk[:, 0:GW])
    mk = qk[:, GW:2 * GW] * QK_SCALE
    mkb = _bf(mk)
    mvb = _bf(col(M_V))

    pre = stack([seq(ext, b).T[0:8, :] for b in batches]) + tiled(gb_ref[...])
    row8 = lax.broadcasted_iota(jnp.int32, (8 * nb, c), 0)
    valid = row8 % 8 >= N_HEADS
    lf = jnp.where(valid, _log_sigmoid(pre), 0.0)
    gi = jnp.where(valid, pltpu.roll(pre, N_HEADS, 0), 0.0)
    bcum = _lane_cumsum(lf)
    a = gi - bcum
    m_prev = m_ml[...]
    mrow = jnp.maximum(m_prev, _lane_cummax(a))
    m_last = jnp.broadcast_to(mrow[:, c - 1:c], (8 * nb, c))
    b_last = jnp.broadcast_to(bcum[:, c - 1:c], (8 * nb, c))
    wi = jnp.exp(m_prev - mrow)
    em = jnp.exp(-(bcum + mrow))
    wk = jnp.exp(a - m_last)
    dec = jnp.exp(m_prev - m_last)
    m_ml[...] = jnp.where(valid, b_last + m_last, 0.0)
    kinds = [mrow, wi, em, wk]
    if 32 * nb < c:
        kinds.append(jnp.zeros((c - 32 * nb, c), F32))
    cols = stack(kinds).T

    def first_col(kind, b):
        return kind * 8 * nb + 8 * b + N_HEADS

    mexp = [jnp.concatenate([jnp.broadcast_to(cols[:, first_col(0, b) + h:first_col(0, b) + h + 1], (c, c))
                             for h in range(N_HEADS)], axis=1) for b in batches]
    a_row = [jnp.concatenate([a[8 * b + N_HEADS + h:8 * b + N_HEADS + h + 1, :] for h in range(N_HEADS)], axis=1)
             for b in batches]
    qkm = [_dot_nt(seq(mq, b), stackmask(seq(mkb, b))) for b in batches]
    inter = [_dot(seq(mq, b), _bf(s_cn[b])) for b in batches]
    sm = [_bf(qkm[b] * jnp.exp(jnp.where(causal, a_row[b] - mexp[b], -jnp.inf))) for b in batches]
    onesbd = onesbd_ref[...]
    wi_l = stack([expand_heads(cols, first_col(1, b)) for b in batches])
    em_l = stack([expand_heads(cols, first_col(2, b)) for b in batches])
    wk_l = stack([expand_heads(cols, first_col(3, b)) for b in batches])
    nd = (stack([_dot(sm[b], jnp.concatenate([stackmask(seq(mvb, b)), onesbd], axis=1)) for b in batches])
          + jnp.concatenate([wi_l, wi_l], axis=1) * stack(inter))
    mh = nd[:, 0:GW] / jnp.maximum(jnp.abs(nd[:, GW:2 * GW]), em_l)
    kt = mk * wk_l
    ones = jnp.ones((c, GW), BF16)
    upd = [_dot(_bf(seq(kt, b).T), jnp.concatenate([seq(mvb, b), ones], axis=1)) for b in batches]
    dec2 = jnp.concatenate([dec, dec], axis=1)
    bdf2 = jnp.concatenate([bdf, bdf], axis=1)
    for b in batches:
        dec_l = jnp.zeros((1, GW), F32)
        for h in range(N_HEADS):
            dec_l = jnp.where(lane_head == h, dec2[8 * b + N_HEADS + h:8 * b + N_HEADS + h + 1, :], dec_l)
        s_cn[b] = s_cn[b] * jnp.concatenate([dec_l, dec_l], axis=1) + upd[b] * bdf2
    put(GW, head_norm(mh * _sigmoid(col(M_O)), mlg_ref[...], True))

    span = span_ref[...]
    level_masks = [lv == code for code in range(2, 2 + LEVELS)]

    def decay_attention(q, k, v, g, st_ref):
        g3 = _split3(g)
        dec = [jnp.exp(_dot(span, stack([seq(part, b) for part in g3]))) for b in batches]
        e_cum = stack([d[0:c] for d in dec])
        e_end = stack([d[c:2 * c] for d in dec])
        qe = _bf(q * e_cum)
        out = stack([_dot_nt(seq(qe, b), _bf(st_ref[b])) for b in batches])
        out = out + _dot(_bf(q * k), hh) * v
        scores = [jnp.zeros((c, N_HEADS * c), F32) for _ in batches]
        for lvl in range(LEVELS):
            e = stack([d[(2 + lvl) * c:(3 + lvl) * c] for d in dec])
            qb = _bf(q * e)
            kb = _bf(k * e)
            level = [_dot_nt(seq(qb, b), stackmask(seq(kb, b))) for b in batches]
            scores = [jnp.where(level_masks[lvl], level[b], scores[b]) for b in batches]
        vb = _bf(v)
        out = out + stack([_dot(_bf(scores[b]), stackmask(seq(vb, b))) for b in batches])
        ke = _bf(k * e_end)
        upd = [_dot(_bf(seq(v, b).T), seq(ke, b)) for b in batches]
        for b in batches:
            st_ref[b] = st_ref[b] * seq(e_cum, b)[c - 1:c, :] + upd[b] * bdf
        return out

    issue(H_Q)
    x_lr = _dot_f32(ext, w2_ref[...]) + b2_ref[...]
    log_a = _log_sigmoid(x_lr) * (1.0 / GLA_TAU)
    go = decay_attention(col(G_Q), col(G_K) * QK_SCALE, col(G_V), log_a, s_gla)
    put(2 * GW, head_norm(go, glag_ref[...], False) * _silu(col(G_G)))

    lb_all = lb_ref[...]
    lb_e = jnp.exp(lb_all - jnp.max(lb_all, axis=0, keepdims=True))
    lb_p = lb_e / jnp.sum(lb_e, axis=0, keepdims=True)
    lb = jnp.zeros((1, GW), F32)
    for l in range(1, layer + 1):
        lb = lb + lb_p[l:l + 1, :]
    z = col(H_F)
    x1 = jnp.log(lb)
    x2 = jnp.log1p(-lb) + _log_sigmoid(z)
    log_f = jnp.maximum(x1, x2) + jnp.log1p(jnp.exp(-jnp.abs(x1 - x2)))
    k_h = (1.0 - lb) * _sigmoid(-z)
    ho = decay_attention(_silu(col(H_Q)), k_h, col(H_I), log_f, s_hg)
    put(3 * GW, head_norm(ho * _sigmoid(col(H_G)), hgg_ref[...], False))


MIXER_BATCH = 4


def _mixer(xt, modl, w_in_p, cosf, sins, conv_w, gate_rows, w2_pad, b2, lb, ret_g, ml_g, gla_g, hg_g, consts,
           layer, bsz, seq):
    c = CHUNK
    nc = seq // c
    t, d = xt.shape
    row = lambda a: a.reshape(1, -1)
    full = lambda a: pl.BlockSpec(a.shape, lambda b, n: (0,) * a.ndim)
    cs = consts
    args = [xt.reshape(bsz, seq, d), modl, w_in_p, cosf, sins, conv_w, gate_rows, w2_pad, row(b2), lb,
            row(ret_g), row(ml_g), row(gla_g), row(hg_g),
            cs["dret"], cs["dq"], cs["dk"], cs["gc"], cs["lv"], cs["hh"], cs["bdf"], cs["ones_bd"], cs["span"]]
    nb = MIXER_BATCH
    in_specs = [
        pl.BlockSpec((nb, c, d), lambda b, n: (b, n, 0)),
        pl.BlockSpec((nb, 6, d), lambda b, n: (b, 0, 0)),
        full(w_in_p),
        pl.BlockSpec((c, GW), lambda b, n: (n, 0)),
        pl.BlockSpec((c, GW), lambda b, n: (n, 0)),
    ] + [full(a) for a in args[5:]]
    y = pl.pallas_call(
        functools.partial(_mixer_kernel, layer, nb),
        grid=(bsz // nb, nc),
        in_specs=in_specs,
        out_specs=pl.BlockSpec((nb, c, D_MODEL), lambda b, n: (b, n, 0)),
        out_shape=jax.ShapeDtypeStruct((bsz, seq, D_MODEL), BF16),
        scratch_shapes=[
            pltpu.VMEM((nb, GW, GW), F32),
            pltpu.VMEM((nb, GW, 2 * GW), F32),
            pltpu.VMEM((nb, GW, GW), F32),
            pltpu.VMEM((nb, GW, GW), F32),
            pltpu.VMEM((nb * 8, c), F32),
            pltpu.VMEM((nb, c + CONV_PAD, 2 * GW), F32),
        ],
        compiler_params=pltpu.CompilerParams(
            dimension_semantics=("arbitrary", "arbitrary"), vmem_limit_bytes=VMEM_LIMIT),
        name="mixer",
    )(*args)
    return y.reshape(t, D_MODEL)


def _group_shift(v, pos, k):
    return jnp.where(pos < EXPERTS_PER_GROUP - k,
                     pltpu.roll(v, LANES - k, 1), pltpu.roll(v, EXPERTS_PER_GROUP - k, 1))


def _router(h, rw, rb):
    lane = lax.broadcasted_iota(jnp.int32, (1, LANES), 1)
    real = lane < N_EXPERTS
    logits = jnp.where(real, _dot_f32(h, rw), -jnp.inf)
    ex = jnp.exp(logits - jnp.max(logits, axis=-1, keepdims=True))
    probs = ex / jnp.sum(ex, axis=-1, keepdims=True)
    sel = jnp.where(real, probs + rb, -jnp.inf)
    pos = lane % EXPERTS_PER_GROUP
    gid = lane // EXPERTS_PER_GROUP
    r1 = _group_shift(sel, pos, 1)
    r2 = _group_shift(sel, pos, 2)
    r3 = _group_shift(sel, pos, 3)
    pair = jnp.maximum(jnp.maximum(jnp.maximum(sel + r1, sel + r2), jnp.maximum(sel + r3, r1 + r2)),
                       jnp.maximum(r1 + r3, r2 + r3))
    pair = jnp.where(real, pair, -jnp.inf)
    best = jnp.max(pair, axis=-1, keepdims=True)
    first = jnp.min(jnp.where(pair == best, gid, N_GROUPS), axis=-1, keepdims=True)
    rank = jnp.zeros(sel.shape, jnp.int32)
    for k, r in ((1, r1), (2, r2), (3, r3)):
        beats = (r > sel) | ((r == sel) & (pos + k >= EXPERTS_PER_GROUP))
        rank = rank + beats.astype(jnp.int32)
    return probs, first, rank


def _post_kernel(alpha, y_ref, x_ref, mod_ref, wout_ref, lng_ref, lnb_ref, rw_ref, rb_ref,
                 stri_ref, x1_ref, h2_ref, slots_ref, counts_ref, carry_ref):
    tm = y_ref.shape[0]

    @pl.when(pl.program_id(0) == 0)
    def _():
        carry_ref[...] = jnp.zeros_like(carry_ref)

    g1 = mod_ref[0, 2:3, :]
    sh2 = mod_ref[0, 3:4, :]
    sc2 = mod_ref[0, 4:5, :]
    mix = _dot(_bf(y_ref[...]), wout_ref[...])
    x1 = _layer_norm(alpha * x_ref[...] + g1 * mix, lng_ref[...], lnb_ref[...])
    x1_ref[...] = x1
    h = x1 * (1.0 + sc2) + sh2
    h2_ref[...] = h

    _, first, _ = _router(h, rw_ref[...], rb_ref[...])
    lane = lax.broadcasted_iota(jnp.int32, (1, LANES), 1)
    onehot = jnp.where(lane == first, 1.0, 0.0)
    before = _dot(stri_ref[...], _bf(onehot)) + carry_ref[...]
    code = first.astype(F32) * float(1 << POS_BITS) + jnp.sum(onehot * before, axis=-1, keepdims=True)
    slots_ref[0] = jnp.broadcast_to(code, (tm, LANES)).T[0:1, :].astype(jnp.int32)
    carry_ref[...] += jnp.sum(onehot, axis=0, keepdims=True)
    counts_ref[...] = carry_ref[...].astype(jnp.int32)


def _post(y, xt, modl, w_out, ln_g, ln_b, rw_pad, rb_pad, seq, alpha):
    t, d = xt.shape
    tm = 256
    per_batch = seq // tm
    row = lambda a: a.reshape(1, -1)
    tile = pl.BlockSpec((tm, d), lambda i: (i, 0))
    full = lambda a: pl.BlockSpec(a.shape, lambda i: (0,) * a.ndim)
    stri = jnp.asarray(np.tril(np.ones((tm, tm), np.float32), -1), BF16)
    args = [y, xt, modl, w_out, row(ln_g), row(ln_b), rw_pad, rb_pad, stri]
    return pl.pallas_call(
        functools.partial(_post_kernel, alpha),
        grid=(t // tm,),
        in_specs=[tile, tile, pl.BlockSpec((1, 6, d), lambda i: (i // per_batch, 0, 0))]
        + [full(a) for a in args[3:]],
        out_specs=[tile, tile, pl.BlockSpec((1, 1, tm), lambda i: (i, 0, 0)),
                   pl.BlockSpec((1, LANES), lambda i: (0, 0))],
        out_shape=[jax.ShapeDtypeStruct((t, d), F32), jax.ShapeDtypeStruct((t, d), F32),
                   jax.ShapeDtypeStruct((t // tm, 1, tm), jnp.int32),
                   jax.ShapeDtypeStruct((1, LANES), jnp.int32)],
        scratch_shapes=[pltpu.VMEM((1, LANES), F32)],
        compiler_params=pltpu.CompilerParams(
            dimension_semantics=("arbitrary",), vmem_limit_bytes=VMEM_LIMIT),
        name="post",
    )(*args)


MOE_TILE = 512


POS_BITS = 15


def _slots(codes, row_start):
    return row_start[codes >> POS_BITS] + (codes & ((1 << POS_BITS) - 1))


def _dispatch_kernel(slots_ref, zero_tiles_ref, h_ref, hs_hbm, zero_buf, sem, zsem):
    tm = h_ref.shape[0]
    i = pl.program_id(0)

    @pl.when(i == 0)
    def _():
        zero_buf[...] = jnp.zeros_like(zero_buf)
        for k in range(zero_tiles_ref.shape[0]):
            blk = zero_tiles_ref[k]

            @pl.when(blk >= 0)
            def _():
                dst = hs_hbm.at[pl.ds(pl.multiple_of(blk * MOE_TILE, MOE_TILE), MOE_TILE)]
                pltpu.make_async_copy(zero_buf, dst, zsem).start()
                pltpu.make_async_copy(zero_buf, dst, zsem).wait()

    def issue(w, carry):
        for r in range(8):
            row = w * 8 + r
            s = slots_ref[i * tm + row]
            pltpu.make_async_copy(h_ref.at[pl.ds(row, 1)], hs_hbm.at[pl.ds(s, 1)], sem).start()
        return carry

    lax.fori_loop(0, tm // 8, issue, 0)
    pltpu.make_async_copy(h_ref, hs_hbm.at[pl.ds(0, tm)], sem).wait()


def _dispatch(slots, zero_tiles, h2, n_tiles_max):
    t, d = h2.shape
    tm = MOE_TILE
    return pl.pallas_call(
        _dispatch_kernel,
        grid_spec=pltpu.PrefetchScalarGridSpec(
            num_scalar_prefetch=2,
            grid=(t // tm,),
            in_specs=[pl.BlockSpec((tm, d), lambda i, s, z: (i, 0))],
            out_specs=pl.BlockSpec(memory_space=pl.ANY),
            scratch_shapes=[pltpu.VMEM((MOE_TILE, d), F32), pltpu.SemaphoreType.DMA(()),
                            pltpu.SemaphoreType.DMA(())],
        ),
        out_shape=jax.ShapeDtypeStruct((n_tiles_max * MOE_TILE, d), F32),
        compiler_params=pltpu.CompilerParams(
            dimension_semantics=("arbitrary",), vmem_limit_bytes=VMEM_LIMIT),
        name="dispatch",
    )(slots, zero_tiles, h2)


def _moe_kernel(tile_group_ref, n_tiles_ref, hs_ref, rw_ref, rb_ref, w1_ref, w3_ref, w2_ref, ys_ref):
    i = pl.program_id(0)

    @pl.when(i >= n_tiles_ref[0])
    def _():
        ys_ref[...] = jnp.zeros_like(ys_ref)

    @pl.when(i < n_tiles_ref[0])
    def _():
        group = tile_group_ref[i]
        h = hs_ref[...]
        probs, _, rank = _router(h, rw_ref[...], rb_ref[...])
        lane = lax.broadcasted_iota(jnp.int32, (1, LANES), 1)
        chosen = (lane // EXPERTS_PER_GROUP == group) & (rank < 2)
        w = jnp.where(chosen, probs, 0.0)
        gates = w / jnp.sum(w, axis=-1, keepdims=True)
        hb = _bf(h)
        acc = jnp.zeros(ys_ref.shape, F32)
        for e in range(EXPERTS_PER_GROUP):
            ge = jnp.sum(jnp.where(lane == group * EXPERTS_PER_GROUP + e, gates, 0.0),
                         axis=-1, keepdims=True)
            a = _silu(_dot(hb, w1_ref[e])) * _dot(hb, w3_ref[e])
            acc = acc + _dot(_bf(a * ge), w2_ref[e])
        ys_ref[...] = acc


def _moe(tile_group, n_tiles, hs, rw_pad, rb_pad, w1, w3, w2):
    rows, d = hs.shape
    tile = pl.BlockSpec((MOE_TILE, d), lambda i, tg, nt: (i, 0))
    full = lambda a: pl.BlockSpec(a.shape, lambda i, tg, nt: (0,) * a.ndim)
    up = pl.BlockSpec((EXPERTS_PER_GROUP, d, D_EXPERT), lambda i, tg, nt: (tg[i], 0, 0))
    down = pl.BlockSpec((EXPERTS_PER_GROUP, D_EXPERT, d), lambda i, tg, nt: (tg[i], 0, 0))
    return pl.pallas_call(
        _moe_kernel,
        grid_spec=pltpu.PrefetchScalarGridSpec(
            num_scalar_prefetch=2,
            grid=(rows // MOE_TILE,),
            in_specs=[tile, full(rw_pad), full(rb_pad), up, up, down],
            out_specs=tile,
        ),
        out_shape=jax.ShapeDtypeStruct((rows, d), F32),
        compiler_params=pltpu.CompilerParams(
            dimension_semantics=("arbitrary",), vmem_limit_bytes=VMEM_LIMIT),
        name="moe",
    )(tile_group, n_tiles, hs, rw_pad, rb_pad, w1, w3, w2)


def _tile_table(counts, n_tiles_max):
    per_group = (counts + MOE_TILE - 1) // MOE_TILE
    ends = jnp.cumsum(per_group)
    starts = ends - per_group
    total = ends[-1]
    i = jnp.arange(n_tiles_max, dtype=jnp.int32)
    group = jnp.sum((jnp.minimum(i, total - 1)[:, None] >= ends[None, :]).astype(jnp.int32), axis=1)
    spare = i[n_tiles_max - N_GROUPS:]
    zero_tiles = jnp.concatenate([jnp.where(per_group > 0, ends - 1, -1), jnp.where(spare >= total, spare, -1)])
    as_i32 = lambda a: a.astype(jnp.int32)
    return as_i32(group), as_i32(total.reshape(1)), as_i32(starts * MOE_TILE), as_i32(zero_tiles)


def _combine_kernel(alpha, slots_ref, ys_hbm, x1_ref, mod_ref, lng_ref, lnb_ref, o_ref, buf, sem):
    tm = o_ref.shape[0]
    i = pl.program_id(0)
    n = pl.num_programs(0)

    def gather(tile, b):
        def body(w, carry):
            for r in range(8):
                row = w * 8 + r
                s = slots_ref[tile * tm + row]
                pltpu.make_async_copy(ys_hbm.at[pl.ds(s, 1)], buf.at[b, pl.ds(row, 1)], sem.at[b]).start()
            return carry
        lax.fori_loop(0, tm // 8, body, 0)

    @pl.when(i == 0)
    def _():
        gather(0, 0)

    @pl.when(i + 1 < n)
    def _():
        gather(i + 1, (i + 1) % 2)

    cur = i % 2
    pltpu.make_async_copy(ys_hbm.at[pl.ds(0, tm)], buf.at[cur], sem.at[cur]).wait()
    g2 = mod_ref[0, 5:6, :]
    o_ref[...] = _layer_norm(alpha * x1_ref[...] + g2 * buf[cur], lng_ref[...], lnb_ref[...])


def _combine(slots, ys, x1, modl, ln_g, ln_b, seq, alpha):
    t, d = x1.shape
    tm = 256
    per_batch = seq // tm
    row = lambda a: a.reshape(1, -1)
    tile = pl.BlockSpec((tm, d), lambda i, s: (i, 0))
    vec = pl.BlockSpec((1, d), lambda i, s: (0, 0))
    return pl.pallas_call(
        functools.partial(_combine_kernel, alpha),
        grid_spec=pltpu.PrefetchScalarGridSpec(
            num_scalar_prefetch=1,
            grid=(t // tm,),
            in_specs=[pl.BlockSpec(memory_space=pl.ANY), tile,
                      pl.BlockSpec((1, 6, d), lambda i, s: (i // per_batch, 0, 0)), vec, vec],
            out_specs=tile,
            scratch_shapes=[pltpu.VMEM((2, tm, d), F32), pltpu.SemaphoreType.DMA((2,))],
        ),
        out_shape=jax.ShapeDtypeStruct((t, d), F32),
        compiler_params=pltpu.CompilerParams(
            dimension_semantics=("arbitrary",), vmem_limit_bytes=VMEM_LIMIT),
        name="combine",
    )(slots, ys, x1, modl, row(ln_g), row(ln_b))


def _regroup_w_in(w):
    a = 8 * GW
    b = a + 2 * N_HEADS
    cc = b + 4 * GW
    dd = cc + GLA_RANK
    pad = jnp.zeros((w.shape[0], LANES - 2 * N_HEADS - GLA_RANK), w.dtype)
    return jnp.concatenate([w[:, :a], w[:, b:cc], w[:, dd:], w[:, a:b], w[:, cc:dd], pad], axis=1)


def _rotary_tables(seq):
    inv = ROPE_BASE ** (-jnp.arange(0, HEAD_DIM, 2, dtype=F32) / HEAD_DIM)
    ang = jnp.arange(seq, dtype=F32)[:, None] * inv[None, :]
    cos = jnp.cos(ang)
    sin = jnp.sin(ang)
    cosf = jnp.tile(jnp.concatenate([cos, cos], axis=1), (1, N_HEADS))
    sins = jnp.tile(jnp.concatenate([-sin, sin], axis=1), (1, N_HEADS))
    return cosf, sins


def kernel(x, c, ada_w, ada_b, w_in, mlstm_conv, mlstm_gate_b, gla_w2, gla_b2, hgrn_lb, ret_norm, mlstm_norm, gla_norm, hgrn_norm, w_out, ln1_g, ln1_b, router_w, router_b, exp_w1, exp_w3, exp_w2, ln2_g, ln2_b):
    bsz, seq, d = x.shape
    depth = ada_w.shape[0]
    assert d == D_MODEL and seq % 512 == 0 and bsz <= 16
    t = bsz * seq
    alpha = (2.0 * depth) ** 0.25
    assert t % MOE_TILE == 0 and t < (1 << POS_BITS)
    n_tiles_max = t // MOE_TILE + N_GROUPS
    consts = _mixer_consts()
    cosf, sins = _rotary_tables(seq)
    c_rows = jnp.pad(c, ((0, 16 - bsz), (0, 0)))
    mod = _adaln(c_rows, ada_w, ada_b)[:, :bsz]
    rw_pad = jnp.pad(router_w, ((0, 0), (0, LANES - N_EXPERTS)))
    rb_pad = jnp.pad(router_b, (0, LANES - N_EXPERTS)).reshape(1, LANES)
    xt = x.reshape(t, d)
    for l in range(depth):
        modl = mod[l].reshape(bsz, 6, d)
        gate_rows = jnp.broadcast_to(mlstm_gate_b[l][:, None], (2 * N_HEADS, CHUNK))
        w2_pad = jnp.zeros((LANES, GW), F32).at[2 * N_HEADS:2 * N_HEADS + GLA_RANK].set(gla_w2[l])
        y = _mixer(xt, modl, _bf(_regroup_w_in(w_in[l])), cosf, sins, mlstm_conv[l], gate_rows, w2_pad,
                   gla_b2[l], hgrn_lb, ret_norm[l], mlstm_norm[l], gla_norm[l], hgrn_norm[l], consts,
                   l, bsz, seq)
        x1, h2, codes, counts = _post(y, xt, modl, _bf(w_out[l]), ln1_g[l], ln1_b[l], rw_pad, rb_pad,
                                      seq, alpha)
        tile_group, n_tiles, row_start, zero_tiles = _tile_table(counts[0, :N_GROUPS], n_tiles_max)
        slots = _slots(codes.reshape(t), row_start)
        hs = _dispatch(slots, zero_tiles, h2, n_tiles_max)
        ys = _moe(tile_group, n_tiles, hs, rw_pad, rb_pad, _bf(exp_w1[l]), _bf(exp_w3[l]), _bf(exp_w2[l]))
        xt = _combine(slots, ys, x1, modl, ln2_g[l], ln2_b[l], seq, alpha)
    return xt.reshape(bsz, seq, d)
```

```python
import functools

import numpy as np
import jax
import jax.numpy as jnp
from jax import lax
from jax.experimental import pallas as pl
from jax.experimental.pallas import tpu as pltpu

F32 = jnp.float32
BF16 = jnp.bfloat16

D_MODEL = 1024
N_MIXERS = 4
GW = D_MODEL // N_MIXERS
N_HEADS = 4
HEAD_DIM = GW // N_HEADS
CONV_WIDTH = 4
GLA_RANK = 16
GLA_TAU = 16.0
ROPE_BASE = 10000.0
N_GROUPS = 4
EXPERTS_PER_GROUP = 4
N_EXPERTS = 16
D_EXPERT = D_MODEL // 2
LN_EPS = 1e-5
NORM_EPS = 1e-6
QK_SCALE = HEAD_DIM ** -0.5

LANES = 128
CHUNK = 128
LEVELS = 7
CONV_PAD = 8
NPROJ = 16 * GW + LANES
EXTRA = 16 * GW
VMEM_LIMIT = 56 * 1024 * 1024

(R_Q, R_K, R_V, R_G, M_Q, M_K, M_V, M_O, G_Q, G_K, G_V, G_G, H_Q, H_F, H_I, H_G) = (
    GW * i for i in range(16))


def _bf(x):
    return x.astype(BF16)


def _dot(a, b):
    return jnp.dot(a, b, preferred_element_type=F32)


def _dot_nt(a, b):
    return lax.dot_general(a, b, (((1,), (1,)), ((), ())), preferred_element_type=F32)


def _split2(x):
    hi = _bf(x)
    lo = _bf(x - hi.astype(F32))
    return hi, lo


def _split3(x):
    hi = _bf(x)
    r = x - hi.astype(F32)
    mid = _bf(r)
    lo = _bf(r - mid.astype(F32))
    return hi, mid, lo


def _dot_f32(a, b):
    ah, al = _split2(a)
    bh, bl = _split2(b)
    return _dot(ah, bh) + _dot(ah, bl) + _dot(al, bh)


def _sigmoid(x):
    return 0.5 * jnp.tanh(0.5 * x) + 0.5


def _silu(x):
    return x * _sigmoid(x)


def _log_sigmoid(x):
    return jnp.minimum(x, 0.0) - jnp.log1p(jnp.exp(-jnp.abs(x)))


def _layer_norm(v, g, b):
    mu = jnp.mean(v, axis=-1, keepdims=True)
    d = v - mu
    var = jnp.mean(d * d, axis=-1, keepdims=True)
    return d * lax.rsqrt(var + LN_EPS) * g + b


def _adaln_kernel(c_ref, w_ref, b_ref, o_ref):
    cond = _silu(c_ref[...])
    o_ref[0] = _dot_f32(cond, w_ref[0]) + b_ref[0]


def _adaln(c, ada_w, ada_b):
    depth, d, n = ada_w.shape
    bsz = c.shape[0]
    tn = 1536
    return pl.pallas_call(
        _adaln_kernel,
        grid=(depth, n // tn),
        in_specs=[
            pl.BlockSpec((bsz, d), lambda l, j: (0, 0)),
            pl.BlockSpec((1, d, tn), lambda l, j: (l, 0, j)),
            pl.BlockSpec((1, 1, tn), lambda l, j: (l, 0, j)),
        ],
        out_specs=pl.BlockSpec((1, bsz, tn), lambda l, j: (l, 0, j)),
        out_shape=jax.ShapeDtypeStruct((depth, bsz, n), F32),
        compiler_params=pltpu.CompilerParams(
            dimension_semantics=("arbitrary", "arbitrary"), vmem_limit_bytes=VMEM_LIMIT),
        name="adaln",
    )(c, ada_w, ada_b.reshape(depth, 1, n))


def _mixer_consts():
    c = CHUNK
    i = np.arange(c)[:, None]
    j = np.arange(c)[None, :]
    rel = i - j
    log_gamma = np.log1p(-np.exp2(-5.0 - np.arange(N_HEADS, dtype=np.float64)))
    dret = np.concatenate(
        [np.where(rel >= 0, np.exp(log_gamma[h] * np.maximum(rel, 0)), 0.0) for h in range(N_HEADS)], axis=1)
    lane_head = np.arange(GW) // HEAD_DIM
    dq = np.exp(log_gamma[lane_head][None, :] * (np.arange(c)[:, None] + 1.0))
    dk = np.exp(log_gamma[lane_head][None, :] * (c - 1.0 - np.arange(c)[:, None]))
    gc = np.exp(log_gamma[lane_head] * c)[None, :]
    lv = np.zeros((c, c), np.int32)
    lv[rel == 0] = 1
    spans = [(rel >= 0), (rel < 0)]
    code, b = 2, 1
    while b < c:
        lv[(i // (2 * b) == j // (2 * b)) & ((i // b) % 2 == 1) & ((j // b) % 2 == 0)] = code
        ref = (i // (2 * b)) * 2 * b + b - 1
        right = (i // b) % 2 == 1
        spans.append(np.where(right, (j > ref) & (j <= i), (j > i) & (j <= ref)))
        code, b = code + 1, b * 2
    lv = np.tile(lv, (1, N_HEADS))
    span = np.concatenate(spans, axis=0).astype(np.float32)
    span = np.tile(span, (1, 3))
    hh = (lane_head[:, None] == lane_head[None, :]).astype(np.float32)
    ones_bd = np.repeat(np.eye(N_HEADS, dtype=np.float32), c, axis=0)
    ones_bd = np.repeat(ones_bd, HEAD_DIM, axis=1)
    return dict(
        dret=jnp.asarray(dret, F32), dq=jnp.asarray(dq, F32), dk=jnp.asarray(dk, F32),
        gc=jnp.asarray(gc, F32), lv=jnp.asarray(lv), hh=jnp.asarray(hh, BF16),
        bdf=jnp.asarray(hh, F32), ones_bd=jnp.asarray(ones_bd, BF16), span=jnp.asarray(span, BF16))


def _lane_cumsum(x):
    lane = lax.broadcasted_iota(jnp.int32, x.shape, 1)
    s = 1
    while s < x.shape[1]:
        x = x + jnp.where(lane >= s, pltpu.roll(x, s, 1), 0.0)
        s *= 2
    return x


def _lane_cummax(x):
    lane = lax.broadcasted_iota(jnp.int32, x.shape, 1)
    s = 1
    while s < x.shape[1]:
        x = jnp.maximum(x, jnp.where(lane >= s, pltpu.roll(x, s, 1), -jnp.inf))
        s *= 2
    return x


def _mixer_kernel(layer, nb,
                  x_ref, mod_ref, win_ref, cos_ref, sin_ref, conv_ref, gb_ref, w2_ref, b2_ref, lb_ref,
                  retg_ref, mlg_ref, glag_ref, hgg_ref,
                  dret_ref, dq_ref, dk_ref, gc_ref, lv_ref, hh_ref, bdf_ref, onesbd_ref, span_ref,
                  y_ref,
                  s_ret, s_cn, s_gla, s_hg, m_ml, conv_buf):
    c = CHUNK
    batches = range(nb)

    @pl.when(pl.program_id(1) == 0)
    def _():
        for s in (s_ret, s_cn, s_gla, s_hg, m_ml):
            s[...] = jnp.zeros_like(s)
        conv_buf[:, 0:CONV_PAD, :] = jnp.zeros((nb, CONV_PAD, 2 * GW), F32)

    lane_head = lax.broadcasted_iota(jnp.int32, (1, GW), 1) // HEAD_DIM
    head_rows = [(lane_head == h).astype(BF16) for h in range(N_HEADS)]
    hh = hh_ref[...]
    bdf = bdf_ref[...]
    lv = lv_ref[...]
    causal = lv >= 1

    def stack(xs):
        return jnp.concatenate(xs, axis=0)

    def tiled(x):
        return stack([x] * nb)

    def seq(x, b):
        return x[b * c:(b + 1) * c]

    hb = _bf(stack([x_ref[b] * (1.0 + mod_ref[b, 1:2, :]) + mod_ref[b, 0:1, :] for b in batches]))

    issued = {}

    def issue(first):
        issued[first] = _dot(hb, win_ref[:, first:first + N_MIXERS * GW])

    def col(off, width=GW):
        first = off // (N_MIXERS * GW) * (N_MIXERS * GW)
        return issued[first][:, off - first:off - first + width]

    issue(R_Q)
    issue(M_Q)
    ext = _dot(hb, win_ref[:, EXTRA:EXTRA + LANES])

    def put(off, val):
        for b in batches:
            y_ref[b, :, off:off + GW] = seq(val, b).astype(y_ref.dtype)

    def stackmask(xb):
        return jnp.concatenate([xb * head_rows[h] for h in range(N_HEADS)], axis=0)

    def expand_heads(cols, first):
        out = jnp.zeros((cols.shape[0], GW), F32)
        for h in range(N_HEADS):
            out = jnp.where(lane_head == h, cols[:, first + h:first + h + 1], out)
        return out

    def head_mean(v):
        hi, lo = _split2(v)
        return (_dot(hi, hh) + _dot(lo, hh)) * (1.0 / HEAD_DIM)

    def head_norm(v, gain, center):
        if center:
            v = v - head_mean(v)
        return v * lax.rsqrt(head_mean(v * v) + NORM_EPS) * gain

    cosf = tiled(cos_ref[...])
    sins = tiled(sin_ref[...])
    half = lax.broadcasted_iota(jnp.int32, (1, GW), 1) % HEAD_DIM < HEAD_DIM // 2

    def rotary(v):
        swapped = jnp.where(half, pltpu.roll(v, GW - HEAD_DIM // 2, 1), pltpu.roll(v, HEAD_DIM // 2, 1))
        return v * cosf + swapped * sins

    rq = _bf(rotary(col(R_Q)) * QK_SCALE)
    rk = rotary(col(R_K))
    rkb = _bf(rk)
    rvb = _bf(col(R_V))
    rkd = rk * tiled(dk_ref[...])
    dret = dret_ref[...]
    dq = dq_ref[...]
    gc = gc_ref[...]
    sc = [_dot_nt(seq(rq, b), stackmask(seq(rkb, b))) for b in batches]
    inter = [_dot(seq(rq, b), _bf(s_ret[b])) for b in batches]
    scb = [_bf(s * dret) for s in sc]
    ro = stack([_dot(scb[b], stackmask(seq(rvb, b))) + inter[b] * dq for b in batches])
    upd = [_dot(_bf(seq(rkd, b).T), seq(rvb, b)) for b in batches]
    for b in batches:
        s_ret[b] = s_ret[b] * gc + upd[b] * bdf
    put(0, head_norm(ro, retg_ref[...], True) * _silu(col(R_G)))

    issue(G_Q)
    mqk = col(M_Q, 2 * GW)
    for b in batches:
        conv_buf[b, CONV_PAD:CONV_PAD + c, :] = seq(mqk, b)
    qk = jnp.zeros((nb * c, 2 * GW), F32)
    for t in range(CONV_WIDTH):
        s = CONV_WIDTH - 1 - t
        qk = qk + stack([conv_buf[b, CONV_PAD - s:CONV_PAD - s + c, :] for b in batches]) * conv_ref[t:t + 1, :]
    for b in batches:
        conv_buf[b, 0:CONV_PAD, :] = conv_buf[b, c:c + CONV_PAD, :]
    qk = _silu(qk)
    mq = _bf(qk[:, 0:GW])
    mk = qk[:, GW:2 * GW] * QK_SCALE
    mkb = _bf(mk)
    mvb = _bf(col(M_V))

    pre = stack([seq(ext, b).T[0:8, :] for b in batches]) + tiled(gb_ref[...])
    row8 = lax.broadcasted_iota(jnp.int32, (8 * nb, c), 0)
    valid = row8 % 8 >= N_HEADS
    lf = jnp.where(valid, _log_sigmoid(pre), 0.0)
    gi = jnp.where(valid, pltpu.roll(pre, N_HEADS, 0), 0.0)
    bcum = _lane_cumsum(lf)
    a = gi - bcum
    m_prev = m_ml[...]
    mrow = jnp.maximum(m_prev, _lane_cummax(a))
    m_last = jnp.broadcast_to(mrow[:, c - 1:c], (8 * nb, c))
    b_last = jnp.broadcast_to(bcum[:, c - 1:c], (8 * nb, c))
    wi = jnp.exp(m_prev - mrow)
    em = jnp.exp(-(bcum + mrow))
    wk = jnp.exp(a - m_last)
    dec = jnp.exp(m_prev - m_last)
    m_ml[...] = jnp.where(valid, b_last + m_last, 0.0)
    kinds = [mrow, wi, em, wk]
    if 32 * nb < c:
        kinds.append(jnp.zeros((c - 32 * nb, c), F32))
    cols = stack(kinds).T

    def first_col(kind, b):
        return kind * 8 * nb + 8 * b + N_HEADS

    mexp = [jnp.concatenate([jnp.broadcast_to(cols[:, first_col(0, b) + h:first_col(0, b) + h + 1], (c, c))
                             for h in range(N_HEADS)], axis=1) for b in batches]
    a_row = [jnp.concatenate([a[8 * b + N_HEADS + h:8 * b + N_HEADS + h + 1, :] for h in range(N_HEADS)], axis=1)
             for b in batches]
    qkm = [_dot_nt(seq(mq, b), stackmask(seq(mkb, b))) for b in batches]
    inter = [_dot(seq(mq, b), _bf(s_cn[b])) for b in batches]
    sm = [_bf(qkm[b] * jnp.exp(jnp.where(causal, a_row[b] - mexp[b], -jnp.inf))) for b in batches]
    onesbd = onesbd_ref[...]
    wi_l = stack([expand_heads(cols, first_col(1, b)) for b in batches])
    em_l = stack([expand_heads(cols, first_col(2, b)) for b in batches])
    wk_l = stack([expand_heads(cols, first_col(3, b)) for b in batches])
    nd = (stack([_dot(sm[b], jnp.concatenate([stackmask(seq(mvb, b)), onesbd], axis=1)) for b in batches])
          + jnp.concatenate([wi_l, wi_l], axis=1) * stack(inter))
    mh = nd[:, 0:GW] / jnp.maximum(jnp.abs(nd[:, GW:2 * GW]), em_l)
    kt = mk * wk_l
    ones = jnp.ones((c, GW), BF16)
    upd = [_dot(_bf(seq(kt, b).T), jnp.concatenate([seq(mvb, b), ones], axis=1)) for b in batches]
    dec2 = jnp.concatenate([dec, dec], axis=1)
    bdf2 = jnp.concatenate([bdf, bdf], axis=1)
    for b in batches:
        dec_l = jnp.zeros((1, GW), F32)
        for h in range(N_HEADS):
            dec_l = jnp.where(lane_head == h, dec2[8 * b + N_HEADS + h:8 * b + N_HEADS + h + 1, :], dec_l)
        s_cn[b] = s_cn[b] * jnp.concatenate([dec_l, dec_l], axis=1) + upd[b] * bdf2
    put(GW, head_norm(mh * _sigmoid(col(M_O)), mlg_ref[...], True))

    span = span_ref[...]
    level_masks = [lv == code for code in range(2, 2 + LEVELS)]

    def decay_attention(q, k, v, g, st_ref):
        g3 = _split3(g)
        dec = [jnp.exp(_dot(span, stack([seq(part, b) for part in g3]))) for b in batches]
        e_cum = stack([d[0:c] for d in dec])
        e_end = stack([d[c:2 * c] for d in dec])
        qe = _bf(q * e_cum)
        out = stack([_dot_nt(seq(qe, b), _bf(st_ref[b])) for b in batches])
        out = out + _dot(_bf(q * k), hh) * v
        scores = [jnp.zeros((c, N_HEADS * c), F32) for _ in batches]
        for lvl in range(LEVELS):
            e = stack([d[(2 + lvl) * c:(3 + lvl) * c] for d in dec])
            qb = _bf(q * e)
            kb = _bf(k * e)
            level = [_dot_nt(seq(qb, b), stackmask(seq(kb, b))) for b in batches]
            scores = [jnp.where(level_masks[lvl], level[b], scores[b]) for b in batches]
        vb = _bf(v)
        out = out + stack([_dot(_bf(scores[b]), stackmask(seq(vb, b))) for b in batches])
        ke = _bf(k * e_end)
        upd = [_dot(_bf(seq(v, b).T), seq(ke, b)) for b in batches]
        for b in batches:
            st_ref[b] = st_ref[b] * seq(e_cum, b)[c - 1:c, :] + upd[b] * bdf
        return out

    issue(H_Q)
    x_lr = _dot_f32(ext, w2_ref[...]) + b2_ref[...]
    log_a = _log_sigmoid(x_lr) * (1.0 / GLA_TAU)
    go = decay_attention(col(G_Q), col(G_K) * QK_SCALE, col(G_V), log_a, s_gla)
    put(2 * GW, head_norm(go, glag_ref[...], False) * _silu(col(G_G)))

    lb_all = lb_ref[...]
    lb_e = jnp.exp(lb_all - jnp.max(lb_all, axis=0, keepdims=True))
    lb_p = lb_e / jnp.sum(lb_e, axis=0, keepdims=True)
    lb = jnp.zeros((1, GW), F32)
    for l in range(1, layer + 1):
        lb = lb + lb_p[l:l + 1, :]
    z = col(H_F)
    x1 = jnp.log(lb)
    x2 = jnp.log1p(-lb) + _log_sigmoid(z)
    log_f = jnp.maximum(x1, x2) + jnp.log1p(jnp.exp(-jnp.abs(x1 - x2)))
    k_h = (1.0 - lb) * _sigmoid(-z)
    ho = decay_attention(_silu(col(H_Q)), k_h, col(H_I), log_f, s_hg)
    put(3 * GW, head_norm(ho * _sigmoid(col(H_G)), hgg_ref[...], False))


MIXER_BATCH = 4


def _mixer(xt, modl, w_in_p, cosf, sins, conv_w, gate_rows, w2_pad, b2, lb, ret_g, ml_g, gla_g, hg_g, consts,
           layer, bsz, seq):
    c = CHUNK
    nc = seq // c
    t, d = xt.shape
    row = lambda a: a.reshape(1, -1)
    full = lambda a: pl.BlockSpec(a.shape, lambda b, n: (0,) * a.ndim)
    cs = consts
    args = [xt.reshape(bsz, seq, d), modl, w_in_p, cosf, sins, conv_w, gate_rows, w2_pad, row(b2), lb,
            row(ret_g), row(ml_g), row(gla_g), row(hg_g),
            cs["dret"], cs["dq"], cs["dk"], cs["gc"], cs["lv"], cs["hh"], cs["bdf"], cs["ones_bd"], cs["span"]]
    nb = MIXER_BATCH
    in_specs = [
        pl.BlockSpec((nb, c, d), lambda b, n: (b, n, 0)),
        pl.BlockSpec((nb, 6, d), lambda b, n: (b, 0, 0)),
        full(w_in_p),
        pl.BlockSpec((c, GW), lambda b, n: (n, 0)),
        pl.BlockSpec((c, GW), lambda b, n: (n, 0)),
    ] + [full(a) for a in args[5:]]
    y = pl.pallas_call(
        functools.partial(_mixer_kernel, layer, nb),
        grid=(bsz // nb, nc),
        in_specs=in_specs,
        out_specs=pl.BlockSpec((nb, c, D_MODEL), lambda b, n: (b, n, 0)),
        out_shape=jax.ShapeDtypeStruct((bsz, seq, D_MODEL), BF16),
        scratch_shapes=[
            pltpu.VMEM((nb, GW, GW), F32),
            pltpu.VMEM((nb, GW, 2 * GW), F32),
            pltpu.VMEM((nb, GW, GW), F32),
            pltpu.VMEM((nb, GW, GW), F32),
            pltpu.VMEM((nb * 8, c), F32),
            pltpu.VMEM((nb, c + CONV_PAD, 2 * GW), F32),
        ],
        compiler_params=pltpu.CompilerParams(
            dimension_semantics=("arbitrary", "arbitrary"), vmem_limit_bytes=VMEM_LIMIT),
        name="mixer",
    )(*args)
    return y.reshape(t, D_MODEL)


def _group_shift(v, pos, k):
    return jnp.where(pos < EXPERTS_PER_GROUP - k,
                     pltpu.roll(v, LANES - k, 1), pltpu.roll(v, EXPERTS_PER_GROUP - k, 1))


def _router(h, rw, rb):
    lane = lax.broadcasted_iota(jnp.int32, (1, LANES), 1)
    real = lane < N_EXPERTS
    logits = jnp.where(real, _dot_f32(h, rw), -jnp.inf)
    ex = jnp.exp(logits - jnp.max(logits, axis=-1, keepdims=True))
    probs = ex / jnp.sum(ex, axis=-1, keepdims=True)
    sel = jnp.where(real, probs + rb, -jnp.inf)
    pos = lane % EXPERTS_PER_GROUP
    gid = lane // EXPERTS_PER_GROUP
    r1 = _group_shift(sel, pos, 1)
    r2 = _group_shift(sel, pos, 2)
    r3 = _group_shift(sel, pos, 3)
    pair = jnp.maximum(jnp.maximum(jnp.maximum(sel + r1, sel + r2), jnp.maximum(sel + r3, r1 + r2)),
                       jnp.maximum(r1 + r3, r2 + r3))
    pair = jnp.where(real, pair, -jnp.inf)
    best = jnp.max(pair, axis=-1, keepdims=True)
    first = jnp.min(jnp.where(pair == best, gid, N_GROUPS), axis=-1, keepdims=True)
    rank = jnp.zeros(sel.shape, jnp.int32)
    for k, r in ((1, r1), (2, r2), (3, r3)):
        beats = (r > sel) | ((r == sel) & (pos + k >= EXPERTS_PER_GROUP))
        rank = rank + beats.astype(jnp.int32)
    return probs, first, rank


def _post_kernel(alpha, y_ref, x_ref, mod_ref, wout_ref, lng_ref, lnb_ref, rw_ref, rb_ref,
                 stri_ref, x1_ref, h2_ref, slots_ref, counts_ref, carry_ref):
    tm = y_ref.shape[0]

    @pl.when(pl.program_id(0) == 0)
    def _():
        carry_ref[...] = jnp.zeros_like(carry_ref)

    g1 = mod_ref[0, 2:3, :]
    sh2 = mod_ref[0, 3:4, :]
    sc2 = mod_ref[0, 4:5, :]
    mix = _dot(_bf(y_ref[...]), wout_ref[...])
    x1 = _layer_norm(alpha * x_ref[...] + g1 * mix, lng_ref[...], lnb_ref[...])
    x1_ref[...] = x1
    h = x1 * (1.0 + sc2) + sh2
    h2_ref[...] = h

    _, first, rank = _router(h, rw_ref[...], rb_ref[...])
    lane = lax.broadcasted_iota(jnp.int32, (1, LANES), 1)
    chosen = (lane // EXPERTS_PER_GROUP == first) & (rank < 2)
    bit = jnp.left_shift(1, lane % EXPERTS_PER_GROUP).astype(F32)
    bits = jnp.sum(jnp.where(chosen, bit, 0.0), axis=-1, keepdims=True)
    pair = jnp.full(bits.shape, len(PAIR_BITS) - 1, jnp.int32)
    for index in range(len(PAIR_BITS) - 2, -1, -1):
        pair = jnp.where(bits == float(PAIR_BITS[index]), index, pair)
    cls = first * len(PAIR_BITS) + pair
    onehot = jnp.where(lane == cls, 1.0, 0.0)
    before = _dot(stri_ref[...], _bf(onehot)) + carry_ref[...]
    code = cls.astype(F32) * float(1 << POS_BITS) + jnp.sum(onehot * before, axis=-1, keepdims=True)
    slots_ref[0] = jnp.broadcast_to(code, (tm, LANES)).T[0:1, :].astype(jnp.int32)
    carry_ref[...] += jnp.sum(onehot, axis=0, keepdims=True)
    counts_ref[...] = carry_ref[...].astype(jnp.int32)


def _post(y, xt, modl, w_out, ln_g, ln_b, rw_pad, rb_pad, seq, alpha):
    t, d = xt.shape
    tm = 256
    per_batch = seq // tm
    row = lambda a: a.reshape(1, -1)
    tile = pl.BlockSpec((tm, d), lambda i: (i, 0))
    full = lambda a: pl.BlockSpec(a.shape, lambda i: (0,) * a.ndim)
    stri = jnp.asarray(np.tril(np.ones((tm, tm), np.float32), -1), BF16)
    args = [y, xt, modl, w_out, row(ln_g), row(ln_b), rw_pad, rb_pad, stri]
    return pl.pallas_call(
        functools.partial(_post_kernel, alpha),
        grid=(t // tm,),
        in_specs=[tile, tile, pl.BlockSpec((1, 6, d), lambda i: (i // per_batch, 0, 0))]
        + [full(a) for a in args[3:]],
        out_specs=[tile, tile, pl.BlockSpec((1, 1, tm), lambda i: (i, 0, 0)),
                   pl.BlockSpec((1, LANES), lambda i: (0, 0))],
        out_shape=[jax.ShapeDtypeStruct((t, d), F32), jax.ShapeDtypeStruct((t, d), F32),
                   jax.ShapeDtypeStruct((t // tm, 1, tm), jnp.int32),
                   jax.ShapeDtypeStruct((1, LANES), jnp.int32)],
        scratch_shapes=[pltpu.VMEM((1, LANES), F32)],
        compiler_params=pltpu.CompilerParams(
            dimension_semantics=("arbitrary",), vmem_limit_bytes=VMEM_LIMIT),
        name="post",
    )(*args)


MOE_TILE = 512
POS_BITS = 15
PAIR_BITS = (0b0011, 0b0101, 0b1001, 0b1010, 0b0110, 0b1100)
N_CLASSES = N_GROUPS * len(PAIR_BITS)


def _dispatch_kernel(slots_ref, h_ref, hs_hbm, sem):
    tm = h_ref.shape[0]
    i = pl.program_id(0)

    def issue(w, carry):
        for r in range(8):
            row = w * 8 + r
            s = slots_ref[i * tm + row]
            pltpu.make_async_copy(h_ref.at[pl.ds(row, 1)], hs_hbm.at[pl.ds(s, 1)], sem).start()
        return carry

    lax.fori_loop(0, tm // 8, issue, 0)
    pltpu.make_async_copy(h_ref, hs_hbm.at[pl.ds(0, tm)], sem).wait()


def _dispatch(slots, h2):
    t, d = h2.shape
    tm = MOE_TILE
    return pl.pallas_call(
        _dispatch_kernel,
        grid_spec=pltpu.PrefetchScalarGridSpec(
            num_scalar_prefetch=1,
            grid=(t // tm,),
            in_specs=[pl.BlockSpec((tm, d), lambda i, s: (i, 0))],
            out_specs=pl.BlockSpec(memory_space=pl.ANY),
            scratch_shapes=[pltpu.SemaphoreType.DMA(())],
        ),
        out_shape=jax.ShapeDtypeStruct((t, d), F32),
        compiler_params=pltpu.CompilerParams(
            dimension_semantics=("arbitrary",), vmem_limit_bytes=VMEM_LIMIT),
        name="dispatch",
    )(slots, h2)


def _moe_kernel(item_tile_ref, item_expert_ref, item_first_ref, n_items_ref,
                hs_ref, rw_ref, rb_ref, w1_ref, w3_ref, w2_ref, ys_ref, gates_ref, hb_ref):
    s = pl.program_id(0)

    @pl.when(s < n_items_ref[0])
    def _():
        lane = lax.broadcasted_iota(jnp.int32, (1, LANES), 1)
        first_of_tile = item_first_ref[s] == 1

        @pl.when(first_of_tile)
        def _():
            h = hs_ref[...]
            probs, first, rank = _router(h, rw_ref[...], rb_ref[...])
            w = jnp.where((lane // EXPERTS_PER_GROUP == first) & (rank < 2), probs, 0.0)
            gates_ref[...] = w / jnp.sum(w, axis=-1, keepdims=True)
            hb_ref[...] = _bf(h)

        ge = jnp.sum(jnp.where(lane == item_expert_ref[s], gates_ref[...], 0.0), axis=-1, keepdims=True)
        hb = hb_ref[...]
        a = _silu(_dot(hb, _bf(w1_ref[0]))) * _dot(hb, _bf(w3_ref[0]))
        out = _dot(_bf(a * ge), _bf(w2_ref[0]))

        @pl.when(first_of_tile)
        def _():
            ys_ref[...] = out

        @pl.when(jnp.logical_not(first_of_tile))
        def _():
            ys_ref[...] += out


def _moe(items, hs, rw_pad, rb_pad, w1, w3, w2):
    item_tile, item_expert, item_first, n_items = items
    rows, d = hs.shape
    tile = pl.BlockSpec((MOE_TILE, d), lambda s, it, ie, fi, n: (it[s], 0))
    full = lambda a: pl.BlockSpec(a.shape, lambda s, it, ie, fi, n: (0,) * a.ndim)
    up = pl.BlockSpec((1, d, D_EXPERT), lambda s, it, ie, fi, n: (ie[s], 0, 0))
    down = pl.BlockSpec((1, D_EXPERT, d), lambda s, it, ie, fi, n: (ie[s], 0, 0))
    return pl.pallas_call(
        _moe_kernel,
        grid_spec=pltpu.PrefetchScalarGridSpec(
            num_scalar_prefetch=4,
            grid=(item_tile.shape[0],),
            in_specs=[tile, full(rw_pad), full(rb_pad), up, up, down],
            out_specs=tile,
            scratch_shapes=[pltpu.VMEM((MOE_TILE, LANES), F32), pltpu.VMEM((MOE_TILE, d), BF16)],
        ),
        out_shape=jax.ShapeDtypeStruct((rows, d), F32),
        compiler_params=pltpu.CompilerParams(
            dimension_semantics=("arbitrary",), vmem_limit_bytes=VMEM_LIMIT),
        name="moe",
    )(item_tile, item_expert, item_first, n_items, hs, rw_pad, rb_pad, w1, w3, w2)


def _class_experts():
    member = np.zeros((N_CLASSES, N_EXPERTS), np.int32)
    for g in range(N_GROUPS):
        for p, bits in enumerate(PAIR_BITS):
            for e in range(EXPERTS_PER_GROUP):
                if bits >> e & 1:
                    member[g * len(PAIR_BITS) + p, g * EXPERTS_PER_GROUP + e] = 1
    return member


def _work_items(counts, n_tokens):
    n_tiles = n_tokens // MOE_TILE
    n_items_max = 2 * (n_tiles + 2 * N_CLASSES)
    ends = jnp.cumsum(counts)
    starts = ends - counts
    lo = jnp.arange(n_tiles, dtype=jnp.int32)[:, None] * MOE_TILE
    overlap = (counts[None, :] > 0) & (starts[None, :] < lo + MOE_TILE) & (ends[None, :] > lo)
    needed = (overlap.astype(jnp.int32) @ jnp.asarray(_class_experts())) > 0
    flat = needed.reshape(-1)
    n_items = jnp.sum(flat.astype(jnp.int32))
    place = jnp.where(flat, jnp.cumsum(flat.astype(jnp.int32)) - 1, n_items_max)
    ids = jnp.zeros((n_items_max,), jnp.int32).at[place].set(
        jnp.arange(flat.shape[0], dtype=jnp.int32), mode="drop")
    ids = ids[jnp.minimum(jnp.arange(n_items_max), n_items - 1)]
    item_tile = ids // N_EXPERTS
    item_expert = ids % N_EXPERTS
    item_first = jnp.concatenate([jnp.ones((1,), jnp.int32),
                                  (item_tile[1:] != item_tile[:-1]).astype(jnp.int32)])
    return starts.astype(jnp.int32), (item_tile, item_expert, item_first, n_items.reshape(1))


def _combine_kernel(alpha, slots_ref, ys_hbm, x1_ref, mod_ref, lng_ref, lnb_ref, o_ref, buf, sem):
    tm = o_ref.shape[0]
    i = pl.program_id(0)
    n = pl.num_programs(0)

    def gather(tile, b):
        def body(w, carry):
            for r in range(8):
                row = w * 8 + r
                s = slots_ref[tile * tm + row]
                pltpu.make_async_copy(ys_hbm.at[pl.ds(s, 1)], buf.at[b, pl.ds(row, 1)], sem.at[b]).start()
            return carry
        lax.fori_loop(0, tm // 8, body, 0)

    @pl.when(i == 0)
    def _():
        gather(0, 0)

    @pl.when(i + 1 < n)
    def _():
        gather(i + 1, (i + 1) % 2)

    cur = i % 2
    pltpu.make_async_copy(ys_hbm.at[pl.ds(0, tm)], buf.at[cur], sem.at[cur]).wait()
    g2 = mod_ref[0, 5:6, :]
    o_ref[...] = _layer_norm(alpha * x1_ref[...] + g2 * buf[cur], lng_ref[...], lnb_ref[...])


def _combine(slots, ys, x1, modl, ln_g, ln_b, seq, alpha):
    t, d = x1.shape
    tm = 256
    per_batch = seq // tm
    row = lambda a: a.reshape(1, -1)
    tile = pl.BlockSpec((tm, d), lambda i, s: (i, 0))
    vec = pl.BlockSpec((1, d), lambda i, s: (0, 0))
    return pl.pallas_call(
        functools.partial(_combine_kernel, alpha),
        grid_spec=pltpu.PrefetchScalarGridSpec(
            num_scalar_prefetch=1,
            grid=(t // tm,),
            in_specs=[pl.BlockSpec(memory_space=pl.ANY), tile,
                      pl.BlockSpec((1, 6, d), lambda i, s: (i // per_batch, 0, 0)), vec, vec],
            out_specs=tile,
            scratch_shapes=[pltpu.VMEM((2, tm, d), F32), pltpu.SemaphoreType.DMA((2,))],
        ),
        out_shape=jax.ShapeDtypeStruct((t, d), F32),
        compiler_params=pltpu.CompilerParams(
            dimension_semantics=("arbitrary",), vmem_limit_bytes=VMEM_LIMIT),
        name="combine",
    )(slots, ys, x1, modl, row(ln_g), row(ln_b))


def _regroup_w_in(w):
    a = 8 * GW
    b = a + 2 * N_HEADS
    cc = b + 4 * GW
    dd = cc + GLA_RANK
    pad = jnp.zeros((w.shape[0], LANES - 2 * N_HEADS - GLA_RANK), w.dtype)
    return jnp.concatenate([w[:, :a], w[:, b:cc], w[:, dd:], w[:, a:b], w[:, cc:dd], pad], axis=1)


def _rotary_tables(seq):
    inv = ROPE_BASE ** (-jnp.arange(0, HEAD_DIM, 2, dtype=F32) / HEAD_DIM)
    ang = jnp.arange(seq, dtype=F32)[:, None] * inv[None, :]
    cos = jnp.cos(ang)
    sin = jnp.sin(ang)
    cosf = jnp.tile(jnp.concatenate([cos, cos], axis=1), (1, N_HEADS))
    sins = jnp.tile(jnp.concatenate([-sin, sin], axis=1), (1, N_HEADS))
    return cosf, sins


def kernel(x, c, ada_w, ada_b, w_in, mlstm_conv, mlstm_gate_b, gla_w2, gla_b2, hgrn_lb, ret_norm, mlstm_norm, gla_norm, hgrn_norm, w_out, ln1_g, ln1_b, router_w, router_b, exp_w1, exp_w3, exp_w2, ln2_g, ln2_b):
    bsz, seq, d = x.shape
    depth = ada_w.shape[0]
    assert d == D_MODEL and seq % 512 == 0 and bsz <= 16
    t = bsz * seq
    alpha = (2.0 * depth) ** 0.25
    assert t % MOE_TILE == 0 and t < (1 << POS_BITS) and bsz % MIXER_BATCH == 0
    consts = _mixer_consts()
    cosf, sins = _rotary_tables(seq)
    c_rows = jnp.pad(c, ((0, 16 - bsz), (0, 0)))
    mod = _adaln(c_rows, ada_w, ada_b)[:, :bsz]
    rw_pad = jnp.pad(router_w, ((0, 0), (0, LANES - N_EXPERTS)))
    rb_pad = jnp.pad(router_b, (0, LANES - N_EXPERTS)).reshape(1, LANES)
    xt = x.reshape(t, d)
    for l in range(depth):
        modl = mod[l].reshape(bsz, 6, d)
        gate_rows = jnp.broadcast_to(mlstm_gate_b[l][:, None], (2 * N_HEADS, CHUNK))
        w2_pad = jnp.zeros((LANES, GW), F32).at[2 * N_HEADS:2 * N_HEADS + GLA_RANK].set(gla_w2[l])
        y = _mixer(xt, modl, _bf(_regroup_w_in(w_in[l])), cosf, sins, mlstm_conv[l], gate_rows, w2_pad,
                   gla_b2[l], hgrn_lb, ret_norm[l], mlstm_norm[l], gla_norm[l], hgrn_norm[l], consts,
                   l, bsz, seq)
        x1, h2, codes, counts = _post(y, xt, modl, _bf(w_out[l]), ln1_g[l], ln1_b[l], rw_pad, rb_pad,
                                      seq, alpha)
        class_start, items = _work_items(counts[0, :N_CLASSES], t)
        codes = codes.reshape(t)
        slots = class_start[codes >> POS_BITS] + (codes & ((1 << POS_BITS) - 1))
        hs = _dispatch(slots, h2)
        ys = _moe(items, hs, rw_pad, rb_pad, exp_w1[l], exp_w3[l], exp_w2[l])
        xt = _combine(slots, ys, x1, modl, ln2_g[l], ln2_b[l], seq, alpha)
    return xt.reshape(bsz, seq, d)
```

```python
import functools

import numpy as np
import jax
import jax.numpy as jnp
from jax import lax
from jax.experimental import pallas as pl
from jax.experimental.pallas import tpu as pltpu

F32 = jnp.float32
BF16 = jnp.bfloat16

D_MODEL = 1024
N_MIXERS = 4
GW = D_MODEL // N_MIXERS
N_HEADS = 4
HEAD_DIM = GW // N_HEADS
CONV_WIDTH = 4
GLA_RANK = 16
GLA_TAU = 16.0
ROPE_BASE = 10000.0
N_GROUPS = 4
EXPERTS_PER_GROUP = 4
N_EXPERTS = 16
D_EXPERT = D_MODEL // 2
LN_EPS = 1e-5
NORM_EPS = 1e-6
QK_SCALE = HEAD_DIM ** -0.5

LANES = 128
CHUNK = 128
LEVELS = 7
CONV_PAD = 8
NPROJ = 16 * GW + LANES
EXTRA = 16 * GW
VMEM_LIMIT = 56 * 1024 * 1024

(R_Q, R_K, R_V, R_G, M_Q, M_K, M_V, M_O, G_Q, G_K, G_V, G_G, H_Q, H_F, H_I, H_G) = (
    GW * i for i in range(16))


def _bf(x):
    return x.astype(BF16)


def _dot(a, b):
    return jnp.dot(a, b, preferred_element_type=F32)


def _dot_nt(a, b):
    return lax.dot_general(a, b, (((1,), (1,)), ((), ())), preferred_element_type=F32)


def _split2(x):
    hi = _bf(x)
    lo = _bf(x - hi.astype(F32))
    return hi, lo


def _split3(x):
    hi = _bf(x)
    r = x - hi.astype(F32)
    mid = _bf(r)
    lo = _bf(r - mid.astype(F32))
    return hi, mid, lo


def _dot_f32(a, b):
    ah, al = _split2(a)
    bh, bl = _split2(b)
    return _dot(ah, bh) + _dot(ah, bl) + _dot(al, bh)


def _sigmoid(x):
    return 0.5 * jnp.tanh(0.5 * x) + 0.5


def _silu(x):
    return x * _sigmoid(x)


def _log_sigmoid(x):
    return jnp.minimum(x, 0.0) - jnp.log1p(jnp.exp(-jnp.abs(x)))


def _layer_norm(v, g, b):
    mu = jnp.mean(v, axis=-1, keepdims=True)
    d = v - mu
    var = jnp.mean(d * d, axis=-1, keepdims=True)
    return d * lax.rsqrt(var + LN_EPS) * g + b


def _adaln_kernel(c_ref, w_ref, b_ref, o_ref):
    cond = _silu(c_ref[...])
    o_ref[0] = _dot_f32(cond, w_ref[0]) + b_ref[0]


def _adaln(c, ada_w, ada_b):
    depth, d, n = ada_w.shape
    bsz = c.shape[0]
    tn = 1536
    return pl.pallas_call(
        _adaln_kernel,
        grid=(depth, n // tn),
        in_specs=[
            pl.BlockSpec((bsz, d), lambda l, j: (0, 0)),
            pl.BlockSpec((1, d, tn), lambda l, j: (l, 0, j)),
            pl.BlockSpec((1, 1, tn), lambda l, j: (l, 0, j)),
        ],
        out_specs=pl.BlockSpec((1, bsz, tn), lambda l, j: (l, 0, j)),
        out_shape=jax.ShapeDtypeStruct((depth, bsz, n), F32),
        compiler_params=pltpu.CompilerParams(
            dimension_semantics=("arbitrary", "arbitrary"), vmem_limit_bytes=VMEM_LIMIT),
        name="adaln",
    )(c, ada_w, ada_b.reshape(depth, 1, n))


def _mixer_consts():
    c = CHUNK
    i = np.arange(c)[:, None]
    j = np.arange(c)[None, :]
    rel = i - j
    log_gamma = np.log1p(-np.exp2(-5.0 - np.arange(N_HEADS, dtype=np.float64)))
    dret = np.concatenate(
        [np.where(rel >= 0, np.exp(log_gamma[h] * np.maximum(rel, 0)), 0.0) for h in range(N_HEADS)], axis=1)
    lane_head = np.arange(GW) // HEAD_DIM
    dq = np.exp(log_gamma[lane_head][None, :] * (np.arange(c)[:, None] + 1.0))
    dk = np.exp(log_gamma[lane_head][None, :] * (c - 1.0 - np.arange(c)[:, None]))
    gc = np.exp(log_gamma[lane_head] * c)[None, :]
    lv = np.zeros((c, c), np.int32)
    lv[rel == 0] = 1
    spans = [(rel >= 0), (rel < 0)]
    code, b = 2, 1
    while b < c:
        lv[(i // (2 * b) == j // (2 * b)) & ((i // b) % 2 == 1) & ((j // b) % 2 == 0)] = code
        ref = (i // (2 * b)) * 2 * b + b - 1
        right = (i // b) % 2 == 1
        spans.append(np.where(right, (j > ref) & (j <= i), (j > i) & (j <= ref)))
        code, b = code + 1, b * 2
    lv = np.tile(lv, (1, N_HEADS))
    span = np.concatenate(spans, axis=0).astype(np.float32)
    span = np.tile(span, (1, 3))
    hh = (lane_head[:, None] == lane_head[None, :]).astype(np.float32)
    ones_bd = np.repeat(np.eye(N_HEADS, dtype=np.float32), c, axis=0)
    ones_bd = np.repeat(ones_bd, HEAD_DIM, axis=1)
    return dict(
        dret=jnp.asarray(dret, F32), dq=jnp.asarray(dq, F32), dk=jnp.asarray(dk, F32),
        gc=jnp.asarray(gc, F32), lv=jnp.asarray(lv), hh=jnp.asarray(hh, BF16),
        bdf=jnp.asarray(hh, F32), ones_bd=jnp.asarray(ones_bd, BF16), span=jnp.asarray(span, BF16))


def _lane_cumsum(x):
    lane = lax.broadcasted_iota(jnp.int32, x.shape, 1)
    s = 1
    while s < x.shape[1]:
        x = x + jnp.where(lane >= s, pltpu.roll(x, s, 1), 0.0)
        s *= 2
    return x


def _lane_cummax(x):
    lane = lax.broadcasted_iota(jnp.int32, x.shape, 1)
    s = 1
    while s < x.shape[1]:
        x = jnp.maximum(x, jnp.where(lane >= s, pltpu.roll(x, s, 1), -jnp.inf))
        s *= 2
    return x


def _mixer_kernel(layer, nb,
                  x_ref, mod_ref, win_ref, cos_ref, sin_ref, conv_ref, gb_ref, w2_ref, b2_ref, lb_ref,
                  retg_ref, mlg_ref, glag_ref, hgg_ref,
                  dret_ref, dq_ref, dk_ref, gc_ref, lv_ref, hh_ref, bdf_ref, onesbd_ref, span_ref,
                  y_ref,
                  s_ret, s_cn, s_gla, s_hg, m_ml, conv_buf):
    c = CHUNK
    batches = range(nb)

    @pl.when(pl.program_id(1) == 0)
    def _():
        for s in (s_ret, s_cn, s_gla, s_hg, m_ml):
            s[...] = jnp.zeros_like(s)
        conv_buf[:, 0:CONV_PAD, :] = jnp.zeros((nb, CONV_PAD, 2 * GW), F32)

    lane_head = lax.broadcasted_iota(jnp.int32, (1, GW), 1) // HEAD_DIM
    head_rows = [(lane_head == h).astype(BF16) for h in range(N_HEADS)]
    hh = hh_ref[...]
    bdf = bdf_ref[...]
    lv = lv_ref[...]
    causal = lv >= 1

    def stack(xs):
        return jnp.concatenate(xs, axis=0)

    def tiled(x):
        return stack([x] * nb)

    def seq(x, b):
        return x[b * c:(b + 1) * c]

    hb = _bf(stack([x_ref[b] * (1.0 + mod_ref[b, 1:2, :]) + mod_ref[b, 0:1, :] for b in batches]))

    issued = {}

    def issue(first):
        issued[first] = _dot(hb, win_ref[:, first:first + N_MIXERS * GW])

    def col(off, width=GW):
        first = off // (N_MIXERS * GW) * (N_MIXERS * GW)
        return issued[first][:, off - first:off - first + width]

    issue(R_Q)
    issue(M_Q)
    ext = _dot(hb, win_ref[:, EXTRA:EXTRA + LANES])

    def put(off, val):
        for b in batches:
            y_ref[b, :, off:off + GW] = seq(val, b).astype(y_ref.dtype)

    def stackmask(xb):
        return jnp.concatenate([xb * head_rows[h] for h in range(N_HEADS)], axis=0)

    def expand_heads(cols, first):
        out = jnp.zeros((cols.shape[0], GW), F32)
        for h in range(N_HEADS):
            out = jnp.where(lane_head == h, cols[:, first + h:first + h + 1], out)
        return out

    def head_mean(v):
        hi, lo = _split2(v)
        return (_dot(hi, hh) + _dot(lo, hh)) * (1.0 / HEAD_DIM)

    def head_norm(v, gain, center):
        if center:
            v = v - head_mean(v)
        return v * lax.rsqrt(head_mean(v * v) + NORM_EPS) * gain

    cosf = tiled(cos_ref[...])
    sins = tiled(sin_ref[...])
    half = lax.broadcasted_iota(jnp.int32, (1, GW), 1) % HEAD_DIM < HEAD_DIM // 2

    def rotary(v):
        swapped = jnp.where(half, pltpu.roll(v, GW - HEAD_DIM // 2, 1), pltpu.roll(v, HEAD_DIM // 2, 1))
        return v * cosf + swapped * sins

    rq = _bf(rotary(col(R_Q)) * QK_SCALE)
    rk = rotary(col(R_K))
    rkb = _bf(rk)
    rvb = _bf(col(R_V))
    rkd = rk * tiled(dk_ref[...])
    dret = dret_ref[...]
    dq = dq_ref[...]
    gc = gc_ref[...]
    sc = [_dot_nt(seq(rq, b), stackmask(seq(rkb, b))) for b in batches]
    inter = [_dot(seq(rq, b), _bf(s_ret[b])) for b in batches]
    scb = [_bf(s * dret) for s in sc]
    ro = stack([_dot(scb[b], stackmask(seq(rvb, b))) + inter[b] * dq for b in batches])
    upd = [_dot(_bf(seq(rkd, b).T), seq(rvb, b)) for b in batches]
    for b in batches:
        s_ret[b] = s_ret[b] * gc + upd[b] * bdf
    put(0, head_norm(ro, retg_ref[...], True) * _silu(col(R_G)))

    issue(G_Q)
    mqk = col(M_Q, 2 * GW)
    for b in batches:
        conv_buf[b, CONV_PAD:CONV_PAD + c, :] = seq(mqk, b)
    qk = jnp.zeros((nb * c, 2 * GW), F32)
    for t in range(CONV_WIDTH):
        s = CONV_WIDTH - 1 - t
        qk = qk + stack([conv_buf[b, CONV_PAD - s:CONV_PAD - s + c, :] for b in batches]) * conv_ref[t:t + 1, :]
    for b in batches:
        conv_buf[b, 0:CONV_PAD, :] = conv_buf[b, c:c + CONV_PAD, :]
    qk = _silu(qk)
    mq = _bf(qk[:, 0:GW])
    mk = qk[:, GW:2 * GW] * QK_SCALE
    mkb = _bf(mk)
    mvb = _bf(col(M_V))

    pre = stack([seq(ext, b).T[0:8, :] for b in batches]) + tiled(gb_ref[...])
    row8 = lax.broadcasted_iota(jnp.int32, (8 * nb, c), 0)
    valid = row8 % 8 >= N_HEADS
    lf = jnp.where(valid, _log_sigmoid(pre), 0.0)
    gi = jnp.where(valid, pltpu.roll(pre, N_HEADS, 0), 0.0)
    bcum = _lane_cumsum(lf)
    a = gi - bcum
    m_prev = m_ml[...]
    mrow = jnp.maximum(m_prev, _lane_cummax(a))
    m_last = jnp.broadcast_to(mrow[:, c - 1:c], (8 * nb, c))
    b_last = jnp.broadcast_to(bcum[:, c - 1:c], (8 * nb, c))
    wi = jnp.exp(m_prev - mrow)
    em = jnp.exp(-(bcum + mrow))
    wk = jnp.exp(a - m_last)
    dec = jnp.exp(m_prev - m_last)
    m_ml[...] = jnp.where(valid, b_last + m_last, 0.0)
    kinds = [mrow, wi, em, wk]
    if 32 * nb < c:
        kinds.append(jnp.zeros((c - 32 * nb, c), F32))
    cols = stack(kinds).T

    def first_col(kind, b):
        return kind * 8 * nb + 8 * b + N_HEADS

    mexp = [jnp.concatenate([jnp.broadcast_to(cols[:, first_col(0, b) + h:first_col(0, b) + h + 1], (c, c))
                             for h in range(N_HEADS)], axis=1) for b in batches]
    a_row = [jnp.concatenate([a[8 * b + N_HEADS + h:8 * b + N_HEADS + h + 1, :] for h in range(N_HEADS)], axis=1)
             for b in batches]
    qkm = [_dot_nt(seq(mq, b), stackmask(seq(mkb, b))) for b in batches]
    inter = [_dot(seq(mq, b), _bf(s_cn[b])) for b in batches]
    sm = [_bf(qkm[b] * jnp.exp(jnp.where(causal, a_row[b] - mexp[b], -jnp.inf))) for b in batches]
    onesbd = onesbd_ref[...]
    wi_l = stack([expand_heads(cols, first_col(1, b)) for b in batches])
    em_l = stack([expand_heads(cols, first_col(2, b)) for b in batches])
    wk_l = stack([expand_heads(cols, first_col(3, b)) for b in batches])
    nd = (stack([_dot(sm[b], jnp.concatenate([stackmask(seq(mvb, b)), onesbd], axis=1)) for b in batches])
          + jnp.concatenate([wi_l, wi_l], axis=1) * stack(inter))
    mh = nd[:, 0:GW] / jnp.maximum(jnp.abs(nd[:, GW:2 * GW]), em_l)
    kt = mk * wk_l
    ones = jnp.ones((c, GW), BF16)
    upd = [_dot(_bf(seq(kt, b).T), jnp.concatenate([seq(mvb, b), ones], axis=1)) for b in batches]
    dec2 = jnp.concatenate([dec, dec], axis=1)
    bdf2 = jnp.concatenate([bdf, bdf], axis=1)
    for b in batches:
        dec_l = jnp.zeros((1, GW), F32)
        for h in range(N_HEADS):
            dec_l = jnp.where(lane_head == h, dec2[8 * b + N_HEADS + h:8 * b + N_HEADS + h + 1, :], dec_l)
        s_cn[b] = s_cn[b] * jnp.concatenate([dec_l, dec_l], axis=1) + upd[b] * bdf2
    put(GW, head_norm(mh * _sigmoid(col(M_O)), mlg_ref[...], True))

    span = span_ref[...]
    level_masks = [lv == code for code in range(2, 2 + LEVELS)]

    def decay_attention(q, k, v, g, st_ref):
        g3 = _split3(g)
        dec = [jnp.exp(_dot(span, stack([seq(part, b) for part in g3]))) for b in batches]
        e_cum = stack([d[0:c] for d in dec])
        e_end = stack([d[c:2 * c] for d in dec])
        qe = _bf(q * e_cum)
        out = stack([_dot_nt(seq(qe, b), _bf(st_ref[b])) for b in batches])
        out = out + _dot(_bf(q * k), hh) * v
        scores = [jnp.zeros((c, N_HEADS * c), F32) for _ in batches]
        for lvl in range(LEVELS):
            e = stack([d[(2 + lvl) * c:(3 + lvl) * c] for d in dec])
            qb = _bf(q * e)
            kb = _bf(k * e)
            level = [_dot_nt(seq(qb, b), stackmask(seq(kb, b))) for b in batches]
            scores = [jnp.where(level_masks[lvl], level[b], scores[b]) for b in batches]
        vb = _bf(v)
        out = out + stack([_dot(_bf(scores[b]), stackmask(seq(vb, b))) for b in batches])
        ke = _bf(k * e_end)
        upd = [_dot(_bf(seq(v, b).T), seq(ke, b)) for b in batches]
        for b in batches:
            st_ref[b] = st_ref[b] * seq(e_cum, b)[c - 1:c, :] + upd[b] * bdf
        return out

    issue(H_Q)
    x_lr = _dot_f32(ext, w2_ref[...]) + b2_ref[...]
    log_a = _log_sigmoid(x_lr) * (1.0 / GLA_TAU)
    go = decay_attention(col(G_Q), col(G_K) * QK_SCALE, col(G_V), log_a, s_gla)
    put(2 * GW, head_norm(go, glag_ref[...], False) * _silu(col(G_G)))

    lb_all = lb_ref[...]
    lb_e = jnp.exp(lb_all - jnp.max(lb_all, axis=0, keepdims=True))
    lb_p = lb_e / jnp.sum(lb_e, axis=0, keepdims=True)
    lb = jnp.zeros((1, GW), F32)
    for l in range(1, layer + 1):
        lb = lb + lb_p[l:l + 1, :]
    z = col(H_F)
    x1 = jnp.log(lb)
    x2 = jnp.log1p(-lb) + _log_sigmoid(z)
    log_f = jnp.maximum(x1, x2) + jnp.log1p(jnp.exp(-jnp.abs(x1 - x2)))
    k_h = (1.0 - lb) * _sigmoid(-z)
    ho = decay_attention(_silu(col(H_Q)), k_h, col(H_I), log_f, s_hg)
    put(3 * GW, head_norm(ho * _sigmoid(col(H_G)), hgg_ref[...], False))


MIXER_BATCH = 4


def _mixer(xt, modl, w_in_p, cosf, sins, conv_w, gate_rows, w2_pad, b2, lb, ret_g, ml_g, gla_g, hg_g, consts,
           layer, bsz, seq):
    c = CHUNK
    nc = seq // c
    t, d = xt.shape
    row = lambda a: a.reshape(1, -1)
    full = lambda a: pl.BlockSpec(a.shape, lambda b, n: (0,) * a.ndim)
    cs = consts
    args = [xt.reshape(bsz, seq, d), modl, w_in_p, cosf, sins, conv_w, gate_rows, w2_pad, row(b2), lb,
            row(ret_g), row(ml_g), row(gla_g), row(hg_g),
            cs["dret"], cs["dq"], cs["dk"], cs["gc"], cs["lv"], cs["hh"], cs["bdf"], cs["ones_bd"], cs["span"]]
    nb = MIXER_BATCH
    in_specs = [
        pl.BlockSpec((nb, c, d), lambda b, n: (b, n, 0)),
        pl.BlockSpec((nb, 6, d), lambda b, n: (b, 0, 0)),
        full(w_in_p),
        pl.BlockSpec((c, GW), lambda b, n: (n, 0)),
        pl.BlockSpec((c, GW), lambda b, n: (n, 0)),
    ] + [full(a) for a in args[5:]]
    y = pl.pallas_call(
        functools.partial(_mixer_kernel, layer, nb),
        grid=(bsz // nb, nc),
        in_specs=in_specs,
        out_specs=pl.BlockSpec((nb, c, D_MODEL), lambda b, n: (b, n, 0)),
        out_shape=jax.ShapeDtypeStruct((bsz, seq, D_MODEL), BF16),
        scratch_shapes=[
            pltpu.VMEM((nb, GW, GW), F32),
            pltpu.VMEM((nb, GW, 2 * GW), F32),
            pltpu.VMEM((nb, GW, GW), F32),
            pltpu.VMEM((nb, GW, GW), F32),
            pltpu.VMEM((nb * 8, c), F32),
            pltpu.VMEM((nb, c + CONV_PAD, 2 * GW), F32),
        ],
        compiler_params=pltpu.CompilerParams(
            dimension_semantics=("arbitrary", "arbitrary"), vmem_limit_bytes=VMEM_LIMIT),
        name="mixer",
    )(*args)
    return y.reshape(t, D_MODEL)


def _group_shift(v, pos, k):
    return jnp.where(pos < EXPERTS_PER_GROUP - k,
                     pltpu.roll(v, LANES - k, 1), pltpu.roll(v, EXPERTS_PER_GROUP - k, 1))


def _router(h, rw, rb):
    lane = lax.broadcasted_iota(jnp.int32, (1, LANES), 1)
    real = lane < N_EXPERTS
    logits = jnp.where(real, _dot_f32(h, rw), -jnp.inf)
    ex = jnp.exp(logits - jnp.max(logits, axis=-1, keepdims=True))
    probs = ex / jnp.sum(ex, axis=-1, keepdims=True)
    sel = jnp.where(real, probs + rb, -jnp.inf)
    pos = lane % EXPERTS_PER_GROUP
    gid = lane // EXPERTS_PER_GROUP
    r1 = _group_shift(sel, pos, 1)
    r2 = _group_shift(sel, pos, 2)
    r3 = _group_shift(sel, pos, 3)
    pair = jnp.maximum(jnp.maximum(jnp.maximum(sel + r1, sel + r2), jnp.maximum(sel + r3, r1 + r2)),
                       jnp.maximum(r1 + r3, r2 + r3))
    pair = jnp.where(real, pair, -jnp.inf)
    best = jnp.max(pair, axis=-1, keepdims=True)
    first = jnp.min(jnp.where(pair == best, gid, N_GROUPS), axis=-1, keepdims=True)
    rank = jnp.zeros(sel.shape, jnp.int32)
    for k, r in ((1, r1), (2, r2), (3, r3)):
        beats = (r > sel) | ((r == sel) & (pos + k >= EXPERTS_PER_GROUP))
        rank = rank + beats.astype(jnp.int32)
    return probs, first, rank


def _post_kernel(alpha, y_ref, x_ref, mod_ref, wout_ref, lng_ref, lnb_ref, rw_ref, rb_ref,
                 stri_ref, x1_ref, h2_ref, slots_ref, counts_ref, carry_ref):
    tm = y_ref.shape[0]

    @pl.when(pl.program_id(0) == 0)
    def _():
        carry_ref[...] = jnp.zeros_like(carry_ref)

    g1 = mod_ref[0, 2:3, :]
    sh2 = mod_ref[0, 3:4, :]
    sc2 = mod_ref[0, 4:5, :]
    mix = _dot(_bf(y_ref[...]), wout_ref[...])
    x1 = _layer_norm(alpha * x_ref[...] + g1 * mix, lng_ref[...], lnb_ref[...])
    x1_ref[...] = x1
    h = x1 * (1.0 + sc2) + sh2
    d = h.shape[1]
    h2_ref[:, 0:d] = h

    probs, first, rank = _router(h, rw_ref[...], rb_ref[...])
    lane = lax.broadcasted_iota(jnp.int32, (1, LANES), 1)
    chosen = (lane // EXPERTS_PER_GROUP == first) & (rank < 2)
    w = jnp.where(chosen, probs, 0.0)
    h2_ref[:, d:d + LANES] = w / jnp.sum(w, axis=-1, keepdims=True)

    bit = jnp.left_shift(1, lane % EXPERTS_PER_GROUP).astype(F32)
    bits = jnp.sum(jnp.where(chosen, bit, 0.0), axis=-1, keepdims=True)
    pair = jnp.full(bits.shape, len(PAIR_BITS) - 1, jnp.int32)
    for index in range(len(PAIR_BITS) - 2, -1, -1):
        pair = jnp.where(bits == float(PAIR_BITS[index]), index, pair)
    cls = first * len(PAIR_BITS) + pair
    onehot = jnp.where(lane == cls, 1.0, 0.0)
    before = _dot(stri_ref[...], _bf(onehot)) + carry_ref[...]
    code = cls.astype(F32) * float(1 << POS_BITS) + jnp.sum(onehot * before, axis=-1, keepdims=True)
    slots_ref[0] = jnp.broadcast_to(code, (tm, LANES)).T[0:1, :].astype(jnp.int32)
    carry_ref[...] += jnp.sum(onehot, axis=0, keepdims=True)
    counts_ref[...] = carry_ref[...].astype(jnp.int32)


def _post(y, xt, modl, w_out, ln_g, ln_b, rw_pad, rb_pad, seq, alpha):
    t, d = xt.shape
    tm = 256
    per_batch = seq // tm
    row = lambda a: a.reshape(1, -1)
    tile = pl.BlockSpec((tm, d), lambda i: (i, 0))
    full = lambda a: pl.BlockSpec(a.shape, lambda i: (0,) * a.ndim)
    stri = jnp.asarray(np.tril(np.ones((tm, tm), np.float32), -1), BF16)
    args = [y, xt, modl, w_out, row(ln_g), row(ln_b), rw_pad, rb_pad, stri]
    return pl.pallas_call(
        functools.partial(_post_kernel, alpha),
        grid=(t // tm,),
        in_specs=[tile, tile, pl.BlockSpec((1, 6, d), lambda i: (i // per_batch, 0, 0))]
        + [full(a) for a in args[3:]],
        out_specs=[tile, pl.BlockSpec((tm, d + LANES), lambda i: (i, 0)),
                   pl.BlockSpec((1, 1, tm), lambda i: (i, 0, 0)),
                   pl.BlockSpec((1, LANES), lambda i: (0, 0))],
        out_shape=[jax.ShapeDtypeStruct((t, d), F32), jax.ShapeDtypeStruct((t, d + LANES), F32),
                   jax.ShapeDtypeStruct((t // tm, 1, tm), jnp.int32),
                   jax.ShapeDtypeStruct((1, LANES), jnp.int32)],
        scratch_shapes=[pltpu.VMEM((1, LANES), F32)],
        compiler_params=pltpu.CompilerParams(
            dimension_semantics=("arbitrary",), vmem_limit_bytes=VMEM_LIMIT),
        name="post",
    )(*args)


MOE_TILE = 512
POS_BITS = 15
PAIR_BITS = (0b0011, 0b0101, 0b1001, 0b1010, 0b0110, 0b1100)
N_CLASSES = N_GROUPS * len(PAIR_BITS)


def _dispatch_kernel(slots_ref, h_ref, hs_hbm, sem):
    tm = h_ref.shape[0]
    i = pl.program_id(0)

    def issue(w, carry):
        for r in range(8):
            row = w * 8 + r
            s = slots_ref[i * tm + row]
            pltpu.make_async_copy(h_ref.at[pl.ds(row, 1)], hs_hbm.at[pl.ds(s, 1)], sem).start()
        return carry

    lax.fori_loop(0, tm // 8, issue, 0)
    pltpu.make_async_copy(h_ref, hs_hbm.at[pl.ds(0, tm)], sem).wait()


def _dispatch(slots, h2):
    t, d = h2.shape
    tm = MOE_TILE
    return pl.pallas_call(
        _dispatch_kernel,
        grid_spec=pltpu.PrefetchScalarGridSpec(
            num_scalar_prefetch=1,
            grid=(t // tm,),
            in_specs=[pl.BlockSpec((tm, d), lambda i, s: (i, 0))],
            out_specs=pl.BlockSpec(memory_space=pl.ANY),
            scratch_shapes=[pltpu.SemaphoreType.DMA(())],
        ),
        out_shape=jax.ShapeDtypeStruct((t, d), F32),
        compiler_params=pltpu.CompilerParams(
            dimension_semantics=("arbitrary",), vmem_limit_bytes=VMEM_LIMIT),
        name="dispatch",
    )(slots, h2)


def _moe_kernel(item_tile_ref, item_expert_ref, item_first_ref, n_items_ref,
                hs_ref, w1_ref, w3_ref, w2_ref, ys_ref, hb_ref):
    s = pl.program_id(0)
    d = ys_ref.shape[1]

    @pl.when(s < n_items_ref[0])
    def _():
        lane = lax.broadcasted_iota(jnp.int32, (1, LANES), 1)
        first_of_tile = item_first_ref[s] == 1

        @pl.when(first_of_tile)
        def _():
            hb_ref[...] = _bf(hs_ref[:, 0:d])

        gates = hs_ref[:, d:d + LANES]
        ge = jnp.sum(jnp.where(lane == item_expert_ref[s], gates, 0.0), axis=-1, keepdims=True)
        hb = hb_ref[...]
        a = _silu(_dot(hb, _bf(w1_ref[0, 0]))) * _dot(hb, _bf(w3_ref[0, 0]))
        out = _dot(_bf(a * ge), _bf(w2_ref[0, 0]))

        @pl.when(first_of_tile)
        def _():
            ys_ref[...] = out

        @pl.when(jnp.logical_not(first_of_tile))
        def _():
            ys_ref[...] += out


def _moe(items, hs, w1, w3, w2, layer):
    item_tile, item_expert, item_first, n_items = items
    rows = hs.shape[0]
    d = w1.shape[2]
    up = pl.BlockSpec((1, 1, d, D_EXPERT), lambda s, it, ie, fi, n: (layer, ie[s], 0, 0))
    down = pl.BlockSpec((1, 1, D_EXPERT, d), lambda s, it, ie, fi, n: (layer, ie[s], 0, 0))
    return pl.pallas_call(
        _moe_kernel,
        grid_spec=pltpu.PrefetchScalarGridSpec(
            num_scalar_prefetch=4,
            grid=(item_tile.shape[0],),
            in_specs=[pl.BlockSpec((MOE_TILE, d + LANES), lambda s, it, ie, fi, n: (it[s], 0)), up, up, down],
            out_specs=pl.BlockSpec((MOE_TILE, d), lambda s, it, ie, fi, n: (it[s], 0)),
            scratch_shapes=[pltpu.VMEM((MOE_TILE, d), BF16)],
        ),
        out_shape=jax.ShapeDtypeStruct((rows, d), F32),
        compiler_params=pltpu.CompilerParams(
            dimension_semantics=("arbitrary",), vmem_limit_bytes=VMEM_LIMIT),
        name="moe",
    )(item_tile, item_expert, item_first, n_items, hs, w1, w3, w2)


def _class_experts():
    member = np.zeros((N_CLASSES, N_EXPERTS), np.int32)
    for g in range(N_GROUPS):
        for p, bits in enumerate(PAIR_BITS):
            for e in range(EXPERTS_PER_GROUP):
                if bits >> e & 1:
                    member[g * len(PAIR_BITS) + p, g * EXPERTS_PER_GROUP + e] = 1
    return member


def _work_items(counts, n_tokens):
    n_tiles = n_tokens // MOE_TILE
    n_items_max = 2 * (n_tiles + 2 * N_CLASSES)
    ends = jnp.cumsum(counts)
    starts = ends - counts
    lo = jnp.arange(n_tiles, dtype=jnp.int32)[:, None] * MOE_TILE
    overlap = (counts[None, :] > 0) & (starts[None, :] < lo + MOE_TILE) & (ends[None, :] > lo)
    needed = (overlap.astype(jnp.int32) @ jnp.asarray(_class_experts())) > 0
    flat = needed.reshape(-1)
    n_items = jnp.sum(flat.astype(jnp.int32))
    place = jnp.where(flat, jnp.cumsum(flat.astype(jnp.int32)) - 1, n_items_max)
    ids = jnp.zeros((n_items_max,), jnp.int32).at[place].set(
        jnp.arange(flat.shape[0], dtype=jnp.int32), mode="drop")
    ids = ids[jnp.minimum(jnp.arange(n_items_max), n_items - 1)]
    item_tile = ids // N_EXPERTS
    item_expert = ids % N_EXPERTS
    item_first = jnp.concatenate([jnp.ones((1,), jnp.int32),
                                  (item_tile[1:] != item_tile[:-1]).astype(jnp.int32)])
    return starts.astype(jnp.int32), (item_tile, item_expert, item_first, n_items.reshape(1))


def _combine_kernel(alpha, slots_ref, ys_hbm, x1_ref, mod_ref, lng_ref, lnb_ref, o_ref, buf, sem):
    tm = o_ref.shape[0]
    i = pl.program_id(0)
    n = pl.num_programs(0)

    def gather(tile, b):
        def body(w, carry):
            for r in range(8):
                row = w * 8 + r
                s = slots_ref[tile * tm + row]
                pltpu.make_async_copy(ys_hbm.at[pl.ds(s, 1)], buf.at[b, pl.ds(row, 1)], sem.at[b]).start()
            return carry
        lax.fori_loop(0, tm // 8, body, 0)

    @pl.when(i == 0)
    def _():
        gather(0, 0)

    @pl.when(i + 1 < n)
    def _():
        gather(i + 1, (i + 1) % 2)

    cur = i % 2
    pltpu.make_async_copy(ys_hbm.at[pl.ds(0, tm)], buf.at[cur], sem.at[cur]).wait()
    g2 = mod_ref[0, 5:6, :]
    o_ref[...] = _layer_norm(alpha * x1_ref[...] + g2 * buf[cur], lng_ref[...], lnb_ref[...])


def _combine(slots, ys, x1, modl, ln_g, ln_b, seq, alpha):
    t, d = x1.shape
    tm = 256
    per_batch = seq // tm
    row = lambda a: a.reshape(1, -1)
    tile = pl.BlockSpec((tm, d), lambda i, s: (i, 0))
    vec = pl.BlockSpec((1, d), lambda i, s: (0, 0))
    return pl.pallas_call(
        functools.partial(_combine_kernel, alpha),
        grid_spec=pltpu.PrefetchScalarGridSpec(
            num_scalar_prefetch=1,
            grid=(t // tm,),
            in_specs=[pl.BlockSpec(memory_space=pl.ANY), tile,
                      pl.BlockSpec((1, 6, d), lambda i, s: (i // per_batch, 0, 0)), vec, vec],
            out_specs=tile,
            scratch_shapes=[pltpu.VMEM((2, tm, d), F32), pltpu.SemaphoreType.DMA((2,))],
        ),
        out_shape=jax.ShapeDtypeStruct((t, d), F32),
        compiler_params=pltpu.CompilerParams(
            dimension_semantics=("arbitrary",), vmem_limit_bytes=VMEM_LIMIT),
        name="combine",
    )(slots, ys, x1, modl, row(ln_g), row(ln_b))


def _regroup_w_in(w):
    a = 8 * GW
    b = a + 2 * N_HEADS
    cc = b + 4 * GW
    dd = cc + GLA_RANK
    pad = jnp.zeros((w.shape[0], LANES - 2 * N_HEADS - GLA_RANK), w.dtype)
    return jnp.concatenate([w[:, :a], w[:, b:cc], w[:, dd:], w[:, a:b], w[:, cc:dd], pad], axis=1)


def _rotary_tables(seq):
    inv = ROPE_BASE ** (-jnp.arange(0, HEAD_DIM, 2, dtype=F32) / HEAD_DIM)
    ang = jnp.arange(seq, dtype=F32)[:, None] * inv[None, :]
    cos = jnp.cos(ang)
    sin = jnp.sin(ang)
    cosf = jnp.tile(jnp.concatenate([cos, cos], axis=1), (1, N_HEADS))
    sins = jnp.tile(jnp.concatenate([-sin, sin], axis=1), (1, N_HEADS))
    return cosf, sins


def kernel(x, c, ada_w, ada_b, w_in, mlstm_conv, mlstm_gate_b, gla_w2, gla_b2, hgrn_lb, ret_norm, mlstm_norm, gla_norm, hgrn_norm, w_out, ln1_g, ln1_b, router_w, router_b, exp_w1, exp_w3, exp_w2, ln2_g, ln2_b):
    bsz, seq, d = x.shape
    depth = ada_w.shape[0]
    assert d == D_MODEL and seq % 512 == 0 and bsz <= 16
    t = bsz * seq
    alpha = (2.0 * depth) ** 0.25
    assert t % MOE_TILE == 0 and t < (1 << POS_BITS) and bsz % MIXER_BATCH == 0
    consts = _mixer_consts()
    cosf, sins = _rotary_tables(seq)
    c_rows = jnp.pad(c, ((0, 16 - bsz), (0, 0)))
    mod = _adaln(c_rows, ada_w, ada_b)[:, :bsz]
    rw_pad = jnp.pad(router_w, ((0, 0), (0, LANES - N_EXPERTS)))
    rb_pad = jnp.pad(router_b, (0, LANES - N_EXPERTS)).reshape(1, LANES)
    xt = x.reshape(t, d)
    for l in range(depth):
        modl = mod[l].reshape(bsz, 6, d)
        gate_rows = jnp.broadcast_to(mlstm_gate_b[l][:, None], (2 * N_HEADS, CHUNK))
        w2_pad = jnp.zeros((LANES, GW), F32).at[2 * N_HEADS:2 * N_HEADS + GLA_RANK].set(gla_w2[l])
        y = _mixer(xt, modl, _bf(_regroup_w_in(w_in[l])), cosf, sins, mlstm_conv[l], gate_rows, w2_pad,
                   gla_b2[l], hgrn_lb, ret_norm[l], mlstm_norm[l], gla_norm[l], hgrn_norm[l], consts,
                   l, bsz, seq)
        x1, h2, codes, counts = _post(y, xt, modl, _bf(w_out[l]), ln1_g[l], ln1_b[l], rw_pad, rb_pad,
                                      seq, alpha)
        class_start, items = _work_items(counts[0, :N_CLASSES], t)
        codes = codes.reshape(t)
        slots = class_start[codes >> POS_BITS] + (codes & ((1 << POS_BITS) - 1))
        hs = _dispatch(slots, h2)
        ys = _moe(items, hs, exp_w1, exp_w3, exp_w2, l)
        xt = _combine(slots, ys, x1, modl, ln2_g[l], ln2_b[l], seq, alpha)
    return xt.reshape(bsz, seq, d)
```

```python
import functools

import numpy as np
import jax
import jax.numpy as jnp
from jax import lax
from jax.experimental import pallas as pl
from jax.experimental.pallas import tpu as pltpu

F32 = jnp.float32
BF16 = jnp.bfloat16

D_MODEL = 1024
N_MIXERS = 4
GW = D_MODEL // N_MIXERS
N_HEADS = 4
HEAD_DIM = GW // N_HEADS
CONV_WIDTH = 4
GLA_RANK = 16
GLA_TAU = 16.0
ROPE_BASE = 10000.0
N_GROUPS = 4
EXPERTS_PER_GROUP = 4
N_EXPERTS = 16
D_EXPERT = D_MODEL // 2
LN_EPS = 1e-5
NORM_EPS = 1e-6
QK_SCALE = HEAD_DIM ** -0.5

LANES = 128
CHUNK = 128
LEVELS = 7
CONV_PAD = 8
NPROJ = 16 * GW + LANES
EXTRA = 16 * GW
VMEM_LIMIT = 56 * 1024 * 1024

(R_Q, R_K, R_V, R_G, M_Q, M_K, M_V, M_O, G_Q, G_K, G_V, G_G, H_Q, H_F, H_I, H_G) = (
    GW * i for i in range(16))


def _bf(x):
    return x.astype(BF16)


def _dot(a, b):
    return jnp.dot(a, b, preferred_element_type=F32)


def _dot_nt(a, b):
    return lax.dot_general(a, b, (((1,), (1,)), ((), ())), preferred_element_type=F32)


def _split2(x):
    hi = _bf(x)
    lo = _bf(x - hi.astype(F32))
    return hi, lo


def _dot_f32(a, b):
    ah, al = _split2(a)
    bh, bl = _split2(b)
    return _dot(ah, bh) + _dot(ah, bl) + _dot(al, bh)


def _sigmoid(x):
    return 0.5 * jnp.tanh(0.5 * x) + 0.5


def _silu(x):
    return x * _sigmoid(x)


def _log_sigmoid(x):
    return jnp.minimum(x, 0.0) - jnp.log1p(jnp.exp(-jnp.abs(x)))


def _layer_norm(v, g, b):
    mu = jnp.mean(v, axis=-1, keepdims=True)
    d = v - mu
    var = jnp.mean(d * d, axis=-1, keepdims=True)
    return d * lax.rsqrt(var + LN_EPS) * g + b


def _adaln_kernel(c_ref, w_ref, b_ref, o_ref):
    cond = _silu(c_ref[...])
    o_ref[0] = _dot_f32(cond, w_ref[0]) + b_ref[0]


def _adaln(c, ada_w, ada_b):
    depth, d, n = ada_w.shape
    bsz = c.shape[0]
    tn = 1536
    return pl.pallas_call(
        _adaln_kernel,
        grid=(depth, n // tn),
        in_specs=[
            pl.BlockSpec((bsz, d), lambda l, j: (0, 0)),
            pl.BlockSpec((1, d, tn), lambda l, j: (l, 0, j)),
            pl.BlockSpec((1, 1, tn), lambda l, j: (l, 0, j)),
        ],
        out_specs=pl.BlockSpec((1, bsz, tn), lambda l, j: (l, 0, j)),
        out_shape=jax.ShapeDtypeStruct((depth, bsz, n), F32),
        compiler_params=pltpu.CompilerParams(
            dimension_semantics=("arbitrary", "arbitrary"), vmem_limit_bytes=VMEM_LIMIT),
        name="adaln",
    )(c, ada_w, ada_b.reshape(depth, 1, n))


def _mixer_consts():
    c = CHUNK
    i = np.arange(c)[:, None]
    j = np.arange(c)[None, :]
    rel = i - j
    log_gamma = np.log1p(-np.exp2(-5.0 - np.arange(N_HEADS, dtype=np.float64)))
    dret = np.concatenate(
        [np.where(rel >= 0, np.exp(log_gamma[h] * np.maximum(rel, 0)), 0.0) for h in range(N_HEADS)], axis=1)
    lane_head = np.arange(GW) // HEAD_DIM
    dq = np.exp(log_gamma[lane_head][None, :] * (np.arange(c)[:, None] + 1.0))
    dk = np.exp(log_gamma[lane_head][None, :] * (c - 1.0 - np.arange(c)[:, None]))
    gc = np.exp(log_gamma[lane_head] * c)[None, :]
    lv = np.zeros((c, c), np.int32)
    lv[rel == 0] = 1
    spans = [(rel >= 0), (rel < 0)]
    code, b = 2, 1
    while b < c:
        lv[(i // (2 * b) == j // (2 * b)) & ((i // b) % 2 == 1) & ((j // b) % 2 == 0)] = code
        ref = (i // (2 * b)) * 2 * b + b - 1
        right = (i // b) % 2 == 1
        spans.append(np.where(right, (j > ref) & (j <= i), (j > i) & (j <= ref)))
        code, b = code + 1, b * 2
    lv = np.tile(lv, (1, N_HEADS))
    span = np.concatenate(spans, axis=0).astype(np.float32)
    span = np.tile(span, (1, 2))
    hh = (lane_head[:, None] == lane_head[None, :]).astype(np.float32)
    ones_bd = np.repeat(np.eye(N_HEADS, dtype=np.float32), c, axis=0)
    ones_bd = np.repeat(ones_bd, HEAD_DIM, axis=1)
    return dict(
        dret=jnp.asarray(dret, F32), dq=jnp.asarray(dq, F32), dk=jnp.asarray(dk, F32),
        gc=jnp.asarray(gc, F32), lv=jnp.asarray(lv), hh=jnp.asarray(hh, BF16),
        bdf=jnp.asarray(hh, F32), ones_bd=jnp.asarray(ones_bd, BF16), span=jnp.asarray(span, BF16))


def _lane_cumsum(x):
    lane = lax.broadcasted_iota(jnp.int32, x.shape, 1)
    s = 1
    while s < x.shape[1]:
        x = x + jnp.where(lane >= s, pltpu.roll(x, s, 1), 0.0)
        s *= 2
    return x


def _lane_cummax(x):
    lane = lax.broadcasted_iota(jnp.int32, x.shape, 1)
    s = 1
    while s < x.shape[1]:
        x = jnp.maximum(x, jnp.where(lane >= s, pltpu.roll(x, s, 1), -jnp.inf))
        s *= 2
    return x


def _mixer_kernel(layer, nb,
                  x_ref, mod_ref, win_ref, cos_ref, sin_ref, conv_ref, gb_ref, w2_ref, b2_ref, lb_ref,
                  retg_ref, mlg_ref, glag_ref, hgg_ref,
                  dret_ref, dq_ref, dk_ref, gc_ref, lv_ref, hh_ref, bdf_ref, onesbd_ref, span_ref,
                  y_ref,
                  s_ret, s_cn, s_gla, s_hg, m_ml, conv_buf):
    c = CHUNK
    batches = range(nb)

    @pl.when(pl.program_id(1) == 0)
    def _():
        for s in (s_ret, s_cn, s_gla, s_hg, m_ml):
            s[...] = jnp.zeros_like(s)
        conv_buf[:, 0:CONV_PAD, :] = jnp.zeros((nb, CONV_PAD, 2 * GW), F32)

    lane_head = lax.broadcasted_iota(jnp.int32, (1, GW), 1) // HEAD_DIM
    head_rows = [(lane_head == h).astype(BF16) for h in range(N_HEADS)]
    hh = hh_ref[...]
    bdf = bdf_ref[...]
    lv = lv_ref[...]
    causal = lv >= 1

    def stack(xs):
        return jnp.concatenate(xs, axis=0)

    def tiled(x):
        return stack([x] * nb)

    def seq(x, b):
        return x[b * c:(b + 1) * c]

    hb = _bf(stack([x_ref[b] * (1.0 + mod_ref[b, 1:2, :]) + mod_ref[b, 0:1, :] for b in batches]))

    issued = {}

    def issue(first):
        issued[first] = _dot(hb, win_ref[:, first:first + N_MIXERS * GW])

    def col(off, width=GW):
        first = off // (N_MIXERS * GW) * (N_MIXERS * GW)
        return issued[first][:, off - first:off - first + width]

    issue(R_Q)
    issue(M_Q)
    ext = _dot(hb, win_ref[:, EXTRA:EXTRA + LANES])

    def put(off, val):
        for b in batches:
            y_ref[b, :, off:off + GW] = seq(val, b).astype(y_ref.dtype)

    def stackmask(xb):
        return jnp.concatenate([xb * head_rows[h] for h in range(N_HEADS)], axis=0)

    def expand_heads(cols, first):
        out = jnp.zeros((cols.shape[0], GW), F32)
        for h in range(N_HEADS):
            out = jnp.where(lane_head == h, cols[:, first + h:first + h + 1], out)
        return out

    def head_mean(v):
        return _dot(_bf(v), hh) * (1.0 / HEAD_DIM)

    def head_norm(v, gain, center):
        if center:
            v = v - head_mean(v)
        return v * lax.rsqrt(head_mean(v * v) + NORM_EPS) * gain

    cosf = tiled(cos_ref[...])
    sins = tiled(sin_ref[...])
    half = lax.broadcasted_iota(jnp.int32, (1, GW), 1) % HEAD_DIM < HEAD_DIM // 2

    def rotary(v):
        swapped = jnp.where(half, pltpu.roll(v, GW - HEAD_DIM // 2, 1), pltpu.roll(v, HEAD_DIM // 2, 1))
        return v * cosf + swapped * sins

    rq = _bf(rotary(col(R_Q)) * QK_SCALE)
    rk = rotary(col(R_K))
    rkb = _bf(rk)
    rvb = _bf(col(R_V))
    rkd = rk * tiled(dk_ref[...])
    dret = dret_ref[...]
    dq = dq_ref[...]
    gc = gc_ref[...]
    sc = [_dot_nt(seq(rq, b), stackmask(seq(rkb, b))) for b in batches]
    inter = [_dot(seq(rq, b), _bf(s_ret[b])) for b in batches]
    scb = [_bf(s * dret) for s in sc]
    ro = stack([_dot(scb[b], stackmask(seq(rvb, b))) + inter[b] * dq for b in batches])
    upd = [_dot(_bf(seq(rkd, b).T), seq(rvb, b)) for b in batches]
    for b in batches:
        s_ret[b] = s_ret[b] * gc + upd[b] * bdf
    put(0, head_norm(ro, retg_ref[...], True) * _silu(col(R_G)))

    issue(G_Q)
    mqk = col(M_Q, 2 * GW)
    for b in batches:
        conv_buf[b, CONV_PAD:CONV_PAD + c, :] = seq(mqk, b)
    qk = jnp.zeros((nb * c, 2 * GW), F32)
    for t in range(CONV_WIDTH):
        s = CONV_WIDTH - 1 - t
        qk = qk + stack([conv_buf[b, CONV_PAD - s:CONV_PAD - s + c, :] for b in batches]) * conv_ref[t:t + 1, :]
    for b in batches:
        conv_buf[b, 0:CONV_PAD, :] = conv_buf[b, c:c + CONV_PAD, :]
    qk = _silu(qk)
    mq = _bf(qk[:, 0:GW])
    mk = qk[:, GW:2 * GW] * QK_SCALE
    mkb = _bf(mk)
    mvb = _bf(col(M_V))

    pre = stack([seq(ext, b).T[0:8, :] for b in batches]) + tiled(gb_ref[...])
    row8 = lax.broadcasted_iota(jnp.int32, (8 * nb, c), 0)
    valid = row8 % 8 >= N_HEADS
    lf = jnp.where(valid, _log_sigmoid(pre), 0.0)
    gi = jnp.where(valid, pltpu.roll(pre, N_HEADS, 0), 0.0)
    bcum = _lane_cumsum(lf)
    a = gi - bcum
    m_prev = m_ml[...]
    mrow = jnp.maximum(m_prev, _lane_cummax(a))
    m_last = jnp.broadcast_to(mrow[:, c - 1:c], (8 * nb, c))
    b_last = jnp.broadcast_to(bcum[:, c - 1:c], (8 * nb, c))
    wi = jnp.exp(m_prev - mrow)
    em = jnp.exp(-(bcum + mrow))
    wk = jnp.exp(a - m_last)
    dec = jnp.exp(m_prev - m_last)
    m_ml[...] = jnp.where(valid, b_last + m_last, 0.0)
    kinds = [mrow, wi, em, wk]
    if 32 * nb < c:
        kinds.append(jnp.zeros((c - 32 * nb, c), F32))
    cols = stack(kinds).T

    def first_col(kind, b):
        return kind * 8 * nb + 8 * b + N_HEADS

    mexp = [jnp.concatenate([jnp.broadcast_to(cols[:, first_col(0, b) + h:first_col(0, b) + h + 1], (c, c))
                             for h in range(N_HEADS)], axis=1) for b in batches]
    a_row = [jnp.concatenate([a[8 * b + N_HEADS + h:8 * b + N_HEADS + h + 1, :] for h in range(N_HEADS)], axis=1)
             for b in batches]
    qkm = [_dot_nt(seq(mq, b), stackmask(seq(mkb, b))) for b in batches]
    inter = [_dot(seq(mq, b), _bf(s_cn[b])) for b in batches]
    sm = [_bf(qkm[b] * jnp.exp(jnp.where(causal, a_row[b] - mexp[b], -jnp.inf))) for b in batches]
    onesbd = onesbd_ref[...]
    wi_l = stack([expand_heads(cols, first_col(1, b)) for b in batches])
    em_l = stack([expand_heads(cols, first_col(2, b)) for b in batches])
    wk_l = stack([expand_heads(cols, first_col(3, b)) for b in batches])
    nd = (stack([_dot(sm[b], jnp.concatenate([stackmask(seq(mvb, b)), onesbd], axis=1)) for b in batches])
          + jnp.concatenate([wi_l, wi_l], axis=1) * stack(inter))
    mh = nd[:, 0:GW] / jnp.maximum(jnp.abs(nd[:, GW:2 * GW]), em_l)
    kt = mk * wk_l
    ones = jnp.ones((c, GW), BF16)
    upd = [_dot(_bf(seq(kt, b).T), jnp.concatenate([seq(mvb, b), ones], axis=1)) for b in batches]
    dec2 = jnp.concatenate([dec, dec], axis=1)
    bdf2 = jnp.concatenate([bdf, bdf], axis=1)
    for b in batches:
        dec_l = jnp.zeros((1, GW), F32)
        for h in range(N_HEADS):
            dec_l = jnp.where(lane_head == h, dec2[8 * b + N_HEADS + h:8 * b + N_HEADS + h + 1, :], dec_l)
        s_cn[b] = s_cn[b] * jnp.concatenate([dec_l, dec_l], axis=1) + upd[b] * bdf2
    put(GW, head_norm(mh * _sigmoid(col(M_O)), mlg_ref[...], True))

    span = span_ref[...]
    level_masks = [lv == code for code in range(2, 2 + LEVELS)]

    def decay_attention(q, k, v, g, st_ref):
        g2 = _split2(g)
        dec = [jnp.exp(_dot(span, stack([seq(part, b) for part in g2]))) for b in batches]
        e_cum = stack([d[0:c] for d in dec])
        e_end = stack([d[c:2 * c] for d in dec])
        qh = _bf(q)
        kh = _bf(k)
        qe = qh * _bf(e_cum)
        out = stack([_dot_nt(seq(qe, b), _bf(st_ref[b])) for b in batches])
        out = out + _dot(_bf(q * k), hh) * v
        scores = [jnp.zeros((c, N_HEADS * c), F32) for _ in batches]
        for lvl in range(LEVELS):
            e = _bf(stack([d[(2 + lvl) * c:(3 + lvl) * c] for d in dec]))
            qb = qh * e
            kb = kh * e
            level = [_dot_nt(seq(qb, b), stackmask(seq(kb, b))) for b in batches]
            scores = [jnp.where(level_masks[lvl], level[b], scores[b]) for b in batches]
        vb = _bf(v)
        out = out + stack([_dot(_bf(scores[b]), stackmask(seq(vb, b))) for b in batches])
        ke = kh * _bf(e_end)
        upd = [_dot(_bf(seq(v, b).T), seq(ke, b)) for b in batches]
        for b in batches:
            st_ref[b] = st_ref[b] * seq(e_cum, b)[c - 1:c, :] + upd[b] * bdf
        return out

    issue(H_Q)
    x_lr = _dot_f32(ext, w2_ref[...]) + b2_ref[...]
    log_a = _log_sigmoid(x_lr) * (1.0 / GLA_TAU)
    go = decay_attention(col(G_Q), col(G_K) * QK_SCALE, col(G_V), log_a, s_gla)
    put(2 * GW, head_norm(go, glag_ref[...], False) * _silu(col(G_G)))

    lb_all = lb_ref[...]
    lb_e = jnp.exp(lb_all - jnp.max(lb_all, axis=0, keepdims=True))
    lb_p = lb_e / jnp.sum(lb_e, axis=0, keepdims=True)
    lb = jnp.zeros((1, GW), F32)
    for l in range(1, layer + 1):
        lb = lb + lb_p[l:l + 1, :]
    z = col(H_F)
    x1 = jnp.log(lb)
    x2 = jnp.log1p(-lb) + _log_sigmoid(z)
    log_f = jnp.maximum(x1, x2) + jnp.log1p(jnp.exp(-jnp.abs(x1 - x2)))
    k_h = (1.0 - lb) * _sigmoid(-z)
    ho = decay_attention(_silu(col(H_Q)), k_h, col(H_I), log_f, s_hg)
    put(3 * GW, head_norm(ho * _sigmoid(col(H_G)), hgg_ref[...], False))


MIXER_BATCH = 4


def _mixer(xt, modl, w_in_p, cosf, sins, conv_w, gate_rows, w2_pad, b2, lb, ret_g, ml_g, gla_g, hg_g, consts,
           layer, bsz, seq):
    c = CHUNK
    nc = seq // c
    t, d = xt.shape
    row = lambda a: a.reshape(1, -1)
    full = lambda a: pl.BlockSpec(a.shape, lambda b, n: (0,) * a.ndim)
    cs = consts
    args = [xt.reshape(bsz, seq, d), modl, w_in_p, cosf, sins, conv_w, gate_rows, w2_pad, row(b2), lb,
            row(ret_g), row(ml_g), row(gla_g), row(hg_g),
            cs["dret"], cs["dq"], cs["dk"], cs["gc"], cs["lv"], cs["hh"], cs["bdf"], cs["ones_bd"], cs["span"]]
    nb = MIXER_BATCH
    in_specs = [
        pl.BlockSpec((nb, c, d), lambda b, n: (b, n, 0)),
        pl.BlockSpec((nb, 6, d), lambda b, n: (b, 0, 0)),
        full(w_in_p),
        pl.BlockSpec((c, GW), lambda b, n: (n, 0)),
        pl.BlockSpec((c, GW), lambda b, n: (n, 0)),
    ] + [full(a) for a in args[5:]]
    y = pl.pallas_call(
        functools.partial(_mixer_kernel, layer, nb),
        grid=(bsz // nb, nc),
        in_specs=in_specs,
        out_specs=pl.BlockSpec((nb, c, D_MODEL), lambda b, n: (b, n, 0)),
        out_shape=jax.ShapeDtypeStruct((bsz, seq, D_MODEL), BF16),
        scratch_shapes=[
            pltpu.VMEM((nb, GW, GW), F32),
            pltpu.VMEM((nb, GW, 2 * GW), F32),
            pltpu.VMEM((nb, GW, GW), F32),
            pltpu.VMEM((nb, GW, GW), F32),
            pltpu.VMEM((nb * 8, c), F32),
            pltpu.VMEM((nb, c + CONV_PAD, 2 * GW), F32),
        ],
        compiler_params=pltpu.CompilerParams(
            dimension_semantics=("arbitrary", "arbitrary"), vmem_limit_bytes=VMEM_LIMIT),
        name="mixer",
    )(*args)
    return y.reshape(t, D_MODEL)


def _group_shift(v, pos, k):
    return jnp.where(pos < EXPERTS_PER_GROUP - k,
                     pltpu.roll(v, LANES - k, 1), pltpu.roll(v, EXPERTS_PER_GROUP - k, 1))


def _router(h, rw, rb):
    lane = lax.broadcasted_iota(jnp.int32, (1, LANES), 1)
    real = lane < N_EXPERTS
    logits = jnp.where(real, _dot_f32(h, rw), -jnp.inf)
    ex = jnp.exp(logits - jnp.max(logits, axis=-1, keepdims=True))
    probs = ex / jnp.sum(ex, axis=-1, keepdims=True)
    sel = jnp.where(real, probs + rb, -jnp.inf)
    pos = lane % EXPERTS_PER_GROUP
    gid = lane // EXPERTS_PER_GROUP
    r1 = _group_shift(sel, pos, 1)
    r2 = _group_shift(sel, pos, 2)
    r3 = _group_shift(sel, pos, 3)
    pair = jnp.maximum(jnp.maximum(jnp.maximum(sel + r1, sel + r2), jnp.maximum(sel + r3, r1 + r2)),
                       jnp.maximum(r1 + r3, r2 + r3))
    pair = jnp.where(real, pair, -jnp.inf)
    best = jnp.max(pair, axis=-1, keepdims=True)
    first = jnp.min(jnp.where(pair == best, gid, N_GROUPS), axis=-1, keepdims=True)
    rank = jnp.zeros(sel.shape, jnp.int32)
    for k, r in ((1, r1), (2, r2), (3, r3)):
        beats = (r > sel) | ((r == sel) & (pos + k >= EXPERTS_PER_GROUP))
        rank = rank + beats.astype(jnp.int32)
    return probs, first, rank


def _post_kernel(alpha, y_ref, x_ref, mod_ref, wout_ref, lng_ref, lnb_ref, rw_ref, rb_ref,
                 stri_ref, x1_ref, h2_ref, slots_ref, counts_ref, carry_ref):
    tm = y_ref.shape[0]

    @pl.when(pl.program_id(0) == 0)
    def _():
        carry_ref[...] = jnp.zeros_like(carry_ref)

    g1 = mod_ref[0, 2:3, :]
    sh2 = mod_ref[0, 3:4, :]
    sc2 = mod_ref[0, 4:5, :]
    mix = _dot(_bf(y_ref[...]), wout_ref[...])
    x1 = _layer_norm(alpha * x_ref[...] + g1 * mix, lng_ref[...], lnb_ref[...])
    x1_ref[...] = x1
    h = x1 * (1.0 + sc2) + sh2
    d = h.shape[1]
    h2_ref[:, 0:d] = h

    probs, first, rank = _router(h, rw_ref[...], rb_ref[...])
    lane = lax.broadcasted_iota(jnp.int32, (1, LANES), 1)
    chosen = (lane // EXPERTS_PER_GROUP == first) & (rank < 2)
    w = jnp.where(chosen, probs, 0.0)
    h2_ref[:, d:d + LANES] = w / jnp.sum(w, axis=-1, keepdims=True)

    bit = jnp.left_shift(1, lane % EXPERTS_PER_GROUP).astype(F32)
    bits = jnp.sum(jnp.where(chosen, bit, 0.0), axis=-1, keepdims=True)
    pair = jnp.full(bits.shape, len(PAIR_BITS) - 1, jnp.int32)
    for index in range(len(PAIR_BITS) - 2, -1, -1):
        pair = jnp.where(bits == float(PAIR_BITS[index]), index, pair)
    cls = first * len(PAIR_BITS) + pair
    onehot = jnp.where(lane == cls, 1.0, 0.0)
    before = _dot(stri_ref[...], _bf(onehot)) + carry_ref[...]
    code = cls.astype(F32) * float(1 << POS_BITS) + jnp.sum(onehot * before, axis=-1, keepdims=True)
    slots_ref[0] = jnp.broadcast_to(code, (tm, LANES)).T[0:1, :].astype(jnp.int32)
    carry_ref[...] += jnp.sum(onehot, axis=0, keepdims=True)
    counts_ref[...] = carry_ref[...].astype(jnp.int32)


def _post(y, xt, modl, w_out, ln_g, ln_b, rw_pad, rb_pad, seq, alpha):
    t, d = xt.shape
    tm = 256
    per_batch = seq // tm
    row = lambda a: a.reshape(1, -1)
    tile = pl.BlockSpec((tm, d), lambda i: (i, 0))
    full = lambda a: pl.BlockSpec(a.shape, lambda i: (0,) * a.ndim)
    stri = jnp.asarray(np.tril(np.ones((tm, tm), np.float32), -1), BF16)
    args = [y, xt, modl, w_out, row(ln_g), row(ln_b), rw_pad, rb_pad, stri]
    return pl.pallas_call(
        functools.partial(_post_kernel, alpha),
        grid=(t // tm,),
        in_specs=[tile, tile, pl.BlockSpec((1, 6, d), lambda i: (i // per_batch, 0, 0))]
        + [full(a) for a in args[3:]],
        out_specs=[tile, pl.BlockSpec((tm, d + LANES), lambda i: (i, 0)),
                   pl.BlockSpec((1, 1, tm), lambda i: (i, 0, 0)),
                   pl.BlockSpec((1, LANES), lambda i: (0, 0))],
        out_shape=[jax.ShapeDtypeStruct((t, d), F32), jax.ShapeDtypeStruct((t, d + LANES), F32),
                   jax.ShapeDtypeStruct((t // tm, 1, tm), jnp.int32),
                   jax.ShapeDtypeStruct((1, LANES), jnp.int32)],
        scratch_shapes=[pltpu.VMEM((1, LANES), F32)],
        compiler_params=pltpu.CompilerParams(
            dimension_semantics=("arbitrary",), vmem_limit_bytes=VMEM_LIMIT),
        name="post",
    )(*args)


MOE_TILE = 512
POS_BITS = 15
PAIR_BITS = (0b0011, 0b0101, 0b1001, 0b1010, 0b0110, 0b1100)
N_CLASSES = N_GROUPS * len(PAIR_BITS)


def _dispatch_kernel(slots_ref, h_ref, hs_hbm, sem):
    tm = h_ref.shape[0]
    i = pl.program_id(0)

    def issue(w, carry):
        for r in range(8):
            row = w * 8 + r
            s = slots_ref[i * tm + row]
            pltpu.make_async_copy(h_ref.at[pl.ds(row, 1)], hs_hbm.at[pl.ds(s, 1)], sem).start()
        return carry

    lax.fori_loop(0, tm // 8, issue, 0)
    pltpu.make_async_copy(h_ref, hs_hbm.at[pl.ds(0, tm)], sem).wait()


def _dispatch(slots, h2):
    t, d = h2.shape
    tm = MOE_TILE
    return pl.pallas_call(
        _dispatch_kernel,
        grid_spec=pltpu.PrefetchScalarGridSpec(
            num_scalar_prefetch=1,
            grid=(t // tm,),
            in_specs=[pl.BlockSpec((tm, d), lambda i, s: (i, 0))],
            out_specs=pl.BlockSpec(memory_space=pl.ANY),
            scratch_shapes=[pltpu.SemaphoreType.DMA(())],
        ),
        out_shape=jax.ShapeDtypeStruct((t, d), F32),
        compiler_params=pltpu.CompilerParams(
            dimension_semantics=("arbitrary",), vmem_limit_bytes=VMEM_LIMIT),
        name="dispatch",
    )(slots, h2)


def _moe_kernel(item_tile_ref, item_expert_ref, item_first_ref, n_items_ref,
                hs_ref, w1_ref, w3_ref, w2_ref, ys_ref, hb_ref):
    s = pl.program_id(0)
    d = ys_ref.shape[1]

    @pl.when(s < n_items_ref[0])
    def _():
        lane = lax.broadcasted_iota(jnp.int32, (1, LANES), 1)
        first_of_tile = item_first_ref[s] == 1

        @pl.when(first_of_tile)
        def _():
            hb_ref[...] = _bf(hs_ref[:, 0:d])

        gates = hs_ref[:, d:d + LANES]
        ge = jnp.sum(jnp.where(lane == item_expert_ref[s], gates, 0.0), axis=-1, keepdims=True)
        hb = hb_ref[...]
        a = _silu(_dot(hb, _bf(w1_ref[0, 0]))) * _dot(hb, _bf(w3_ref[0, 0]))
        out = _dot(_bf(a * ge), _bf(w2_ref[0, 0]))

        @pl.when(first_of_tile)
        def _():
            ys_ref[...] = out

        @pl.when(jnp.logical_not(first_of_tile))
        def _():
            ys_ref[...] += out


def _moe(items, hs, w1, w3, w2, layer):
    item_tile, item_expert, item_first, n_items = items
    rows = hs.shape[0]
    d = w1.shape[2]
    up = pl.BlockSpec((1, 1, d, D_EXPERT), lambda s, it, ie, fi, n: (layer, ie[s], 0, 0))
    down = pl.BlockSpec((1, 1, D_EXPERT, d), lambda s, it, ie, fi, n: (layer, ie[s], 0, 0))
    return pl.pallas_call(
        _moe_kernel,
        grid_spec=pltpu.PrefetchScalarGridSpec(
            num_scalar_prefetch=4,
            grid=(item_tile.shape[0],),
            in_specs=[pl.BlockSpec((MOE_TILE, d + LANES), lambda s, it, ie, fi, n: (it[s], 0)), up, up, down],
            out_specs=pl.BlockSpec((MOE_TILE, d), lambda s, it, ie, fi, n: (it[s], 0)),
            scratch_shapes=[pltpu.VMEM((MOE_TILE, d), BF16)],
        ),
        out_shape=jax.ShapeDtypeStruct((rows, d), F32),
        compiler_params=pltpu.CompilerParams(
            dimension_semantics=("arbitrary",), vmem_limit_bytes=VMEM_LIMIT),
        name="moe",
    )(item_tile, item_expert, item_first, n_items, hs, w1, w3, w2)


def _class_experts():
    member = np.zeros((N_CLASSES, N_EXPERTS), np.int32)
    for g in range(N_GROUPS):
        for p, bits in enumerate(PAIR_BITS):
            for e in range(EXPERTS_PER_GROUP):
                if bits >> e & 1:
                    member[g * len(PAIR_BITS) + p, g * EXPERTS_PER_GROUP + e] = 1
    return member


def _work_items(counts, n_tokens):
    n_tiles = n_tokens // MOE_TILE
    n_items_max = 2 * (n_tiles + 2 * N_CLASSES)
    ends = jnp.cumsum(counts)
    starts = ends - counts
    lo = jnp.arange(n_tiles, dtype=jnp.int32)[:, None] * MOE_TILE
    overlap = (counts[None, :] > 0) & (starts[None, :] < lo + MOE_TILE) & (ends[None, :] > lo)
    needed = (overlap.astype(jnp.int32) @ jnp.asarray(_class_experts())) > 0
    flat = needed.reshape(-1)
    n_items = jnp.sum(flat.astype(jnp.int32))
    place = jnp.where(flat, jnp.cumsum(flat.astype(jnp.int32)) - 1, n_items_max)
    ids = jnp.zeros((n_items_max,), jnp.int32).at[place].set(
        jnp.arange(flat.shape[0], dtype=jnp.int32), mode="drop")
    ids = ids[jnp.minimum(jnp.arange(n_items_max), n_items - 1)]
    item_tile = ids // N_EXPERTS
    item_expert = ids % N_EXPERTS
    item_first = jnp.concatenate([jnp.ones((1,), jnp.int32),
                                  (item_tile[1:] != item_tile[:-1]).astype(jnp.int32)])
    return starts.astype(jnp.int32), (item_tile, item_expert, item_first, n_items.reshape(1))


def _combine_kernel(alpha, slots_ref, ys_hbm, x1_ref, mod_ref, lng_ref, lnb_ref, o_ref, buf, sem):
    tm = o_ref.shape[0]
    i = pl.program_id(0)
    n = pl.num_programs(0)

    def gather(tile, b):
        def body(w, carry):
            for r in range(8):
                row = w * 8 + r
                s = slots_ref[tile * tm + row]
                pltpu.make_async_copy(ys_hbm.at[pl.ds(s, 1)], buf.at[b, pl.ds(row, 1)], sem.at[b]).start()
            return carry
        lax.fori_loop(0, tm // 8, body, 0)

    @pl.when(i == 0)
    def _():
        gather(0, 0)

    @pl.when(i + 1 < n)
    def _():
        gather(i + 1, (i + 1) % 2)

    cur = i % 2
    pltpu.make_async_copy(ys_hbm.at[pl.ds(0, tm)], buf.at[cur], sem.at[cur]).wait()
    g2 = mod_ref[0, 5:6, :]
    o_ref[...] = _layer_norm(alpha * x1_ref[...] + g2 * buf[cur], lng_ref[...], lnb_ref[...])


def _combine(slots, ys, x1, modl, ln_g, ln_b, seq, alpha):
    t, d = x1.shape
    tm = 256
    per_batch = seq // tm
    row = lambda a: a.reshape(1, -1)
    tile = pl.BlockSpec((tm, d), lambda i, s: (i, 0))
    vec = pl.BlockSpec((1, d), lambda i, s: (0, 0))
    return pl.pallas_call(
        functools.partial(_combine_kernel, alpha),
        grid_spec=pltpu.PrefetchScalarGridSpec(
            num_scalar_prefetch=1,
            grid=(t // tm,),
            in_specs=[pl.BlockSpec(memory_space=pl.ANY), tile,
                      pl.BlockSpec((1, 6, d), lambda i, s: (i // per_batch, 0, 0)), vec, vec],
            out_specs=tile,
            scratch_shapes=[pltpu.VMEM((2, tm, d), F32), pltpu.SemaphoreType.DMA((2,))],
        ),
        out_shape=jax.ShapeDtypeStruct((t, d), F32),
        compiler_params=pltpu.CompilerParams(
            dimension_semantics=("arbitrary",), vmem_limit_bytes=VMEM_LIMIT),
        name="combine",
    )(slots, ys, x1, modl, row(ln_g), row(ln_b))


def _regroup_w_in(w):
    a = 8 * GW
    b = a + 2 * N_HEADS
    cc = b + 4 * GW
    dd = cc + GLA_RANK
    pad = jnp.zeros((w.shape[0], LANES - 2 * N_HEADS - GLA_RANK), w.dtype)
    return jnp.concatenate([w[:, :a], w[:, b:cc], w[:, dd:], w[:, a:b], w[:, cc:dd], pad], axis=1)


def _rotary_tables(seq):
    inv = ROPE_BASE ** (-jnp.arange(0, HEAD_DIM, 2, dtype=F32) / HEAD_DIM)
    ang = jnp.arange(seq, dtype=F32)[:, None] * inv[None, :]
    cos = jnp.cos(ang)
    sin = jnp.sin(ang)
    cosf = jnp.tile(jnp.concatenate([cos, cos], axis=1), (1, N_HEADS))
    sins = jnp.tile(jnp.concatenate([-sin, sin], axis=1), (1, N_HEADS))
    return cosf, sins


def kernel(x, c, ada_w, ada_b, w_in, mlstm_conv, mlstm_gate_b, gla_w2, gla_b2, hgrn_lb, ret_norm, mlstm_norm, gla_norm, hgrn_norm, w_out, ln1_g, ln1_b, router_w, router_b, exp_w1, exp_w3, exp_w2, ln2_g, ln2_b):
    bsz, seq, d = x.shape
    depth = ada_w.shape[0]
    assert d == D_MODEL and seq % 512 == 0 and bsz <= 16
    t = bsz * seq
    alpha = (2.0 * depth) ** 0.25
    assert t % MOE_TILE == 0 and t < (1 << POS_BITS) and bsz % MIXER_BATCH == 0
    consts = _mixer_consts()
    cosf, sins = _rotary_tables(seq)
    c_rows = jnp.pad(c, ((0, 16 - bsz), (0, 0)))
    mod = _adaln(c_rows, ada_w, ada_b)[:, :bsz]
    rw_pad = jnp.pad(router_w, ((0, 0), (0, LANES - N_EXPERTS)))
    rb_pad = jnp.pad(router_b, (0, LANES - N_EXPERTS)).reshape(1, LANES)
    xt = x.reshape(t, d)
    for l in range(depth):
        modl = mod[l].reshape(bsz, 6, d)
        gate_rows = jnp.broadcast_to(mlstm_gate_b[l][:, None], (2 * N_HEADS, CHUNK))
        w2_pad = jnp.zeros((LANES, GW), F32).at[2 * N_HEADS:2 * N_HEADS + GLA_RANK].set(gla_w2[l])
        y = _mixer(xt, modl, _bf(_regroup_w_in(w_in[l])), cosf, sins, mlstm_conv[l], gate_rows, w2_pad,
                   gla_b2[l], hgrn_lb, ret_norm[l], mlstm_norm[l], gla_norm[l], hgrn_norm[l], consts,
                   l, bsz, seq)
        x1, h2, codes, counts = _post(y, xt, modl, _bf(w_out[l]), ln1_g[l], ln1_b[l], rw_pad, rb_pad,
                                      seq, alpha)
        class_start, items = _work_items(counts[0, :N_CLASSES], t)
        codes = codes.reshape(t)
        in_class = (codes >> POS_BITS)[:, None] == jnp.arange(N_CLASSES, dtype=jnp.int32)[None, :]
        slots = jnp.sum(jnp.where(in_class, class_start[None, :], 0), axis=1) + (codes & ((1 << POS_BITS) - 1))
        hs = _dispatch(slots, h2)
        ys = _moe(items, hs, exp_w1, exp_w3, exp_w2, l)
        xt = _combine(slots, ys, x1, modl, ln2_g[l], ln2_b[l], seq, alpha)
    return xt.reshape(bsz, seq, d)
```

```python
import functools

import numpy as np
import jax
import jax.numpy as jnp
from jax import lax
from jax.experimental import pallas as pl
from jax.experimental.pallas import tpu as pltpu

F32 = jnp.float32
BF16 = jnp.bfloat16

D_MODEL = 1024
N_MIXERS = 4
GW = D_MODEL // N_MIXERS
N_HEADS = 4
HEAD_DIM = GW // N_HEADS
CONV_WIDTH = 4
GLA_RANK = 16
GLA_TAU = 16.0
ROPE_BASE = 10000.0
N_GROUPS = 4
EXPERTS_PER_GROUP = 4
N_EXPERTS = 16
D_EXPERT = D_MODEL // 2
LN_EPS = 1e-5
NORM_EPS = 1e-6
QK_SCALE = HEAD_DIM ** -0.5

LANES = 128
CHUNK = 128
LEVELS = 7
CONV_PAD = 8
NPROJ = 16 * GW + LANES
EXTRA = 16 * GW
VMEM_LIMIT = 56 * 1024 * 1024

(R_Q, R_K, R_V, R_G, M_Q, M_K, M_V, M_O, G_Q, G_K, G_V, G_G, H_Q, H_F, H_I, H_G) = (
    GW * i for i in range(16))


def _bf(x):
    return x.astype(BF16)


def _dot(a, b):
    return jnp.dot(a, b, preferred_element_type=F32)


def _dot_nt(a, b):
    return lax.dot_general(a, b, (((1,), (1,)), ((), ())), preferred_element_type=F32)


def _split2(x):
    hi = _bf(x)
    lo = _bf(x - hi.astype(F32))
    return hi, lo


def _dot_f32(a, b):
    ah, al = _split2(a)
    bh, bl = _split2(b)
    return _dot(ah, bh) + _dot(ah, bl) + _dot(al, bh)


def _sigmoid(x):
    return 0.5 * jnp.tanh(0.5 * x) + 0.5


def _silu(x):
    return x * _sigmoid(x)


def _log_sigmoid(x):
    return jnp.minimum(x, 0.0) - jnp.log1p(jnp.exp(-jnp.abs(x)))


def _layer_norm(v, g, b):
    mu = jnp.mean(v, axis=-1, keepdims=True)
    d = v - mu
    var = jnp.mean(d * d, axis=-1, keepdims=True)
    return d * lax.rsqrt(var + LN_EPS) * g + b


def _adaln_kernel(c_ref, w_ref, b_ref, o_ref):
    cond = _silu(c_ref[...])
    o_ref[0] = _dot_f32(cond, w_ref[0]) + b_ref[0]


def _adaln(c, ada_w, ada_b):
    depth, d, n = ada_w.shape
    bsz = c.shape[0]
    tn = 1536
    return pl.pallas_call(
        _adaln_kernel,
        grid=(depth, n // tn),
        in_specs=[
            pl.BlockSpec((bsz, d), lambda l, j: (0, 0)),
            pl.BlockSpec((1, d, tn), lambda l, j: (l, 0, j)),
            pl.BlockSpec((1, 1, tn), lambda l, j: (l, 0, j)),
        ],
        out_specs=pl.BlockSpec((1, bsz, tn), lambda l, j: (l, 0, j)),
        out_shape=jax.ShapeDtypeStruct((depth, bsz, n), F32),
        compiler_params=pltpu.CompilerParams(
            dimension_semantics=("arbitrary", "arbitrary"), vmem_limit_bytes=VMEM_LIMIT),
        name="adaln",
    )(c, ada_w, ada_b.reshape(depth, 1, n))


def _mixer_consts():
    c = CHUNK
    i = np.arange(c)[:, None]
    j = np.arange(c)[None, :]
    rel = i - j
    log_gamma = np.log1p(-np.exp2(-5.0 - np.arange(N_HEADS, dtype=np.float64)))
    dret = np.concatenate(
        [np.where(rel >= 0, np.exp(log_gamma[h] * np.maximum(rel, 0)), 0.0) for h in range(N_HEADS)], axis=1)
    lane_head = np.arange(GW) // HEAD_DIM
    dq = np.exp(log_gamma[lane_head][None, :] * (np.arange(c)[:, None] + 1.0))
    dk = np.exp(log_gamma[lane_head][None, :] * (c - 1.0 - np.arange(c)[:, None]))
    gc = np.exp(log_gamma[lane_head] * c)[None, :]
    lv = np.zeros((c, c), np.int32)
    lv[rel == 0] = 1
    spans = [(rel >= 0), (rel < 0)]
    code, b = 2, 1
    while b < c:
        lv[(i // (2 * b) == j // (2 * b)) & ((i // b) % 2 == 1) & ((j // b) % 2 == 0)] = code
        ref = (i // (2 * b)) * 2 * b + b - 1
        right = (i // b) % 2 == 1
        spans.append(np.where(right, (j > ref) & (j <= i), (j > i) & (j <= ref)))
        code, b = code + 1, b * 2
    lv = np.tile(lv, (1, N_HEADS))
    span = np.concatenate(spans, axis=0).astype(np.float32)
    span = np.tile(span, (1, 2))
    hh = (lane_head[:, None] == lane_head[None, :]).astype(np.float32)
    ones_bd = np.repeat(np.eye(N_HEADS, dtype=np.float32), c, axis=0)
    ones_bd = np.repeat(ones_bd, HEAD_DIM, axis=1)
    return dict(
        dret=jnp.asarray(dret, F32), dq=jnp.asarray(dq, F32), dk=jnp.asarray(dk, F32),
        gc=jnp.asarray(gc, F32), lv=jnp.asarray(lv), hh=jnp.asarray(hh, BF16),
        bdf=jnp.asarray(hh, F32), ones_bd=jnp.asarray(ones_bd, BF16), span=jnp.asarray(span, BF16))


def _lane_cumsum(x):
    lane = lax.broadcasted_iota(jnp.int32, x.shape, 1)
    s = 1
    while s < x.shape[1]:
        x = x + jnp.where(lane >= s, pltpu.roll(x, s, 1), 0.0)
        s *= 2
    return x


def _lane_cummax(x):
    lane = lax.broadcasted_iota(jnp.int32, x.shape, 1)
    s = 1
    while s < x.shape[1]:
        x = jnp.maximum(x, jnp.where(lane >= s, pltpu.roll(x, s, 1), -jnp.inf))
        s *= 2
    return x


def _mixer_kernel(layer, nb,
                  x_ref, mod_ref, win_ref, cos_ref, sin_ref, conv_ref, gb_ref, w2_ref, b2_ref, lb_ref,
                  retg_ref, mlg_ref, glag_ref, hgg_ref,
                  dret_ref, dq_ref, dk_ref, gc_ref, lv_ref, hh_ref, bdf_ref, onesbd_ref, span_ref,
                  y_ref,
                  s_ret, s_cn, s_gla, s_hg, m_ml, conv_buf):
    c = CHUNK
    batches = range(nb)

    @pl.when(pl.program_id(1) == 0)
    def _():
        for s in (s_ret, s_cn, s_gla, s_hg, m_ml):
            s[...] = jnp.zeros_like(s)
        conv_buf[:, 0:CONV_PAD, :] = jnp.zeros((nb, CONV_PAD, 2 * GW), F32)

    lane_head = lax.broadcasted_iota(jnp.int32, (1, GW), 1) // HEAD_DIM
    head_rows = [(lane_head == h).astype(BF16) for h in range(N_HEADS)]
    hh = hh_ref[...]
    bdf = bdf_ref[...]
    lv = lv_ref[...]
    causal = lv >= 1

    def stack(xs):
        return jnp.concatenate(xs, axis=0)

    def tiled(x):
        return stack([x] * nb)

    def seq(x, b):
        return x[b * c:(b + 1) * c]

    hb = _bf(stack([x_ref[b] * (1.0 + mod_ref[b, 1:2, :]) + mod_ref[b, 0:1, :] for b in batches]))

    issued = {}

    def issue(first):
        issued[first] = _dot(hb, win_ref[:, first:first + N_MIXERS * GW])

    def col(off, width=GW):
        first = off // (N_MIXERS * GW) * (N_MIXERS * GW)
        return issued[first][:, off - first:off - first + width]

    issue(R_Q)
    issue(M_Q)
    ext = _dot(hb, win_ref[:, EXTRA:EXTRA + LANES])

    def put(off, val):
        for b in batches:
            y_ref[b, :, off:off + GW] = seq(val, b).astype(y_ref.dtype)

    def stackmask(xb):
        return jnp.concatenate([xb * head_rows[h] for h in range(N_HEADS)], axis=0)

    def expand_heads(cols, first):
        out = jnp.zeros((cols.shape[0], GW), F32)
        for h in range(N_HEADS):
            out = jnp.where(lane_head == h, cols[:, first + h:first + h + 1], out)
        return out

    def head_mean(v):
        return _dot(_bf(v), hh) * (1.0 / HEAD_DIM)

    def head_norm(v, gain, center):
        if center:
            v = v - head_mean(v)
        return v * lax.rsqrt(head_mean(v * v) + NORM_EPS) * gain

    cosf = tiled(cos_ref[...])
    sins = tiled(sin_ref[...])
    half = lax.broadcasted_iota(jnp.int32, (1, GW), 1) % HEAD_DIM < HEAD_DIM // 2

    def rotary(v):
        swapped = jnp.where(half, pltpu.roll(v, GW - HEAD_DIM // 2, 1), pltpu.roll(v, HEAD_DIM // 2, 1))
        return v * cosf + swapped * sins

    rq = _bf(rotary(col(R_Q)) * QK_SCALE)
    rk = rotary(col(R_K))
    rkb = _bf(rk)
    rvb = _bf(col(R_V))
    rkd = rk * tiled(dk_ref[...])
    dret = dret_ref[...]
    dq = dq_ref[...]
    gc = gc_ref[...]
    sc = [_dot_nt(seq(rq, b), stackmask(seq(rkb, b))) for b in batches]
    inter = [_dot(seq(rq, b), _bf(s_ret[b])) for b in batches]
    scb = [_bf(s * dret) for s in sc]
    ro = stack([_dot(scb[b], stackmask(seq(rvb, b))) + inter[b] * dq for b in batches])
    upd = [_dot(_bf(seq(rkd, b).T), seq(rvb, b)) for b in batches]
    for b in batches:
        s_ret[b] = s_ret[b] * gc + upd[b] * bdf
    put(0, head_norm(ro, retg_ref[...], True) * _silu(col(R_G)))

    issue(G_Q)
    mqk = col(M_Q, 2 * GW)
    for b in batches:
        conv_buf[b, CONV_PAD:CONV_PAD + c, :] = seq(mqk, b)
    qk = jnp.zeros((nb * c, 2 * GW), F32)
    for t in range(CONV_WIDTH):
        s = CONV_WIDTH - 1 - t
        qk = qk + stack([conv_buf[b, CONV_PAD - s:CONV_PAD - s + c, :] for b in batches]) * conv_ref[t:t + 1, :]
    for b in batches:
        conv_buf[b, 0:CONV_PAD, :] = conv_buf[b, c:c + CONV_PAD, :]
    qk = _silu(qk)
    mq = _bf(qk[:, 0:GW])
    mk = qk[:, GW:2 * GW] * QK_SCALE
    mkb = _bf(mk)
    mvb = _bf(col(M_V))

    pre = stack([seq(ext, b).T[0:8, :] for b in batches]) + tiled(gb_ref[...])
    row8 = lax.broadcasted_iota(jnp.int32, (8 * nb, c), 0)
    valid = row8 % 8 >= N_HEADS
    lf = jnp.where(valid, _log_sigmoid(pre), 0.0)
    gi = jnp.where(valid, pltpu.roll(pre, N_HEADS, 0), 0.0)
    bcum = _lane_cumsum(lf)
    a = gi - bcum
    m_prev = m_ml[...]
    mrow = jnp.maximum(m_prev, _lane_cummax(a))
    m_last = jnp.broadcast_to(mrow[:, c - 1:c], (8 * nb, c))
    b_last = jnp.broadcast_to(bcum[:, c - 1:c], (8 * nb, c))
    wi = jnp.exp(m_prev - mrow)
    em = jnp.exp(-(bcum + mrow))
    wk = jnp.exp(a - m_last)
    dec = jnp.exp(m_prev - m_last)
    m_ml[...] = jnp.where(valid, b_last + m_last, 0.0)
    kinds = [mrow, wi, em, wk]
    if 32 * nb < c:
        kinds.append(jnp.zeros((c - 32 * nb, c), F32))
    cols = stack(kinds).T

    def first_col(kind, b):
        return kind * 8 * nb + 8 * b + N_HEADS

    mexp = [jnp.concatenate([jnp.broadcast_to(cols[:, first_col(0, b) + h:first_col(0, b) + h + 1], (c, c))
                             for h in range(N_HEADS)], axis=1) for b in batches]
    a_row = [jnp.concatenate([a[8 * b + N_HEADS + h:8 * b + N_HEADS + h + 1, :] for h in range(N_HEADS)], axis=1)
             for b in batches]
    qkm = [_dot_nt(seq(mq, b), stackmask(seq(mkb, b))) for b in batches]
    inter = [_dot(seq(mq, b), _bf(s_cn[b])) for b in batches]
    sm = [_bf(qkm[b] * jnp.exp(jnp.where(causal, a_row[b] - mexp[b], -jnp.inf))) for b in batches]
    onesbd = onesbd_ref[...]
    wi_l = stack([expand_heads(cols, first_col(1, b)) for b in batches])
    em_l = stack([expand_heads(cols, first_col(2, b)) for b in batches])
    wk_l = stack([expand_heads(cols, first_col(3, b)) for b in batches])
    nd = (stack([_dot(sm[b], jnp.concatenate([stackmask(seq(mvb, b)), onesbd], axis=1)) for b in batches])
          + jnp.concatenate([wi_l, wi_l], axis=1) * stack(inter))
    mh = nd[:, 0:GW] / jnp.maximum(jnp.abs(nd[:, GW:2 * GW]), em_l)
    kt = mk * wk_l
    ones = jnp.ones((c, GW), BF16)
    upd = [_dot(_bf(seq(kt, b).T), jnp.concatenate([seq(mvb, b), ones], axis=1)) for b in batches]
    dec2 = jnp.concatenate([dec, dec], axis=1)
    bdf2 = jnp.concatenate([bdf, bdf], axis=1)
    for b in batches:
        dec_l = jnp.zeros((1, GW), F32)
        for h in range(N_HEADS):
            dec_l = jnp.where(lane_head == h, dec2[8 * b + N_HEADS + h:8 * b + N_HEADS + h + 1, :], dec_l)
        s_cn[b] = s_cn[b] * jnp.concatenate([dec_l, dec_l], axis=1) + upd[b] * bdf2
    put(GW, head_norm(mh * _sigmoid(col(M_O)), mlg_ref[...], True))

    span = span_ref[...]
    level_masks = [lv == code for code in range(2, 2 + LEVELS)]

    def decay_attention(q, k, v, g, st_ref):
        g2 = _split2(g)
        dec = [jnp.exp(_dot(span, stack([seq(part, b) for part in g2]))) for b in batches]
        e_cum = stack([d[0:c] for d in dec])
        e_end = stack([d[c:2 * c] for d in dec])
        qh = _bf(q)
        kh = _bf(k)
        qe = qh * _bf(e_cum)
        out = stack([_dot_nt(seq(qe, b), _bf(st_ref[b])) for b in batches])
        out = out + _dot(_bf(q * k), hh) * v
        scores = [jnp.zeros((c, N_HEADS * c), F32) for _ in batches]
        for lvl in range(LEVELS):
            e = _bf(stack([d[(2 + lvl) * c:(3 + lvl) * c] for d in dec]))
            qb = qh * e
            kb = kh * e
            level = [_dot_nt(seq(qb, b), stackmask(seq(kb, b))) for b in batches]
            scores = [jnp.where(level_masks[lvl], level[b], scores[b]) for b in batches]
        vb = _bf(v)
        out = out + stack([_dot(_bf(scores[b]), stackmask(seq(vb, b))) for b in batches])
        ke = kh * _bf(e_end)
        upd = [_dot(_bf(seq(v, b).T), seq(ke, b)) for b in batches]
        for b in batches:
            st_ref[b] = st_ref[b] * seq(e_cum, b)[c - 1:c, :] + upd[b] * bdf
        return out

    issue(H_Q)
    x_lr = _dot_f32(ext, w2_ref[...]) + b2_ref[...]
    log_a = _log_sigmoid(x_lr) * (1.0 / GLA_TAU)
    go = decay_attention(col(G_Q), col(G_K) * QK_SCALE, col(G_V), log_a, s_gla)
    put(2 * GW, head_norm(go, glag_ref[...], False) * _silu(col(G_G)))

    lb_all = lb_ref[...]
    lb_e = jnp.exp(lb_all - jnp.max(lb_all, axis=0, keepdims=True))
    lb_p = lb_e / jnp.sum(lb_e, axis=0, keepdims=True)
    lb = jnp.zeros((1, GW), F32)
    for l in range(1, layer + 1):
        lb = lb + lb_p[l:l + 1, :]
    z = col(H_F)
    x1 = jnp.log(lb)
    x2 = jnp.log1p(-lb) + _log_sigmoid(z)
    log_f = jnp.maximum(x1, x2) + jnp.log1p(jnp.exp(-jnp.abs(x1 - x2)))
    k_h = (1.0 - lb) * _sigmoid(-z)
    ho = decay_attention(_silu(col(H_Q)), k_h, col(H_I), log_f, s_hg)
    put(3 * GW, head_norm(ho * _sigmoid(col(H_G)), hgg_ref[...], False))


MIXER_BATCH = 4


def _mixer(xt, modl, w_in_p, cosf, sins, conv_w, gate_rows, w2_pad, b2, lb, ret_g, ml_g, gla_g, hg_g, consts,
           layer, bsz, seq):
    c = CHUNK
    nc = seq // c
    t, d = xt.shape
    row = lambda a: a.reshape(1, -1)
    full = lambda a: pl.BlockSpec(a.shape, lambda b, n: (0,) * a.ndim)
    cs = consts
    args = [xt.reshape(bsz, seq, d), modl, w_in_p, cosf, sins, conv_w, gate_rows, w2_pad, row(b2), lb,
            row(ret_g), row(ml_g), row(gla_g), row(hg_g),
            cs["dret"], cs["dq"], cs["dk"], cs["gc"], cs["lv"], cs["hh"], cs["bdf"], cs["ones_bd"], cs["span"]]
    nb = MIXER_BATCH
    in_specs = [
        pl.BlockSpec((nb, c, d), lambda b, n: (b, n, 0)),
        pl.BlockSpec((nb, 6, d), lambda b, n: (b, 0, 0)),
        full(w_in_p),
        pl.BlockSpec((c, GW), lambda b, n: (n, 0)),
        pl.BlockSpec((c, GW), lambda b, n: (n, 0)),
    ] + [full(a) for a in args[5:]]
    y = pl.pallas_call(
        functools.partial(_mixer_kernel, layer, nb),
        grid=(bsz // nb, nc),
        in_specs=in_specs,
        out_specs=pl.BlockSpec((nb, c, D_MODEL), lambda b, n: (b, n, 0)),
        out_shape=jax.ShapeDtypeStruct((bsz, seq, D_MODEL), BF16),
        scratch_shapes=[
            pltpu.VMEM((nb, GW, GW), F32),
            pltpu.VMEM((nb, GW, 2 * GW), F32),
            pltpu.VMEM((nb, GW, GW), F32),
            pltpu.VMEM((nb, GW, GW), F32),
            pltpu.VMEM((nb * 8, c), F32),
            pltpu.VMEM((nb, c + CONV_PAD, 2 * GW), F32),
        ],
        compiler_params=pltpu.CompilerParams(
            dimension_semantics=("arbitrary", "arbitrary"), vmem_limit_bytes=VMEM_LIMIT),
        name="mixer",
    )(*args)
    return y.reshape(t, D_MODEL)


def _group_shift(v, pos, k):
    return jnp.where(pos < EXPERTS_PER_GROUP - k,
                     pltpu.roll(v, LANES - k, 1), pltpu.roll(v, EXPERTS_PER_GROUP - k, 1))


def _router(h, rw, rb):
    lane = lax.broadcasted_iota(jnp.int32, (1, LANES), 1)
    real = lane < N_EXPERTS
    logits = jnp.where(real, _dot_f32(h, rw), -jnp.inf)
    ex = jnp.exp(logits - jnp.max(logits, axis=-1, keepdims=True))
    probs = ex / jnp.sum(ex, axis=-1, keepdims=True)
    sel = jnp.where(real, probs + rb, -jnp.inf)
    pos = lane % EXPERTS_PER_GROUP
    gid = lane // EXPERTS_PER_GROUP
    r1 = _group_shift(sel, pos, 1)
    r2 = _group_shift(sel, pos, 2)
    r3 = _group_shift(sel, pos, 3)
    pair = jnp.maximum(jnp.maximum(jnp.maximum(sel + r1, sel + r2), jnp.maximum(sel + r3, r1 + r2)),
                       jnp.maximum(r1 + r3, r2 + r3))
    pair = jnp.where(real, pair, -jnp.inf)
    best = jnp.max(pair, axis=-1, keepdims=True)
    first = jnp.min(jnp.where(pair == best, gid, N_GROUPS), axis=-1, keepdims=True)
    rank = jnp.zeros(sel.shape, jnp.int32)
    for k, r in ((1, r1), (2, r2), (3, r3)):
        beats = (r > sel) | ((r == sel) & (pos + k >= EXPERTS_PER_GROUP))
        rank = rank + beats.astype(jnp.int32)
    return probs, first, rank


def _post_kernel(alpha, y_ref, x_ref, mod_ref, wout_ref, lng_ref, lnb_ref, rw_ref, rb_ref,
                 stri_ref, x1_ref, h2_ref, slots_ref, counts_ref, carry_ref):
    tm = y_ref.shape[0]

    @pl.when(pl.program_id(0) == 0)
    def _():
        carry_ref[...] = jnp.zeros_like(carry_ref)

    g1 = mod_ref[0, 2:3, :]
    sh2 = mod_ref[0, 3:4, :]
    sc2 = mod_ref[0, 4:5, :]
    mix = _dot(_bf(y_ref[...]), wout_ref[...])
    x1 = _layer_norm(alpha * x_ref[...] + g1 * mix, lng_ref[...], lnb_ref[...])
    x1_ref[...] = x1
    h = x1 * (1.0 + sc2) + sh2
    d = h.shape[1]
    h2_ref[:, 0:d] = h

    probs, first, rank = _router(h, rw_ref[...], rb_ref[...])
    lane = lax.broadcasted_iota(jnp.int32, (1, LANES), 1)
    chosen = (lane // EXPERTS_PER_GROUP == first) & (rank < 2)
    w = jnp.where(chosen, probs, 0.0)
    h2_ref[:, d:d + LANES] = w / jnp.sum(w, axis=-1, keepdims=True)

    bit = jnp.left_shift(1, lane % EXPERTS_PER_GROUP).astype(F32)
    bits = jnp.sum(jnp.where(chosen, bit, 0.0), axis=-1, keepdims=True)
    pair = jnp.full(bits.shape, len(PAIR_BITS) - 1, jnp.int32)
    for index in range(len(PAIR_BITS) - 2, -1, -1):
        pair = jnp.where(bits == float(PAIR_BITS[index]), index, pair)
    cls = first * len(PAIR_BITS) + pair
    onehot = jnp.where(lane == cls, 1.0, 0.0)
    before = _dot(stri_ref[...], _bf(onehot)) + carry_ref[...]
    code = cls.astype(F32) * float(1 << POS_BITS) + jnp.sum(onehot * before, axis=-1, keepdims=True)
    slots_ref[0] = jnp.broadcast_to(code, (tm, LANES)).T[0:1, :].astype(jnp.int32)
    carry_ref[...] += jnp.sum(onehot, axis=0, keepdims=True)
    counts_ref[...] = carry_ref[...].astype(jnp.int32)


def _post(y, xt, modl, w_out, ln_g, ln_b, rw_pad, rb_pad, seq, alpha):
    t, d = xt.shape
    tm = 256
    per_batch = seq // tm
    row = lambda a: a.reshape(1, -1)
    tile = pl.BlockSpec((tm, d), lambda i: (i, 0))
    full = lambda a: pl.BlockSpec(a.shape, lambda i: (0,) * a.ndim)
    stri = jnp.asarray(np.tril(np.ones((tm, tm), np.float32), -1), BF16)
    args = [y, xt, modl, w_out, row(ln_g), row(ln_b), rw_pad, rb_pad, stri]
    return pl.pallas_call(
        functools.partial(_post_kernel, alpha),
        grid=(t // tm,),
        in_specs=[tile, tile, pl.BlockSpec((1, 6, d), lambda i: (i // per_batch, 0, 0))]
        + [full(a) for a in args[3:]],
        out_specs=[tile, pl.BlockSpec((tm, d + LANES), lambda i: (i, 0)),
                   pl.BlockSpec((1, 1, tm), lambda i: (i, 0, 0)),
                   pl.BlockSpec((1, LANES), lambda i: (0, 0))],
        out_shape=[jax.ShapeDtypeStruct((t, d), F32), jax.ShapeDtypeStruct((t, d + LANES), F32),
                   jax.ShapeDtypeStruct((t // tm, 1, tm), jnp.int32),
                   jax.ShapeDtypeStruct((1, LANES), jnp.int32)],
        scratch_shapes=[pltpu.VMEM((1, LANES), F32)],
        compiler_params=pltpu.CompilerParams(
            dimension_semantics=("arbitrary",), vmem_limit_bytes=VMEM_LIMIT),
        name="post",
    )(*args)


MOE_TILE = 512
POS_BITS = 15
PAIR_BITS = (0b0011, 0b0101, 0b1001, 0b1010, 0b0110, 0b1100)
N_CLASSES = N_GROUPS * len(PAIR_BITS)


def _dispatch_kernel(slots_ref, h_ref, hs_hbm, sem):
    tm = h_ref.shape[0]
    i = pl.program_id(0)

    for row in range(tm):
        s = slots_ref[i * tm + row]
        pltpu.make_async_copy(h_ref.at[pl.ds(row, 1)], hs_hbm.at[pl.ds(s, 1)], sem).start()
    pltpu.make_async_copy(h_ref, hs_hbm.at[pl.ds(0, tm)], sem).wait()


def _dispatch(slots, h2):
    t, d = h2.shape
    tm = MOE_TILE
    return pl.pallas_call(
        _dispatch_kernel,
        grid_spec=pltpu.PrefetchScalarGridSpec(
            num_scalar_prefetch=1,
            grid=(t // tm,),
            in_specs=[pl.BlockSpec((tm, d), lambda i, s: (i, 0))],
            out_specs=pl.BlockSpec(memory_space=pl.ANY),
            scratch_shapes=[pltpu.SemaphoreType.DMA(())],
        ),
        out_shape=jax.ShapeDtypeStruct((t, d), F32),
        compiler_params=pltpu.CompilerParams(
            dimension_semantics=("arbitrary",), vmem_limit_bytes=VMEM_LIMIT),
        name="dispatch",
    )(slots, h2)


def _moe_kernel(item_tile_ref, item_expert_ref, item_first_ref, n_items_ref,
                hs_ref, w1_ref, w3_ref, w2_ref, ys_ref, hb_ref):
    s = pl.program_id(0)
    d = ys_ref.shape[1]

    @pl.when(s < n_items_ref[0])
    def _():
        lane = lax.broadcasted_iota(jnp.int32, (1, LANES), 1)
        first_of_tile = item_first_ref[s] == 1

        @pl.when(first_of_tile)
        def _():
            hb_ref[...] = _bf(hs_ref[:, 0:d])

        gates = hs_ref[:, d:d + LANES]
        ge = jnp.sum(jnp.where(lane == item_expert_ref[s], gates, 0.0), axis=-1, keepdims=True)
        hb = hb_ref[...]
        up = _dot(hb, _bf(w1_ref[0, 0]))
        w3 = _bf(w3_ref[0, 0])
        lin = _dot(hb, w3)
        w2 = _bf(w2_ref[0, 0])
        out = _dot(_bf(_silu(up) * lin * ge), w2)

        @pl.when(first_of_tile)
        def _():
            ys_ref[...] = out

        @pl.when(jnp.logical_not(first_of_tile))
        def _():
            ys_ref[...] += out


def _moe(items, hs, w1, w3, w2, layer):
    item_tile, item_expert, item_first, n_items = items
    rows = hs.shape[0]
    d = w1.shape[2]
    up = pl.BlockSpec((1, 1, d, D_EXPERT), lambda s, it, ie, fi, n: (layer, ie[s], 0, 0))
    down = pl.BlockSpec((1, 1, D_EXPERT, d), lambda s, it, ie, fi, n: (layer, ie[s], 0, 0))
    return pl.pallas_call(
        _moe_kernel,
        grid_spec=pltpu.PrefetchScalarGridSpec(
            num_scalar_prefetch=4,
            grid=(item_tile.shape[0],),
            in_specs=[pl.BlockSpec((MOE_TILE, d + LANES), lambda s, it, ie, fi, n: (it[s], 0)), up, up, down],
            out_specs=pl.BlockSpec((MOE_TILE, d), lambda s, it, ie, fi, n: (it[s], 0)),
            scratch_shapes=[pltpu.VMEM((MOE_TILE, d), BF16)],
        ),
        out_shape=jax.ShapeDtypeStruct((rows, d), F32),
        compiler_params=pltpu.CompilerParams(
            dimension_semantics=("arbitrary",), vmem_limit_bytes=VMEM_LIMIT),
        name="moe",
    )(item_tile, item_expert, item_first, n_items, hs, w1, w3, w2)


def _class_experts():
    member = np.zeros((N_CLASSES, N_EXPERTS), np.int32)
    for g in range(N_GROUPS):
        for p, bits in enumerate(PAIR_BITS):
            for e in range(EXPERTS_PER_GROUP):
                if bits >> e & 1:
                    member[g * len(PAIR_BITS) + p, g * EXPERTS_PER_GROUP + e] = 1
    return member


def _work_items(counts, n_tokens):
    n_tiles = n_tokens // MOE_TILE
    n_items_max = 2 * (n_tiles + 2 * N_CLASSES)
    ends = jnp.cumsum(counts)
    starts = ends - counts
    lo = jnp.arange(n_tiles, dtype=jnp.int32)[:, None] * MOE_TILE
    overlap = (counts[None, :] > 0) & (starts[None, :] < lo + MOE_TILE) & (ends[None, :] > lo)
    needed = (overlap.astype(jnp.int32) @ jnp.asarray(_class_experts())) > 0
    flat = needed.reshape(-1)
    n_items = jnp.sum(flat.astype(jnp.int32))
    place = jnp.where(flat, jnp.cumsum(flat.astype(jnp.int32)) - 1, n_items_max)
    ids = jnp.zeros((n_items_max,), jnp.int32).at[place].set(
        jnp.arange(flat.shape[0], dtype=jnp.int32), mode="drop")
    ids = ids[jnp.minimum(jnp.arange(n_items_max), n_items - 1)]
    item_tile = ids // N_EXPERTS
    item_expert = ids % N_EXPERTS
    item_first = jnp.concatenate([jnp.ones((1,), jnp.int32),
                                  (item_tile[1:] != item_tile[:-1]).astype(jnp.int32)])
    return starts.astype(jnp.int32), (item_tile, item_expert, item_first, n_items.reshape(1))


def _combine_kernel(alpha, slots_ref, ys_hbm, x1_ref, mod_ref, lng_ref, lnb_ref, o_ref, buf, sem):
    tm = o_ref.shape[0]
    i = pl.program_id(0)
    n = pl.num_programs(0)

    def gather(tile, b):
        for row in range(tm):
            s = slots_ref[tile * tm + row]
            pltpu.make_async_copy(ys_hbm.at[pl.ds(s, 1)], buf.at[b, pl.ds(row, 1)], sem.at[b]).start()

    def wait(b):
        pltpu.make_async_copy(ys_hbm.at[pl.ds(0, tm)], buf.at[b], sem.at[b]).wait()

    @pl.when(i == 0)
    def _():
        gather(0, 0)

    cur = i % 2
    nxt = (i + 1) % 2
    wait(cur)
    gather(jnp.minimum(i + 1, n - 1), nxt)
    g2 = mod_ref[0, 5:6, :]
    o_ref[...] = _layer_norm(alpha * x1_ref[...] + g2 * buf[cur], lng_ref[...], lnb_ref[...])

    @pl.when(i == n - 1)
    def _():
        wait(nxt)


def _combine(slots, ys, x1, modl, ln_g, ln_b, seq, alpha):
    t, d = x1.shape
    tm = 256
    per_batch = seq // tm
    row = lambda a: a.reshape(1, -1)
    tile = pl.BlockSpec((tm, d), lambda i, s: (i, 0))
    vec = pl.BlockSpec((1, d), lambda i, s: (0, 0))
    return pl.pallas_call(
        functools.partial(_combine_kernel, alpha),
        grid_spec=pltpu.PrefetchScalarGridSpec(
            num_scalar_prefetch=1,
            grid=(t // tm,),
            in_specs=[pl.BlockSpec(memory_space=pl.ANY), tile,
                      pl.BlockSpec((1, 6, d), lambda i, s: (i // per_batch, 0, 0)), vec, vec],
            out_specs=tile,
            scratch_shapes=[pltpu.VMEM((2, tm, d), F32), pltpu.SemaphoreType.DMA((2,))],
        ),
        out_shape=jax.ShapeDtypeStruct((t, d), F32),
        compiler_params=pltpu.CompilerParams(
            dimension_semantics=("arbitrary",), vmem_limit_bytes=VMEM_LIMIT),
        name="combine",
    )(slots, ys, x1, modl, row(ln_g), row(ln_b))


def _regroup_w_in(w):
    a = 8 * GW
    b = a + 2 * N_HEADS
    cc = b + 4 * GW
    dd = cc + GLA_RANK
    pad = jnp.zeros((w.shape[0], LANES - 2 * N_HEADS - GLA_RANK), w.dtype)
    return jnp.concatenate([w[:, :a], w[:, b:cc], w[:, dd:], w[:, a:b], w[:, cc:dd], pad], axis=1)


def _rotary_tables(seq):
    inv = ROPE_BASE ** (-jnp.arange(0, HEAD_DIM, 2, dtype=F32) / HEAD_DIM)
    ang = jnp.arange(seq, dtype=F32)[:, None] * inv[None, :]
    cos = jnp.cos(ang)
    sin = jnp.sin(ang)
    cosf = jnp.tile(jnp.concatenate([cos, cos], axis=1), (1, N_HEADS))
    sins = jnp.tile(jnp.concatenate([-sin, sin], axis=1), (1, N_HEADS))
    return cosf, sins


def kernel(x, c, ada_w, ada_b, w_in, mlstm_conv, mlstm_gate_b, gla_w2, gla_b2, hgrn_lb, ret_norm, mlstm_norm, gla_norm, hgrn_norm, w_out, ln1_g, ln1_b, router_w, router_b, exp_w1, exp_w3, exp_w2, ln2_g, ln2_b):
    bsz, seq, d = x.shape
    depth = ada_w.shape[0]
    assert d == D_MODEL and seq % 512 == 0 and bsz <= 16
    t = bsz * seq
    alpha = (2.0 * depth) ** 0.25
    assert t % MOE_TILE == 0 and t < (1 << POS_BITS) and bsz % MIXER_BATCH == 0
    consts = _mixer_consts()
    cosf, sins = _rotary_tables(seq)
    c_rows = jnp.pad(c, ((0, 16 - bsz), (0, 0)))
    mod = _adaln(c_rows, ada_w, ada_b)[:, :bsz]
    rw_pad = jnp.pad(router_w, ((0, 0), (0, LANES - N_EXPERTS)))
    rb_pad = jnp.pad(router_b, (0, LANES - N_EXPERTS)).reshape(1, LANES)
    xt = x.reshape(t, d)
    for l in range(depth):
        modl = mod[l].reshape(bsz, 6, d)
        gate_rows = jnp.broadcast_to(mlstm_gate_b[l][:, None], (2 * N_HEADS, CHUNK))
        w2_pad = jnp.zeros((LANES, GW), F32).at[2 * N_HEADS:2 * N_HEADS + GLA_RANK].set(gla_w2[l])
        y = _mixer(xt, modl, _bf(_regroup_w_in(w_in[l])), cosf, sins, mlstm_conv[l], gate_rows, w2_pad,
                   gla_b2[l], hgrn_lb, ret_norm[l], mlstm_norm[l], gla_norm[l], hgrn_norm[l], consts,
                   l, bsz, seq)
        x1, h2, codes, counts = _post(y, xt, modl, _bf(w_out[l]), ln1_g[l], ln1_b[l], rw_pad, rb_pad,
                                      seq, alpha)
        class_start, items = _work_items(counts[0, :N_CLASSES], t)
        codes = codes.reshape(t)
        in_class = (codes >> POS_BITS)[:, None] == jnp.arange(N_CLASSES, dtype=jnp.int32)[None, :]
        slots = jnp.sum(jnp.where(in_class, class_start[None, :], 0), axis=1) + (codes & ((1 << POS_BITS) - 1))
        hs = _dispatch(slots, h2)
        ys = _moe(items, hs, exp_w1, exp_w3, exp_w2, l)
        xt = _combine(slots, ys, x1, modl, ln2_g[l], ln2_b[l], seq, alpha)
    return xt.reshape(bsz, seq, d)
```

```python
import functools

import numpy as np
import jax
import jax.numpy as jnp
from jax import lax
from jax.experimental import pallas as pl
from jax.experimental.pallas import tpu as pltpu

F32 = jnp.float32
BF16 = jnp.bfloat16

D_MODEL = 1024
N_MIXERS = 4
GW = D_MODEL // N_MIXERS
N_HEADS = 4
HEAD_DIM = GW // N_HEADS
CONV_WIDTH = 4
GLA_RANK = 16
GLA_TAU = 16.0
ROPE_BASE = 10000.0
N_GROUPS = 4
EXPERTS_PER_GROUP = 4
N_EXPERTS = 16
D_EXPERT = D_MODEL // 2
LN_EPS = 1e-5
NORM_EPS = 1e-6
QK_SCALE = HEAD_DIM ** -0.5

LANES = 128
CHUNK = 128
LEVELS = 7
CONV_PAD = 8
NPROJ = 16 * GW + LANES
EXTRA = 16 * GW
VMEM_LIMIT = 56 * 1024 * 1024

(R_Q, R_K, R_V, R_G, M_Q, M_K, M_V, M_O, G_Q, G_K, G_V, G_G, H_Q, H_F, H_I, H_G) = (
    GW * i for i in range(16))


def _bf(x):
    return x.astype(BF16)


def _dot(a, b):
    return jnp.dot(a, b, preferred_element_type=F32)


def _dot_nt(a, b):
    return lax.dot_general(a, b, (((1,), (1,)), ((), ())), preferred_element_type=F32)


def _split2(x):
    hi = _bf(x)
    lo = _bf(x - hi.astype(F32))
    return hi, lo


def _dot_f32(a, b):
    ah, al = _split2(a)
    bh, bl = _split2(b)
    return _dot(ah, bh) + _dot(ah, bl) + _dot(al, bh)


def _sigmoid(x):
    return 0.5 * jnp.tanh(0.5 * x) + 0.5


def _silu(x):
    return x * _sigmoid(x)


def _log_sigmoid(x):
    return jnp.minimum(x, 0.0) - jnp.log1p(jnp.exp(-jnp.abs(x)))


def _layer_norm(v, g, b):
    mu = jnp.mean(v, axis=-1, keepdims=True)
    d = v - mu
    var = jnp.mean(d * d, axis=-1, keepdims=True)
    return d * lax.rsqrt(var + LN_EPS) * g + b


def _adaln_kernel(c_ref, w_ref, b_ref, o_ref):
    cond = _silu(c_ref[...])
    o_ref[0] = _dot_f32(cond, w_ref[0]) + b_ref[0]


def _adaln(c, ada_w, ada_b):
    depth, d, n = ada_w.shape
    bsz = c.shape[0]
    tn = 1536
    return pl.pallas_call(
        _adaln_kernel,
        grid=(depth, n // tn),
        in_specs=[
            pl.BlockSpec((bsz, d), lambda l, j: (0, 0)),
            pl.BlockSpec((1, d, tn), lambda l, j: (l, 0, j)),
            pl.BlockSpec((1, 1, tn), lambda l, j: (l, 0, j)),
        ],
        out_specs=pl.BlockSpec((1, bsz, tn), lambda l, j: (l, 0, j)),
        out_shape=jax.ShapeDtypeStruct((depth, bsz, n), F32),
        compiler_params=pltpu.CompilerParams(
            dimension_semantics=("arbitrary", "arbitrary"), vmem_limit_bytes=VMEM_LIMIT),
        name="adaln",
    )(c, ada_w, ada_b.reshape(depth, 1, n))


def _mixer_consts():
    c = CHUNK
    i = np.arange(c)[:, None]
    j = np.arange(c)[None, :]
    rel = i - j
    log_gamma = np.log1p(-np.exp2(-5.0 - np.arange(N_HEADS, dtype=np.float64)))
    dret = np.concatenate(
        [np.where(rel >= 0, np.exp(log_gamma[h] * np.maximum(rel, 0)), 0.0) for h in range(N_HEADS)], axis=1)
    lane_head = np.arange(GW) // HEAD_DIM
    dq = np.exp(log_gamma[lane_head][None, :] * (np.arange(c)[:, None] + 1.0))
    dk = np.exp(log_gamma[lane_head][None, :] * (c - 1.0 - np.arange(c)[:, None]))
    gc = np.exp(log_gamma[lane_head] * c)[None, :]
    lv = np.zeros((c, c), np.int32)
    lv[rel == 0] = 1
    spans = [(rel >= 0), (rel < 0)]
    code, b = 2, 1
    while b < c:
        lv[(i // (2 * b) == j // (2 * b)) & ((i // b) % 2 == 1) & ((j // b) % 2 == 0)] = code
        ref = (i // (2 * b)) * 2 * b + b - 1
        right = (i // b) % 2 == 1
        spans.append(np.where(right, (j > ref) & (j <= i), (j > i) & (j <= ref)))
        code, b = code + 1, b * 2
    lv = np.tile(lv, (1, N_HEADS))
    span = np.concatenate(spans, axis=0).astype(np.float32)
    span = np.tile(span, (1, 2))
    hh = (lane_head[:, None] == lane_head[None, :]).astype(np.float32)
    ones_bd = np.repeat(np.eye(N_HEADS, dtype=np.float32), c, axis=0)
    ones_bd = np.repeat(ones_bd, HEAD_DIM, axis=1)
    return dict(
        dret=jnp.asarray(dret, F32), dq=jnp.asarray(dq, F32), dk=jnp.asarray(dk, F32),
        gc=jnp.asarray(gc, F32), lv=jnp.asarray(lv), hh=jnp.asarray(hh, BF16),
        bdf=jnp.asarray(hh, F32), ones_bd=jnp.asarray(ones_bd, BF16), span=jnp.asarray(span, BF16))


def _lane_cumsum(x):
    lane = lax.broadcasted_iota(jnp.int32, x.shape, 1)
    s = 1
    while s < x.shape[1]:
        x = x + jnp.where(lane >= s, pltpu.roll(x, s, 1), 0.0)
        s *= 2
    return x


def _lane_cummax(x):
    lane = lax.broadcasted_iota(jnp.int32, x.shape, 1)
    s = 1
    while s < x.shape[1]:
        x = jnp.maximum(x, jnp.where(lane >= s, pltpu.roll(x, s, 1), -jnp.inf))
        s *= 2
    return x


def _mixer_kernel(layer, nb,
                  x_ref, mod_ref, win_ref, cos_ref, sin_ref, conv_ref, gb_ref, w2_ref, b2_ref, lb_ref,
                  retg_ref, mlg_ref, glag_ref, hgg_ref,
                  dret_ref, dq_ref, dk_ref, gc_ref, lv_ref, hh_ref, bdf_ref, onesbd_ref, span_ref,
                  y_ref,
                  s_ret, s_cn, s_gla, s_hg, m_ml, conv_buf):
    c = CHUNK
    batches = range(nb)

    @pl.when(pl.program_id(1) == 0)
    def _():
        for s in (s_ret, s_cn, s_gla, s_hg, m_ml):
            s[...] = jnp.zeros_like(s)
        conv_buf[:, 0:CONV_PAD, :] = jnp.zeros((nb, CONV_PAD, 2 * GW), F32)

    lane_head = lax.broadcasted_iota(jnp.int32, (1, GW), 1) // HEAD_DIM
    head_rows = [(lane_head == h).astype(BF16) for h in range(N_HEADS)]
    hh = hh_ref[...]
    bdf = bdf_ref[...]
    lv = lv_ref[...]
    causal = lv >= 1

    def stack(xs):
        return jnp.concatenate(xs, axis=0)

    def tiled(x):
        return stack([x] * nb)

    def seq(x, b):
        return x[b * c:(b + 1) * c]

    hb = _bf(stack([x_ref[b] * (1.0 + mod_ref[b, 1:2, :]) + mod_ref[b, 0:1, :] for b in batches]))

    issued = {}

    def issue(first):
        issued[first] = _dot(hb, win_ref[:, first:first + N_MIXERS * GW])

    def col(off, width=GW):
        first = off // (N_MIXERS * GW) * (N_MIXERS * GW)
        return issued[first][:, off - first:off - first + width]

    issue(R_Q)
    issue(M_Q)
    ext = _dot(hb, win_ref[:, EXTRA:EXTRA + LANES])

    def put(off, val):
        for b in batches:
            y_ref[b, :, off:off + GW] = seq(val, b).astype(y_ref.dtype)

    def stackmask(xb):
        return jnp.concatenate([xb * head_rows[h] for h in range(N_HEADS)], axis=0)

    def expand_heads(cols, first):
        out = jnp.zeros((cols.shape[0], GW), F32)
        for h in range(N_HEADS):
            out = jnp.where(lane_head == h, cols[:, first + h:first + h + 1], out)
        return out

    def head_mean(v):
        return _dot(_bf(v), hh) * (1.0 / HEAD_DIM)

    def head_norm(v, gain, center):
        if center:
            v = v - head_mean(v)
        return v * lax.rsqrt(head_mean(v * v) + NORM_EPS) * gain

    cosf = tiled(cos_ref[...])
    sins = tiled(sin_ref[...])
    half = lax.broadcasted_iota(jnp.int32, (1, GW), 1) % HEAD_DIM < HEAD_DIM // 2

    def rotary(v):
        swapped = jnp.where(half, pltpu.roll(v, GW - HEAD_DIM // 2, 1), pltpu.roll(v, HEAD_DIM // 2, 1))
        return v * cosf + swapped * sins

    rq = _bf(rotary(col(R_Q)) * QK_SCALE)
    rk = rotary(col(R_K))
    rkb = _bf(rk)
    rvb = _bf(col(R_V))
    rkd = rk * tiled(dk_ref[...])
    dret = dret_ref[...]
    dq = dq_ref[...]
    gc = gc_ref[...]
    sc = [_dot_nt(seq(rq, b), stackmask(seq(rkb, b))) for b in batches]
    inter = [_dot(seq(rq, b), _bf(s_ret[b])) for b in batches]
    scb = [_bf(s * dret) for s in sc]
    ro = stack([_dot(scb[b], stackmask(seq(rvb, b))) + inter[b] * dq for b in batches])
    upd = [_dot(_bf(seq(rkd, b).T), seq(rvb, b)) for b in batches]
    for b in batches:
        s_ret[b] = s_ret[b] * gc + upd[b] * bdf
    put(0, head_norm(ro, retg_ref[...], True) * _silu(col(R_G)))

    issue(G_Q)
    mqk = col(M_Q, 2 * GW)
    for b in batches:
        conv_buf[b, CONV_PAD:CONV_PAD + c, :] = seq(mqk, b)
    qk = jnp.zeros((nb * c, 2 * GW), F32)
    for t in range(CONV_WIDTH):
        s = CONV_WIDTH - 1 - t
        qk = qk + stack([conv_buf[b, CONV_PAD - s:CONV_PAD - s + c, :] for b in batches]) * conv_ref[t:t + 1, :]
    for b in batches:
        conv_buf[b, 0:CONV_PAD, :] = conv_buf[b, c:c + CONV_PAD, :]
    qk = _silu(qk)
    mq = _bf(qk[:, 0:GW])
    mk = qk[:, GW:2 * GW] * QK_SCALE
    mkb = _bf(mk)
    mvb = _bf(col(M_V))

    pre = stack([seq(ext, b).T[0:8, :] for b in batches]) + tiled(gb_ref[...])
    row8 = lax.broadcasted_iota(jnp.int32, (8 * nb, c), 0)
    valid = row8 % 8 >= N_HEADS
    lf = jnp.where(valid, _log_sigmoid(pre), 0.0)
    gi = jnp.where(valid, pltpu.roll(pre, N_HEADS, 0), 0.0)
    bcum = _lane_cumsum(lf)
    a = gi - bcum
    m_prev = m_ml[...]
    mrow = jnp.maximum(m_prev, _lane_cummax(a))
    m_last = jnp.broadcast_to(mrow[:, c - 1:c], (8 * nb, c))
    b_last = jnp.broadcast_to(bcum[:, c - 1:c], (8 * nb, c))
    wi = jnp.exp(m_prev - mrow)
    em = jnp.exp(-(bcum + mrow))
    wk = jnp.exp(a - m_last)
    dec = jnp.exp(m_prev - m_last)
    m_ml[...] = jnp.where(valid, b_last + m_last, 0.0)
    kinds = [mrow, wi, em, wk]
    if 32 * nb < c:
        kinds.append(jnp.zeros((c - 32 * nb, c), F32))
    cols = stack(kinds).T

    def first_col(kind, b):
        return kind * 8 * nb + 8 * b + N_HEADS

    mexp = [jnp.concatenate([jnp.broadcast_to(cols[:, first_col(0, b) + h:first_col(0, b) + h + 1], (c, c))
                             for h in range(N_HEADS)], axis=1) for b in batches]
    a_row = [jnp.concatenate([a[8 * b + N_HEADS + h:8 * b + N_HEADS + h + 1, :] for h in range(N_HEADS)], axis=1)
             for b in batches]
    qkm = [_dot_nt(seq(mq, b), stackmask(seq(mkb, b))) for b in batches]
    inter = [_dot(seq(mq, b), _bf(s_cn[b])) for b in batches]
    sm = [_bf(qkm[b] * jnp.exp(jnp.where(causal, a_row[b] - mexp[b], -jnp.inf))) for b in batches]
    onesbd = onesbd_ref[...]
    wi_l = stack([expand_heads(cols, first_col(1, b)) for b in batches])
    em_l = stack([expand_heads(cols, first_col(2, b)) for b in batches])
    wk_l = stack([expand_heads(cols, first_col(3, b)) for b in batches])
    nd = (stack([_dot(sm[b], jnp.concatenate([stackmask(seq(mvb, b)), onesbd], axis=1)) for b in batches])
          + jnp.concatenate([wi_l, wi_l], axis=1) * stack(inter))
    mh = nd[:, 0:GW] / jnp.maximum(jnp.abs(nd[:, GW:2 * GW]), em_l)
    kt = mk * wk_l
    ones = jnp.ones((c, GW), BF16)
    upd = [_dot(_bf(seq(kt, b).T), jnp.concatenate([seq(mvb, b), ones], axis=1)) for b in batches]
    dec2 = jnp.concatenate([dec, dec], axis=1)
    bdf2 = jnp.concatenate([bdf, bdf], axis=1)
    for b in batches:
        dec_l = jnp.zeros((1, GW), F32)
        for h in range(N_HEADS):
            dec_l = jnp.where(lane_head == h, dec2[8 * b + N_HEADS + h:8 * b + N_HEADS + h + 1, :], dec_l)
        s_cn[b] = s_cn[b] * jnp.concatenate([dec_l, dec_l], axis=1) + upd[b] * bdf2
    put(GW, head_norm(mh * _sigmoid(col(M_O)), mlg_ref[...], True))

    span = span_ref[...]
    level_masks = [lv == code for code in range(2, 2 + LEVELS)]

    def decay_attention(q, k, v, g, st_ref):
        g2 = _split2(g)
        dec = [jnp.exp(_dot(span, stack([seq(part, b) for part in g2]))) for b in batches]
        e_cum = stack([d[0:c] for d in dec])
        e_end = stack([d[c:2 * c] for d in dec])
        qh = _bf(q)
        kh = _bf(k)
        qe = qh * _bf(e_cum)
        out = stack([_dot_nt(seq(qe, b), _bf(st_ref[b])) for b in batches])
        out = out + _dot(_bf(q * k), hh) * v
        scores = [jnp.zeros((c, N_HEADS * c), F32) for _ in batches]
        for lvl in range(LEVELS):
            e = _bf(stack([d[(2 + lvl) * c:(3 + lvl) * c] for d in dec]))
            qb = qh * e
            kb = kh * e
            level = [_dot_nt(seq(qb, b), stackmask(seq(kb, b))) for b in batches]
            scores = [jnp.where(level_masks[lvl], level[b], scores[b]) for b in batches]
        vb = _bf(v)
        out = out + stack([_dot(_bf(scores[b]), stackmask(seq(vb, b))) for b in batches])
        ke = kh * _bf(e_end)
        upd = [_dot(_bf(seq(v, b).T), seq(ke, b)) for b in batches]
        for b in batches:
            st_ref[b] = st_ref[b] * seq(e_cum, b)[c - 1:c, :] + upd[b] * bdf
        return out

    issue(H_Q)
    x_lr = _dot_f32(ext, w2_ref[...]) + b2_ref[...]
    log_a = _log_sigmoid(x_lr) * (1.0 / GLA_TAU)
    go = decay_attention(col(G_Q), col(G_K) * QK_SCALE, col(G_V), log_a, s_gla)
    put(2 * GW, head_norm(go, glag_ref[...], False) * _silu(col(G_G)))

    lb_all = lb_ref[...]
    lb_e = jnp.exp(lb_all - jnp.max(lb_all, axis=0, keepdims=True))
    lb_p = lb_e / jnp.sum(lb_e, axis=0, keepdims=True)
    lb = jnp.zeros((1, GW), F32)
    for l in range(1, layer + 1):
        lb = lb + lb_p[l:l + 1, :]
    z = col(H_F)
    x1 = jnp.log(lb)
    x2 = jnp.log1p(-lb) + _log_sigmoid(z)
    log_f = jnp.maximum(x1, x2) + jnp.log1p(jnp.exp(-jnp.abs(x1 - x2)))
    k_h = (1.0 - lb) * _sigmoid(-z)
    ho = decay_attention(_silu(col(H_Q)), k_h, col(H_I), log_f, s_hg)
    put(3 * GW, head_norm(ho * _sigmoid(col(H_G)), hgg_ref[...], False))


MIXER_BATCH = 4


def _mixer(xt, modl, w_in_p, cosf, sins, conv_w, gate_rows, w2_pad, b2, lb, ret_g, ml_g, gla_g, hg_g, consts,
           layer, bsz, seq):
    c = CHUNK
    nc = seq // c
    t, d = xt.shape
    row = lambda a: a.reshape(1, -1)
    full = lambda a: pl.BlockSpec(a.shape, lambda b, n: (0,) * a.ndim)
    cs = consts
    args = [xt.reshape(bsz, seq, d), modl, w_in_p, cosf, sins, conv_w, gate_rows, w2_pad, row(b2), lb,
            row(ret_g), row(ml_g), row(gla_g), row(hg_g),
            cs["dret"], cs["dq"], cs["dk"], cs["gc"], cs["lv"], cs["hh"], cs["bdf"], cs["ones_bd"], cs["span"]]
    nb = MIXER_BATCH
    in_specs = [
        pl.BlockSpec((nb, c, d), lambda b, n: (b, n, 0)),
        pl.BlockSpec((nb, 6, d), lambda b, n: (b, 0, 0)),
        full(w_in_p),
        pl.BlockSpec((c, GW), lambda b, n: (n, 0)),
        pl.BlockSpec((c, GW), lambda b, n: (n, 0)),
    ] + [full(a) for a in args[5:]]
    y = pl.pallas_call(
        functools.partial(_mixer_kernel, layer, nb),
        grid=(bsz // nb, nc),
        in_specs=in_specs,
        out_specs=pl.BlockSpec((nb, c, D_MODEL), lambda b, n: (b, n, 0)),
        out_shape=jax.ShapeDtypeStruct((bsz, seq, D_MODEL), BF16),
        scratch_shapes=[
            pltpu.VMEM((nb, GW, GW), F32),
            pltpu.VMEM((nb, GW, 2 * GW), F32),
            pltpu.VMEM((nb, GW, GW), F32),
            pltpu.VMEM((nb, GW, GW), F32),
            pltpu.VMEM((nb * 8, c), F32),
            pltpu.VMEM((nb, c + CONV_PAD, 2 * GW), F32),
        ],
        compiler_params=pltpu.CompilerParams(
            dimension_semantics=("arbitrary", "arbitrary"), vmem_limit_bytes=VMEM_LIMIT),
        name="mixer",
    )(*args)
    return y.reshape(t, D_MODEL)


def _group_shift(v, pos, k):
    return jnp.where(pos < EXPERTS_PER_GROUP - k,
                     pltpu.roll(v, LANES - k, 1), pltpu.roll(v, EXPERTS_PER_GROUP - k, 1))


def _router(h, rw, rb):
    lane = lax.broadcasted_iota(jnp.int32, (1, LANES), 1)
    real = lane < N_EXPERTS
    logits = jnp.where(real, _dot_f32(h, rw), -jnp.inf)
    ex = jnp.exp(logits - jnp.max(logits, axis=-1, keepdims=True))
    probs = ex / jnp.sum(ex, axis=-1, keepdims=True)
    sel = jnp.where(real, probs + rb, -jnp.inf)
    pos = lane % EXPERTS_PER_GROUP
    gid = lane // EXPERTS_PER_GROUP
    r1 = _group_shift(sel, pos, 1)
    r2 = _group_shift(sel, pos, 2)
    r3 = _group_shift(sel, pos, 3)
    pair = jnp.maximum(jnp.maximum(jnp.maximum(sel + r1, sel + r2), jnp.maximum(sel + r3, r1 + r2)),
                       jnp.maximum(r1 + r3, r2 + r3))
    pair = jnp.where(real, pair, -jnp.inf)
    best = jnp.max(pair, axis=-1, keepdims=True)
    first = jnp.min(jnp.where(pair == best, gid, N_GROUPS), axis=-1, keepdims=True)
    rank = jnp.zeros(sel.shape, jnp.int32)
    for k, r in ((1, r1), (2, r2), (3, r3)):
        beats = (r > sel) | ((r == sel) & (pos + k >= EXPERTS_PER_GROUP))
        rank = rank + beats.astype(jnp.int32)
    return probs, first, rank


def _post_kernel(alpha, y_ref, x_ref, mod_ref, wout_ref, lng_ref, lnb_ref, rw_ref, rb_ref,
                 stri_ref, x1_ref, h2_ref, slots_ref, counts_ref, carry_ref):
    tm = y_ref.shape[0]

    @pl.when(pl.program_id(0) == 0)
    def _():
        carry_ref[...] = jnp.zeros_like(carry_ref)

    g1 = mod_ref[0, 2:3, :]
    sh2 = mod_ref[0, 3:4, :]
    sc2 = mod_ref[0, 4:5, :]
    mix = _dot(_bf(y_ref[...]), wout_ref[...])
    x1 = _layer_norm(alpha * x_ref[...] + g1 * mix, lng_ref[...], lnb_ref[...])
    x1_ref[...] = x1
    h = x1 * (1.0 + sc2) + sh2
    d = h.shape[1]
    h2_ref[:, 0:d] = h

    probs, first, rank = _router(h, rw_ref[...], rb_ref[...])
    lane = lax.broadcasted_iota(jnp.int32, (1, LANES), 1)
    chosen = (lane // EXPERTS_PER_GROUP == first) & (rank < 2)
    w = jnp.where(chosen, probs, 0.0)
    h2_ref[:, d:d + LANES] = w / jnp.sum(w, axis=-1, keepdims=True)

    bit = jnp.left_shift(1, lane % EXPERTS_PER_GROUP).astype(F32)
    bits = jnp.sum(jnp.where(chosen, bit, 0.0), axis=-1, keepdims=True)
    pair = jnp.full(bits.shape, len(PAIR_BITS) - 1, jnp.int32)
    for index in range(len(PAIR_BITS) - 2, -1, -1):
        pair = jnp.where(bits == float(PAIR_BITS[index]), index, pair)
    cls = first * len(PAIR_BITS) + pair
    onehot = jnp.where(lane == cls, 1.0, 0.0)
    before = _dot(stri_ref[...], _bf(onehot)) + carry_ref[...]
    code = cls.astype(F32) * float(1 << POS_BITS) + jnp.sum(onehot * before, axis=-1, keepdims=True)
    slots_ref[0] = jnp.broadcast_to(code, (tm, LANES)).T[0:1, :].astype(jnp.int32)
    carry_ref[...] += jnp.sum(onehot, axis=0, keepdims=True)
    counts_ref[...] = carry_ref[...].astype(jnp.int32)


def _post(y, xt, modl, w_out, ln_g, ln_b, rw_pad, rb_pad, seq, alpha):
    t, d = xt.shape
    tm = 256
    per_batch = seq // tm
    row = lambda a: a.reshape(1, -1)
    tile = pl.BlockSpec((tm, d), lambda i: (i, 0))
    full = lambda a: pl.BlockSpec(a.shape, lambda i: (0,) * a.ndim)
    stri = jnp.asarray(np.tril(np.ones((tm, tm), np.float32), -1), BF16)
    args = [y, xt, modl, w_out, row(ln_g), row(ln_b), rw_pad, rb_pad, stri]
    return pl.pallas_call(
        functools.partial(_post_kernel, alpha),
        grid=(t // tm,),
        in_specs=[tile, tile, pl.BlockSpec((1, 6, d), lambda i: (i // per_batch, 0, 0))]
        + [full(a) for a in args[3:]],
        out_specs=[tile, pl.BlockSpec((tm, d + LANES), lambda i: (i, 0)),
                   pl.BlockSpec((1, 1, tm), lambda i: (i, 0, 0)),
                   pl.BlockSpec((1, LANES), lambda i: (0, 0))],
        out_shape=[jax.ShapeDtypeStruct((t, d), F32), jax.ShapeDtypeStruct((t, d + LANES), F32),
                   jax.ShapeDtypeStruct((t // tm, 1, tm), jnp.int32),
                   jax.ShapeDtypeStruct((1, LANES), jnp.int32)],
        scratch_shapes=[pltpu.VMEM((1, LANES), F32)],
        compiler_params=pltpu.CompilerParams(
            dimension_semantics=("arbitrary",), vmem_limit_bytes=VMEM_LIMIT),
        name="post",
    )(*args)


MOE_TILE = 512
POS_BITS = 15
PAIR_BITS = (0b0011, 0b0101, 0b1001, 0b1010, 0b0110, 0b1100)
N_CLASSES = N_GROUPS * len(PAIR_BITS)


def _dispatch_kernel(slots_ref, h_ref, hs_hbm, sem):
    tm = h_ref.shape[0]
    i = pl.program_id(0)

    for row in range(tm):
        s = slots_ref[i * tm + row]
        pltpu.make_async_copy(h_ref.at[pl.ds(row, 1)], hs_hbm.at[pl.ds(s, 1)], sem).start(priority=row % 2)
    pltpu.make_async_copy(h_ref, hs_hbm.at[pl.ds(0, tm)], sem).wait()


def _dispatch(slots, h2):
    t, d = h2.shape
    tm = MOE_TILE
    return pl.pallas_call(
        _dispatch_kernel,
        grid_spec=pltpu.PrefetchScalarGridSpec(
            num_scalar_prefetch=1,
            grid=(t // tm,),
            in_specs=[pl.BlockSpec((tm, d), lambda i, s: (i, 0))],
            out_specs=pl.BlockSpec(memory_space=pl.ANY),
            scratch_shapes=[pltpu.SemaphoreType.DMA(())],
        ),
        out_shape=jax.ShapeDtypeStruct((t, d), F32),
        compiler_params=pltpu.CompilerParams(
            dimension_semantics=("arbitrary",), vmem_limit_bytes=VMEM_LIMIT),
        name="dispatch",
    )(slots, h2)


def _moe_kernel(item_tile_ref, item_expert_ref, item_first_ref, n_items_ref,
                hs_ref, w1_ref, w3_ref, w2_ref, ys_ref, hb_ref):
    s = pl.program_id(0)
    d = ys_ref.shape[1]

    @pl.when(s < n_items_ref[0])
    def _():
        lane = lax.broadcasted_iota(jnp.int32, (1, LANES), 1)
        first_of_tile = item_first_ref[s] == 1

        @pl.when(first_of_tile)
        def _():
            hb_ref[...] = _bf(hs_ref[:, 0:d])

        gates = hs_ref[:, d:d + LANES]
        ge = jnp.sum(jnp.where(lane == item_expert_ref[s], gates, 0.0), axis=-1, keepdims=True)
        hb = hb_ref[...]
        up = _dot(hb, _bf(w1_ref[0, 0]))
        w3 = _bf(w3_ref[0, 0])
        lin = _dot(hb, w3)
        w2 = _bf(w2_ref[0, 0])
        out = _dot(_bf(_silu(up) * lin * ge), w2)

        @pl.when(first_of_tile)
        def _():
            ys_ref[...] = out

        @pl.when(jnp.logical_not(first_of_tile))
        def _():
            ys_ref[...] += out


def _moe(items, hs, w1, w3, w2, layer):
    item_tile, item_expert, item_first, n_items = items
    rows = hs.shape[0]
    d = w1.shape[2]
    up = pl.BlockSpec((1, 1, d, D_EXPERT), lambda s, it, ie, fi, n: (layer, ie[s], 0, 0))
    down = pl.BlockSpec((1, 1, D_EXPERT, d), lambda s, it, ie, fi, n: (layer, ie[s], 0, 0))
    return pl.pallas_call(
        _moe_kernel,
        grid_spec=pltpu.PrefetchScalarGridSpec(
            num_scalar_prefetch=4,
            grid=(item_tile.shape[0],),
            in_specs=[pl.BlockSpec((MOE_TILE, d + LANES), lambda s, it, ie, fi, n: (it[s], 0)), up, up, down],
            out_specs=pl.BlockSpec((MOE_TILE, d), lambda s, it, ie, fi, n: (it[s], 0)),
            scratch_shapes=[pltpu.VMEM((MOE_TILE, d), BF16)],
        ),
        out_shape=jax.ShapeDtypeStruct((rows, d), F32),
        compiler_params=pltpu.CompilerParams(
            dimension_semantics=("arbitrary",), vmem_limit_bytes=VMEM_LIMIT),
        name="moe",
    )(item_tile, item_expert, item_first, n_items, hs, w1, w3, w2)


def _class_experts():
    member = np.zeros((N_CLASSES, N_EXPERTS), np.int32)
    for g in range(N_GROUPS):
        for p, bits in enumerate(PAIR_BITS):
            for e in range(EXPERTS_PER_GROUP):
                if bits >> e & 1:
                    member[g * len(PAIR_BITS) + p, g * EXPERTS_PER_GROUP + e] = 1
    return member


def _work_items(counts, n_tokens):
    n_tiles = n_tokens // MOE_TILE
    n_items_max = 2 * (n_tiles + 2 * N_CLASSES)
    ends = jnp.cumsum(counts)
    starts = ends - counts
    lo = jnp.arange(n_tiles, dtype=jnp.int32)[:, None] * MOE_TILE
    overlap = (counts[None, :] > 0) & (starts[None, :] < lo + MOE_TILE) & (ends[None, :] > lo)
    needed = (overlap.astype(jnp.int32) @ jnp.asarray(_class_experts())) > 0
    flat = needed.reshape(-1)
    n_items = jnp.sum(flat.astype(jnp.int32))
    place = jnp.where(flat, jnp.cumsum(flat.astype(jnp.int32)) - 1, n_items_max)
    ids = jnp.zeros((n_items_max,), jnp.int32).at[place].set(
        jnp.arange(flat.shape[0], dtype=jnp.int32), mode="drop")
    ids = ids[jnp.minimum(jnp.arange(n_items_max), n_items - 1)]
    item_tile = ids // N_EXPERTS
    item_expert = ids % N_EXPERTS
    item_first = jnp.concatenate([jnp.ones((1,), jnp.int32),
                                  (item_tile[1:] != item_tile[:-1]).astype(jnp.int32)])
    return starts.astype(jnp.int32), (item_tile, item_expert, item_first, n_items.reshape(1))


def _combine_kernel(alpha, slots_ref, ys_hbm, x1_ref, mod_ref, lng_ref, lnb_ref, o_ref, buf, sem):
    tm = o_ref.shape[0]
    i = pl.program_id(0)
    n = pl.num_programs(0)

    def gather(tile, b):
        for row in range(tm):
            s = slots_ref[tile * tm + row]
            pltpu.make_async_copy(ys_hbm.at[pl.ds(s, 1)], buf.at[b, pl.ds(row, 1)], sem.at[b]).start(
                priority=row % 2)

    def wait(b):
        pltpu.make_async_copy(ys_hbm.at[pl.ds(0, tm)], buf.at[b], sem.at[b]).wait()

    @pl.when(i == 0)
    def _():
        gather(0, 0)

    cur = i % 2
    nxt = (i + 1) % 2
    wait(cur)
    gather(jnp.minimum(i + 1, n - 1), nxt)
    g2 = mod_ref[0, 5:6, :]
    o_ref[...] = _layer_norm(alpha * x1_ref[...] + g2 * buf[cur], lng_ref[...], lnb_ref[...])

    @pl.when(i == n - 1)
    def _():
        wait(nxt)


def _combine(slots, ys, x1, modl, ln_g, ln_b, seq, alpha):
    t, d = x1.shape
    tm = 256
    per_batch = seq // tm
    row = lambda a: a.reshape(1, -1)
    tile = pl.BlockSpec((tm, d), lambda i, s: (i, 0))
    vec = pl.BlockSpec((1, d), lambda i, s: (0, 0))
    return pl.pallas_call(
        functools.partial(_combine_kernel, alpha),
        grid_spec=pltpu.PrefetchScalarGridSpec(
            num_scalar_prefetch=1,
            grid=(t // tm,),
            in_specs=[pl.BlockSpec(memory_space=pl.ANY), tile,
                      pl.BlockSpec((1, 6, d), lambda i, s: (i // per_batch, 0, 0)), vec, vec],
            out_specs=tile,
            scratch_shapes=[pltpu.VMEM((2, tm, d), F32), pltpu.SemaphoreType.DMA((2,))],
        ),
        out_shape=jax.ShapeDtypeStruct((t, d), F32),
        compiler_params=pltpu.CompilerParams(
            dimension_semantics=("arbitrary",), vmem_limit_bytes=VMEM_LIMIT),
        name="combine",
    )(slots, ys, x1, modl, row(ln_g), row(ln_b))


def _regroup_w_in(w):
    a = 8 * GW
    b = a + 2 * N_HEADS
    cc = b + 4 * GW
    dd = cc + GLA_RANK
    pad = jnp.zeros((w.shape[0], LANES - 2 * N_HEADS - GLA_RANK), w.dtype)
    return jnp.concatenate([w[:, :a], w[:, b:cc], w[:, dd:], w[:, a:b], w[:, cc:dd], pad], axis=1)


def _rotary_tables(seq):
    inv = ROPE_BASE ** (-jnp.arange(0, HEAD_DIM, 2, dtype=F32) / HEAD_DIM)
    ang = jnp.arange(seq, dtype=F32)[:, None] * inv[None, :]
    cos = jnp.cos(ang)
    sin = jnp.sin(ang)
    cosf = jnp.tile(jnp.concatenate([cos, cos], axis=1), (1, N_HEADS))
    sins = jnp.tile(jnp.concatenate([-sin, sin], axis=1), (1, N_HEADS))
    return cosf, sins


def kernel(x, c, ada_w, ada_b, w_in, mlstm_conv, mlstm_gate_b, gla_w2, gla_b2, hgrn_lb, ret_norm, mlstm_norm, gla_norm, hgrn_norm, w_out, ln1_g, ln1_b, router_w, router_b, exp_w1, exp_w3, exp_w2, ln2_g, ln2_b):
    bsz, seq, d = x.shape
    depth = ada_w.shape[0]
    assert d == D_MODEL and seq % 512 == 0 and bsz <= 16
    t = bsz * seq
    alpha = (2.0 * depth) ** 0.25
    assert t % MOE_TILE == 0 and t < (1 << POS_BITS) and bsz % MIXER_BATCH == 0
    consts = _mixer_consts()
    cosf, sins = _rotary_tables(seq)
    c_rows = jnp.pad(c, ((0, 16 - bsz), (0, 0)))
    mod = _adaln(c_rows, ada_w, ada_b)[:, :bsz]
    rw_pad = jnp.pad(router_w, ((0, 0), (0, LANES - N_EXPERTS)))
    rb_pad = jnp.pad(router_b, (0, LANES - N_EXPERTS)).reshape(1, LANES)
    xt = x.reshape(t, d)
    for l in range(depth):
        modl = mod[l].reshape(bsz, 6, d)
        gate_rows = jnp.broadcast_to(mlstm_gate_b[l][:, None], (2 * N_HEADS, CHUNK))
        w2_pad = jnp.zeros((LANES, GW), F32).at[2 * N_HEADS:2 * N_HEADS + GLA_RANK].set(gla_w2[l])
        y = _mixer(xt, modl, _bf(_regroup_w_in(w_in[l])), cosf, sins, mlstm_conv[l], gate_rows, w2_pad,
                   gla_b2[l], hgrn_lb, ret_norm[l], mlstm_norm[l], gla_norm[l], hgrn_norm[l], consts,
                   l, bsz, seq)
        x1, h2, codes, counts = _post(y, xt, modl, _bf(w_out[l]), ln1_g[l], ln1_b[l], rw_pad, rb_pad,
                                      seq, alpha)
        class_start, items = _work_items(counts[0, :N_CLASSES], t)
        codes = codes.reshape(t)
        in_class = (codes >> POS_BITS)[:, None] == jnp.arange(N_CLASSES, dtype=jnp.int32)[None, :]
        slots = jnp.sum(jnp.where(in_class, class_start[None, :], 0), axis=1) + (codes & ((1 << POS_BITS) - 1))
        hs = _dispatch(slots, h2)
        ys = _moe(items, hs, exp_w1, exp_w3, exp_w2, l)
        xt = _combine(slots, ys, x1, modl, ln2_g[l], ln2_b[l], seq, alpha)
    return xt.reshape(bsz, seq, d)
```

```python
import functools

import numpy as np
import jax
import jax.numpy as jnp
from jax import lax
from jax.experimental import pallas as pl
from jax.experimental.pallas import tpu as pltpu

F32 = jnp.float32
BF16 = jnp.bfloat16

D_MODEL = 1024
N_MIXERS = 4
GW = D_MODEL // N_MIXERS
N_HEADS = 4
HEAD_DIM = GW // N_HEADS
CONV_WIDTH = 4
GLA_RANK = 16
GLA_TAU = 16.0
ROPE_BASE = 10000.0
N_GROUPS = 4
EXPERTS_PER_GROUP = 4
N_EXPERTS = 16
D_EXPERT = D_MODEL // 2
LN_EPS = 1e-5
NORM_EPS = 1e-6
QK_SCALE = HEAD_DIM ** -0.5

LANES = 128
CHUNK = 128
LEVELS = 7
CONV_PAD = 8
NPROJ = 16 * GW + LANES
EXTRA = 16 * GW
VMEM_LIMIT = 56 * 1024 * 1024

(R_Q, R_K, R_V, R_G, M_Q, M_K, M_V, M_O, G_Q, G_K, G_V, G_G, H_Q, H_F, H_I, H_G) = (
    GW * i for i in range(16))


def _bf(x):
    return x.astype(BF16)


def _dot(a, b):
    return jnp.dot(a, b, preferred_element_type=F32)


def _dot_nt(a, b):
    return lax.dot_general(a, b, (((1,), (1,)), ((), ())), preferred_element_type=F32)


def _split2(x):
    hi = _bf(x)
    lo = _bf(x - hi.astype(F32))
    return hi, lo


def _dot_f32(a, b):
    ah, al = _split2(a)
    bh, bl = _split2(b)
    return _dot(ah, bh) + _dot(ah, bl) + _dot(al, bh)


def _sigmoid(x):
    return 0.5 * jnp.tanh(0.5 * x) + 0.5


def _silu(x):
    h = 0.5 * x
    return h + h * jnp.tanh(h)


def _log_sigmoid(x):
    return jnp.minimum(x, 0.0) - jnp.log1p(jnp.exp(-jnp.abs(x)))


def _layer_norm(v, g, b):
    mu = jnp.mean(v, axis=-1, keepdims=True)
    d = v - mu
    var = jnp.mean(d * d, axis=-1, keepdims=True)
    return d * lax.rsqrt(var + LN_EPS) * g + b


def _adaln_kernel(c_ref, w_ref, b_ref, o_ref):
    cond = _silu(c_ref[...])
    o_ref[0] = _dot_f32(cond, w_ref[0]) + b_ref[0]


def _adaln(c, ada_w, ada_b):
    depth, d, n = ada_w.shape
    bsz = c.shape[0]
    tn = 1536
    return pl.pallas_call(
        _adaln_kernel,
        grid=(depth, n // tn),
        in_specs=[
            pl.BlockSpec((bsz, d), lambda l, j: (0, 0)),
            pl.BlockSpec((1, d, tn), lambda l, j: (l, 0, j)),
            pl.BlockSpec((1, 1, tn), lambda l, j: (l, 0, j)),
        ],
        out_specs=pl.BlockSpec((1, bsz, tn), lambda l, j: (l, 0, j)),
        out_shape=jax.ShapeDtypeStruct((depth, bsz, n), F32),
        compiler_params=pltpu.CompilerParams(
            dimension_semantics=("arbitrary", "arbitrary"), vmem_limit_bytes=VMEM_LIMIT),
        name="adaln",
    )(c, ada_w, ada_b.reshape(depth, 1, n))


def _mixer_consts():
    c = CHUNK
    i = np.arange(c)[:, None]
    j = np.arange(c)[None, :]
    rel = i - j
    log_gamma = np.log1p(-np.exp2(-5.0 - np.arange(N_HEADS, dtype=np.float64)))
    dret = np.concatenate(
        [np.where(rel >= 0, np.exp(log_gamma[h] * np.maximum(rel, 0)), 0.0) for h in range(N_HEADS)], axis=1)
    lane_head = np.arange(GW) // HEAD_DIM
    dq = np.exp(log_gamma[lane_head][None, :] * (np.arange(c)[:, None] + 1.0))
    dk = np.exp(log_gamma[lane_head][None, :] * (c - 1.0 - np.arange(c)[:, None]))
    gc = np.exp(log_gamma[lane_head] * c)[None, :]
    lv = np.zeros((c, c), np.int32)
    lv[rel == 0] = 1
    spans = [(rel >= 0), (rel < 0)]
    code, b = 2, 1
    while b < c:
        lv[(i // (2 * b) == j // (2 * b)) & ((i // b) % 2 == 1) & ((j // b) % 2 == 0)] = code
        ref = (i // (2 * b)) * 2 * b + b - 1
        right = (i // b) % 2 == 1
        spans.append(np.where(right, (j > ref) & (j <= i), (j > i) & (j <= ref)))
        code, b = code + 1, b * 2
    lv = np.tile(lv, (1, N_HEADS))
    span = np.concatenate(spans, axis=0).astype(np.float32)
    span = np.tile(span, (1, 2))
    hh = (lane_head[:, None] == lane_head[None, :]).astype(np.float32)
    ones_bd = np.repeat(np.eye(N_HEADS, dtype=np.float32), c, axis=0)
    ones_bd = np.repeat(ones_bd, HEAD_DIM, axis=1)
    return dict(
        dret=jnp.asarray(dret, F32), dq=jnp.asarray(dq, F32), dk=jnp.asarray(dk, F32),
        gc=jnp.asarray(gc, F32), lv=jnp.asarray(lv), hh=jnp.asarray(hh, BF16),
        bdf=jnp.asarray(hh, F32), ones_bd=jnp.asarray(ones_bd, BF16), span=jnp.asarray(span, BF16))


def _lane_cumsum(x):
    lane = lax.broadcasted_iota(jnp.int32, x.shape, 1)
    s = 1
    while s < x.shape[1]:
        x = x + jnp.where(lane >= s, pltpu.roll(x, s, 1), 0.0)
        s *= 2
    return x


def _lane_cummax(x):
    lane = lax.broadcasted_iota(jnp.int32, x.shape, 1)
    s = 1
    while s < x.shape[1]:
        x = jnp.maximum(x, jnp.where(lane >= s, pltpu.roll(x, s, 1), -jnp.inf))
        s *= 2
    return x


def _mixer_kernel(layer, nb,
                  x_ref, mod_ref, win_ref, cos_ref, sin_ref, conv_ref, gb_ref, w2_ref, b2_ref, lb_ref,
                  retg_ref, mlg_ref, glag_ref, hgg_ref,
                  dret_ref, dq_ref, dk_ref, gc_ref, lv_ref, hh_ref, bdf_ref, onesbd_ref, span_ref,
                  y_ref,
                  s_ret, s_cn, s_gla, s_hg, m_ml, conv_buf):
    c = CHUNK
    batches = range(nb)

    @pl.when(pl.program_id(1) == 0)
    def _():
        for s in (s_ret, s_cn, s_gla, s_hg, m_ml):
            s[...] = jnp.zeros_like(s)
        conv_buf[:, 0:CONV_PAD, :] = jnp.zeros((nb, CONV_PAD, 2 * GW), F32)

    lane_head = lax.broadcasted_iota(jnp.int32, (1, GW), 1) // HEAD_DIM
    head_rows = [(lane_head == h).astype(BF16) for h in range(N_HEADS)]
    hh = hh_ref[...]
    bdf = bdf_ref[...]
    lv = lv_ref[...]
    causal = lv >= 1

    def stack(xs):
        return jnp.concatenate(xs, axis=0)

    def tiled(x):
        return stack([x] * nb)

    def seq(x, b):
        return x[b * c:(b + 1) * c]

    hb = _bf(stack([x_ref[b] * (1.0 + mod_ref[b, 1:2, :]) + mod_ref[b, 0:1, :] for b in batches]))

    issued = {}

    def issue(first):
        issued[first] = _dot(hb, win_ref[:, first:first + N_MIXERS * GW])

    def col(off, width=GW):
        first = off // (N_MIXERS * GW) * (N_MIXERS * GW)
        return issued[first][:, off - first:off - first + width]

    issue(R_Q)
    issue(M_Q)
    ext = _dot(hb, win_ref[:, EXTRA:EXTRA + LANES])

    def put(off, val):
        for b in batches:
            y_ref[b, :, off:off + GW] = seq(val, b).astype(y_ref.dtype)

    def stackmask(xb):
        return jnp.concatenate([xb * head_rows[h] for h in range(N_HEADS)], axis=0)

    def expand_heads(cols, first):
        out = jnp.zeros((cols.shape[0], GW), F32)
        for h in range(N_HEADS):
            out = jnp.where(lane_head == h, cols[:, first + h:first + h + 1], out)
        return out

    def head_mean(v):
        return _dot(_bf(v), hh) * (1.0 / HEAD_DIM)

    def head_norm(v, gain, center):
        if center:
            v = v - head_mean(v)
        return v * lax.rsqrt(head_mean(v * v) + NORM_EPS) * gain

    cosf = tiled(cos_ref[...])
    sins = tiled(sin_ref[...])
    half = lax.broadcasted_iota(jnp.int32, (1, GW), 1) % HEAD_DIM < HEAD_DIM // 2

    def rotary(v):
        swapped = jnp.where(half, pltpu.roll(v, GW - HEAD_DIM // 2, 1), pltpu.roll(v, HEAD_DIM // 2, 1))
        return v * cosf + swapped * sins

    rq = _bf(rotary(col(R_Q)) * QK_SCALE)
    rk = rotary(col(R_K))
    rkb = _bf(rk)
    rvb = _bf(col(R_V))
    rkd = rk * tiled(dk_ref[...])
    dret = dret_ref[...]
    dq = dq_ref[...]
    gc = gc_ref[...]
    sc = [_dot_nt(seq(rq, b), stackmask(seq(rkb, b))) for b in batches]
    inter = [_dot(seq(rq, b), _bf(s_ret[b])) for b in batches]
    scb = [_bf(s * dret) for s in sc]
    ro = stack([_dot(scb[b], stackmask(seq(rvb, b))) + inter[b] * dq for b in batches])
    upd = [_dot(_bf(seq(rkd, b).T), seq(rvb, b)) for b in batches]
    for b in batches:
        s_ret[b] = s_ret[b] * gc + upd[b] * bdf
    put(0, head_norm(ro, retg_ref[...], True) * _silu(col(R_G)))

    issue(G_Q)
    mqk = col(M_Q, 2 * GW)
    for b in batches:
        conv_buf[b, CONV_PAD:CONV_PAD + c, :] = seq(mqk, b)
    qk = jnp.zeros((nb * c, 2 * GW), F32)
    for t in range(CONV_WIDTH):
        s = CONV_WIDTH - 1 - t
        qk = qk + stack([conv_buf[b, CONV_PAD - s:CONV_PAD - s + c, :] for b in batches]) * conv_ref[t:t + 1, :]
    for b in batches:
        conv_buf[b, 0:CONV_PAD, :] = conv_buf[b, c:c + CONV_PAD, :]
    qk = _silu(qk)
    mq = _bf(qk[:, 0:GW])
    mk = qk[:, GW:2 * GW] * QK_SCALE
    mkb = _bf(mk)
    mvb = _bf(col(M_V))

    pre = stack([seq(ext, b).T[0:8, :] for b in batches]) + tiled(gb_ref[...])
    row8 = lax.broadcasted_iota(jnp.int32, (8 * nb, c), 0)
    valid = row8 % 8 >= N_HEADS
    lf = jnp.where(valid, _log_sigmoid(pre), 0.0)
    gi = jnp.where(valid, pltpu.roll(pre, N_HEADS, 0), 0.0)
    bcum = _lane_cumsum(lf)
    a = gi - bcum
    m_prev = m_ml[...]
    mrow = jnp.maximum(m_prev, _lane_cummax(a))
    m_last = jnp.broadcast_to(mrow[:, c - 1:c], (8 * nb, c))
    b_last = jnp.broadcast_to(bcum[:, c - 1:c], (8 * nb, c))
    wi = jnp.exp(m_prev - mrow)
    em = jnp.exp(-(bcum + mrow))
    wk = jnp.exp(a - m_last)
    dec = jnp.exp(m_prev - m_last)
    m_ml[...] = jnp.where(valid, b_last + m_last, 0.0)
    kinds = [mrow, wi, em, wk]
    if 32 * nb < c:
        kinds.append(jnp.zeros((c - 32 * nb, c), F32))
    cols = stack(kinds).T

    def first_col(kind, b):
        return kind * 8 * nb + 8 * b + N_HEADS

    mexp = [jnp.concatenate([jnp.broadcast_to(cols[:, first_col(0, b) + h:first_col(0, b) + h + 1], (c, c))
                             for h in range(N_HEADS)], axis=1) for b in batches]
    a_row = [jnp.concatenate([a[8 * b + N_HEADS + h:8 * b + N_HEADS + h + 1, :] for h in range(N_HEADS)], axis=1)
             for b in batches]
    qkm = [_dot_nt(seq(mq, b), stackmask(seq(mkb, b))) for b in batches]
    inter = [_dot(seq(mq, b), _bf(s_cn[b])) for b in batches]
    sm = [_bf(qkm[b] * jnp.exp(jnp.where(causal, a_row[b] - mexp[b], -jnp.inf))) for b in batches]
    onesbd = onesbd_ref[...]
    wi_l = stack([expand_heads(cols, first_col(1, b)) for b in batches])
    em_l = stack([expand_heads(cols, first_col(2, b)) for b in batches])
    wk_l = stack([expand_heads(cols, first_col(3, b)) for b in batches])
    nd = (stack([_dot(sm[b], jnp.concatenate([stackmask(seq(mvb, b)), onesbd], axis=1)) for b in batches])
          + jnp.concatenate([wi_l, wi_l], axis=1) * stack(inter))
    mh = nd[:, 0:GW] / jnp.maximum(jnp.abs(nd[:, GW:2 * GW]), em_l)
    kt = mk * wk_l
    ones = jnp.ones((c, GW), BF16)
    upd = [_dot(_bf(seq(kt, b).T), jnp.concatenate([seq(mvb, b), ones], axis=1)) for b in batches]
    dec2 = jnp.concatenate([dec, dec], axis=1)
    bdf2 = jnp.concatenate([bdf, bdf], axis=1)
    for b in batches:
        dec_l = jnp.zeros((1, GW), F32)
        for h in range(N_HEADS):
            dec_l = jnp.where(lane_head == h, dec2[8 * b + N_HEADS + h:8 * b + N_HEADS + h + 1, :], dec_l)
        s_cn[b] = s_cn[b] * jnp.concatenate([dec_l, dec_l], axis=1) + upd[b] * bdf2
    put(GW, head_norm(mh * _sigmoid(col(M_O)), mlg_ref[...], True))

    span = span_ref[...]
    level_masks = [lv == code for code in range(2, 2 + LEVELS)]

    def decay_attention(q, k, v, g, st_ref):
        g2 = _split2(g)
        dec = [jnp.exp(_dot(span, stack([seq(part, b) for part in g2]))) for b in batches]
        e_cum = stack([d[0:c] for d in dec])
        e_end = stack([d[c:2 * c] for d in dec])
        qh = _bf(q)
        kh = _bf(k)
        qe = qh * _bf(e_cum)
        out = stack([_dot_nt(seq(qe, b), _bf(st_ref[b])) for b in batches])
        out = out + _dot(_bf(q * k), hh) * v
        scores = [jnp.zeros((c, N_HEADS * c), F32) for _ in batches]
        for lvl in range(LEVELS):
            e = _bf(stack([d[(2 + lvl) * c:(3 + lvl) * c] for d in dec]))
            qb = qh * e
            kb = kh * e
            level = [_dot_nt(seq(qb, b), stackmask(seq(kb, b))) for b in batches]
            scores = [jnp.where(level_masks[lvl], level[b], scores[b]) for b in batches]
        vb = _bf(v)
        out = out + stack([_dot(_bf(scores[b]), stackmask(seq(vb, b))) for b in batches])
        ke = kh * _bf(e_end)
        upd = [_dot(_bf(seq(v, b).T), seq(ke, b)) for b in batches]
        for b in batches:
            st_ref[b] = st_ref[b] * seq(e_cum, b)[c - 1:c, :] + upd[b] * bdf
        return out

    issue(H_Q)
    x_lr = _dot_f32(ext, w2_ref[...]) + b2_ref[...]
    log_a = _log_sigmoid(x_lr) * (1.0 / GLA_TAU)
    go = decay_attention(col(G_Q), col(G_K) * QK_SCALE, col(G_V), log_a, s_gla)
    put(2 * GW, head_norm(go, glag_ref[...], False) * _silu(col(G_G)))

    lb_all = lb_ref[...]
    lb_e = jnp.exp(lb_all - jnp.max(lb_all, axis=0, keepdims=True))
    lb_p = lb_e / jnp.sum(lb_e, axis=0, keepdims=True)
    lb = jnp.zeros((1, GW), F32)
    for l in range(1, layer + 1):
        lb = lb + lb_p[l:l + 1, :]
    z = col(H_F)
    x1 = jnp.log(lb)
    x2 = jnp.log1p(-lb) + _log_sigmoid(z)
    log_f = jnp.maximum(x1, x2) + jnp.log1p(jnp.exp(-jnp.abs(x1 - x2)))
    k_h = (1.0 - lb) * _sigmoid(-z)
    ho = decay_attention(_silu(col(H_Q)), k_h, col(H_I), log_f, s_hg)
    put(3 * GW, head_norm(ho * _sigmoid(col(H_G)), hgg_ref[...], False))


MIXER_BATCH = 4


def _mixer(xt, modl, w_in_p, cosf, sins, conv_w, gate_rows, w2_pad, b2, lb, ret_g, ml_g, gla_g, hg_g, consts,
           layer, bsz, seq):
    c = CHUNK
    nc = seq // c
    t, d = xt.shape
    row = lambda a: a.reshape(1, -1)
    full = lambda a: pl.BlockSpec(a.shape, lambda b, n: (0,) * a.ndim)
    cs = consts
    args = [xt.reshape(bsz, seq, d), modl, w_in_p, cosf, sins, conv_w, gate_rows, w2_pad, row(b2), lb,
            row(ret_g), row(ml_g), row(gla_g), row(hg_g),
            cs["dret"], cs["dq"], cs["dk"], cs["gc"], cs["lv"], cs["hh"], cs["bdf"], cs["ones_bd"], cs["span"]]
    nb = MIXER_BATCH
    in_specs = [
        pl.BlockSpec((nb, c, d), lambda b, n: (b, n, 0)),
        pl.BlockSpec((nb, 6, d), lambda b, n: (b, 0, 0)),
        full(w_in_p),
        pl.BlockSpec((c, GW), lambda b, n: (n, 0)),
        pl.BlockSpec((c, GW), lambda b, n: (n, 0)),
    ] + [full(a) for a in args[5:]]
    y = pl.pallas_call(
        functools.partial(_mixer_kernel, layer, nb),
        grid=(bsz // nb, nc),
        in_specs=in_specs,
        out_specs=pl.BlockSpec((nb, c, D_MODEL), lambda b, n: (b, n, 0)),
        out_shape=jax.ShapeDtypeStruct((bsz, seq, D_MODEL), BF16),
        scratch_shapes=[
            pltpu.VMEM((nb, GW, GW), F32),
            pltpu.VMEM((nb, GW, 2 * GW), F32),
            pltpu.VMEM((nb, GW, GW), F32),
            pltpu.VMEM((nb, GW, GW), F32),
            pltpu.VMEM((nb * 8, c), F32),
            pltpu.VMEM((nb, c + CONV_PAD, 2 * GW), F32),
        ],
        compiler_params=pltpu.CompilerParams(
            dimension_semantics=("arbitrary", "arbitrary"), vmem_limit_bytes=VMEM_LIMIT),
        name="mixer",
    )(*args)
    return y.reshape(t, D_MODEL)


def _group_shift(v, pos, k):
    return jnp.where(pos < EXPERTS_PER_GROUP - k,
                     pltpu.roll(v, LANES - k, 1), pltpu.roll(v, EXPERTS_PER_GROUP - k, 1))


def _router(h, rw, rb):
    lane = lax.broadcasted_iota(jnp.int32, (1, LANES), 1)
    real = lane < N_EXPERTS
    logits = jnp.where(real, _dot_f32(h, rw), -jnp.inf)
    ex = jnp.exp(logits - jnp.max(logits, axis=-1, keepdims=True))
    probs = ex / jnp.sum(ex, axis=-1, keepdims=True)
    sel = jnp.where(real, probs + rb, -jnp.inf)
    pos = lane % EXPERTS_PER_GROUP
    gid = lane // EXPERTS_PER_GROUP
    r1 = _group_shift(sel, pos, 1)
    r2 = _group_shift(sel, pos, 2)
    r3 = _group_shift(sel, pos, 3)
    pair = jnp.maximum(jnp.maximum(jnp.maximum(sel + r1, sel + r2), jnp.maximum(sel + r3, r1 + r2)),
                       jnp.maximum(r1 + r3, r2 + r3))
    pair = jnp.where(real, pair, -jnp.inf)
    best = jnp.max(pair, axis=-1, keepdims=True)
    first = jnp.min(jnp.where(pair == best, gid, N_GROUPS), axis=-1, keepdims=True)
    rank = jnp.zeros(sel.shape, jnp.int32)
    for k, r in ((1, r1), (2, r2), (3, r3)):
        beats = (r > sel) | ((r == sel) & (pos + k >= EXPERTS_PER_GROUP))
        rank = rank + beats.astype(jnp.int32)
    return probs, first, rank


def _post_kernel(alpha, y_ref, x_ref, mod_ref, wout_ref, lng_ref, lnb_ref, rw_ref, rb_ref,
                 stri_ref, x1_ref, h2_ref, slots_ref, counts_ref, carry_ref):
    tm = y_ref.shape[0]

    @pl.when(pl.program_id(0) == 0)
    def _():
        carry_ref[...] = jnp.zeros_like(carry_ref)

    g1 = mod_ref[0, 2:3, :]
    sh2 = mod_ref[0, 3:4, :]
    sc2 = mod_ref[0, 4:5, :]
    mix = _dot(_bf(y_ref[...]), wout_ref[...])
    x1 = _layer_norm(alpha * x_ref[...] + g1 * mix, lng_ref[...], lnb_ref[...])
    x1_ref[...] = x1
    h = x1 * (1.0 + sc2) + sh2
    d = h.shape[1]
    h2_ref[:, 0:d] = h

    probs, first, rank = _router(h, rw_ref[...], rb_ref[...])
    lane = lax.broadcasted_iota(jnp.int32, (1, LANES), 1)
    chosen = (lane // EXPERTS_PER_GROUP == first) & (rank < 2)
    w = jnp.where(chosen, probs, 0.0)
    h2_ref[:, d:d + LANES] = w / jnp.sum(w, axis=-1, keepdims=True)

    bit = jnp.left_shift(1, lane % EXPERTS_PER_GROUP).astype(F32)
    bits = jnp.sum(jnp.where(chosen, bit, 0.0), axis=-1, keepdims=True)
    pair = jnp.full(bits.shape, len(PAIR_BITS) - 1, jnp.int32)
    for index in range(len(PAIR_BITS) - 2, -1, -1):
        pair = jnp.where(bits == float(PAIR_BITS[index]), index, pair)
    cls = first * len(PAIR_BITS) + pair
    onehot = jnp.where(lane == cls, 1.0, 0.0)
    before = _dot(stri_ref[...], _bf(onehot)) + carry_ref[...]
    code = cls.astype(F32) * float(1 << POS_BITS) + jnp.sum(onehot * before, axis=-1, keepdims=True)
    slots_ref[0] = jnp.broadcast_to(code, (tm, LANES)).T[0:1, :].astype(jnp.int32)
    carry_ref[...] += jnp.sum(onehot, axis=0, keepdims=True)
    counts_ref[...] = carry_ref[...].astype(jnp.int32)


def _post(y, xt, modl, w_out, ln_g, ln_b, rw_pad, rb_pad, seq, alpha):
    t, d = xt.shape
    tm = 256
    per_batch = seq // tm
    row = lambda a: a.reshape(1, -1)
    tile = pl.BlockSpec((tm, d), lambda i: (i, 0))
    full = lambda a: pl.BlockSpec(a.shape, lambda i: (0,) * a.ndim)
    stri = jnp.asarray(np.tril(np.ones((tm, tm), np.float32), -1), BF16)
    args = [y, xt, modl, w_out, row(ln_g), row(ln_b), rw_pad, rb_pad, stri]
    return pl.pallas_call(
        functools.partial(_post_kernel, alpha),
        grid=(t // tm,),
        in_specs=[tile, tile, pl.BlockSpec((1, 6, d), lambda i: (i // per_batch, 0, 0))]
        + [full(a) for a in args[3:]],
        out_specs=[tile, pl.BlockSpec((tm, d + LANES), lambda i: (i, 0)),
                   pl.BlockSpec((1, 1, tm), lambda i: (i, 0, 0)),
                   pl.BlockSpec((1, LANES), lambda i: (0, 0))],
        out_shape=[jax.ShapeDtypeStruct((t, d), F32), jax.ShapeDtypeStruct((t, d + LANES), F32),
                   jax.ShapeDtypeStruct((t // tm, 1, tm), jnp.int32),
                   jax.ShapeDtypeStruct((1, LANES), jnp.int32)],
        scratch_shapes=[pltpu.VMEM((1, LANES), F32)],
        compiler_params=pltpu.CompilerParams(
            dimension_semantics=("arbitrary",), vmem_limit_bytes=VMEM_LIMIT),
        name="post",
    )(*args)


MOE_TILE = 512
POS_BITS = 15
PAIR_BITS = (0b0011, 0b0101, 0b1001, 0b1010, 0b0110, 0b1100)
N_CLASSES = N_GROUPS * len(PAIR_BITS)


def _dispatch_kernel(slots_ref, h_ref, hs_hbm, sem):
    tm = h_ref.shape[0]
    i = pl.program_id(0)

    for row in range(tm):
        s = slots_ref[i * tm + row]
        pltpu.make_async_copy(h_ref.at[pl.ds(row, 1)], hs_hbm.at[pl.ds(s, 1)], sem).start(priority=row % 2)
    pltpu.make_async_copy(h_ref, hs_hbm.at[pl.ds(0, tm)], sem).wait()


def _dispatch(slots, h2):
    t, d = h2.shape
    tm = MOE_TILE
    return pl.pallas_call(
        _dispatch_kernel,
        grid_spec=pltpu.PrefetchScalarGridSpec(
            num_scalar_prefetch=1,
            grid=(t // tm,),
            in_specs=[pl.BlockSpec((tm, d), lambda i, s: (i, 0))],
            out_specs=pl.BlockSpec(memory_space=pl.ANY),
            scratch_shapes=[pltpu.SemaphoreType.DMA(())],
        ),
        out_shape=jax.ShapeDtypeStruct((t, d), F32),
        compiler_params=pltpu.CompilerParams(
            dimension_semantics=("arbitrary",), vmem_limit_bytes=VMEM_LIMIT),
        name="dispatch",
    )(slots, h2)


def _moe_kernel(item_tile_ref, item_expert_ref, item_first_ref, n_items_ref,
                hs_ref, w1_ref, w3_ref, w2_ref, ys_ref, hb_ref):
    s = pl.program_id(0)
    d = ys_ref.shape[1]

    @pl.when(s < n_items_ref[0])
    def _():
        lane = lax.broadcasted_iota(jnp.int32, (1, LANES), 1)
        first_of_tile = item_first_ref[s] == 1

        @pl.when(first_of_tile)
        def _():
            hb_ref[...] = _bf(hs_ref[:, 0:d])

        gates = hs_ref[:, d:d + LANES]
        ge = jnp.sum(jnp.where(lane == item_expert_ref[s], gates, 0.0), axis=-1, keepdims=True)
        hb = hb_ref[...]
        up = _dot(hb, _bf(w1_ref[0, 0]))
        w3 = _bf(w3_ref[0, 0])
        lin = _dot(hb, w3)
        w2 = _bf(w2_ref[0, 0])
        out = _dot(_bf(_silu(up) * lin * ge), w2)

        @pl.when(first_of_tile)
        def _():
            ys_ref[...] = out

        @pl.when(jnp.logical_not(first_of_tile))
        def _():
            ys_ref[...] += out


def _moe(items, hs, w1, w3, w2, layer):
    item_tile, item_expert, item_first, n_items = items
    rows = hs.shape[0]
    d = w1.shape[2]
    up = pl.BlockSpec((1, 1, d, D_EXPERT), lambda s, it, ie, fi, n: (layer, ie[s], 0, 0))
    down = pl.BlockSpec((1, 1, D_EXPERT, d), lambda s, it, ie, fi, n: (layer, ie[s], 0, 0))
    return pl.pallas_call(
        _moe_kernel,
        grid_spec=pltpu.PrefetchScalarGridSpec(
            num_scalar_prefetch=4,
            grid=(item_tile.shape[0],),
            in_specs=[pl.BlockSpec((MOE_TILE, d + LANES), lambda s, it, ie, fi, n: (it[s], 0)), up, up, down],
            out_specs=pl.BlockSpec((MOE_TILE, d), lambda s, it, ie, fi, n: (it[s], 0)),
            scratch_shapes=[pltpu.VMEM((MOE_TILE, d), BF16)],
        ),
        out_shape=jax.ShapeDtypeStruct((rows, d), F32),
        compiler_params=pltpu.CompilerParams(
            dimension_semantics=("arbitrary",), vmem_limit_bytes=VMEM_LIMIT),
        name="moe",
    )(item_tile, item_expert, item_first, n_items, hs, w1, w3, w2)


def _class_experts():
    member = np.zeros((N_CLASSES, N_EXPERTS), np.int32)
    for g in range(N_GROUPS):
        for p, bits in enumerate(PAIR_BITS):
            for e in range(EXPERTS_PER_GROUP):
                if bits >> e & 1:
                    member[g * len(PAIR_BITS) + p, g * EXPERTS_PER_GROUP + e] = 1
    return member


def _work_items(counts, n_tokens):
    n_tiles = n_tokens // MOE_TILE
    n_items_max = 2 * (n_tiles + 2 * N_CLASSES)
    ends = jnp.cumsum(counts)
    starts = ends - counts
    lo = jnp.arange(n_tiles, dtype=jnp.int32)[:, None] * MOE_TILE
    overlap = (counts[None, :] > 0) & (starts[None, :] < lo + MOE_TILE) & (ends[None, :] > lo)
    needed = (overlap.astype(jnp.int32) @ jnp.asarray(_class_experts())) > 0
    flip = (jnp.arange(n_tiles, dtype=jnp.int32) % 2 == 1)[:, None]
    needed = jnp.where(flip, needed[:, ::-1], needed)
    flat = needed.reshape(-1)
    n_items = jnp.sum(flat.astype(jnp.int32))
    place = jnp.where(flat, jnp.cumsum(flat.astype(jnp.int32)) - 1, n_items_max)
    ids = jnp.zeros((n_items_max,), jnp.int32).at[place].set(
        jnp.arange(flat.shape[0], dtype=jnp.int32), mode="drop")
    ids = ids[jnp.minimum(jnp.arange(n_items_max), n_items - 1)]
    item_tile = ids // N_EXPERTS
    item_expert = jnp.where(item_tile % 2 == 1, N_EXPERTS - 1 - ids % N_EXPERTS, ids % N_EXPERTS)
    item_first = jnp.concatenate([jnp.ones((1,), jnp.int32),
                                  (item_tile[1:] != item_tile[:-1]).astype(jnp.int32)])
    return starts.astype(jnp.int32), (item_tile, item_expert, item_first, n_items.reshape(1))


def _combine_kernel(alpha, slots_ref, ys_hbm, x1_ref, mod_ref, lng_ref, lnb_ref, o_ref, buf, sem):
    tm = o_ref.shape[0]
    i = pl.program_id(0)
    n = pl.num_programs(0)

    def gather(tile, b):
        for row in range(tm):
            s = slots_ref[tile * tm + row]
            pltpu.make_async_copy(ys_hbm.at[pl.ds(s, 1)], buf.at[b, pl.ds(row, 1)], sem.at[b]).start(
                priority=row % 2)

    def wait(b):
        pltpu.make_async_copy(ys_hbm.at[pl.ds(0, tm)], buf.at[b], sem.at[b]).wait()

    @pl.when(i == 0)
    def _():
        gather(0, 0)

    cur = i % 2
    nxt = (i + 1) % 2
    wait(cur)
    gather(jnp.minimum(i + 1, n - 1), nxt)
    g2 = mod_ref[0, 5:6, :]
    o_ref[...] = _layer_norm(alpha * x1_ref[...] + g2 * buf[cur], lng_ref[...], lnb_ref[...])

    @pl.when(i == n - 1)
    def _():
        wait(nxt)


def _combine(slots, ys, x1, modl, ln_g, ln_b, seq, alpha):
    t, d = x1.shape
    tm = 256
    per_batch = seq // tm
    row = lambda a: a.reshape(1, -1)
    tile = pl.BlockSpec((tm, d), lambda i, s: (i, 0))
    vec = pl.BlockSpec((1, d), lambda i, s: (0, 0))
    return pl.pallas_call(
        functools.partial(_combine_kernel, alpha),
        grid_spec=pltpu.PrefetchScalarGridSpec(
            num_scalar_prefetch=1,
            grid=(t // tm,),
            in_specs=[pl.BlockSpec(memory_space=pl.ANY), tile,
                      pl.BlockSpec((1, 6, d), lambda i, s: (i // per_batch, 0, 0)), vec, vec],
            out_specs=tile,
            scratch_shapes=[pltpu.VMEM((2, tm, d), F32), pltpu.SemaphoreType.DMA((2,))],
        ),
        out_shape=jax.ShapeDtypeStruct((t, d), F32),
        compiler_params=pltpu.CompilerParams(
            dimension_semantics=("arbitrary",), vmem_limit_bytes=VMEM_LIMIT),
        name="combine",
    )(slots, ys, x1, modl, row(ln_g), row(ln_b))


def _regroup_w_in(w):
    a = 8 * GW
    b = a + 2 * N_HEADS
    cc = b + 4 * GW
    dd = cc + GLA_RANK
    pad = jnp.zeros((w.shape[0], LANES - 2 * N_HEADS - GLA_RANK), w.dtype)
    return jnp.concatenate([w[:, :a], w[:, b:cc], w[:, dd:], w[:, a:b], w[:, cc:dd], pad], axis=1)


def _rotary_tables(seq):
    inv = ROPE_BASE ** (-jnp.arange(0, HEAD_DIM, 2, dtype=F32) / HEAD_DIM)
    ang = jnp.arange(seq, dtype=F32)[:, None] * inv[None, :]
    cos = jnp.cos(ang)
    sin = jnp.sin(ang)
    cosf = jnp.tile(jnp.concatenate([cos, cos], axis=1), (1, N_HEADS))
    sins = jnp.tile(jnp.concatenate([-sin, sin], axis=1), (1, N_HEADS))
    return cosf, sins


def kernel(x, c, ada_w, ada_b, w_in, mlstm_conv, mlstm_gate_b, gla_w2, gla_b2, hgrn_lb, ret_norm, mlstm_norm, gla_norm, hgrn_norm, w_out, ln1_g, ln1_b, router_w, router_b, exp_w1, exp_w3, exp_w2, ln2_g, ln2_b):
    bsz, seq, d = x.shape
    depth = ada_w.shape[0]
    assert d == D_MODEL and seq % 512 == 0 and bsz <= 16
    t = bsz * seq
    alpha = (2.0 * depth) ** 0.25
    assert t % MOE_TILE == 0 and t < (1 << POS_BITS) and bsz % MIXER_BATCH == 0
    consts = _mixer_consts()
    cosf, sins = _rotary_tables(seq)
    c_rows = jnp.pad(c, ((0, 16 - bsz), (0, 0)))
    mod = _adaln(c_rows, ada_w, ada_b)[:, :bsz]
    rw_pad = jnp.pad(router_w, ((0, 0), (0, LANES - N_EXPERTS)))
    rb_pad = jnp.pad(router_b, (0, LANES - N_EXPERTS)).reshape(1, LANES)
    xt = x.reshape(t, d)
    for l in range(depth):
        modl = mod[l].reshape(bsz, 6, d)
        gate_rows = jnp.broadcast_to(mlstm_gate_b[l][:, None], (2 * N_HEADS, CHUNK))
        w2_pad = jnp.zeros((LANES, GW), F32).at[2 * N_HEADS:2 * N_HEADS + GLA_RANK].set(gla_w2[l])
        y = _mixer(xt, modl, _regroup_w_in(_bf(w_in[l])), cosf, sins, mlstm_conv[l], gate_rows, w2_pad,
                   gla_b2[l], hgrn_lb, ret_norm[l], mlstm_norm[l], gla_norm[l], hgrn_norm[l], consts,
                   l, bsz, seq)
        x1, h2, codes, counts = _post(y, xt, modl, _bf(w_out[l]), ln1_g[l], ln1_b[l], rw_pad, rb_pad,
                                      seq, alpha)
        class_start, items = _work_items(counts[0, :N_CLASSES], t)
        codes = codes.reshape(t)
        in_class = (codes >> POS_BITS)[:, None] == jnp.arange(N_CLASSES, dtype=jnp.int32)[None, :]
        slots = jnp.sum(jnp.where(in_class, class_start[None, :], 0), axis=1) + (codes & ((1 << POS_BITS) - 1))
        hs = _dispatch(slots, h2)
        ys = _moe(items, hs, exp_w1, exp_w3, exp_w2, l)
        xt = _combine(slots, ys, x1, modl, ln2_g[l], ln2_b[l], seq, alpha)
    return xt.reshape(bsz, seq, d)
```

```python
import functools

import numpy as np
import jax
import jax.numpy as jnp
from jax import lax
from jax.experimental import pallas as pl
from jax.experimental.pallas import tpu as pltpu

F32 = jnp.float32
BF16 = jnp.bfloat16

D_MODEL = 1024
N_MIXERS = 4
GW = D_MODEL // N_MIXERS
N_HEADS = 4
HEAD_DIM = GW // N_HEADS
CONV_WIDTH = 4
GLA_RANK = 16
GLA_TAU = 16.0
ROPE_BASE = 10000.0
N_GROUPS = 4
EXPERTS_PER_GROUP = 4
N_EXPERTS = 16
D_EXPERT = D_MODEL // 2
LN_EPS = 1e-5
NORM_EPS = 1e-6
QK_SCALE = HEAD_DIM ** -0.5

LANES = 128
CHUNK = 128
LEVELS = 7
CONV_PAD = 8
VMEM_LIMIT = 56 * 1024 * 1024

(R_Q, R_K, R_V, R_G, M_Q, M_K, M_V, M_O, G_Q, G_K, G_V, G_G, H_Q, H_F, H_I, H_G) = (
    GW * i for i in range(16))


def _bf(x):
    return x.astype(BF16)


def _dot(a, b):
    return jnp.dot(a, b, preferred_element_type=F32)


def _dot_nt(a, b):
    return lax.dot_general(a, b, (((1,), (1,)), ((), ())), preferred_element_type=F32)


def _split2(x):
    hi = _bf(x)
    lo = _bf(x - hi.astype(F32))
    return hi, lo


def _dot_f32(a, b):
    ah, al = _split2(a)
    bh, bl = _split2(b)
    return _dot(ah, bh) + _dot(ah, bl) + _dot(al, bh)


def _sigmoid(x):
    return 0.5 * jnp.tanh(0.5 * x) + 0.5


def _silu(x):
    h = 0.5 * x
    return h + h * jnp.tanh(h)


def _log_sigmoid(x):
    return jnp.minimum(x, 0.0) - jnp.log1p(jnp.exp(-jnp.abs(x)))


def _layer_norm(v, g, b):
    mu = jnp.mean(v, axis=-1, keepdims=True)
    d = v - mu
    var = jnp.mean(d * d, axis=-1, keepdims=True)
    return d * lax.rsqrt(var + LN_EPS) * g + b


def _adaln_kernel(c_ref, w_ref, b_ref, o_ref):
    cond = _silu(c_ref[...])
    o_ref[0] = _dot_f32(cond, w_ref[0]) + b_ref[0]


def _adaln(c, ada_w, ada_b):
    depth, d, n = ada_w.shape
    bsz = c.shape[0]
    tn = 1536
    return pl.pallas_call(
        _adaln_kernel,
        grid=(depth, n // tn),
        in_specs=[
            pl.BlockSpec((bsz, d), lambda l, j: (0, 0)),
            pl.BlockSpec((1, d, tn), lambda l, j: (l, 0, j)),
            pl.BlockSpec((1, 1, tn), lambda l, j: (l, 0, j)),
        ],
        out_specs=pl.BlockSpec((1, bsz, tn), lambda l, j: (l, 0, j)),
        out_shape=jax.ShapeDtypeStruct((depth, bsz, n), F32),
        compiler_params=pltpu.CompilerParams(
            dimension_semantics=("arbitrary", "arbitrary"), vmem_limit_bytes=VMEM_LIMIT),
        name="adaln",
    )(c, ada_w, ada_b.reshape(depth, 1, n))


def _mixer_consts():
    c = CHUNK
    i = np.arange(c)[:, None]
    j = np.arange(c)[None, :]
    rel = i - j
    log_gamma = np.log1p(-np.exp2(-5.0 - np.arange(N_HEADS, dtype=np.float64)))
    dret = np.concatenate(
        [np.where(rel >= 0, np.exp(log_gamma[h] * np.maximum(rel, 0)), 0.0) for h in range(N_HEADS)], axis=1)
    lane_head = np.arange(GW) // HEAD_DIM
    dq = np.exp(log_gamma[lane_head][None, :] * (np.arange(c)[:, None] + 1.0))
    dk = np.exp(log_gamma[lane_head][None, :] * (c - 1.0 - np.arange(c)[:, None]))
    gc = np.exp(log_gamma[lane_head] * c)[None, :]
    lv = np.zeros((c, c), np.int32)
    lv[rel == 0] = 1
    spans = [(rel >= 0), (rel < 0)]
    code, b = 2, 1
    while b < c:
        lv[(i // (2 * b) == j // (2 * b)) & ((i // b) % 2 == 1) & ((j // b) % 2 == 0)] = code
        ref = (i // (2 * b)) * 2 * b + b - 1
        right = (i // b) % 2 == 1
        spans.append(np.where(right, (j > ref) & (j <= i), (j > i) & (j <= ref)))
        code, b = code + 1, b * 2
    lv = np.tile(lv, (1, N_HEADS))
    span = np.concatenate(spans, axis=0).astype(np.float32)
    span = np.tile(span, (1, 2))
    hh = (lane_head[:, None] == lane_head[None, :]).astype(np.float32)
    ones_bd = np.repeat(np.eye(N_HEADS, dtype=np.float32), c, axis=0)
    ones_bd = np.repeat(ones_bd, HEAD_DIM, axis=1)
    return dict(
        dret=jnp.asarray(dret, F32), dq=jnp.asarray(dq, F32), dk=jnp.asarray(dk, F32),
        gc=jnp.asarray(gc, F32), lv=jnp.asarray(lv), hh=jnp.asarray(hh, BF16),
        bdf=jnp.asarray(hh, F32), ones_bd=jnp.asarray(ones_bd, BF16), span=jnp.asarray(span, BF16))


def _lane_cumsum(x):
    lane = lax.broadcasted_iota(jnp.int32, x.shape, 1)
    s = 1
    while s < x.shape[1]:
        x = x + jnp.where(lane >= s, pltpu.roll(x, s, 1), 0.0)
        s *= 2
    return x


def _lane_cummax(x):
    lane = lax.broadcasted_iota(jnp.int32, x.shape, 1)
    s = 1
    while s < x.shape[1]:
        x = jnp.maximum(x, jnp.where(lane >= s, pltpu.roll(x, s, 1), -jnp.inf))
        s *= 2
    return x


def _mixer_kernel(layer, nb,
                  x_ref, mod_ref, wa_ref, wg_ref, wh_ref, wx_ref, cos_ref, sin_ref, conv_ref, gb_ref, w2_ref, b2_ref, lb_ref,
                  retg_ref, mlg_ref, glag_ref, hgg_ref,
                  dret_ref, dq_ref, dk_ref, gc_ref, lv_ref, hh_ref, bdf_ref, onesbd_ref, span_ref,
                  y_ref,
                  s_ret, s_cn, s_gla, s_hg, m_ml, conv_buf):
    c = CHUNK
    batches = range(nb)

    @pl.when(pl.program_id(1) == 0)
    def _():
        for s in (s_ret, s_cn, s_gla, s_hg, m_ml):
            s[...] = jnp.zeros_like(s)
        conv_buf[:, 0:CONV_PAD, :] = jnp.zeros((nb, CONV_PAD, 2 * GW), F32)

    lane_head = lax.broadcasted_iota(jnp.int32, (1, GW), 1) // HEAD_DIM
    head_rows = [(lane_head == h).astype(BF16) for h in range(N_HEADS)]
    hh = hh_ref[...]
    bdf = bdf_ref[...]
    lv = lv_ref[...]
    causal = lv >= 1

    def stack(xs):
        return jnp.concatenate(xs, axis=0)

    def tiled(x):
        return stack([x] * nb)

    def seq(x, b):
        return x[b * c:(b + 1) * c]

    hb = _bf(stack([x_ref[b] * (1.0 + mod_ref[b, 1:2, :]) + mod_ref[b, 0:1, :] for b in batches]))

    issued = {}

    def issue(first):
        block = N_MIXERS * GW
        ref, start = {R_Q: (wa_ref, 0), M_Q: (wa_ref, block), G_Q: (wg_ref, 0), H_Q: (wh_ref, 0)}[first]
        issued[first] = _dot(hb, ref[:, start:start + block])

    def col(off, width=GW):
        first = off // (N_MIXERS * GW) * (N_MIXERS * GW)
        return issued[first][:, off - first:off - first + width]

    issue(R_Q)
    issue(M_Q)
    ext = _dot(hb, wx_ref[...])

    def put(off, val):
        for b in batches:
            y_ref[b, :, off:off + GW] = seq(val, b).astype(y_ref.dtype)

    def stackmask(xb):
        return jnp.concatenate([xb * head_rows[h] for h in range(N_HEADS)], axis=0)

    def expand_heads(cols, first):
        out = jnp.zeros((cols.shape[0], GW), F32)
        for h in range(N_HEADS):
            out = jnp.where(lane_head == h, cols[:, first + h:first + h + 1], out)
        return out

    def head_mean(v):
        return _dot(_bf(v), hh) * (1.0 / HEAD_DIM)

    def head_norm(v, gain, center):
        if center:
            v = v - head_mean(v)
        return v * lax.rsqrt(head_mean(v * v) + NORM_EPS) * gain

    cosf = tiled(cos_ref[...])
    sins = tiled(sin_ref[...])
    half = lax.broadcasted_iota(jnp.int32, (1, GW), 1) % HEAD_DIM < HEAD_DIM // 2

    def rotary(v):
        swapped = jnp.where(half, pltpu.roll(v, GW - HEAD_DIM // 2, 1), pltpu.roll(v, HEAD_DIM // 2, 1))
        return v * cosf + swapped * sins

    rq = _bf(rotary(col(R_Q)) * QK_SCALE)
    rk = rotary(col(R_K))
    rkb = _bf(rk)
    rvb = _bf(col(R_V))
    rkd = rk * tiled(dk_ref[...])
    dret = dret_ref[...]
    dq = dq_ref[...]
    gc = gc_ref[...]
    sc = [_dot_nt(seq(rq, b), stackmask(seq(rkb, b))) for b in batches]
    inter = [_dot(seq(rq, b), _bf(s_ret[b])) for b in batches]
    scb = [_bf(s * dret) for s in sc]
    ro = stack([_dot(scb[b], stackmask(seq(rvb, b))) + inter[b] * dq for b in batches])
    upd = [_dot(_bf(seq(rkd, b).T), seq(rvb, b)) for b in batches]
    for b in batches:
        s_ret[b] = s_ret[b] * gc + upd[b] * bdf
    put(0, head_norm(ro, retg_ref[...], True) * _silu(col(R_G)))

    issue(G_Q)
    mqk = col(M_Q, 2 * GW)
    for b in batches:
        conv_buf[b, CONV_PAD:CONV_PAD + c, :] = seq(mqk, b)
    qk = jnp.zeros((nb * c, 2 * GW), F32)
    for t in range(CONV_WIDTH):
        s = CONV_WIDTH - 1 - t
        qk = qk + stack([conv_buf[b, CONV_PAD - s:CONV_PAD - s + c, :] for b in batches]) * conv_ref[t:t + 1, :]
    for b in batches:
        conv_buf[b, 0:CONV_PAD, :] = conv_buf[b, c:c + CONV_PAD, :]
    qk = _silu(qk)
    mq = _bf(qk[:, 0:GW])
    mk = qk[:, GW:2 * GW] * QK_SCALE
    mkb = _bf(mk)
    mvb = _bf(col(M_V))

    pre = stack([seq(ext, b).T[0:8, :] for b in batches]) + tiled(gb_ref[...])
    row8 = lax.broadcasted_iota(jnp.int32, (8 * nb, c), 0)
    valid = row8 % 8 >= N_HEADS
    lf = jnp.where(valid, _log_sigmoid(pre), 0.0)
    gi = jnp.where(valid, pltpu.roll(pre, N_HEADS, 0), 0.0)
    bcum = _lane_cumsum(lf)
    a = gi - bcum
    m_prev = m_ml[...]
    mrow = jnp.maximum(m_prev, _lane_cummax(a))
    m_last = jnp.broadcast_to(mrow[:, c - 1:c], (8 * nb, c))
    b_last = jnp.broadcast_to(bcum[:, c - 1:c], (8 * nb, c))
    wi = jnp.exp(m_prev - mrow)
    em = jnp.exp(-(bcum + mrow))
    wk = jnp.exp(a - m_last)
    dec = jnp.exp(m_prev - m_last)
    m_ml[...] = jnp.where(valid, b_last + m_last, 0.0)
    kinds = [mrow, wi, em, wk]
    if 32 * nb < c:
        kinds.append(jnp.zeros((c - 32 * nb, c), F32))
    cols = stack(kinds).T

    def first_col(kind, b):
        return kind * 8 * nb + 8 * b + N_HEADS

    mexp = [jnp.concatenate([jnp.broadcast_to(cols[:, first_col(0, b) + h:first_col(0, b) + h + 1], (c, c))
                             for h in range(N_HEADS)], axis=1) for b in batches]
    a_row = [jnp.concatenate([a[8 * b + N_HEADS + h:8 * b + N_HEADS + h + 1, :] for h in range(N_HEADS)], axis=1)
             for b in batches]
    qkm = [_dot_nt(seq(mq, b), stackmask(seq(mkb, b))) for b in batches]
    inter = [_dot(seq(mq, b), _bf(s_cn[b])) for b in batches]
    sm = [_bf(qkm[b] * jnp.exp(jnp.where(causal, a_row[b] - mexp[b], -jnp.inf))) for b in batches]
    onesbd = onesbd_ref[...]
    wi_l = stack([expand_heads(cols, first_col(1, b)) for b in batches])
    em_l = stack([expand_heads(cols, first_col(2, b)) for b in batches])
    wk_l = stack([expand_heads(cols, first_col(3, b)) for b in batches])
    nd = (stack([_dot(sm[b], jnp.concatenate([stackmask(seq(mvb, b)), onesbd], axis=1)) for b in batches])
          + jnp.concatenate([wi_l, wi_l], axis=1) * stack(inter))
    mh = nd[:, 0:GW] / jnp.maximum(jnp.abs(nd[:, GW:2 * GW]), em_l)
    kt = mk * wk_l
    ones = jnp.ones((c, GW), BF16)
    upd = [_dot(_bf(seq(kt, b).T), jnp.concatenate([seq(mvb, b), ones], axis=1)) for b in batches]
    dec2 = jnp.concatenate([dec, dec], axis=1)
    bdf2 = jnp.concatenate([bdf, bdf], axis=1)
    for b in batches:
        dec_l = jnp.zeros((1, GW), F32)
        for h in range(N_HEADS):
            dec_l = jnp.where(lane_head == h, dec2[8 * b + N_HEADS + h:8 * b + N_HEADS + h + 1, :], dec_l)
        s_cn[b] = s_cn[b] * jnp.concatenate([dec_l, dec_l], axis=1) + upd[b] * bdf2
    put(GW, head_norm(mh * _sigmoid(col(M_O)), mlg_ref[...], True))

    span = span_ref[...]
    level_masks = [lv == code for code in range(2, 2 + LEVELS)]

    def decay_attention(q, k, v, g, st_ref):
        g2 = _split2(g)
        dec = [jnp.exp(_dot(span, stack([seq(part, b) for part in g2]))) for b in batches]
        e_cum = stack([d[0:c] for d in dec])
        e_end = stack([d[c:2 * c] for d in dec])
        qh = _bf(q)
        kh = _bf(k)
        qe = qh * _bf(e_cum)
        out = stack([_dot_nt(seq(qe, b), _bf(st_ref[b])) for b in batches])
        out = out + _dot(_bf(q * k), hh) * v
        scores = [jnp.zeros((c, N_HEADS * c), F32) for _ in batches]
        for lvl in range(LEVELS):
            e = _bf(stack([d[(2 + lvl) * c:(3 + lvl) * c] for d in dec]))
            qb = qh * e
            kb = kh * e
            level = [_dot_nt(seq(qb, b), stackmask(seq(kb, b))) for b in batches]
            scores = [jnp.where(level_masks[lvl], level[b], scores[b]) for b in batches]
        vb = _bf(v)
        out = out + stack([_dot(_bf(scores[b]), stackmask(seq(vb, b))) for b in batches])
        ke = kh * _bf(e_end)
        upd = [_dot(_bf(seq(v, b).T), seq(ke, b)) for b in batches]
        for b in batches:
            st_ref[b] = st_ref[b] * seq(e_cum, b)[c - 1:c, :] + upd[b] * bdf
        return out

    issue(H_Q)
    x_lr = _dot_f32(ext, w2_ref[...]) + b2_ref[...]
    log_a = _log_sigmoid(x_lr) * (1.0 / GLA_TAU)
    go = decay_attention(col(G_Q), col(G_K) * QK_SCALE, col(G_V), log_a, s_gla)
    put(2 * GW, head_norm(go, glag_ref[...], False) * _silu(col(G_G)))

    lb_all = lb_ref[...]
    lb_e = jnp.exp(lb_all - jnp.max(lb_all, axis=0, keepdims=True))
    lb_p = lb_e / jnp.sum(lb_e, axis=0, keepdims=True)
    lb = jnp.zeros((1, GW), F32)
    for l in range(1, layer + 1):
        lb = lb + lb_p[l:l + 1, :]
    z = col(H_F)
    x1 = jnp.log(lb)
    x2 = jnp.log1p(-lb) + _log_sigmoid(z)
    log_f = jnp.maximum(x1, x2) + jnp.log1p(jnp.exp(-jnp.abs(x1 - x2)))
    k_h = (1.0 - lb) * _sigmoid(-z)
    ho = decay_attention(_silu(col(H_Q)), k_h, col(H_I), log_f, s_hg)
    put(3 * GW, head_norm(ho * _sigmoid(col(H_G)), hgg_ref[...], False))


MIXER_BATCH = 4


def _mixer(xt, modl, w_in_parts, cosf, sins, conv_w, gate_rows, w2_pad, b2, lb, ret_g, ml_g, gla_g, hg_g, consts,
           layer, bsz, seq):
    c = CHUNK
    nc = seq // c
    t, d = xt.shape
    row = lambda a: a.reshape(1, -1)
    full = lambda a: pl.BlockSpec(a.shape, lambda b, n: (0,) * a.ndim)
    cs = consts
    args = [xt.reshape(bsz, seq, d), modl, *w_in_parts, cosf, sins, conv_w, gate_rows, w2_pad, row(b2), lb,
            row(ret_g), row(ml_g), row(gla_g), row(hg_g),
            cs["dret"], cs["dq"], cs["dk"], cs["gc"], cs["lv"], cs["hh"], cs["bdf"], cs["ones_bd"], cs["span"]]
    nb = MIXER_BATCH
    n_w = len(w_in_parts)
    in_specs = [
        pl.BlockSpec((nb, c, d), lambda b, n: (b, n, 0)),
        pl.BlockSpec((nb, 6, d), lambda b, n: (b, 0, 0)),
    ] + [full(a) for a in w_in_parts] + [
        pl.BlockSpec((c, GW), lambda b, n: (n, 0)),
        pl.BlockSpec((c, GW), lambda b, n: (n, 0)),
    ] + [full(a) for a in args[4 + n_w:]]
    y = pl.pallas_call(
        functools.partial(_mixer_kernel, layer, nb),
        grid=(bsz // nb, nc),
        in_specs=in_specs,
        out_specs=pl.BlockSpec((nb, c, D_MODEL), lambda b, n: (b, n, 0)),
        out_shape=jax.ShapeDtypeStruct((bsz, seq, D_MODEL), BF16),
        scratch_shapes=[
            pltpu.VMEM((nb, GW, GW), F32),
            pltpu.VMEM((nb, GW, 2 * GW), F32),
            pltpu.VMEM((nb, GW, GW), F32),
            pltpu.VMEM((nb, GW, GW), F32),
            pltpu.VMEM((nb * 8, c), F32),
            pltpu.VMEM((nb, c + CONV_PAD, 2 * GW), F32),
        ],
        compiler_params=pltpu.CompilerParams(
            dimension_semantics=("arbitrary", "arbitrary"), vmem_limit_bytes=VMEM_LIMIT),
        name="mixer",
    )(*args)
    return y.reshape(t, D_MODEL)


def _group_shift(v, pos, k):
    return jnp.where(pos < EXPERTS_PER_GROUP - k,
                     pltpu.roll(v, LANES - k, 1), pltpu.roll(v, EXPERTS_PER_GROUP - k, 1))


def _router(h, rw, rb):
    lane = lax.broadcasted_iota(jnp.int32, (1, LANES), 1)
    real = lane < N_EXPERTS
    logits = jnp.where(real, _dot_f32(h, rw), -jnp.inf)
    ex = jnp.exp(logits - jnp.max(logits, axis=-1, keepdims=True))
    probs = ex / jnp.sum(ex, axis=-1, keepdims=True)
    sel = jnp.where(real, probs + rb, -jnp.inf)
    pos = lane % EXPERTS_PER_GROUP
    gid = lane // EXPERTS_PER_GROUP
    r1 = _group_shift(sel, pos, 1)
    r2 = _group_shift(sel, pos, 2)
    r3 = _group_shift(sel, pos, 3)
    pair = jnp.maximum(jnp.maximum(jnp.maximum(sel + r1, sel + r2), jnp.maximum(sel + r3, r1 + r2)),
                       jnp.maximum(r1 + r3, r2 + r3))
    pair = jnp.where(real, pair, -jnp.inf)
    best = jnp.max(pair, axis=-1, keepdims=True)
    first = jnp.min(jnp.where(pair == best, gid, N_GROUPS), axis=-1, keepdims=True)
    rank = jnp.zeros(sel.shape, jnp.int32)
    for k, r in ((1, r1), (2, r2), (3, r3)):
        beats = (r > sel) | ((r == sel) & (pos + k >= EXPERTS_PER_GROUP))
        rank = rank + beats.astype(jnp.int32)
    return probs, first, rank


def _post_kernel(alpha, y_ref, x_ref, mod_ref, wout_ref, lng_ref, lnb_ref, rw_ref, rb_ref,
                 stri_ref, x1_ref, h2_ref, slots_ref, counts_ref, carry_ref):
    tm = y_ref.shape[0]

    @pl.when(pl.program_id(0) == 0)
    def _():
        carry_ref[...] = jnp.zeros_like(carry_ref)

    g1 = mod_ref[0, 2:3, :]
    sh2 = mod_ref[0, 3:4, :]
    sc2 = mod_ref[0, 4:5, :]
    mix = _dot(_bf(y_ref[...]), wout_ref[...])
    x1 = _layer_norm(alpha * x_ref[...] + g1 * mix, lng_ref[...], lnb_ref[...])
    x1_ref[...] = x1
    h = x1 * (1.0 + sc2) + sh2
    d = h.shape[1]
    h2_ref[:, 0:d] = h

    probs, first, rank = _router(h, rw_ref[...], rb_ref[...])
    lane = lax.broadcasted_iota(jnp.int32, (1, LANES), 1)
    chosen = (lane // EXPERTS_PER_GROUP == first) & (rank < 2)
    w = jnp.where(chosen, probs, 0.0)
    h2_ref[:, d:d + LANES] = w / jnp.sum(w, axis=-1, keepdims=True)

    bit = jnp.left_shift(1, lane % EXPERTS_PER_GROUP).astype(F32)
    bits = jnp.sum(jnp.where(chosen, bit, 0.0), axis=-1, keepdims=True)
    pair = jnp.full(bits.shape, len(PAIR_BITS) - 1, jnp.int32)
    for index in range(len(PAIR_BITS) - 2, -1, -1):
        pair = jnp.where(bits == float(PAIR_BITS[index]), index, pair)
    cls = first * len(PAIR_BITS) + pair
    onehot = jnp.where(lane == cls, 1.0, 0.0)
    before = _dot(stri_ref[...], _bf(onehot)) + carry_ref[...]
    code = cls.astype(F32) * float(1 << POS_BITS) + jnp.sum(onehot * before, axis=-1, keepdims=True)
    slots_ref[0] = jnp.broadcast_to(code, (tm, LANES)).T[0:1, :].astype(jnp.int32)
    carry_ref[...] += jnp.sum(onehot, axis=0, keepdims=True)
    counts_ref[...] = carry_ref[...].astype(jnp.int32)


def _post(y, xt, modl, w_out, ln_g, ln_b, rw_pad, rb_pad, seq, alpha):
    t, d = xt.shape
    tm = 256
    per_batch = seq // tm
    row = lambda a: a.reshape(1, -1)
    tile = pl.BlockSpec((tm, d), lambda i: (i, 0))
    full = lambda a: pl.BlockSpec(a.shape, lambda i: (0,) * a.ndim)
    stri = jnp.asarray(np.tril(np.ones((tm, tm), np.float32), -1), BF16)
    args = [y, xt, modl, w_out, row(ln_g), row(ln_b), rw_pad, rb_pad, stri]
    return pl.pallas_call(
        functools.partial(_post_kernel, alpha),
        grid=(t // tm,),
        in_specs=[tile, tile, pl.BlockSpec((1, 6, d), lambda i: (i // per_batch, 0, 0))]
        + [full(a) for a in args[3:]],
        out_specs=[tile, pl.BlockSpec((tm, d + LANES), lambda i: (i, 0)),
                   pl.BlockSpec((1, 1, tm), lambda i: (i, 0, 0)),
                   pl.BlockSpec((1, LANES), lambda i: (0, 0))],
        out_shape=[jax.ShapeDtypeStruct((t, d), F32), jax.ShapeDtypeStruct((t, d + LANES), F32),
                   jax.ShapeDtypeStruct((t // tm, 1, tm), jnp.int32),
                   jax.ShapeDtypeStruct((1, LANES), jnp.int32)],
        scratch_shapes=[pltpu.VMEM((1, LANES), F32)],
        compiler_params=pltpu.CompilerParams(
            dimension_semantics=("arbitrary",), vmem_limit_bytes=VMEM_LIMIT),
        name="post",
    )(*args)


MOE_TILE = 512
POS_BITS = 15
PAIR_BITS = (0b0011, 0b0101, 0b1001, 0b1010, 0b0110, 0b1100)
N_CLASSES = N_GROUPS * len(PAIR_BITS)


def _dispatch_kernel(slots_ref, h_ref, hs_hbm, sem):
    tm = h_ref.shape[0]
    i = pl.program_id(0)

    for row in range(tm):
        s = slots_ref[i * tm + row]
        pltpu.make_async_copy(h_ref.at[pl.ds(row, 1)], hs_hbm.at[pl.ds(s, 1)], sem).start(priority=row % 2)
    pltpu.make_async_copy(h_ref, hs_hbm.at[pl.ds(0, tm)], sem).wait()


def _dispatch(slots, h2):
    t, d = h2.shape
    tm = MOE_TILE
    return pl.pallas_call(
        _dispatch_kernel,
        grid_spec=pltpu.PrefetchScalarGridSpec(
            num_scalar_prefetch=1,
            grid=(t // tm,),
            in_specs=[pl.BlockSpec((tm, d), lambda i, s: (i, 0))],
            out_specs=pl.BlockSpec(memory_space=pl.ANY),
            scratch_shapes=[pltpu.SemaphoreType.DMA(())],
        ),
        out_shape=jax.ShapeDtypeStruct((t, d), F32),
        compiler_params=pltpu.CompilerParams(
            dimension_semantics=("arbitrary",), vmem_limit_bytes=VMEM_LIMIT),
        name="dispatch",
    )(slots, h2)


def _moe_kernel(item_tile_ref, item_expert_ref, item_first_ref, n_items_ref,
                hs_ref, w1_ref, w3_ref, w2_ref, ys_ref, hb_ref):
    s = pl.program_id(0)
    d = ys_ref.shape[1]

    @pl.when(s < n_items_ref[0])
    def _():
        lane = lax.broadcasted_iota(jnp.int32, (1, LANES), 1)
        first_of_tile = item_first_ref[s] == 1

        @pl.when(first_of_tile)
        def _():
            hb_ref[...] = _bf(hs_ref[:, 0:d])

        gates = hs_ref[:, d:d + LANES]
        ge = jnp.sum(jnp.where(lane == item_expert_ref[s], gates, 0.0), axis=-1, keepdims=True)
        hb = hb_ref[...]
        up = _dot(hb, _bf(w1_ref[0, 0]))
        w3 = _bf(w3_ref[0, 0])
        lin = _dot(hb, w3)
        w2 = _bf(w2_ref[0, 0])
        out = _dot(_bf(_silu(up) * lin * ge), w2)

        @pl.when(first_of_tile)
        def _():
            ys_ref[...] = out

        @pl.when(jnp.logical_not(first_of_tile))
        def _():
            ys_ref[...] += out


def _moe(items, hs, w1, w3, w2, layer):
    item_tile, item_expert, item_first, n_items = items
    rows = hs.shape[0]
    d = w1.shape[2]
    up = pl.BlockSpec((1, 1, d, D_EXPERT), lambda s, it, ie, fi, n: (layer, ie[s], 0, 0))
    down = pl.BlockSpec((1, 1, D_EXPERT, d), lambda s, it, ie, fi, n: (layer, ie[s], 0, 0))
    return pl.pallas_call(
        _moe_kernel,
        grid_spec=pltpu.PrefetchScalarGridSpec(
            num_scalar_prefetch=4,
            grid=(item_tile.shape[0],),
            in_specs=[pl.BlockSpec((MOE_TILE, d + LANES), lambda s, it, ie, fi, n: (it[s], 0)), up, up, down],
            out_specs=pl.BlockSpec((MOE_TILE, d), lambda s, it, ie, fi, n: (it[s], 0)),
            scratch_shapes=[pltpu.VMEM((MOE_TILE, d), BF16)],
        ),
        out_shape=jax.ShapeDtypeStruct((rows, d), F32),
        compiler_params=pltpu.CompilerParams(
            dimension_semantics=("arbitrary",), vmem_limit_bytes=VMEM_LIMIT),
        name="moe",
    )(item_tile, item_expert, item_first, n_items, hs, w1, w3, w2)


def _class_experts():
    member = np.zeros((N_CLASSES, N_EXPERTS), np.int32)
    for g in range(N_GROUPS):
        for p, bits in enumerate(PAIR_BITS):
            for e in range(EXPERTS_PER_GROUP):
                if bits >> e & 1:
                    member[g * len(PAIR_BITS) + p, g * EXPERTS_PER_GROUP + e] = 1
    return member


def _work_items(counts, n_tokens):
    n_tiles = n_tokens // MOE_TILE
    n_items_max = 2 * (n_tiles + 2 * N_CLASSES)
    ends = jnp.cumsum(counts)
    starts = ends - counts
    lo = jnp.arange(n_tiles, dtype=jnp.int32)[:, None] * MOE_TILE
    overlap = (counts[None, :] > 0) & (starts[None, :] < lo + MOE_TILE) & (ends[None, :] > lo)
    needed = (overlap.astype(jnp.int32) @ jnp.asarray(_class_experts())) > 0
    flip = (jnp.arange(n_tiles, dtype=jnp.int32) % 2 == 1)[:, None]
    needed = jnp.where(flip, needed[:, ::-1], needed)
    flat = needed.reshape(-1)
    n_items = jnp.sum(flat.astype(jnp.int32))
    place = jnp.where(flat, jnp.cumsum(flat.astype(jnp.int32)) - 1, n_items_max)
    ids = jnp.zeros((n_items_max,), jnp.int32).at[place].set(
        jnp.arange(flat.shape[0], dtype=jnp.int32), mode="drop")
    ids = ids[jnp.minimum(jnp.arange(n_items_max), n_items - 1)]
    item_tile = ids // N_EXPERTS
    item_expert = jnp.where(item_tile % 2 == 1, N_EXPERTS - 1 - ids % N_EXPERTS, ids % N_EXPERTS)
    item_first = jnp.concatenate([jnp.ones((1,), jnp.int32),
                                  (item_tile[1:] != item_tile[:-1]).astype(jnp.int32)])
    return starts.astype(jnp.int32), (item_tile, item_expert, item_first, n_items.reshape(1))


def _combine_kernel(alpha, slots_ref, ys_hbm, x1_ref, mod_ref, lng_ref, lnb_ref, o_ref, buf, sem):
    tm = o_ref.shape[0]
    i = pl.program_id(0)
    n = pl.num_programs(0)

    def gather(tile, b):
        for row in range(tm):
            s = slots_ref[tile * tm + row]
            pltpu.make_async_copy(ys_hbm.at[pl.ds(s, 1)], buf.at[b, pl.ds(row, 1)], sem.at[b]).start(
                priority=row % 2)

    def wait(b):
        pltpu.make_async_copy(ys_hbm.at[pl.ds(0, tm)], buf.at[b], sem.at[b]).wait()

    @pl.when(i == 0)
    def _():
        gather(0, 0)

    cur = i % 2
    nxt = (i + 1) % 2
    wait(cur)
    gather(jnp.minimum(i + 1, n - 1), nxt)
    g2 = mod_ref[0, 5:6, :]
    o_ref[...] = _layer_norm(alpha * x1_ref[...] + g2 * buf[cur], lng_ref[...], lnb_ref[...])

    @pl.when(i == n - 1)
    def _():
        wait(nxt)


def _combine(slots, ys, x1, modl, ln_g, ln_b, seq, alpha):
    t, d = x1.shape
    tm = 256
    per_batch = seq // tm
    row = lambda a: a.reshape(1, -1)
    tile = pl.BlockSpec((tm, d), lambda i, s: (i, 0))
    vec = pl.BlockSpec((1, d), lambda i, s: (0, 0))
    return pl.pallas_call(
        functools.partial(_combine_kernel, alpha),
        grid_spec=pltpu.PrefetchScalarGridSpec(
            num_scalar_prefetch=1,
            grid=(t // tm,),
            in_specs=[pl.BlockSpec(memory_space=pl.ANY), tile,
                      pl.BlockSpec((1, 6, d), lambda i, s: (i // per_batch, 0, 0)), vec, vec],
            out_specs=tile,
            scratch_shapes=[pltpu.VMEM((2, tm, d), F32), pltpu.SemaphoreType.DMA((2,))],
        ),
        out_shape=jax.ShapeDtypeStruct((t, d), F32),
        compiler_params=pltpu.CompilerParams(
            dimension_semantics=("arbitrary",), vmem_limit_bytes=VMEM_LIMIT),
        name="combine",
    )(slots, ys, x1, modl, row(ln_g), row(ln_b))


def _split_w_in(w):
    a = 8 * GW
    b = a + 2 * N_HEADS
    cc = b + 4 * GW
    dd = cc + GLA_RANK
    pad = jnp.zeros((w.shape[0], LANES - 2 * N_HEADS - GLA_RANK), w.dtype)
    small = jnp.concatenate([w[:, a:b], w[:, cc:dd], pad], axis=1)
    return _bf(w[:, :a]), _bf(w[:, b:cc]), _bf(w[:, dd:]), _bf(small)


def _rotary_tables(seq):
    inv = ROPE_BASE ** (-jnp.arange(0, HEAD_DIM, 2, dtype=F32) / HEAD_DIM)
    ang = jnp.arange(seq, dtype=F32)[:, None] * inv[None, :]
    cos = jnp.cos(ang)
    sin = jnp.sin(ang)
    cosf = jnp.tile(jnp.concatenate([cos, cos], axis=1), (1, N_HEADS))
    sins = jnp.tile(jnp.concatenate([-sin, sin], axis=1), (1, N_HEADS))
    return cosf, sins


def kernel(x, c, ada_w, ada_b, w_in, mlstm_conv, mlstm_gate_b, gla_w2, gla_b2, hgrn_lb, ret_norm, mlstm_norm, gla_norm, hgrn_norm, w_out, ln1_g, ln1_b, router_w, router_b, exp_w1, exp_w3, exp_w2, ln2_g, ln2_b):
    bsz, seq, d = x.shape
    depth = ada_w.shape[0]
    assert d == D_MODEL and seq % 512 == 0 and bsz <= 16
    t = bsz * seq
    alpha = (2.0 * depth) ** 0.25
    assert t % MOE_TILE == 0 and t < (1 << POS_BITS) and bsz % MIXER_BATCH == 0
    consts = _mixer_consts()
    cosf, sins = _rotary_tables(seq)
    c_rows = jnp.pad(c, ((0, 16 - bsz), (0, 0)))
    mod = _adaln(c_rows, ada_w, ada_b)[:, :bsz]
    rw_pad = jnp.pad(router_w, ((0, 0), (0, LANES - N_EXPERTS)))
    rb_pad = jnp.pad(router_b, (0, LANES - N_EXPERTS)).reshape(1, LANES)
    xt = x.reshape(t, d)
    for l in range(depth):
        modl = mod[l].reshape(bsz, 6, d)
        gate_rows = jnp.broadcast_to(mlstm_gate_b[l][:, None], (2 * N_HEADS, CHUNK))
        w2_pad = jnp.zeros((LANES, GW), F32).at[2 * N_HEADS:2 * N_HEADS + GLA_RANK].set(gla_w2[l])
        y = _mixer(xt, modl, _split_w_in(w_in[l]), cosf, sins, mlstm_conv[l], gate_rows, w2_pad,
                   gla_b2[l], hgrn_lb, ret_norm[l], mlstm_norm[l], gla_norm[l], hgrn_norm[l], consts,
                   l, bsz, seq)
        x1, h2, codes, counts = _post(y, xt, modl, _bf(w_out[l]), ln1_g[l], ln1_b[l], rw_pad, rb_pad,
                                      seq, alpha)
        class_start, items = _work_items(counts[0, :N_CLASSES], t)
        codes = codes.reshape(t)
        in_class = (codes >> POS_BITS)[:, None] == jnp.arange(N_CLASSES, dtype=jnp.int32)[None, :]
        slots = jnp.sum(jnp.where(in_class, class_start[None, :], 0), axis=1) + (codes & ((1 << POS_BITS) - 1))
        hs = _dispatch(slots, h2)
        ys = _moe(items, hs, exp_w1, exp_w3, exp_w2, l)
        xt = _combine(slots, ys, x1, modl, ln2_g[l], ln2_b[l], seq, alpha)
    return xt.reshape(bsz, seq, d)
```

```python
import functools

import numpy as np
import jax
import jax.numpy as jnp
from jax import lax
from jax.experimental import pallas as pl
from jax.experimental.pallas import tpu as pltpu

F32 = jnp.float32
BF16 = jnp.bfloat16

D_MODEL = 1024
N_MIXERS = 4
GW = D_MODEL // N_MIXERS
N_HEADS = 4
HEAD_DIM = GW // N_HEADS
CONV_WIDTH = 4
GLA_RANK = 16
GLA_TAU = 16.0
ROPE_BASE = 10000.0
N_GROUPS = 4
EXPERTS_PER_GROUP = 4
N_EXPERTS = 16
D_EXPERT = D_MODEL // 2
LN_EPS = 1e-5
NORM_EPS = 1e-6
QK_SCALE = HEAD_DIM ** -0.5

LANES = 128
CHUNK = 128
LEVELS = 7
CONV_PAD = 8
VMEM_LIMIT = 56 * 1024 * 1024

(R_Q, R_K, R_V, R_G, M_Q, M_K, M_V, M_O, G_Q, G_K, G_V, G_G, H_Q, H_F, H_I, H_G) = (
    GW * i for i in range(16))


def _bf(x):
    return x.astype(BF16)


def _dot(a, b):
    return jnp.dot(a, b, preferred_element_type=F32)


def _dot_nt(a, b):
    return lax.dot_general(a, b, (((1,), (1,)), ((), ())), preferred_element_type=F32)


def _split2(x):
    hi = _bf(x)
    lo = _bf(x - hi.astype(F32))
    return hi, lo


def _dot_f32(a, b):
    ah, al = _split2(a)
    bh, bl = _split2(b)
    return _dot(ah, bh) + _dot(ah, bl) + _dot(al, bh)


def _sigmoid(x):
    return 0.5 * jnp.tanh(0.5 * x) + 0.5


def _silu(x):
    h = 0.5 * x
    return h + h * jnp.tanh(h)


def _log_sigmoid(x):
    return jnp.minimum(x, 0.0) - jnp.log1p(jnp.exp(-jnp.abs(x)))


def _layer_norm(v, g, b):
    mu = jnp.mean(v, axis=-1, keepdims=True)
    d = v - mu
    var = jnp.mean(d * d, axis=-1, keepdims=True)
    return d * lax.rsqrt(var + LN_EPS) * g + b


def _adaln_kernel(c_ref, w_ref, b_ref, o_ref):
    cond = _silu(c_ref[...])
    o_ref[0] = _dot_f32(cond, w_ref[0]) + b_ref[0]


def _adaln(c, ada_w, ada_b):
    depth, d, n = ada_w.shape
    bsz = c.shape[0]
    tn = 1536
    return pl.pallas_call(
        _adaln_kernel,
        grid=(depth, n // tn),
        in_specs=[
            pl.BlockSpec((bsz, d), lambda l, j: (0, 0)),
            pl.BlockSpec((1, d, tn), lambda l, j: (l, 0, j)),
            pl.BlockSpec((1, 1, tn), lambda l, j: (l, 0, j)),
        ],
        out_specs=pl.BlockSpec((1, bsz, tn), lambda l, j: (l, 0, j)),
        out_shape=jax.ShapeDtypeStruct((depth, bsz, n), F32),
        compiler_params=pltpu.CompilerParams(
            dimension_semantics=("arbitrary", "arbitrary"), vmem_limit_bytes=VMEM_LIMIT),
        name="adaln",
    )(c, ada_w, ada_b.reshape(depth, 1, n))


def _mixer_consts():
    c = CHUNK
    i = np.arange(c)[:, None]
    j = np.arange(c)[None, :]
    rel = i - j
    log_gamma = np.log1p(-np.exp2(-5.0 - np.arange(N_HEADS, dtype=np.float64)))
    dret = np.concatenate(
        [np.where(rel >= 0, np.exp(log_gamma[h] * np.maximum(rel, 0)), 0.0) for h in range(N_HEADS)], axis=1)
    lane_head = np.arange(GW) // HEAD_DIM
    dq = np.exp(log_gamma[lane_head][None, :] * (np.arange(c)[:, None] + 1.0))
    dk = np.exp(log_gamma[lane_head][None, :] * (c - 1.0 - np.arange(c)[:, None]))
    gc = np.exp(log_gamma[lane_head] * c)[None, :]
    lv = np.zeros((c, c), np.int32)
    lv[rel == 0] = 1
    spans = [(rel >= 0), (rel < 0)]
    code, b = 2, 1
    while b < c:
        lv[(i // (2 * b) == j // (2 * b)) & ((i // b) % 2 == 1) & ((j // b) % 2 == 0)] = code
        ref = (i // (2 * b)) * 2 * b + b - 1
        right = (i // b) % 2 == 1
        spans.append(np.where(right, (j > ref) & (j <= i), (j > i) & (j <= ref)))
        code, b = code + 1, b * 2
    lv = np.tile(lv, (1, N_HEADS))
    span = np.concatenate(spans, axis=0).astype(np.float32)
    span = np.tile(span, (1, 2))
    hh = (lane_head[:, None] == lane_head[None, :]).astype(np.float32)
    ones_bd = np.repeat(np.eye(N_HEADS, dtype=np.float32), c, axis=0)
    ones_bd = np.repeat(ones_bd, HEAD_DIM, axis=1)
    return dict(
        dret=jnp.asarray(dret, F32), dq=jnp.asarray(dq, F32), dk=jnp.asarray(dk, F32),
        gc=jnp.asarray(gc, F32), lv=jnp.asarray(lv), hh=jnp.asarray(hh, BF16),
        bdf=jnp.asarray(hh, F32), ones_bd=jnp.asarray(ones_bd, BF16), span=jnp.asarray(span, BF16))


def _lane_cumsum(x):
    lane = lax.broadcasted_iota(jnp.int32, x.shape, 1)
    s = 1
    while s < x.shape[1]:
        x = x + jnp.where(lane >= s, pltpu.roll(x, s, 1), 0.0)
        s *= 2
    return x


def _lane_cummax(x):
    lane = lax.broadcasted_iota(jnp.int32, x.shape, 1)
    s = 1
    while s < x.shape[1]:
        x = jnp.maximum(x, jnp.where(lane >= s, pltpu.roll(x, s, 1), -jnp.inf))
        s *= 2
    return x


N_MIXER_PARAMS = 25


def _mixer_kernel(layer, nb, alpha, seq_len, fused_in, *refs):
    if fused_in:
        slots_ref, x1p_ref, ys_hbm, modp_ref, lng2_ref, lnb2_ref = refs[:6]
        refs = refs[6:]
    else:
        x_ref = refs[0]
        refs = refs[1:]
    (mod_ref, wa_ref, wg_ref, wh_ref, wx_ref, cos_ref, sin_ref, conv_ref, gb_ref, w2_ref, b2_ref, lb_ref,
     retg_ref, mlg_ref, glag_ref, hgg_ref,
     dret_ref, dq_ref, dk_ref, gc_ref, lv_ref, hh_ref, bdf_ref, onesbd_ref, span_ref) = refs[:N_MIXER_PARAMS]
    refs = refs[N_MIXER_PARAMS:]
    if fused_in:
        y_ref, xout_ref, s_ret, s_cn, s_gla, s_hg, m_ml, conv_buf, gbuf, gsem = refs
    else:
        y_ref, s_ret, s_cn, s_gla, s_hg, m_ml, conv_buf = refs

    c = CHUNK
    batches = range(nb)
    step = pl.program_id(1)
    n_steps = pl.num_programs(1)

    def gather(chunk, slot):
        for b in batches:
            for i in range(c):
                s = slots_ref[b * seq_len + chunk * c + i]
                pltpu.make_async_copy(ys_hbm.at[pl.ds(s, 1)], gbuf.at[slot, pl.ds(b * c + i, 1)],
                                      gsem.at[slot]).start(priority=i % 2)

    def gather_wait(slot):
        pltpu.make_async_copy(ys_hbm.at[pl.ds(0, nb * c)], gbuf.at[slot], gsem.at[slot]).wait()

    @pl.when(step == 0)
    def _():
        for s in (s_ret, s_cn, s_gla, s_hg, m_ml):
            s[...] = jnp.zeros_like(s)
        conv_buf[:, 0:CONV_PAD, :] = jnp.zeros((nb, CONV_PAD, 2 * GW), F32)
        if fused_in:
            gather(0, 0)

    lane_head = lax.broadcasted_iota(jnp.int32, (1, GW), 1) // HEAD_DIM
    head_rows = [(lane_head == h).astype(BF16) for h in range(N_HEADS)]
    hh = hh_ref[...]
    bdf = bdf_ref[...]
    lv = lv_ref[...]
    causal = lv >= 1

    def stack(xs):
        return jnp.concatenate(xs, axis=0)

    def tiled(x):
        return stack([x] * nb)

    def seq(x, b):
        return x[b * c:(b + 1) * c]

    if fused_in:
        cur = step % 2
        gather_wait(cur)
        ffn = gbuf[cur]
        xs = _layer_norm(stack([alpha * x1p_ref[b] + modp_ref[b, 5:6, :] * seq(ffn, b) for b in batches]),
                         lng2_ref[...], lnb2_ref[...])
        x_rows = [seq(xs, b) for b in batches]
        for b in batches:
            xout_ref[b] = x_rows[b]
    else:
        x_rows = [x_ref[b] for b in batches]
    hb = _bf(stack([x_rows[b] * (1.0 + mod_ref[b, 1:2, :]) + mod_ref[b, 0:1, :] for b in batches]))

    issued = {}

    def issue(first):
        block = N_MIXERS * GW
        ref, start = {R_Q: (wa_ref, 0), M_Q: (wa_ref, block), G_Q: (wg_ref, 0), H_Q: (wh_ref, 0)}[first]
        issued[first] = _dot(hb, ref[:, start:start + block])

    def col(off, width=GW):
        first = off // (N_MIXERS * GW) * (N_MIXERS * GW)
        return issued[first][:, off - first:off - first + width]

    issue(R_Q)
    issue(M_Q)
    ext = _dot(hb, wx_ref[...])

    def put(off, val):
        for b in batches:
            y_ref[b, :, off:off + GW] = seq(val, b).astype(y_ref.dtype)

    def stackmask(xb):
        return jnp.concatenate([xb * head_rows[h] for h in range(N_HEADS)], axis=0)

    def expand_heads(cols, first):
        out = jnp.zeros((cols.shape[0], GW), F32)
        for h in range(N_HEADS):
            out = jnp.where(lane_head == h, cols[:, first + h:first + h + 1], out)
        return out

    def head_mean(v):
        return _dot(_bf(v), hh) * (1.0 / HEAD_DIM)

    def head_norm(v, gain, center):
        if center:
            v = v - head_mean(v)
        return v * lax.rsqrt(head_mean(v * v) + NORM_EPS) * gain

    cosf = tiled(cos_ref[...])
    sins = tiled(sin_ref[...])
    half = lax.broadcasted_iota(jnp.int32, (1, GW), 1) % HEAD_DIM < HEAD_DIM // 2

    def rotary(v):
        swapped = jnp.where(half, pltpu.roll(v, GW - HEAD_DIM // 2, 1), pltpu.roll(v, HEAD_DIM // 2, 1))
        return v * cosf + swapped * sins

    rq = _bf(rotary(col(R_Q)) * QK_SCALE)
    rk = rotary(col(R_K))
    rkb = _bf(rk)
    rvb = _bf(col(R_V))
    rkd = rk * tiled(dk_ref[...])
    dret = dret_ref[...]
    dq = dq_ref[...]
    gc = gc_ref[...]
    sc = [_dot_nt(seq(rq, b), stackmask(seq(rkb, b))) for b in batches]
    inter = [_dot(seq(rq, b), _bf(s_ret[b])) for b in batches]
    scb = [_bf(s * dret) for s in sc]
    ro = stack([_dot(scb[b], stackmask(seq(rvb, b))) + inter[b] * dq for b in batches])
    upd = [_dot(_bf(seq(rkd, b).T), seq(rvb, b)) for b in batches]
    for b in batches:
        s_ret[b] = s_ret[b] * gc + upd[b] * bdf
    put(0, head_norm(ro, retg_ref[...], True) * _silu(col(R_G)))

    if fused_in:
        gather(jnp.minimum(step + 1, n_steps - 1), (step + 1) % 2)

    issue(G_Q)
    mqk = col(M_Q, 2 * GW)
    for b in batches:
        conv_buf[b, CONV_PAD:CONV_PAD + c, :] = seq(mqk, b)
    qk = jnp.zeros((nb * c, 2 * GW), F32)
    for t in range(CONV_WIDTH):
        s = CONV_WIDTH - 1 - t
        qk = qk + stack([conv_buf[b, CONV_PAD - s:CONV_PAD - s + c, :] for b in batches]) * conv_ref[t:t + 1, :]
    for b in batches:
        conv_buf[b, 0:CONV_PAD, :] = conv_buf[b, c:c + CONV_PAD, :]
    qk = _silu(qk)
    mq = _bf(qk[:, 0:GW])
    mk = qk[:, GW:2 * GW] * QK_SCALE
    mkb = _bf(mk)
    mvb = _bf(col(M_V))

    pre = stack([seq(ext, b).T[0:8, :] for b in batches]) + tiled(gb_ref[...])
    row8 = lax.broadcasted_iota(jnp.int32, (8 * nb, c), 0)
    valid = row8 % 8 >= N_HEADS
    lf = jnp.where(valid, _log_sigmoid(pre), 0.0)
    gi = jnp.where(valid, pltpu.roll(pre, N_HEADS, 0), 0.0)
    bcum = _lane_cumsum(lf)
    a = gi - bcum
    m_prev = m_ml[...]
    mrow = jnp.maximum(m_prev, _lane_cummax(a))
    m_last = jnp.broadcast_to(mrow[:, c - 1:c], (8 * nb, c))
    b_last = jnp.broadcast_to(bcum[:, c - 1:c], (8 * nb, c))
    wi = jnp.exp(m_prev - mrow)
    em = jnp.exp(-(bcum + mrow))
    wk = jnp.exp(a - m_last)
    dec = jnp.exp(m_prev - m_last)
    m_ml[...] = jnp.where(valid, b_last + m_last, 0.0)
    kinds = [mrow, wi, em, wk]
    if 32 * nb < c:
        kinds.append(jnp.zeros((c - 32 * nb, c), F32))
    cols = stack(kinds).T

    def first_col(kind, b):
        return kind * 8 * nb + 8 * b + N_HEADS

    mexp = [jnp.concatenate([jnp.broadcast_to(cols[:, first_col(0, b) + h:first_col(0, b) + h + 1], (c, c))
                             for h in range(N_HEADS)], axis=1) for b in batches]
    a_row = [jnp.concatenate([a[8 * b + N_HEADS + h:8 * b + N_HEADS + h + 1, :] for h in range(N_HEADS)], axis=1)
             for b in batches]
    qkm = [_dot_nt(seq(mq, b), stackmask(seq(mkb, b))) for b in batches]
    inter = [_dot(seq(mq, b), _bf(s_cn[b])) for b in batches]
    sm = [_bf(qkm[b] * jnp.exp(jnp.where(causal, a_row[b] - mexp[b], -jnp.inf))) for b in batches]
    onesbd = onesbd_ref[...]
    wi_l = stack([expand_heads(cols, first_col(1, b)) for b in batches])
    em_l = stack([expand_heads(cols, first_col(2, b)) for b in batches])
    wk_l = stack([expand_heads(cols, first_col(3, b)) for b in batches])
    nd = (stack([_dot(sm[b], jnp.concatenate([stackmask(seq(mvb, b)), onesbd], axis=1)) for b in batches])
          + jnp.concatenate([wi_l, wi_l], axis=1) * stack(inter))
    mh = nd[:, 0:GW] / jnp.maximum(jnp.abs(nd[:, GW:2 * GW]), em_l)
    kt = mk * wk_l
    ones = jnp.ones((c, GW), BF16)
    upd = [_dot(_bf(seq(kt, b).T), jnp.concatenate([seq(mvb, b), ones], axis=1)) for b in batches]
    dec2 = jnp.concatenate([dec, dec], axis=1)
    bdf2 = jnp.concatenate([bdf, bdf], axis=1)
    for b in batches:
        dec_l = jnp.zeros((1, GW), F32)
        for h in range(N_HEADS):
            dec_l = jnp.where(lane_head == h, dec2[8 * b + N_HEADS + h:8 * b + N_HEADS + h + 1, :], dec_l)
        s_cn[b] = s_cn[b] * jnp.concatenate([dec_l, dec_l], axis=1) + upd[b] * bdf2
    put(GW, head_norm(mh * _sigmoid(col(M_O)), mlg_ref[...], True))

    span = span_ref[...]
    level_masks = [lv == code for code in range(2, 2 + LEVELS)]

    def decay_attention(q, k, v, g, st_ref):
        g2 = _split2(g)
        dec = [jnp.exp(_dot(span, stack([seq(part, b) for part in g2]))) for b in batches]
        e_cum = stack([d[0:c] for d in dec])
        e_end = stack([d[c:2 * c] for d in dec])
        qh = _bf(q)
        kh = _bf(k)
        qe = qh * _bf(e_cum)
        out = stack([_dot_nt(seq(qe, b), _bf(st_ref[b])) for b in batches])
        out = out + _dot(_bf(q * k), hh) * v
        scores = [jnp.zeros((c, N_HEADS * c), F32) for _ in batches]
        for lvl in range(LEVELS):
            e = _bf(stack([d[(2 + lvl) * c:(3 + lvl) * c] for d in dec]))
            qb = qh * e
            kb = kh * e
            level = [_dot_nt(seq(qb, b), stackmask(seq(kb, b))) for b in batches]
            scores = [jnp.where(level_masks[lvl], level[b], scores[b]) for b in batches]
        vb = _bf(v)
        out = out + stack([_dot(_bf(scores[b]), stackmask(seq(vb, b))) for b in batches])
        ke = kh * _bf(e_end)
        upd = [_dot(_bf(seq(v, b).T), seq(ke, b)) for b in batches]
        for b in batches:
            st_ref[b] = st_ref[b] * seq(e_cum, b)[c - 1:c, :] + upd[b] * bdf
        return out

    issue(H_Q)
    x_lr = _dot_f32(ext, w2_ref[...]) + b2_ref[...]
    log_a = _log_sigmoid(x_lr) * (1.0 / GLA_TAU)
    go = decay_attention(col(G_Q), col(G_K) * QK_SCALE, col(G_V), log_a, s_gla)
    put(2 * GW, head_norm(go, glag_ref[...], False) * _silu(col(G_G)))

    lb_all = lb_ref[...]
    lb_e = jnp.exp(lb_all - jnp.max(lb_all, axis=0, keepdims=True))
    lb_p = lb_e / jnp.sum(lb_e, axis=0, keepdims=True)
    lb = jnp.zeros((1, GW), F32)
    for l in range(1, layer + 1):
        lb = lb + lb_p[l:l + 1, :]
    z = col(H_F)
    x1 = jnp.log(lb)
    x2 = jnp.log1p(-lb) + _log_sigmoid(z)
    log_f = jnp.maximum(x1, x2) + jnp.log1p(jnp.exp(-jnp.abs(x1 - x2)))
    k_h = (1.0 - lb) * _sigmoid(-z)
    ho = decay_attention(_silu(col(H_Q)), k_h, col(H_I), log_f, s_hg)
    put(3 * GW, head_norm(ho * _sigmoid(col(H_G)), hgg_ref[...], False))

    if fused_in:
        @pl.when(step == n_steps - 1)
        def _():
            gather_wait((step + 1) % 2)


MIXER_BATCH = 4


def _mixer(x_in, modl, w_in_parts, cosf, sins, conv_w, gate_rows, w2_pad, b2, lb, ret_g, ml_g, gla_g, hg_g, consts,
           layer, bsz, seq, alpha):
    c = CHUNK
    nc = seq // c
    nb = MIXER_BATCH
    assert bsz == nb
    d = D_MODEL
    t = bsz * seq
    fused_in = isinstance(x_in, tuple)
    row = lambda a: a.reshape(1, -1)
    full = lambda a: pl.BlockSpec(a.shape, lambda b, n, *_: (0,) * a.ndim)
    tile = lambda width: pl.BlockSpec((nb, c, width), lambda b, n, *_: (b, n, 0))
    mod_spec = pl.BlockSpec((nb, 6, d), lambda b, n, *_: (b, 0, 0))
    cs = consts
    params = [modl, *w_in_parts, cosf, sins, conv_w, gate_rows, w2_pad, row(b2), lb,
              row(ret_g), row(ml_g), row(gla_g), row(hg_g),
              cs["dret"], cs["dq"], cs["dk"], cs["gc"], cs["lv"], cs["hh"], cs["bdf"], cs["ones_bd"], cs["span"]]
    assert len(params) == N_MIXER_PARAMS
    n_w = len(w_in_parts)
    param_specs = [mod_spec] + [full(a) for a in w_in_parts] + [
        pl.BlockSpec((c, GW), lambda b, n, *_: (n, 0)),
        pl.BlockSpec((c, GW), lambda b, n, *_: (n, 0)),
    ] + [full(a) for a in params[3 + n_w:]]
    scratch = [
        pltpu.VMEM((nb, GW, GW), F32),
        pltpu.VMEM((nb, GW, 2 * GW), F32),
        pltpu.VMEM((nb, GW, GW), F32),
        pltpu.VMEM((nb, GW, GW), F32),
        pltpu.VMEM((nb * 8, c), F32),
        pltpu.VMEM((nb, c + CONV_PAD, 2 * GW), F32),
    ]
    y_shape = jax.ShapeDtypeStruct((bsz, seq, d), BF16)
    if fused_in:
        slots, ys, x1, modp, ln2_g, ln2_b = x_in
        args = [slots, x1.reshape(bsz, seq, d), ys, modp, row(ln2_g), row(ln2_b)] + params
        in_specs = [tile(d), pl.BlockSpec(memory_space=pl.ANY), mod_spec, full(row(ln2_g)), full(row(ln2_b))]
        out_specs = [tile(d), tile(d)]
        out_shape = [y_shape, jax.ShapeDtypeStruct((bsz, seq, d), F32)]
        scratch = scratch + [pltpu.VMEM((2, nb * c, d), F32), pltpu.SemaphoreType.DMA((2,))]
        n_prefetch = 1
    else:
        args = [x_in.reshape(bsz, seq, d)] + params
        in_specs = [tile(d)]
        out_specs = tile(d)
        out_shape = y_shape
        n_prefetch = 0
    out = pl.pallas_call(
        functools.partial(_mixer_kernel, layer, nb, alpha, seq, fused_in),
        grid_spec=pltpu.PrefetchScalarGridSpec(
            num_scalar_prefetch=n_prefetch,
            grid=(bsz // nb, nc),
            in_specs=in_specs + param_specs,
            out_specs=out_specs,
            scratch_shapes=scratch,
        ),
        out_shape=out_shape,
        compiler_params=pltpu.CompilerParams(
            dimension_semantics=("arbitrary", "arbitrary"), vmem_limit_bytes=VMEM_LIMIT),
        name="mixer",
    )(*args)
    if fused_in:
        return out[0].reshape(t, d), out[1].reshape(t, d)
    return out.reshape(t, d), x_in


def _group_shift(v, pos, k):
    return jnp.where(pos < EXPERTS_PER_GROUP - k,
                     pltpu.roll(v, LANES - k, 1), pltpu.roll(v, EXPERTS_PER_GROUP - k, 1))


def _router(h, rw, rb):
    lane = lax.broadcasted_iota(jnp.int32, (1, LANES), 1)
    real = lane < N_EXPERTS
    logits = jnp.where(real, _dot_f32(h, rw), -jnp.inf)
    ex = jnp.exp(logits - jnp.max(logits, axis=-1, keepdims=True))
    probs = ex / jnp.sum(ex, axis=-1, keepdims=True)
    sel = jnp.where(real, probs + rb, -jnp.inf)
    pos = lane % EXPERTS_PER_GROUP
    gid = lane // EXPERTS_PER_GROUP
    r1 = _group_shift(sel, pos, 1)
    r2 = _group_shift(sel, pos, 2)
    r3 = _group_shift(sel, pos, 3)
    pair = jnp.maximum(jnp.maximum(jnp.maximum(sel + r1, sel + r2), jnp.maximum(sel + r3, r1 + r2)),
                       jnp.maximum(r1 + r3, r2 + r3))
    pair = jnp.where(real, pair, -jnp.inf)
    best = jnp.max(pair, axis=-1, keepdims=True)
    first = jnp.min(jnp.where(pair == best, gid, N_GROUPS), axis=-1, keepdims=True)
    rank = jnp.zeros(sel.shape, jnp.int32)
    for k, r in ((1, r1), (2, r2), (3, r3)):
        beats = (r > sel) | ((r == sel) & (pos + k >= EXPERTS_PER_GROUP))
        rank = rank + beats.astype(jnp.int32)
    return probs, first, rank


def _post_kernel(alpha, y_ref, x_ref, mod_ref, wout_ref, lng_ref, lnb_ref, rw_ref, rb_ref,
                 stri_ref, x1_ref, h2_ref, slots_ref, counts_ref, carry_ref):
    tm = y_ref.shape[0]

    @pl.when(pl.program_id(0) == 0)
    def _():
        carry_ref[...] = jnp.zeros_like(carry_ref)

    g1 = mod_ref[0, 2:3, :]
    sh2 = mod_ref[0, 3:4, :]
    sc2 = mod_ref[0, 4:5, :]
    mix = _dot(_bf(y_ref[...]), wout_ref[...])
    x1 = _layer_norm(alpha * x_ref[...] + g1 * mix, lng_ref[...], lnb_ref[...])
    x1_ref[...] = x1
    h = x1 * (1.0 + sc2) + sh2
    d = h.shape[1]
    h2_ref[:, 0:d] = h

    probs, first, rank = _router(h, rw_ref[...], rb_ref[...])
    lane = lax.broadcasted_iota(jnp.int32, (1, LANES), 1)
    chosen = (lane // EXPERTS_PER_GROUP == first) & (rank < 2)
    w = jnp.where(chosen, probs, 0.0)
    h2_ref[:, d:d + LANES] = w / jnp.sum(w, axis=-1, keepdims=True)

    bit = jnp.left_shift(1, lane % EXPERTS_PER_GROUP).astype(F32)
    bits = jnp.sum(jnp.where(chosen, bit, 0.0), axis=-1, keepdims=True)
    pair = jnp.full(bits.shape, len(PAIR_BITS) - 1, jnp.int32)
    for index in range(len(PAIR_BITS) - 2, -1, -1):
        pair = jnp.where(bits == float(PAIR_BITS[index]), index, pair)
    cls = first * len(PAIR_BITS) + pair
    onehot = jnp.where(lane == cls, 1.0, 0.0)
    before = _dot(stri_ref[...], _bf(onehot)) + carry_ref[...]
    code = cls.astype(F32) * float(1 << POS_BITS) + jnp.sum(onehot * before, axis=-1, keepdims=True)
    slots_ref[0] = jnp.broadcast_to(code, (tm, LANES)).T[0:1, :].astype(jnp.int32)
    carry_ref[...] += jnp.sum(onehot, axis=0, keepdims=True)
    counts_ref[...] = carry_ref[...].astype(jnp.int32)


def _post(y, xt, modl, w_out, ln_g, ln_b, rw_pad, rb_pad, seq, alpha):
    t, d = xt.shape
    tm = 256
    per_batch = seq // tm
    row = lambda a: a.reshape(1, -1)
    tile = pl.BlockSpec((tm, d), lambda i: (i, 0))
    full = lambda a: pl.BlockSpec(a.shape, lambda i: (0,) * a.ndim)
    stri = jnp.asarray(np.tril(np.ones((tm, tm), np.float32), -1), BF16)
    args = [y, xt, modl, w_out, row(ln_g), row(ln_b), rw_pad, rb_pad, stri]
    return pl.pallas_call(
        functools.partial(_post_kernel, alpha),
        grid=(t // tm,),
        in_specs=[tile, tile, pl.BlockSpec((1, 6, d), lambda i: (i // per_batch, 0, 0))]
        + [full(a) for a in args[3:]],
        out_specs=[tile, pl.BlockSpec((tm, d + LANES), lambda i: (i, 0)),
                   pl.BlockSpec((1, 1, tm), lambda i: (i, 0, 0)),
                   pl.BlockSpec((1, LANES), lambda i: (0, 0))],
        out_shape=[jax.ShapeDtypeStruct((t, d), F32), jax.ShapeDtypeStruct((t, d + LANES), F32),
                   jax.ShapeDtypeStruct((t // tm, 1, tm), jnp.int32),
                   jax.ShapeDtypeStruct((1, LANES), jnp.int32)],
        scratch_shapes=[pltpu.VMEM((1, LANES), F32)],
        compiler_params=pltpu.CompilerParams(
            dimension_semantics=("arbitrary",), vmem_limit_bytes=VMEM_LIMIT),
        name="post",
    )(*args)


MOE_TILE = 512
POS_BITS = 15
PAIR_BITS = (0b0011, 0b0101, 0b1001, 0b1010, 0b0110, 0b1100)
N_CLASSES = N_GROUPS * len(PAIR_BITS)


def _dispatch_kernel(slots_ref, h_ref, hs_hbm, sem):
    tm = h_ref.shape[0]
    i = pl.program_id(0)

    for row in range(tm):
        s = slots_ref[i * tm + row]
        pltpu.make_async_copy(h_ref.at[pl.ds(row, 1)], hs_hbm.at[pl.ds(s, 1)], sem).start(priority=row % 2)
    pltpu.make_async_copy(h_ref, hs_hbm.at[pl.ds(0, tm)], sem).wait()


def _dispatch(slots, h2):
    t, d = h2.shape
    tm = MOE_TILE
    return pl.pallas_call(
        _dispatch_kernel,
        grid_spec=pltpu.PrefetchScalarGridSpec(
            num_scalar_prefetch=1,
            grid=(t // tm,),
            in_specs=[pl.BlockSpec((tm, d), lambda i, s: (i, 0))],
            out_specs=pl.BlockSpec(memory_space=pl.ANY),
            scratch_shapes=[pltpu.SemaphoreType.DMA(())],
        ),
        out_shape=jax.ShapeDtypeStruct((t, d), F32),
        compiler_params=pltpu.CompilerParams(
            dimension_semantics=("arbitrary",), vmem_limit_bytes=VMEM_LIMIT),
        name="dispatch",
    )(slots, h2)


def _moe_kernel(item_tile_ref, item_expert_ref, item_first_ref, n_items_ref,
                hs_ref, w1_ref, w3_ref, w2_ref, ys_ref, hb_ref):
    s = pl.program_id(0)
    d = ys_ref.shape[1]

    @pl.when(s < n_items_ref[0])
    def _():
        lane = lax.broadcasted_iota(jnp.int32, (1, LANES), 1)
        first_of_tile = item_first_ref[s] == 1

        @pl.when(first_of_tile)
        def _():
            hb_ref[...] = _bf(hs_ref[:, 0:d])

        gates = hs_ref[:, d:d + LANES]
        ge = jnp.sum(jnp.where(lane == item_expert_ref[s], gates, 0.0), axis=-1, keepdims=True)
        hb = hb_ref[...]
        up = _dot(hb, _bf(w1_ref[0, 0]))
        w3 = _bf(w3_ref[0, 0])
        lin = _dot(hb, w3)
        w2 = _bf(w2_ref[0, 0])
        out = _dot(_bf(_silu(up) * lin * ge), w2)

        @pl.when(first_of_tile)
        def _():
            ys_ref[...] = out

        @pl.when(jnp.logical_not(first_of_tile))
        def _():
            ys_ref[...] += out


def _moe(items, hs, w1, w3, w2, layer):
    item_tile, item_expert, item_first, n_items = items
    rows = hs.shape[0]
    d = w1.shape[2]
    up = pl.BlockSpec((1, 1, d, D_EXPERT), lambda s, it, ie, fi, n: (layer, ie[s], 0, 0))
    down = pl.BlockSpec((1, 1, D_EXPERT, d), lambda s, it, ie, fi, n: (layer, ie[s], 0, 0))
    return pl.pallas_call(
        _moe_kernel,
        grid_spec=pltpu.PrefetchScalarGridSpec(
            num_scalar_prefetch=4,
            grid=(item_tile.shape[0],),
            in_specs=[pl.BlockSpec((MOE_TILE, d + LANES), lambda s, it, ie, fi, n: (it[s], 0)), up, up, down],
            out_specs=pl.BlockSpec((MOE_TILE, d), lambda s, it, ie, fi, n: (it[s], 0)),
            scratch_shapes=[pltpu.VMEM((MOE_TILE, d), BF16)],
        ),
        out_shape=jax.ShapeDtypeStruct((rows, d), F32),
        compiler_params=pltpu.CompilerParams(
            dimension_semantics=("arbitrary",), vmem_limit_bytes=VMEM_LIMIT),
        name="moe",
    )(item_tile, item_expert, item_first, n_items, hs, w1, w3, w2)


def _class_experts():
    member = np.zeros((N_CLASSES, N_EXPERTS), np.int32)
    for g in range(N_GROUPS):
        for p, bits in enumerate(PAIR_BITS):
            for e in range(EXPERTS_PER_GROUP):
                if bits >> e & 1:
                    member[g * len(PAIR_BITS) + p, g * EXPERTS_PER_GROUP + e] = 1
    return member


def _work_items(counts, n_tokens):
    n_tiles = n_tokens // MOE_TILE
    n_items_max = 2 * (n_tiles + 2 * N_CLASSES)
    ends = jnp.cumsum(counts)
    starts = ends - counts
    lo = jnp.arange(n_tiles, dtype=jnp.int32)[:, None] * MOE_TILE
    overlap = (counts[None, :] > 0) & (starts[None, :] < lo + MOE_TILE) & (ends[None, :] > lo)
    needed = (overlap.astype(jnp.int32) @ jnp.asarray(_class_experts())) > 0
    flip = (jnp.arange(n_tiles, dtype=jnp.int32) % 2 == 1)[:, None]
    needed = jnp.where(flip, needed[:, ::-1], needed)
    flat = needed.reshape(-1)
    n_items = jnp.sum(flat.astype(jnp.int32))
    place = jnp.where(flat, jnp.cumsum(flat.astype(jnp.int32)) - 1, n_items_max)
    ids = jnp.zeros((n_items_max,), jnp.int32).at[place].set(
        jnp.arange(flat.shape[0], dtype=jnp.int32), mode="drop")
    ids = ids[jnp.minimum(jnp.arange(n_items_max), n_items - 1)]
    item_tile = ids // N_EXPERTS
    item_expert = jnp.where(item_tile % 2 == 1, N_EXPERTS - 1 - ids % N_EXPERTS, ids % N_EXPERTS)
    item_first = jnp.concatenate([jnp.ones((1,), jnp.int32),
                                  (item_tile[1:] != item_tile[:-1]).astype(jnp.int32)])
    return starts.astype(jnp.int32), (item_tile, item_expert, item_first, n_items.reshape(1))


def _combine_kernel(alpha, slots_ref, ys_hbm, x1_ref, mod_ref, lng_ref, lnb_ref, o_ref, buf, sem):
    tm = o_ref.shape[0]
    i = pl.program_id(0)
    n = pl.num_programs(0)

    def gather(tile, b):
        for row in range(tm):
            s = slots_ref[tile * tm + row]
            pltpu.make_async_copy(ys_hbm.at[pl.ds(s, 1)], buf.at[b, pl.ds(row, 1)], sem.at[b]).start(
                priority=row % 2)

    def wait(b):
        pltpu.make_async_copy(ys_hbm.at[pl.ds(0, tm)], buf.at[b], sem.at[b]).wait()

    @pl.when(i == 0)
    def _():
        gather(0, 0)

    cur = i % 2
    nxt = (i + 1) % 2
    wait(cur)
    gather(jnp.minimum(i + 1, n - 1), nxt)
    g2 = mod_ref[0, 5:6, :]
    o_ref[...] = _layer_norm(alpha * x1_ref[...] + g2 * buf[cur], lng_ref[...], lnb_ref[...])

    @pl.when(i == n - 1)
    def _():
        wait(nxt)


def _combine(slots, ys, x1, modl, ln_g, ln_b, seq, alpha):
    t, d = x1.shape
    tm = 256
    per_batch = seq // tm
    row = lambda a: a.reshape(1, -1)
    tile = pl.BlockSpec((tm, d), lambda i, s: (i, 0))
    vec = pl.BlockSpec((1, d), lambda i, s: (0, 0))
    return pl.pallas_call(
        functools.partial(_combine_kernel, alpha),
        grid_spec=pltpu.PrefetchScalarGridSpec(
            num_scalar_prefetch=1,
            grid=(t // tm,),
            in_specs=[pl.BlockSpec(memory_space=pl.ANY), tile,
                      pl.BlockSpec((1, 6, d), lambda i, s: (i // per_batch, 0, 0)), vec, vec],
            out_specs=tile,
            scratch_shapes=[pltpu.VMEM((2, tm, d), F32), pltpu.SemaphoreType.DMA((2,))],
        ),
        out_shape=jax.ShapeDtypeStruct((t, d), F32),
        compiler_params=pltpu.CompilerParams(
            dimension_semantics=("arbitrary",), vmem_limit_bytes=VMEM_LIMIT),
        name="combine",
    )(slots, ys, x1, modl, row(ln_g), row(ln_b))


def _split_w_in(w):
    a = 8 * GW
    b = a + 2 * N_HEADS
    cc = b + 4 * GW
    dd = cc + GLA_RANK
    pad = jnp.zeros((w.shape[0], LANES - 2 * N_HEADS - GLA_RANK), w.dtype)
    small = jnp.concatenate([w[:, a:b], w[:, cc:dd], pad], axis=1)
    return _bf(w[:, :a]), _bf(w[:, b:cc]), _bf(w[:, dd:]), _bf(small)


def _rotary_tables(seq):
    inv = ROPE_BASE ** (-jnp.arange(0, HEAD_DIM, 2, dtype=F32) / HEAD_DIM)
    ang = jnp.arange(seq, dtype=F32)[:, None] * inv[None, :]
    cos = jnp.cos(ang)
    sin = jnp.sin(ang)
    cosf = jnp.tile(jnp.concatenate([cos, cos], axis=1), (1, N_HEADS))
    sins = jnp.tile(jnp.concatenate([-sin, sin], axis=1), (1, N_HEADS))
    return cosf, sins


def kernel(x, c, ada_w, ada_b, w_in, mlstm_conv, mlstm_gate_b, gla_w2, gla_b2, hgrn_lb, ret_norm, mlstm_norm, gla_norm, hgrn_norm, w_out, ln1_g, ln1_b, router_w, router_b, exp_w1, exp_w3, exp_w2, ln2_g, ln2_b):
    bsz, seq, d = x.shape
    depth = ada_w.shape[0]
    assert d == D_MODEL and seq % 512 == 0 and bsz <= 16
    t = bsz * seq
    alpha = (2.0 * depth) ** 0.25
    assert t % MOE_TILE == 0 and t < (1 << POS_BITS) and bsz % MIXER_BATCH == 0
    consts = _mixer_consts()
    cosf, sins = _rotary_tables(seq)
    c_rows = jnp.pad(c, ((0, 16 - bsz), (0, 0)))
    mod = _adaln(c_rows, ada_w, ada_b)[:, :bsz]
    rw_pad = jnp.pad(router_w, ((0, 0), (0, LANES - N_EXPERTS)))
    rb_pad = jnp.pad(router_b, (0, LANES - N_EXPERTS)).reshape(1, LANES)
    x_in = x.reshape(t, d)
    for l in range(depth):
        modl = mod[l].reshape(bsz, 6, d)
        gate_rows = jnp.broadcast_to(mlstm_gate_b[l][:, None], (2 * N_HEADS, CHUNK))
        w2_pad = jnp.zeros((LANES, GW), F32).at[2 * N_HEADS:2 * N_HEADS + GLA_RANK].set(gla_w2[l])
        y, xt = _mixer(x_in, modl, _split_w_in(w_in[l]), cosf, sins, mlstm_conv[l], gate_rows, w2_pad,
                       gla_b2[l], hgrn_lb, ret_norm[l], mlstm_norm[l], gla_norm[l], hgrn_norm[l], consts,
                       l, bsz, seq, alpha)
        x1, h2, codes, counts = _post(y, xt, modl, _bf(w_out[l]), ln1_g[l], ln1_b[l], rw_pad, rb_pad,
                                      seq, alpha)
        class_start, items = _work_items(counts[0, :N_CLASSES], t)
        codes = codes.reshape(t)
        in_class = (codes >> POS_BITS)[:, None] == jnp.arange(N_CLASSES, dtype=jnp.int32)[None, :]
        slots = jnp.sum(jnp.where(in_class, class_start[None, :], 0), axis=1) + (codes & ((1 << POS_BITS) - 1))
        hs = _dispatch(slots, h2)
        ys = _moe(items, hs, exp_w1, exp_w3, exp_w2, l)
        x_in = (slots, ys, x1, modl, ln2_g[l], ln2_b[l])
    return _combine(*x_in, seq, alpha).reshape(bsz, seq, d)
```

```python
import functools

import numpy as np
import jax
import jax.numpy as jnp
from jax import lax
from jax.experimental import pallas as pl
from jax.experimental.pallas import tpu as pltpu

F32 = jnp.float32
BF16 = jnp.bfloat16

D_MODEL = 1024
N_MIXERS = 4
GW = D_MODEL // N_MIXERS
N_HEADS = 4
HEAD_DIM = GW // N_HEADS
CONV_WIDTH = 4
GLA_RANK = 16
GLA_TAU = 16.0
ROPE_BASE = 10000.0
N_GROUPS = 4
EXPERTS_PER_GROUP = 4
N_EXPERTS = 16
D_EXPERT = D_MODEL // 2
LN_EPS = 1e-5
NORM_EPS = 1e-6
QK_SCALE = HEAD_DIM ** -0.5

LANES = 128
CHUNK = 128
LEVELS = 7
CONV_PAD = 8
VMEM_LIMIT = 56 * 1024 * 1024

(R_Q, R_K, R_V, R_G, M_Q, M_K, M_V, M_O, G_Q, G_K, G_V, G_G, H_Q, H_F, H_I, H_G) = (
    GW * i for i in range(16))


def _bf(x):
    return x.astype(BF16)


def _dot(a, b):
    return jnp.dot(a, b, preferred_element_type=F32)


def _dot_nt(a, b):
    return lax.dot_general(a, b, (((1,), (1,)), ((), ())), preferred_element_type=F32)


def _split2(x):
    hi = _bf(x)
    lo = _bf(x - hi.astype(F32))
    return hi, lo


def _dot_f32(a, b):
    ah, al = _split2(a)
    bh, bl = _split2(b)
    return _dot(ah, bh) + _dot(ah, bl) + _dot(al, bh)


def _sigmoid(x):
    return 0.5 * jnp.tanh(0.5 * x) + 0.5


def _silu(x):
    h = 0.5 * x
    return h + h * jnp.tanh(h)


def _log_sigmoid(x):
    return jnp.minimum(x, 0.0) - jnp.log1p(jnp.exp(-jnp.abs(x)))


def _layer_norm(v, g, b):
    mu = jnp.mean(v, axis=-1, keepdims=True)
    d = v - mu
    var = jnp.mean(d * d, axis=-1, keepdims=True)
    return d * lax.rsqrt(var + LN_EPS) * g + b


def _adaln_kernel(c_ref, w_ref, b_ref, o_ref):
    cond = _silu(c_ref[...])
    o_ref[0] = _dot_f32(cond, w_ref[0]) + b_ref[0]


def _adaln(c, ada_w, ada_b):
    depth, d, n = ada_w.shape
    bsz = c.shape[0]
    tn = 1536
    return pl.pallas_call(
        _adaln_kernel,
        grid=(depth, n // tn),
        in_specs=[
            pl.BlockSpec((bsz, d), lambda l, j: (0, 0)),
            pl.BlockSpec((1, d, tn), lambda l, j: (l, 0, j)),
            pl.BlockSpec((1, 1, tn), lambda l, j: (l, 0, j)),
        ],
        out_specs=pl.BlockSpec((1, bsz, tn), lambda l, j: (l, 0, j)),
        out_shape=jax.ShapeDtypeStruct((depth, bsz, n), F32),
        compiler_params=pltpu.CompilerParams(
            dimension_semantics=("arbitrary", "arbitrary"), vmem_limit_bytes=VMEM_LIMIT),
        name="adaln",
    )(c, ada_w, ada_b.reshape(depth, 1, n))


def _mixer_consts():
    c = CHUNK
    i = np.arange(c)[:, None]
    j = np.arange(c)[None, :]
    rel = i - j
    log_gamma = np.log1p(-np.exp2(-5.0 - np.arange(N_HEADS, dtype=np.float64)))
    dret = np.concatenate(
        [np.where(rel >= 0, np.exp(log_gamma[h] * np.maximum(rel, 0)), 0.0) for h in range(N_HEADS)], axis=1)
    lane_head = np.arange(GW) // HEAD_DIM
    dq = np.exp(log_gamma[lane_head][None, :] * (np.arange(c)[:, None] + 1.0))
    dk = np.exp(log_gamma[lane_head][None, :] * (c - 1.0 - np.arange(c)[:, None]))
    gc = np.exp(log_gamma[lane_head] * c)[None, :]
    lv = np.zeros((c, c), np.int32)
    lv[rel == 0] = 1
    spans = [(rel >= 0), (rel < 0)]
    code, b = 2, 1
    while b < c:
        lv[(i // (2 * b) == j // (2 * b)) & ((i // b) % 2 == 1) & ((j // b) % 2 == 0)] = code
        ref = (i // (2 * b)) * 2 * b + b - 1
        right = (i // b) % 2 == 1
        spans.append(np.where(right, (j > ref) & (j <= i), (j > i) & (j <= ref)))
        code, b = code + 1, b * 2
    lv = np.tile(lv, (1, N_HEADS))
    span = np.concatenate(spans, axis=0).astype(np.float32)
    span = np.tile(span, (1, 2))
    hh = (lane_head[:, None] == lane_head[None, :]).astype(np.float32)
    ones_bd = np.repeat(np.eye(N_HEADS, dtype=np.float32), c, axis=0)
    ones_bd = np.repeat(ones_bd, HEAD_DIM, axis=1)
    return dict(
        dret=jnp.asarray(dret, F32), dq=jnp.asarray(dq, F32), dk=jnp.asarray(dk, F32),
        gc=jnp.asarray(gc, F32), lv=jnp.asarray(lv), hh=jnp.asarray(hh, BF16),
        bdf=jnp.asarray(hh, F32), ones_bd=jnp.asarray(ones_bd, BF16), span=jnp.asarray(span, BF16))


def _lane_cumsum(x):
    lane = lax.broadcasted_iota(jnp.int32, x.shape, 1)
    s = 1
    while s < x.shape[1]:
        x = x + jnp.where(lane >= s, pltpu.roll(x, s, 1), 0.0)
        s *= 2
    return x


def _lane_cummax(x):
    lane = lax.broadcasted_iota(jnp.int32, x.shape, 1)
    s = 1
    while s < x.shape[1]:
        x = jnp.maximum(x, jnp.where(lane >= s, pltpu.roll(x, s, 1), -jnp.inf))
        s *= 2
    return x


N_MIXER_PARAMS = 25


def _mixer_kernel(layer, nb, alpha, seq_len, fused_in, *refs):
    if fused_in:
        slots_ref, x1p_ref, ys_hbm, modp_ref, lng2_ref, lnb2_ref = refs[:6]
        refs = refs[6:]
    else:
        x_ref = refs[0]
        refs = refs[1:]
    (mod_ref, wa_ref, wg_ref, wh_ref, wx_ref, cos_ref, sin_ref, conv_ref, gb_ref, w2_ref, b2_ref, lb_ref,
     retg_ref, mlg_ref, glag_ref, hgg_ref,
     dret_ref, dq_ref, dk_ref, gc_ref, lv_ref, hh_ref, bdf_ref, onesbd_ref, span_ref) = refs[:N_MIXER_PARAMS]
    refs = refs[N_MIXER_PARAMS:]
    if fused_in:
        y_ref, xout_ref, s_ret, s_cn, s_gla, s_hg, m_ml, conv_buf, gbuf, gsem = refs
    else:
        y_ref, s_ret, s_cn, s_gla, s_hg, m_ml, conv_buf = refs

    c = CHUNK
    batches = range(nb)
    step = pl.program_id(1)
    n_steps = pl.num_programs(1)

    def gather(chunk, slot):
        for b in batches:
            for i in range(c):
                s = slots_ref[b * seq_len + chunk * c + i]
                pltpu.make_async_copy(ys_hbm.at[pl.ds(s, 1)], gbuf.at[slot, pl.ds(b * c + i, 1)],
                                      gsem.at[slot]).start(priority=i % 2)

    def gather_wait(slot):
        pltpu.make_async_copy(ys_hbm.at[pl.ds(0, nb * c)], gbuf.at[slot], gsem.at[slot]).wait()

    @pl.when(step == 0)
    def _():
        for s in (s_ret, s_cn, s_gla, s_hg, m_ml):
            s[...] = jnp.zeros_like(s)
        conv_buf[:, 0:CONV_PAD, :] = jnp.zeros((nb, CONV_PAD, 2 * GW), F32)
        if fused_in:
            gather(0, 0)

    lane_head = lax.broadcasted_iota(jnp.int32, (1, GW), 1) // HEAD_DIM
    head_rows = [(lane_head == h).astype(BF16) for h in range(N_HEADS)]
    hh = hh_ref[...]
    bdf = bdf_ref[...]
    lv = lv_ref[...]
    causal = lv >= 1

    def stack(xs):
        return jnp.concatenate(xs, axis=0)

    def tiled(x):
        return stack([x] * nb)

    def seq(x, b):
        return x[b * c:(b + 1) * c]

    if fused_in:
        cur = step % 2
        gather_wait(cur)
        ffn = gbuf[cur]
        xs = _layer_norm(stack([alpha * x1p_ref[b] + modp_ref[b, 5:6, :] * seq(ffn, b) for b in batches]),
                         lng2_ref[...], lnb2_ref[...])
        x_rows = [seq(xs, b) for b in batches]
        for b in batches:
            xout_ref[b] = x_rows[b]
    else:
        x_rows = [x_ref[b] for b in batches]
    hb = _bf(stack([x_rows[b] * (1.0 + mod_ref[b, 1:2, :]) + mod_ref[b, 0:1, :] for b in batches]))

    issued = {}

    def issue(first):
        block = N_MIXERS * GW
        ref, start = {R_Q: (wa_ref, 0), M_Q: (wa_ref, block), G_Q: (wg_ref, 0), H_Q: (wh_ref, 0)}[first]
        issued[first] = _dot(hb, ref[:, start:start + block])

    def col(off, width=GW):
        first = off // (N_MIXERS * GW) * (N_MIXERS * GW)
        return issued[first][:, off - first:off - first + width]

    issue(R_Q)
    issue(M_Q)
    ext = _dot(hb, wx_ref[...])

    def put(off, val):
        for b in batches:
            y_ref[b, :, off:off + GW] = seq(val, b).astype(y_ref.dtype)

    def stackmask(xb):
        return jnp.concatenate([xb * head_rows[h] for h in range(N_HEADS)], axis=0)

    def expand_heads(cols, first):
        out = jnp.zeros((cols.shape[0], GW), F32)
        for h in range(N_HEADS):
            out = jnp.where(lane_head == h, cols[:, first + h:first + h + 1], out)
        return out

    def head_mean(v):
        return _dot(_bf(v), hh) * (1.0 / HEAD_DIM)

    def head_norm(v, gain, center):
        if center:
            v = v - head_mean(v)
        return v * lax.rsqrt(head_mean(v * v) + NORM_EPS) * gain

    cosf = tiled(cos_ref[...])
    sins = tiled(sin_ref[...])
    half = lax.broadcasted_iota(jnp.int32, (1, GW), 1) % HEAD_DIM < HEAD_DIM // 2

    def rotary(v):
        swapped = jnp.where(half, pltpu.roll(v, GW - HEAD_DIM // 2, 1), pltpu.roll(v, HEAD_DIM // 2, 1))
        return v * cosf + swapped * sins

    rq = _bf(rotary(col(R_Q)) * QK_SCALE)
    rk = rotary(col(R_K))
    rkb = _bf(rk)
    rvb = _bf(col(R_V))
    rkd = rk * tiled(dk_ref[...])
    dret = dret_ref[...]
    dq = dq_ref[...]
    gc = gc_ref[...]
    sc = [_dot_nt(seq(rq, b), stackmask(seq(rkb, b))) for b in batches]
    inter = [_dot(seq(rq, b), _bf(s_ret[b])) for b in batches]
    scb = [_bf(s * dret) for s in sc]
    ro = stack([_dot(scb[b], stackmask(seq(rvb, b))) + inter[b] * dq for b in batches])
    upd = [_dot(_bf(seq(rkd, b).T), seq(rvb, b)) for b in batches]
    for b in batches:
        s_ret[b] = s_ret[b] * gc + upd[b] * bdf
    put(0, head_norm(ro, retg_ref[...], True) * _silu(col(R_G)))

    if fused_in:
        gather(jnp.minimum(step + 1, n_steps - 1), (step + 1) % 2)

    issue(G_Q)
    mqk = col(M_Q, 2 * GW)
    for b in batches:
        conv_buf[b, CONV_PAD:CONV_PAD + c, :] = seq(mqk, b)
    qk = jnp.zeros((nb * c, 2 * GW), F32)
    for t in range(CONV_WIDTH):
        s = CONV_WIDTH - 1 - t
        qk = qk + stack([conv_buf[b, CONV_PAD - s:CONV_PAD - s + c, :] for b in batches]) * conv_ref[t:t + 1, :]
    for b in batches:
        conv_buf[b, 0:CONV_PAD, :] = conv_buf[b, c:c + CONV_PAD, :]
    qk = _silu(qk)
    mq = _bf(qk[:, 0:GW])
    mk = qk[:, GW:2 * GW] * QK_SCALE
    mkb = _bf(mk)
    mvb = _bf(col(M_V))

    pre = stack([seq(ext, b).T[0:8, :] for b in batches]) + tiled(gb_ref[...])
    row8 = lax.broadcasted_iota(jnp.int32, (8 * nb, c), 0)
    valid = row8 % 8 >= N_HEADS
    lf = jnp.where(valid, _log_sigmoid(pre), 0.0)
    gi = jnp.where(valid, pltpu.roll(pre, N_HEADS, 0), 0.0)
    bcum = _lane_cumsum(lf)
    a = gi - bcum
    m_prev = m_ml[...]
    mrow = jnp.maximum(m_prev, _lane_cummax(a))
    m_last = jnp.broadcast_to(mrow[:, c - 1:c], (8 * nb, c))
    b_last = jnp.broadcast_to(bcum[:, c - 1:c], (8 * nb, c))
    wi = jnp.exp(m_prev - mrow)
    em = jnp.exp(-(bcum + mrow))
    wk = jnp.exp(a - m_last)
    dec = jnp.exp(m_prev - m_last)
    m_ml[...] = jnp.where(valid, b_last + m_last, 0.0)
    kinds = [mrow, wi, em, wk]
    if 32 * nb < c:
        kinds.append(jnp.zeros((c - 32 * nb, c), F32))
    cols = stack(kinds).T

    def first_col(kind, b):
        return kind * 8 * nb + 8 * b + N_HEADS

    mexp = [jnp.concatenate([jnp.broadcast_to(cols[:, first_col(0, b) + h:first_col(0, b) + h + 1], (c, c))
                             for h in range(N_HEADS)], axis=1) for b in batches]
    a_row = [jnp.concatenate([a[8 * b + N_HEADS + h:8 * b + N_HEADS + h + 1, :] for h in range(N_HEADS)], axis=1)
             for b in batches]
    qkm = [_dot_nt(seq(mq, b), stackmask(seq(mkb, b))) for b in batches]
    inter = [_dot(seq(mq, b), _bf(s_cn[b])) for b in batches]
    sm = [_bf(qkm[b] * jnp.exp(jnp.where(causal, a_row[b] - mexp[b], -jnp.inf))) for b in batches]
    onesbd = onesbd_ref[...]
    wi_l = stack([expand_heads(cols, first_col(1, b)) for b in batches])
    em_l = stack([expand_heads(cols, first_col(2, b)) for b in batches])
    wk_l = stack([expand_heads(cols, first_col(3, b)) for b in batches])
    nd = (stack([_dot(sm[b], jnp.concatenate([stackmask(seq(mvb, b)), onesbd], axis=1)) for b in batches])
          + jnp.concatenate([wi_l, wi_l], axis=1) * stack(inter))
    mh = nd[:, 0:GW] / jnp.maximum(jnp.abs(nd[:, GW:2 * GW]), em_l)
    kt = mk * wk_l
    ones = jnp.ones((c, GW), BF16)
    upd = [_dot(_bf(seq(kt, b).T), jnp.concatenate([seq(mvb, b), ones], axis=1)) for b in batches]
    dec2 = jnp.concatenate([dec, dec], axis=1)
    bdf2 = jnp.concatenate([bdf, bdf], axis=1)
    for b in batches:
        dec_l = jnp.zeros((1, GW), F32)
        for h in range(N_HEADS):
            dec_l = jnp.where(lane_head == h, dec2[8 * b + N_HEADS + h:8 * b + N_HEADS + h + 1, :], dec_l)
        s_cn[b] = s_cn[b] * jnp.concatenate([dec_l, dec_l], axis=1) + upd[b] * bdf2
    put(GW, head_norm(mh * _sigmoid(col(M_O)), mlg_ref[...], True))

    span = span_ref[...]
    level_masks = [lv == code for code in range(2, 2 + LEVELS)]

    def decay_attention(q, k, v, g, st_ref):
        g2 = _split2(g)
        dec = [jnp.exp(_dot(span, stack([seq(part, b) for part in g2]))) for b in batches]
        e_cum = stack([d[0:c] for d in dec])
        e_end = stack([d[c:2 * c] for d in dec])
        qh = _bf(q)
        kh = _bf(k)
        qe = qh * _bf(e_cum)
        out = stack([_dot_nt(seq(qe, b), _bf(st_ref[b])) for b in batches])
        out = out + _dot(_bf(q * k), hh) * v
        scores = [jnp.zeros((c, N_HEADS * c), F32) for _ in batches]
        for lvl in range(LEVELS):
            e = _bf(stack([d[(2 + lvl) * c:(3 + lvl) * c] for d in dec]))
            qb = qh * e
            kb = kh * e
            level = [_dot_nt(seq(qb, b), stackmask(seq(kb, b))) for b in batches]
            scores = [jnp.where(level_masks[lvl], level[b], scores[b]) for b in batches]
        vb = _bf(v)
        out = out + stack([_dot(_bf(scores[b]), stackmask(seq(vb, b))) for b in batches])
        ke = kh * _bf(e_end)
        upd = [_dot(_bf(seq(v, b).T), seq(ke, b)) for b in batches]
        for b in batches:
            st_ref[b] = st_ref[b] * seq(e_cum, b)[c - 1:c, :] + upd[b] * bdf
        return out

    issue(H_Q)
    x_lr = _dot_f32(ext, w2_ref[...]) + b2_ref[...]
    log_a = _log_sigmoid(x_lr) * (1.0 / GLA_TAU)
    go = decay_attention(col(G_Q), col(G_K) * QK_SCALE, col(G_V), log_a, s_gla)
    put(2 * GW, head_norm(go, glag_ref[...], False) * _silu(col(G_G)))

    lb_all = lb_ref[...]
    lb_e = jnp.exp(lb_all - jnp.max(lb_all, axis=0, keepdims=True))
    lb_p = lb_e / jnp.sum(lb_e, axis=0, keepdims=True)
    lb = jnp.zeros((1, GW), F32)
    for l in range(1, layer + 1):
        lb = lb + lb_p[l:l + 1, :]
    z = col(H_F)
    x1 = jnp.log(lb)
    x2 = jnp.log1p(-lb) + _log_sigmoid(z)
    log_f = jnp.maximum(x1, x2) + jnp.log1p(jnp.exp(-jnp.abs(x1 - x2)))
    k_h = (1.0 - lb) * _sigmoid(-z)
    ho = decay_attention(_silu(col(H_Q)), k_h, col(H_I), log_f, s_hg)
    put(3 * GW, head_norm(ho * _sigmoid(col(H_G)), hgg_ref[...], False))

    if fused_in:
        @pl.when(step == n_steps - 1)
        def _():
            gather_wait((step + 1) % 2)


MIXER_BATCH = 4


def _mixer(x_in, modl, w_in_parts, cosf, sins, conv_w, gate_rows, w2_pad, b2, lb, ret_g, ml_g, gla_g, hg_g, consts,
           layer, bsz, seq, alpha):
    c = CHUNK
    nc = seq // c
    nb = MIXER_BATCH
    assert bsz == nb
    d = D_MODEL
    t = bsz * seq
    fused_in = isinstance(x_in, tuple)
    row = lambda a: a.reshape(1, -1)
    full = lambda a: pl.BlockSpec(a.shape, lambda b, n, *_: (0,) * a.ndim)
    tile = lambda width: pl.BlockSpec((nb, c, width), lambda b, n, *_: (b, n, 0))
    mod_spec = pl.BlockSpec((nb, 6, d), lambda b, n, *_: (b, 0, 0))
    cs = consts
    params = [modl, *w_in_parts, cosf, sins, conv_w, gate_rows, w2_pad, row(b2), lb,
              row(ret_g), row(ml_g), row(gla_g), row(hg_g),
              cs["dret"], cs["dq"], cs["dk"], cs["gc"], cs["lv"], cs["hh"], cs["bdf"], cs["ones_bd"], cs["span"]]
    assert len(params) == N_MIXER_PARAMS
    n_w = len(w_in_parts)
    param_specs = [mod_spec] + [full(a) for a in w_in_parts] + [
        pl.BlockSpec((c, GW), lambda b, n, *_: (n, 0)),
        pl.BlockSpec((c, GW), lambda b, n, *_: (n, 0)),
    ] + [full(a) for a in params[3 + n_w:]]
    scratch = [
        pltpu.VMEM((nb, GW, GW), F32),
        pltpu.VMEM((nb, GW, 2 * GW), F32),
        pltpu.VMEM((nb, GW, GW), F32),
        pltpu.VMEM((nb, GW, GW), F32),
        pltpu.VMEM((nb * 8, c), F32),
        pltpu.VMEM((nb, c + CONV_PAD, 2 * GW), F32),
    ]
    y_shape = jax.ShapeDtypeStruct((bsz, seq, d), BF16)
    if fused_in:
        slots, ys, x1, modp, ln2_g, ln2_b = x_in
        args = [slots, x1.reshape(bsz, seq, d), ys, modp, row(ln2_g), row(ln2_b)] + params
        in_specs = [tile(d), pl.BlockSpec(memory_space=pl.ANY), mod_spec, full(row(ln2_g)), full(row(ln2_b))]
        out_specs = [tile(d), tile(d)]
        out_shape = [y_shape, jax.ShapeDtypeStruct((bsz, seq, d), F32)]
        scratch = scratch + [pltpu.VMEM((2, nb * c, d), F32), pltpu.SemaphoreType.DMA((2,))]
        n_prefetch = 1
    else:
        args = [x_in.reshape(bsz, seq, d)] + params
        in_specs = [tile(d)]
        out_specs = tile(d)
        out_shape = y_shape
        n_prefetch = 0
    out = pl.pallas_call(
        functools.partial(_mixer_kernel, layer, nb, alpha, seq, fused_in),
        grid_spec=pltpu.PrefetchScalarGridSpec(
            num_scalar_prefetch=n_prefetch,
            grid=(bsz // nb, nc),
            in_specs=in_specs + param_specs,
            out_specs=out_specs,
            scratch_shapes=scratch,
        ),
        out_shape=out_shape,
        compiler_params=pltpu.CompilerParams(
            dimension_semantics=("arbitrary", "arbitrary"), vmem_limit_bytes=VMEM_LIMIT),
        name="mixer",
    )(*args)
    if fused_in:
        return out[0].reshape(t, d), out[1].reshape(t, d)
    return out.reshape(t, d), x_in


def _group_shift(v, pos, k):
    return jnp.where(pos < EXPERTS_PER_GROUP - k,
                     pltpu.roll(v, LANES - k, 1), pltpu.roll(v, EXPERTS_PER_GROUP - k, 1))


def _router(h, rw, rb):
    lane = lax.broadcasted_iota(jnp.int32, (1, LANES), 1)
    real = lane < N_EXPERTS
    logits = jnp.where(real, _dot_f32(h, rw), -jnp.inf)
    ex = jnp.exp(logits - jnp.max(logits, axis=-1, keepdims=True))
    probs = ex / jnp.sum(ex, axis=-1, keepdims=True)
    sel = jnp.where(real, probs + rb, -jnp.inf)
    pos = lane % EXPERTS_PER_GROUP
    gid = lane // EXPERTS_PER_GROUP
    r1 = _group_shift(sel, pos, 1)
    r2 = _group_shift(sel, pos, 2)
    r3 = _group_shift(sel, pos, 3)
    pair = jnp.maximum(jnp.maximum(jnp.maximum(sel + r1, sel + r2), jnp.maximum(sel + r3, r1 + r2)),
                       jnp.maximum(r1 + r3, r2 + r3))
    pair = jnp.where(real, pair, -jnp.inf)
    best = jnp.max(pair, axis=-1, keepdims=True)
    first = jnp.min(jnp.where(pair == best, gid, N_GROUPS), axis=-1, keepdims=True)
    rank = jnp.zeros(sel.shape, jnp.int32)
    for k, r in ((1, r1), (2, r2), (3, r3)):
        beats = (r > sel) | ((r == sel) & (pos + k >= EXPERTS_PER_GROUP))
        rank = rank + beats.astype(jnp.int32)
    return probs, first, rank


def _post_kernel(alpha, y_ref, x_ref, mod_ref, wout_ref, lng_ref, lnb_ref, rw_ref, rb_ref,
                 stri_ref, x1_ref, h2_ref, slots_ref, counts_ref, carry_ref):
    tm = y_ref.shape[0]

    @pl.when(pl.program_id(0) == 0)
    def _():
        carry_ref[...] = jnp.zeros_like(carry_ref)

    g1 = mod_ref[0, 2:3, :]
    sh2 = mod_ref[0, 3:4, :]
    sc2 = mod_ref[0, 4:5, :]
    mix = _dot(_bf(y_ref[...]), wout_ref[...])
    x1 = _layer_norm(alpha * x_ref[...] + g1 * mix, lng_ref[...], lnb_ref[...])
    x1_ref[...] = x1
    h = x1 * (1.0 + sc2) + sh2
    d = h.shape[1]
    h2_ref[:, 0:d] = h

    probs, first, rank = _router(h, rw_ref[...], rb_ref[...])
    lane = lax.broadcasted_iota(jnp.int32, (1, LANES), 1)
    chosen = (lane // EXPERTS_PER_GROUP == first) & (rank < 2)
    w = jnp.where(chosen, probs, 0.0)
    h2_ref[:, d:d + LANES] = w / jnp.sum(w, axis=-1, keepdims=True)

    bit = jnp.left_shift(1, lane % EXPERTS_PER_GROUP).astype(F32)
    bits = jnp.sum(jnp.where(chosen, bit, 0.0), axis=-1, keepdims=True)
    pair = jnp.full(bits.shape, len(PAIR_BITS) - 1, jnp.int32)
    for index in range(len(PAIR_BITS) - 2, -1, -1):
        pair = jnp.where(bits == float(PAIR_BITS[index]), index, pair)
    cls = first * len(PAIR_BITS) + pair
    onehot = jnp.where(lane == cls, 1.0, 0.0)
    before = _dot(stri_ref[...], _bf(onehot)) + carry_ref[...]
    code = cls.astype(F32) * float(1 << POS_BITS) + jnp.sum(onehot * before, axis=-1, keepdims=True)
    slots_ref[0] = jnp.broadcast_to(code, (tm, LANES)).T[0:1, :].astype(jnp.int32)
    carry_ref[...] += jnp.sum(onehot, axis=0, keepdims=True)
    counts_ref[...] = carry_ref[...].astype(jnp.int32)


def _post(y, xt, modl, w_out, ln_g, ln_b, rw_pad, rb_pad, seq, alpha):
    t, d = xt.shape
    tm = 256
    per_batch = seq // tm
    row = lambda a: a.reshape(1, -1)
    tile = pl.BlockSpec((tm, d), lambda i: (i, 0))
    full = lambda a: pl.BlockSpec(a.shape, lambda i: (0,) * a.ndim)
    stri = jnp.asarray(np.tril(np.ones((tm, tm), np.float32), -1), BF16)
    args = [y, xt, modl, w_out, row(ln_g), row(ln_b), rw_pad, rb_pad, stri]
    return pl.pallas_call(
        functools.partial(_post_kernel, alpha),
        grid=(t // tm,),
        in_specs=[tile, tile, pl.BlockSpec((1, 6, d), lambda i: (i // per_batch, 0, 0))]
        + [full(a) for a in args[3:]],
        out_specs=[tile, pl.BlockSpec((tm, d + LANES), lambda i: (i, 0)),
                   pl.BlockSpec((1, 1, tm), lambda i: (i, 0, 0)),
                   pl.BlockSpec((1, LANES), lambda i: (0, 0))],
        out_shape=[jax.ShapeDtypeStruct((t, d), F32), jax.ShapeDtypeStruct((t, d + LANES), F32),
                   jax.ShapeDtypeStruct((t // tm, 1, tm), jnp.int32),
                   jax.ShapeDtypeStruct((1, LANES), jnp.int32)],
        scratch_shapes=[pltpu.VMEM((1, LANES), F32)],
        compiler_params=pltpu.CompilerParams(
            dimension_semantics=("arbitrary",), vmem_limit_bytes=VMEM_LIMIT),
        name="post",
    )(*args)


MOE_TILE = 512
POS_BITS = 15
PAIR_BITS = (0b0011, 0b0101, 0b1001, 0b1010, 0b0110, 0b1100)
N_CLASSES = N_GROUPS * len(PAIR_BITS)


def _moe_kernel(n_tiles, item_tile_ref, item_expert_ref, item_first_ref, n_items_ref, token_ref,
                h_hbm, w1_ref, w3_ref, w2_ref, ys_ref, rows_buf, sem, hb_ref):
    s = pl.program_id(0)
    d = ys_ref.shape[1]
    tile = item_tile_ref[s]
    cur = tile % 2
    lane = lax.broadcasted_iota(jnp.int32, (1, LANES), 1)

    def gather(t, slot):
        for row in range(MOE_TILE):
            token = token_ref[t * MOE_TILE + row]
            pltpu.make_async_copy(h_hbm.at[pl.ds(token, 1)], rows_buf.at[slot, pl.ds(row, 1)],
                                  sem.at[slot]).start(priority=row % 2)

    def gather_wait(slot):
        pltpu.make_async_copy(h_hbm.at[pl.ds(0, MOE_TILE)], rows_buf.at[slot], sem.at[slot]).wait()

    def expert(hb, between=lambda: None):
        gates = rows_buf[cur, :, d:d + LANES]
        ge = jnp.sum(jnp.where(lane == item_expert_ref[s], gates, 0.0), axis=-1, keepdims=True)
        up = _dot(hb, _bf(w1_ref[0, 0]))
        between()
        w3 = _bf(w3_ref[0, 0])
        lin = _dot(hb, w3)
        w2 = _bf(w2_ref[0, 0])
        return _dot(_bf(_silu(up) * lin * ge), w2)

    @pl.when(s == 0)
    def _():
        gather(0, 0)

    valid = s < n_items_ref[0]
    first_of_tile = item_first_ref[s] == 1

    @pl.when(valid & first_of_tile)
    def _():
        gather_wait(cur)
        hb = _bf(rows_buf[cur, :, 0:d])
        hb_ref[...] = hb
        ys_ref[...] = expert(hb, lambda: gather(jnp.minimum(tile + 1, n_tiles - 1), 1 - cur))

    @pl.when(valid & jnp.logical_not(first_of_tile))
    def _():
        ys_ref[...] += expert(hb_ref[...])

    @pl.when(s == n_items_ref[0] - 1)
    def _():
        gather_wait(1 - cur)


def _moe(items, token_of_slot, h2, w1, w3, w2, layer):
    item_tile, item_expert, item_first, n_items = items
    rows, width = h2.shape
    d = w1.shape[2]
    up = pl.BlockSpec((1, 1, d, D_EXPERT), lambda s, it, ie, fi, n, tk: (layer, ie[s], 0, 0))
    down = pl.BlockSpec((1, 1, D_EXPERT, d), lambda s, it, ie, fi, n, tk: (layer, ie[s], 0, 0))
    return pl.pallas_call(
        functools.partial(_moe_kernel, rows // MOE_TILE),
        grid_spec=pltpu.PrefetchScalarGridSpec(
            num_scalar_prefetch=5,
            grid=(item_tile.shape[0],),
            in_specs=[pl.BlockSpec(memory_space=pl.ANY), up, up, down],
            out_specs=pl.BlockSpec((MOE_TILE, d), lambda s, it, ie, fi, n, tk: (it[s], 0)),
            scratch_shapes=[pltpu.VMEM((2, MOE_TILE, width), F32), pltpu.SemaphoreType.DMA((2,)),
                            pltpu.VMEM((MOE_TILE, d), BF16)],
        ),
        out_shape=jax.ShapeDtypeStruct((rows, d), F32),
        compiler_params=pltpu.CompilerParams(
            dimension_semantics=("arbitrary",), vmem_limit_bytes=VMEM_LIMIT),
        name="moe",
    )(item_tile, item_expert, item_first, n_items, token_of_slot, h2, w1, w3, w2)


def _class_experts():
    member = np.zeros((N_CLASSES, N_EXPERTS), np.int32)
    for g in range(N_GROUPS):
        for p, bits in enumerate(PAIR_BITS):
            for e in range(EXPERTS_PER_GROUP):
                if bits >> e & 1:
                    member[g * len(PAIR_BITS) + p, g * EXPERTS_PER_GROUP + e] = 1
    return member


def _work_items(counts, n_tokens):
    n_tiles = n_tokens // MOE_TILE
    n_items_max = 2 * (n_tiles + 2 * N_CLASSES)
    ends = jnp.cumsum(counts)
    starts = ends - counts
    lo = jnp.arange(n_tiles, dtype=jnp.int32)[:, None] * MOE_TILE
    overlap = (counts[None, :] > 0) & (starts[None, :] < lo + MOE_TILE) & (ends[None, :] > lo)
    needed = (overlap.astype(jnp.int32) @ jnp.asarray(_class_experts())) > 0
    flip = (jnp.arange(n_tiles, dtype=jnp.int32) % 2 == 1)[:, None]
    needed = jnp.where(flip, needed[:, ::-1], needed)
    flat = needed.reshape(-1)
    n_items = jnp.sum(flat.astype(jnp.int32))
    place = jnp.where(flat, jnp.cumsum(flat.astype(jnp.int32)) - 1, n_items_max)
    ids = jnp.zeros((n_items_max,), jnp.int32).at[place].set(
        jnp.arange(flat.shape[0], dtype=jnp.int32), mode="drop")
    ids = ids[jnp.minimum(jnp.arange(n_items_max), n_items - 1)]
    item_tile = ids // N_EXPERTS
    item_expert = jnp.where(item_tile % 2 == 1, N_EXPERTS - 1 - ids % N_EXPERTS, ids % N_EXPERTS)
    item_first = jnp.concatenate([jnp.ones((1,), jnp.int32),
                                  (item_tile[1:] != item_tile[:-1]).astype(jnp.int32)])
    return starts.astype(jnp.int32), (item_tile, item_expert, item_first, n_items.reshape(1))


def _combine_kernel(alpha, slots_ref, ys_hbm, x1_ref, mod_ref, lng_ref, lnb_ref, o_ref, buf, sem):
    tm = o_ref.shape[0]
    i = pl.program_id(0)
    n = pl.num_programs(0)

    def gather(tile, b):
        for row in range(tm):
            s = slots_ref[tile * tm + row]
            pltpu.make_async_copy(ys_hbm.at[pl.ds(s, 1)], buf.at[b, pl.ds(row, 1)], sem.at[b]).start(
                priority=row % 2)

    def wait(b):
        pltpu.make_async_copy(ys_hbm.at[pl.ds(0, tm)], buf.at[b], sem.at[b]).wait()

    @pl.when(i == 0)
    def _():
        gather(0, 0)

    cur = i % 2
    nxt = (i + 1) % 2
    wait(cur)
    gather(jnp.minimum(i + 1, n - 1), nxt)
    g2 = mod_ref[0, 5:6, :]
    o_ref[...] = _layer_norm(alpha * x1_ref[...] + g2 * buf[cur], lng_ref[...], lnb_ref[...])

    @pl.when(i == n - 1)
    def _():
        wait(nxt)


def _combine(slots, ys, x1, modl, ln_g, ln_b, seq, alpha):
    t, d = x1.shape
    tm = 256
    per_batch = seq // tm
    row = lambda a: a.reshape(1, -1)
    tile = pl.BlockSpec((tm, d), lambda i, s: (i, 0))
    vec = pl.BlockSpec((1, d), lambda i, s: (0, 0))
    return pl.pallas_call(
        functools.partial(_combine_kernel, alpha),
        grid_spec=pltpu.PrefetchScalarGridSpec(
            num_scalar_prefetch=1,
            grid=(t // tm,),
            in_specs=[pl.BlockSpec(memory_space=pl.ANY), tile,
                      pl.BlockSpec((1, 6, d), lambda i, s: (i // per_batch, 0, 0)), vec, vec],
            out_specs=tile,
            scratch_shapes=[pltpu.VMEM((2, tm, d), F32), pltpu.SemaphoreType.DMA((2,))],
        ),
        out_shape=jax.ShapeDtypeStruct((t, d), F32),
        compiler_params=pltpu.CompilerParams(
            dimension_semantics=("arbitrary",), vmem_limit_bytes=VMEM_LIMIT),
        name="combine",
    )(slots, ys, x1, modl, row(ln_g), row(ln_b))


def _split_w_in(w):
    a = 8 * GW
    b = a + 2 * N_HEADS
    cc = b + 4 * GW
    dd = cc + GLA_RANK
    pad = jnp.zeros((w.shape[0], LANES - 2 * N_HEADS - GLA_RANK), w.dtype)
    small = jnp.concatenate([w[:, a:b], w[:, cc:dd], pad], axis=1)
    return _bf(w[:, :a]), _bf(w[:, b:cc]), _bf(w[:, dd:]), _bf(small)


def _rotary_tables(seq):
    inv = ROPE_BASE ** (-jnp.arange(0, HEAD_DIM, 2, dtype=F32) / HEAD_DIM)
    ang = jnp.arange(seq, dtype=F32)[:, None] * inv[None, :]
    cos = jnp.cos(ang)
    sin = jnp.sin(ang)
    cosf = jnp.tile(jnp.concatenate([cos, cos], axis=1), (1, N_HEADS))
    sins = jnp.tile(jnp.concatenate([-sin, sin], axis=1), (1, N_HEADS))
    return cosf, sins


def kernel(x, c, ada_w, ada_b, w_in, mlstm_conv, mlstm_gate_b, gla_w2, gla_b2, hgrn_lb, ret_norm, mlstm_norm, gla_norm, hgrn_norm, w_out, ln1_g, ln1_b, router_w, router_b, exp_w1, exp_w3, exp_w2, ln2_g, ln2_b):
    bsz, seq, d = x.shape
    depth = ada_w.shape[0]
    assert d == D_MODEL and seq % 512 == 0 and bsz <= 16
    t = bsz * seq
    alpha = (2.0 * depth) ** 0.25
    assert t % MOE_TILE == 0 and t < (1 << POS_BITS) and bsz % MIXER_BATCH == 0
    consts = _mixer_consts()
    cosf, sins = _rotary_tables(seq)
    c_rows = jnp.pad(c, ((0, 16 - bsz), (0, 0)))
    mod = _adaln(c_rows, ada_w, ada_b)[:, :bsz]
    rw_pad = jnp.pad(router_w, ((0, 0), (0, LANES - N_EXPERTS)))
    rb_pad = jnp.pad(router_b, (0, LANES - N_EXPERTS)).reshape(1, LANES)
    x_in = x.reshape(t, d)
    for l in range(depth):
        modl = mod[l].reshape(bsz, 6, d)
        gate_rows = jnp.broadcast_to(mlstm_gate_b[l][:, None], (2 * N_HEADS, CHUNK))
        w2_pad = jnp.zeros((LANES, GW), F32).at[2 * N_HEADS:2 * N_HEADS + GLA_RANK].set(gla_w2[l])
        y, xt = _mixer(x_in, modl, _split_w_in(w_in[l]), cosf, sins, mlstm_conv[l], gate_rows, w2_pad,
                       gla_b2[l], hgrn_lb, ret_norm[l], mlstm_norm[l], gla_norm[l], hgrn_norm[l], consts,
                       l, bsz, seq, alpha)
        x1, h2, codes, counts = _post(y, xt, modl, _bf(w_out[l]), ln1_g[l], ln1_b[l], rw_pad, rb_pad,
                                      seq, alpha)
        class_start, items = _work_items(counts[0, :N_CLASSES], t)
        codes = codes.reshape(t)
        in_class = (codes >> POS_BITS)[:, None] == jnp.arange(N_CLASSES, dtype=jnp.int32)[None, :]
        slots = jnp.sum(jnp.where(in_class, class_start[None, :], 0), axis=1) + (codes & ((1 << POS_BITS) - 1))
        token_of_slot = jnp.zeros((t,), jnp.int32).at[slots].set(jnp.arange(t, dtype=jnp.int32))
        ys = _moe(items, token_of_slot, h2, exp_w1, exp_w3, exp_w2, l)
        x_in = (slots, ys, x1, modl, ln2_g[l], ln2_b[l])
    return _combine(*x_in, seq, alpha).reshape(bsz, seq, d)
```

```python
import functools

import numpy as np
import jax
import jax.numpy as jnp
from jax import lax
from jax.experimental import pallas as pl
from jax.experimental.pallas import tpu as pltpu

F32 = jnp.float32
BF16 = jnp.bfloat16

D_MODEL = 1024
N_MIXERS = 4
GW = D_MODEL // N_MIXERS
N_HEADS = 4
HEAD_DIM = GW // N_HEADS
CONV_WIDTH = 4
GLA_RANK = 16
GLA_TAU = 16.0
ROPE_BASE = 10000.0
N_GROUPS = 4
EXPERTS_PER_GROUP = 4
N_EXPERTS = 16
D_EXPERT = D_MODEL // 2
LN_EPS = 1e-5
NORM_EPS = 1e-6
QK_SCALE = HEAD_DIM ** -0.5

LANES = 128
CHUNK = 128
LEVELS = 7
CONV_PAD = 8
VMEM_LIMIT = 56 * 1024 * 1024

(R_Q, R_K, R_V, R_G, M_Q, M_K, M_V, M_O, G_Q, G_K, G_V, G_G, H_Q, H_F, H_I, H_G) = (
    GW * i for i in range(16))


def _bf(x):
    return x.astype(BF16)


def _dot(a, b):
    return jnp.dot(a, b, preferred_element_type=F32)


def _dot_nt(a, b):
    return lax.dot_general(a, b, (((1,), (1,)), ((), ())), preferred_element_type=F32)


def _split2(x):
    hi = _bf(x)
    lo = _bf(x - hi.astype(F32))
    return hi, lo


def _dot_f32(a, b):
    ah, al = _split2(a)
    bh, bl = _split2(b)
    return _dot(ah, bh) + _dot(ah, bl) + _dot(al, bh)


def _sigmoid(x):
    return 0.5 * jnp.tanh(0.5 * x) + 0.5


def _silu(x):
    h = 0.5 * x
    return h + h * jnp.tanh(h)


def _log_sigmoid(x):
    return jnp.minimum(x, 0.0) - jnp.log1p(jnp.exp(-jnp.abs(x)))


def _layer_norm(v, g, b):
    mu = jnp.mean(v, axis=-1, keepdims=True)
    d = v - mu
    var = jnp.mean(d * d, axis=-1, keepdims=True)
    return d * lax.rsqrt(var + LN_EPS) * g + b


def _adaln_kernel(c_ref, w_ref, b_ref, o_ref):
    cond = _silu(c_ref[...])
    o_ref[0] = _dot_f32(cond, w_ref[0]) + b_ref[0]


def _adaln(c, ada_w, ada_b):
    depth, d, n = ada_w.shape
    bsz = c.shape[0]
    tn = 1536
    return pl.pallas_call(
        _adaln_kernel,
        grid=(depth, n // tn),
        in_specs=[
            pl.BlockSpec((bsz, d), lambda l, j: (0, 0)),
            pl.BlockSpec((1, d, tn), lambda l, j: (l, 0, j)),
            pl.BlockSpec((1, 1, tn), lambda l, j: (l, 0, j)),
        ],
        out_specs=pl.BlockSpec((1, bsz, tn), lambda l, j: (l, 0, j)),
        out_shape=jax.ShapeDtypeStruct((depth, bsz, n), F32),
        compiler_params=pltpu.CompilerParams(
            dimension_semantics=("arbitrary", "arbitrary"), vmem_limit_bytes=VMEM_LIMIT),
        name="adaln",
    )(c, ada_w, ada_b.reshape(depth, 1, n))


def _mixer_consts():
    c = CHUNK
    i = np.arange(c)[:, None]
    j = np.arange(c)[None, :]
    rel = i - j
    log_gamma = np.log1p(-np.exp2(-5.0 - np.arange(N_HEADS, dtype=np.float64)))
    dret = np.concatenate(
        [np.where(rel >= 0, np.exp(log_gamma[h] * np.maximum(rel, 0)), 0.0) for h in range(N_HEADS)], axis=1)
    lane_head = np.arange(GW) // HEAD_DIM
    dq = np.exp(log_gamma[lane_head][None, :] * (np.arange(c)[:, None] + 1.0))
    dk = np.exp(log_gamma[lane_head][None, :] * (c - 1.0 - np.arange(c)[:, None]))
    gc = np.exp(log_gamma[lane_head] * c)[None, :]
    lv = np.zeros((c, c), np.int32)
    lv[rel == 0] = 1
    spans = [(rel >= 0), (rel < 0)]
    code, b = 2, 1
    while b < c:
        lv[(i // (2 * b) == j // (2 * b)) & ((i // b) % 2 == 1) & ((j // b) % 2 == 0)] = code
        ref = (i // (2 * b)) * 2 * b + b - 1
        right = (i // b) % 2 == 1
        spans.append(np.where(right, (j > ref) & (j <= i), (j > i) & (j <= ref)))
        code, b = code + 1, b * 2
    lv = np.tile(lv, (1, N_HEADS))
    span = np.concatenate(spans, axis=0).astype(np.float32)
    span = np.tile(span, (1, 2))
    hh = (lane_head[:, None] == lane_head[None, :]).astype(np.float32)
    ones_bd = np.repeat(np.eye(N_HEADS, dtype=np.float32), c, axis=0)
    ones_bd = np.repeat(ones_bd, HEAD_DIM, axis=1)
    return dict(
        dret=jnp.asarray(dret, F32), dq=jnp.asarray(dq, F32), dk=jnp.asarray(dk, F32),
        gc=jnp.asarray(gc, F32), lv=jnp.asarray(lv), hh=jnp.asarray(hh, BF16),
        bdf=jnp.asarray(hh, F32), ones_bd=jnp.asarray(ones_bd, BF16), span=jnp.asarray(span, BF16))


def _lane_cumsum(x):
    lane = lax.broadcasted_iota(jnp.int32, x.shape, 1)
    s = 1
    while s < x.shape[1]:
        x = x + jnp.where(lane >= s, pltpu.roll(x, s, 1), 0.0)
        s *= 2
    return x


def _lane_cummax(x):
    lane = lax.broadcasted_iota(jnp.int32, x.shape, 1)
    s = 1
    while s < x.shape[1]:
        x = jnp.maximum(x, jnp.where(lane >= s, pltpu.roll(x, s, 1), -jnp.inf))
        s *= 2
    return x


N_MIXER_PARAMS = 25


def _mixer_kernel(layer, nb, alpha, seq_len, fused_in, *refs):
    if fused_in:
        slots_ref, x1p_ref, ys_hbm, modp_ref, lng2_ref, lnb2_ref = refs[:6]
        refs = refs[6:]
    else:
        x_ref = refs[0]
        refs = refs[1:]
    (mod_ref, wa_ref, wg_ref, wh_ref, wx_ref, cos_ref, sin_ref, conv_ref, gb_ref, w2_ref, b2_ref, lb_ref,
     retg_ref, mlg_ref, glag_ref, hgg_ref,
     dret_ref, dq_ref, dk_ref, gc_ref, lv_ref, hh_ref, bdf_ref, onesbd_ref, span_ref) = refs[:N_MIXER_PARAMS]
    refs = refs[N_MIXER_PARAMS:]
    if fused_in:
        y_ref, xout_ref, s_ret, s_cn, s_gla, s_hg, m_ml, conv_buf, gbuf, gsem = refs
    else:
        y_ref, s_ret, s_cn, s_gla, s_hg, m_ml, conv_buf = refs

    c = CHUNK
    batches = range(nb)
    step = pl.program_id(1)
    n_steps = pl.num_programs(1)

    def gather(chunk, slot):
        for b in batches:
            for i in range(c):
                s = slots_ref[b * seq_len + chunk * c + i]
                pltpu.make_async_copy(ys_hbm.at[pl.ds(s, 1)], gbuf.at[slot, pl.ds(b * c + i, 1)],
                                      gsem.at[slot]).start(priority=i % 2)

    def gather_wait(slot):
        pltpu.make_async_copy(ys_hbm.at[pl.ds(0, nb * c)], gbuf.at[slot], gsem.at[slot]).wait()

    @pl.when(step == 0)
    def _():
        for s in (s_ret, s_cn, s_gla, s_hg, m_ml):
            s[...] = jnp.zeros_like(s)
        conv_buf[:, 0:CONV_PAD, :] = jnp.zeros((nb, CONV_PAD, 2 * GW), F32)
        if fused_in:
            gather(0, 0)

    lane_head = lax.broadcasted_iota(jnp.int32, (1, GW), 1) // HEAD_DIM
    head_rows = [(lane_head == h).astype(BF16) for h in range(N_HEADS)]
    hh = hh_ref[...]
    bdf = bdf_ref[...]
    lv = lv_ref[...]
    causal = lv >= 1

    def stack(xs):
        return jnp.concatenate(xs, axis=0)

    def tiled(x):
        return stack([x] * nb)

    def seq(x, b):
        return x[b * c:(b + 1) * c]

    if fused_in:
        cur = step % 2
        gather_wait(cur)
        ffn = gbuf[cur]
        xs = _layer_norm(stack([alpha * x1p_ref[b] + modp_ref[b, 5:6, :] * seq(ffn, b) for b in batches]),
                         lng2_ref[...], lnb2_ref[...])
        x_rows = [seq(xs, b) for b in batches]
        for b in batches:
            xout_ref[b] = x_rows[b]
    else:
        x_rows = [x_ref[b] for b in batches]
    hb = _bf(stack([x_rows[b] * (1.0 + mod_ref[b, 1:2, :]) + mod_ref[b, 0:1, :] for b in batches]))

    issued = {}

    def issue(first):
        block = N_MIXERS * GW
        ref, start = {R_Q: (wa_ref, 0), M_Q: (wa_ref, block), G_Q: (wg_ref, 0), H_Q: (wh_ref, 0)}[first]
        issued[first] = _dot(hb, ref[:, start:start + block])

    def col(off, width=GW):
        first = off // (N_MIXERS * GW) * (N_MIXERS * GW)
        return issued[first][:, off - first:off - first + width]

    issue(R_Q)
    issue(M_Q)
    ext = _dot(hb, wx_ref[...])

    def put(off, val):
        for b in batches:
            y_ref[b, :, off:off + GW] = seq(val, b).astype(y_ref.dtype)

    def stackmask(xb):
        return jnp.concatenate([xb * head_rows[h] for h in range(N_HEADS)], axis=0)

    def expand_heads(cols, first):
        out = jnp.zeros((cols.shape[0], GW), F32)
        for h in range(N_HEADS):
            out = jnp.where(lane_head == h, cols[:, first + h:first + h + 1], out)
        return out

    def head_mean(v):
        return _dot(_bf(v), hh) * (1.0 / HEAD_DIM)

    def head_norm(v, gain, center):
        if center:
            v = v - head_mean(v)
        return v * lax.rsqrt(head_mean(v * v) + NORM_EPS) * gain

    cosf = tiled(cos_ref[...])
    sins = tiled(sin_ref[...])
    half = lax.broadcasted_iota(jnp.int32, (1, GW), 1) % HEAD_DIM < HEAD_DIM // 2

    def rotary(v):
        swapped = jnp.where(half, pltpu.roll(v, GW - HEAD_DIM // 2, 1), pltpu.roll(v, HEAD_DIM // 2, 1))
        return v * cosf + swapped * sins

    rq = _bf(rotary(col(R_Q)) * QK_SCALE)
    rk = rotary(col(R_K))
    rkb = _bf(rk)
    rvb = _bf(col(R_V))
    rkd = rk * tiled(dk_ref[...])
    dret = dret_ref[...]
    dq = dq_ref[...]
    gc = gc_ref[...]
    sc = [_dot_nt(seq(rq, b), stackmask(seq(rkb, b))) for b in batches]
    inter = [_dot(seq(rq, b), _bf(s_ret[b])) for b in batches]
    scb = [_bf(s * dret) for s in sc]
    ro = stack([_dot(scb[b], stackmask(seq(rvb, b))) + inter[b] * dq for b in batches])
    upd = [_dot(_bf(seq(rkd, b).T), seq(rvb, b)) for b in batches]
    for b in batches:
        s_ret[b] = s_ret[b] * gc + upd[b] * bdf
    put(0, head_norm(ro, retg_ref[...], True) * _silu(col(R_G)))

    if fused_in:
        gather(jnp.minimum(step + 1, n_steps - 1), (step + 1) % 2)

    issue(G_Q)
    mqk = col(M_Q, 2 * GW)
    for b in batches:
        conv_buf[b, CONV_PAD:CONV_PAD + c, :] = seq(mqk, b)
    qk = jnp.zeros((nb * c, 2 * GW), F32)
    for t in range(CONV_WIDTH):
        s = CONV_WIDTH - 1 - t
        qk = qk + stack([conv_buf[b, CONV_PAD - s:CONV_PAD - s + c, :] for b in batches]) * conv_ref[t:t + 1, :]
    for b in batches:
        conv_buf[b, 0:CONV_PAD, :] = conv_buf[b, c:c + CONV_PAD, :]
    qk = _silu(qk)
    mq = _bf(qk[:, 0:GW])
    mk = qk[:, GW:2 * GW] * QK_SCALE
    mkb = _bf(mk)
    mvb = _bf(col(M_V))

    pre = stack([seq(ext, b).T[0:8, :] for b in batches]) + tiled(gb_ref[...])
    row8 = lax.broadcasted_iota(jnp.int32, (8 * nb, c), 0)
    valid = row8 % 8 >= N_HEADS
    lf = jnp.where(valid, _log_sigmoid(pre), 0.0)
    gi = jnp.where(valid, pltpu.roll(pre, N_HEADS, 0), 0.0)
    bcum = _lane_cumsum(lf)
    a = gi - bcum
    m_prev = m_ml[...]
    mrow = jnp.maximum(m_prev, _lane_cummax(a))
    m_last = jnp.broadcast_to(mrow[:, c - 1:c], (8 * nb, c))
    b_last = jnp.broadcast_to(bcum[:, c - 1:c], (8 * nb, c))
    wi = jnp.exp(m_prev - mrow)
    em = jnp.exp(-(bcum + mrow))
    wk = jnp.exp(a - m_last)
    dec = jnp.exp(m_prev - m_last)
    m_ml[...] = jnp.where(valid, b_last + m_last, 0.0)
    kinds = [mrow, wi, em, wk]
    if 32 * nb < c:
        kinds.append(jnp.zeros((c - 32 * nb, c), F32))
    cols = stack(kinds).T

    def first_col(kind, b):
        return kind * 8 * nb + 8 * b + N_HEADS

    mexp = [jnp.concatenate([jnp.broadcast_to(cols[:, first_col(0, b) + h:first_col(0, b) + h + 1], (c, c))
                             for h in range(N_HEADS)], axis=1) for b in batches]
    a_row = [jnp.concatenate([a[8 * b + N_HEADS + h:8 * b + N_HEADS + h + 1, :] for h in range(N_HEADS)], axis=1)
             for b in batches]
    qkm = [_dot_nt(seq(mq, b), stackmask(seq(mkb, b))) for b in batches]
    inter = [_dot(seq(mq, b), _bf(s_cn[b])) for b in batches]
    sm = [_bf(qkm[b] * jnp.exp(jnp.where(causal, a_row[b] - mexp[b], -jnp.inf))) for b in batches]
    onesbd = onesbd_ref[...]
    wi_l = stack([expand_heads(cols, first_col(1, b)) for b in batches])
    em_l = stack([expand_heads(cols, first_col(2, b)) for b in batches])
    wk_l = stack([expand_heads(cols, first_col(3, b)) for b in batches])
    nd = (stack([_dot(sm[b], jnp.concatenate([stackmask(seq(mvb, b)), onesbd], axis=1)) for b in batches])
          + jnp.concatenate([wi_l, wi_l], axis=1) * stack(inter))
    mh = nd[:, 0:GW] / jnp.maximum(jnp.abs(nd[:, GW:2 * GW]), em_l)
    kt = mk * wk_l
    ones = jnp.ones((c, GW), BF16)
    upd = [_dot(_bf(seq(kt, b).T), jnp.concatenate([seq(mvb, b), ones], axis=1)) for b in batches]
    dec2 = jnp.concatenate([dec, dec], axis=1)
    bdf2 = jnp.concatenate([bdf, bdf], axis=1)
    for b in batches:
        dec_l = jnp.zeros((1, GW), F32)
        for h in range(N_HEADS):
            dec_l = jnp.where(lane_head == h, dec2[8 * b + N_HEADS + h:8 * b + N_HEADS + h + 1, :], dec_l)
        s_cn[b] = s_cn[b] * jnp.concatenate([dec_l, dec_l], axis=1) + upd[b] * bdf2
    put(GW, head_norm(mh * _sigmoid(col(M_O)), mlg_ref[...], True))

    span = span_ref[...]
    level_masks = [lv == code for code in range(2, 2 + LEVELS)]

    def decay_attention(q, k, v, g, st_ref):
        g2 = _split2(g)
        dec = [jnp.exp(_dot(span, stack([seq(part, b) for part in g2]))) for b in batches]
        e_cum = stack([d[0:c] for d in dec])
        e_end = stack([d[c:2 * c] for d in dec])
        qh = _bf(q)
        kh = _bf(k)
        qe = qh * _bf(e_cum)
        out = stack([_dot_nt(seq(qe, b), _bf(st_ref[b])) for b in batches])
        out = out + _dot(_bf(q * k), hh) * v
        scores = [jnp.zeros((c, N_HEADS * c), F32) for _ in batches]
        for lvl in range(LEVELS):
            e = _bf(stack([d[(2 + lvl) * c:(3 + lvl) * c] for d in dec]))
            qb = qh * e
            kb = kh * e
            level = [_dot_nt(seq(qb, b), stackmask(seq(kb, b))) for b in batches]
            scores = [jnp.where(level_masks[lvl], level[b], scores[b]) for b in batches]
        vb = _bf(v)
        out = out + stack([_dot(_bf(scores[b]), stackmask(seq(vb, b))) for b in batches])
        ke = kh * _bf(e_end)
        upd = [_dot(_bf(seq(v, b).T), seq(ke, b)) for b in batches]
        for b in batches:
            st_ref[b] = st_ref[b] * seq(e_cum, b)[c - 1:c, :] + upd[b] * bdf
        return out

    issue(H_Q)
    x_lr = _dot_f32(ext, w2_ref[...]) + b2_ref[...]
    log_a = _log_sigmoid(x_lr) * (1.0 / GLA_TAU)
    go = decay_attention(col(G_Q), col(G_K) * QK_SCALE, col(G_V), log_a, s_gla)
    put(2 * GW, head_norm(go, glag_ref[...], False) * _silu(col(G_G)))

    lb_all = lb_ref[...]
    lb_e = jnp.exp(lb_all - jnp.max(lb_all, axis=0, keepdims=True))
    lb_p = lb_e / jnp.sum(lb_e, axis=0, keepdims=True)
    lb = jnp.zeros((1, GW), F32)
    for l in range(1, layer + 1):
        lb = lb + lb_p[l:l + 1, :]
    z = col(H_F)
    x1 = jnp.log(lb)
    x2 = jnp.log1p(-lb) + _log_sigmoid(z)
    log_f = jnp.maximum(x1, x2) + jnp.log1p(jnp.exp(-jnp.abs(x1 - x2)))
    k_h = (1.0 - lb) * _sigmoid(-z)
    ho = decay_attention(_silu(col(H_Q)), k_h, col(H_I), log_f, s_hg)
    put(3 * GW, head_norm(ho * _sigmoid(col(H_G)), hgg_ref[...], False))

    if fused_in:
        @pl.when(step == n_steps - 1)
        def _():
            gather_wait((step + 1) % 2)


MIXER_BATCH = 4


def _mixer(x_in, modl, w_in_parts, cosf, sins, conv_w, gate_rows, w2_pad, b2, lb, ret_g, ml_g, gla_g, hg_g, consts,
           layer, bsz, seq, alpha):
    c = CHUNK
    nc = seq // c
    nb = MIXER_BATCH
    assert bsz == nb
    d = D_MODEL
    t = bsz * seq
    fused_in = isinstance(x_in, tuple)
    row = lambda a: a.reshape(1, -1)
    full = lambda a: pl.BlockSpec(a.shape, lambda b, n, *_: (0,) * a.ndim)
    tile = lambda width: pl.BlockSpec((nb, c, width), lambda b, n, *_: (b, n, 0))
    mod_spec = pl.BlockSpec((nb, 6, d), lambda b, n, *_: (b, 0, 0))
    cs = consts
    params = [modl, *w_in_parts, cosf, sins, conv_w, gate_rows, w2_pad, row(b2), lb,
              row(ret_g), row(ml_g), row(gla_g), row(hg_g),
              cs["dret"], cs["dq"], cs["dk"], cs["gc"], cs["lv"], cs["hh"], cs["bdf"], cs["ones_bd"], cs["span"]]
    assert len(params) == N_MIXER_PARAMS
    n_w = len(w_in_parts)
    param_specs = [mod_spec] + [full(a) for a in w_in_parts] + [
        pl.BlockSpec((c, GW), lambda b, n, *_: (n, 0)),
        pl.BlockSpec((c, GW), lambda b, n, *_: (n, 0)),
    ] + [full(a) for a in params[3 + n_w:]]
    scratch = [
        pltpu.VMEM((nb, GW, GW), F32),
        pltpu.VMEM((nb, GW, 2 * GW), F32),
        pltpu.VMEM((nb, GW, GW), F32),
        pltpu.VMEM((nb, GW, GW), F32),
        pltpu.VMEM((nb * 8, c), F32),
        pltpu.VMEM((nb, c + CONV_PAD, 2 * GW), F32),
    ]
    y_shape = jax.ShapeDtypeStruct((bsz, seq, d), BF16)
    if fused_in:
        slots, ys, x1, modp, ln2_g, ln2_b = x_in
        args = [slots, x1.reshape(bsz, seq, d), ys, modp, row(ln2_g), row(ln2_b)] + params
        in_specs = [tile(d), pl.BlockSpec(memory_space=pl.ANY), mod_spec, full(row(ln2_g)), full(row(ln2_b))]
        out_specs = [tile(d), tile(d)]
        out_shape = [y_shape, jax.ShapeDtypeStruct((bsz, seq, d), F32)]
        scratch = scratch + [pltpu.VMEM((2, nb * c, d), F32), pltpu.SemaphoreType.DMA((2,))]
        n_prefetch = 1
    else:
        args = [x_in.reshape(bsz, seq, d)] + params
        in_specs = [tile(d)]
        out_specs = tile(d)
        out_shape = y_shape
        n_prefetch = 0
    out = pl.pallas_call(
        functools.partial(_mixer_kernel, layer, nb, alpha, seq, fused_in),
        grid_spec=pltpu.PrefetchScalarGridSpec(
            num_scalar_prefetch=n_prefetch,
            grid=(bsz // nb, nc),
            in_specs=in_specs + param_specs,
            out_specs=out_specs,
            scratch_shapes=scratch,
        ),
        out_shape=out_shape,
        compiler_params=pltpu.CompilerParams(
            dimension_semantics=("arbitrary", "arbitrary"), vmem_limit_bytes=VMEM_LIMIT),
        name="mixer",
    )(*args)
    if fused_in:
        return out[0].reshape(t, d), out[1].reshape(t, d)
    return out.reshape(t, d), x_in


def _group_shift(v, pos, k):
    return jnp.where(pos < EXPERTS_PER_GROUP - k,
                     pltpu.roll(v, LANES - k, 1), pltpu.roll(v, EXPERTS_PER_GROUP - k, 1))


def _router(h, rw, rb):
    lane = lax.broadcasted_iota(jnp.int32, (1, LANES), 1)
    real = lane < N_EXPERTS
    logits = jnp.where(real, _dot_f32(h, rw), -jnp.inf)
    ex = jnp.exp(logits - jnp.max(logits, axis=-1, keepdims=True))
    probs = ex / jnp.sum(ex, axis=-1, keepdims=True)
    sel = jnp.where(real, probs + rb, -jnp.inf)
    pos = lane % EXPERTS_PER_GROUP
    gid = lane // EXPERTS_PER_GROUP
    r1 = _group_shift(sel, pos, 1)
    r2 = _group_shift(sel, pos, 2)
    r3 = _group_shift(sel, pos, 3)
    pair = jnp.maximum(jnp.maximum(jnp.maximum(sel + r1, sel + r2), jnp.maximum(sel + r3, r1 + r2)),
                       jnp.maximum(r1 + r3, r2 + r3))
    pair = jnp.where(real, pair, -jnp.inf)
    best = jnp.max(pair, axis=-1, keepdims=True)
    first = jnp.min(jnp.where(pair == best, gid, N_GROUPS), axis=-1, keepdims=True)
    rank = jnp.zeros(sel.shape, jnp.int32)
    for k, r in ((1, r1), (2, r2), (3, r3)):
        beats = (r > sel) | ((r == sel) & (pos + k >= EXPERTS_PER_GROUP))
        rank = rank + beats.astype(jnp.int32)
    return probs, first, rank


def _post_kernel(alpha, y_ref, x_ref, mod_ref, wout_ref, lng_ref, lnb_ref, rw_ref, rb_ref,
                 stri_ref, x1_ref, h2_ref, slots_ref, counts_ref, carry_ref):
    tm = y_ref.shape[0]

    @pl.when(pl.program_id(0) == 0)
    def _():
        carry_ref[...] = jnp.zeros_like(carry_ref)

    g1 = mod_ref[0, 2:3, :]
    sh2 = mod_ref[0, 3:4, :]
    sc2 = mod_ref[0, 4:5, :]
    mix = _dot(_bf(y_ref[...]), wout_ref[...])
    x1 = _layer_norm(alpha * x_ref[...] + g1 * mix, lng_ref[...], lnb_ref[...])
    x1_ref[...] = x1
    h = x1 * (1.0 + sc2) + sh2
    d = h.shape[1]
    h2_ref[:, 0:d] = h

    probs, first, rank = _router(h, rw_ref[...], rb_ref[...])
    lane = lax.broadcasted_iota(jnp.int32, (1, LANES), 1)
    chosen = (lane // EXPERTS_PER_GROUP == first) & (rank < 2)
    w = jnp.where(chosen, probs, 0.0)
    h2_ref[:, d:d + LANES] = w / jnp.sum(w, axis=-1, keepdims=True)

    bit = jnp.left_shift(1, lane % EXPERTS_PER_GROUP).astype(F32)
    bits = jnp.sum(jnp.where(chosen, bit, 0.0), axis=-1, keepdims=True)
    pair = jnp.full(bits.shape, len(PAIR_BITS) - 1, jnp.int32)
    for index in range(len(PAIR_BITS) - 2, -1, -1):
        pair = jnp.where(bits == float(PAIR_BITS[index]), index, pair)
    cls = first * len(PAIR_BITS) + pair
    onehot = jnp.where(lane == cls, 1.0, 0.0)
    before = _dot(stri_ref[...], _bf(onehot)) + carry_ref[...]
    code = cls.astype(F32) * float(1 << POS_BITS) + jnp.sum(onehot * before, axis=-1, keepdims=True)
    slots_ref[0] = jnp.broadcast_to(code, (tm, LANES)).T[0:1, :].astype(jnp.int32)
    carry_ref[...] += jnp.sum(onehot, axis=0, keepdims=True)
    counts_ref[...] = carry_ref[...].astype(jnp.int32)


def _post(y, xt, modl, w_out, ln_g, ln_b, rw_pad, rb_pad, seq, alpha):
    t, d = xt.shape
    tm = 256
    per_batch = seq // tm
    row = lambda a: a.reshape(1, -1)
    tile = pl.BlockSpec((tm, d), lambda i: (i, 0))
    full = lambda a: pl.BlockSpec(a.shape, lambda i: (0,) * a.ndim)
    stri = jnp.asarray(np.tril(np.ones((tm, tm), np.float32), -1), BF16)
    args = [y, xt, modl, w_out, row(ln_g), row(ln_b), rw_pad, rb_pad, stri]
    return pl.pallas_call(
        functools.partial(_post_kernel, alpha),
        grid=(t // tm,),
        in_specs=[tile, tile, pl.BlockSpec((1, 6, d), lambda i: (i // per_batch, 0, 0))]
        + [full(a) for a in args[3:]],
        out_specs=[tile, pl.BlockSpec((tm, d + LANES), lambda i: (i, 0)),
                   pl.BlockSpec((1, 1, tm), lambda i: (i, 0, 0)),
                   pl.BlockSpec((1, LANES), lambda i: (0, 0))],
        out_shape=[jax.ShapeDtypeStruct((t, d), F32), jax.ShapeDtypeStruct((t, d + LANES), F32),
                   jax.ShapeDtypeStruct((t // tm, 1, tm), jnp.int32),
                   jax.ShapeDtypeStruct((1, LANES), jnp.int32)],
        scratch_shapes=[pltpu.VMEM((1, LANES), F32)],
        compiler_params=pltpu.CompilerParams(
            dimension_semantics=("arbitrary",), vmem_limit_bytes=VMEM_LIMIT),
        name="post",
    )(*args)


MOE_TILE = 512
POS_BITS = 15
PAIR_BITS = (0b0011, 0b0101, 0b1001, 0b1010, 0b0110, 0b1100)
N_CLASSES = N_GROUPS * len(PAIR_BITS)


def _moe_kernel(n_tiles, item_tile_ref, item_expert_ref, item_first_ref, n_items_ref, token_ref,
                h_hbm, w1_ref, w3_ref, w2_ref, ys_ref, rows_buf, sem, hb_ref):
    s = pl.program_id(0)
    d = ys_ref.shape[1]
    tile = item_tile_ref[s]
    cur = tile % 2
    lane = lax.broadcasted_iota(jnp.int32, (1, LANES), 1)

    def gather(t, slot):
        for row in range(MOE_TILE):
            token = token_ref[t * MOE_TILE + row]
            pltpu.make_async_copy(h_hbm.at[pl.ds(token, 1)], rows_buf.at[slot, pl.ds(row, 1)],
                                  sem.at[slot]).start(priority=row % 2)

    def gather_wait(slot):
        pltpu.make_async_copy(h_hbm.at[pl.ds(0, MOE_TILE)], rows_buf.at[slot], sem.at[slot]).wait()

    def expert(hb, between=lambda: None):
        gates = rows_buf[cur, :, d:d + LANES]
        ge = jnp.sum(jnp.where(lane == item_expert_ref[s], gates, 0.0), axis=-1, keepdims=True)
        up = _dot(hb, _bf(w1_ref[0, 0]))
        between()
        w3 = _bf(w3_ref[0, 0])
        lin = _dot(hb, w3)
        w2 = _bf(w2_ref[0, 0])
        return _dot(_bf(_silu(up) * lin * ge), w2)

    @pl.when(s == 0)
    def _():
        gather(0, 0)

    valid = s < n_items_ref[0]
    first_of_tile = item_first_ref[s] == 1

    @pl.when(valid & first_of_tile)
    def _():
        gather_wait(cur)
        hb = _bf(rows_buf[cur, :, 0:d])
        hb_ref[...] = hb
        ys_ref[...] = expert(hb, lambda: gather(jnp.minimum(tile + 1, n_tiles - 1), 1 - cur))

    @pl.when(valid & jnp.logical_not(first_of_tile))
    def _():
        ys_ref[...] += expert(hb_ref[...])

    @pl.when(s == n_items_ref[0] - 1)
    def _():
        gather_wait(1 - cur)


def _moe(items, token_of_slot, h2, w1, w3, w2, layer):
    item_tile, item_expert, item_first, n_items = items
    rows, width = h2.shape
    d = w1.shape[2]
    up = pl.BlockSpec((1, 1, d, D_EXPERT), lambda s, it, ie, fi, n, tk: (layer, ie[s], 0, 0))
    down = pl.BlockSpec((1, 1, D_EXPERT, d), lambda s, it, ie, fi, n, tk: (layer, ie[s], 0, 0))
    return pl.pallas_call(
        functools.partial(_moe_kernel, rows // MOE_TILE),
        grid_spec=pltpu.PrefetchScalarGridSpec(
            num_scalar_prefetch=5,
            grid=(item_tile.shape[0],),
            in_specs=[pl.BlockSpec(memory_space=pl.ANY), up, up, down],
            out_specs=pl.BlockSpec((MOE_TILE, d), lambda s, it, ie, fi, n, tk: (it[s], 0)),
            scratch_shapes=[pltpu.VMEM((2, MOE_TILE, width), F32), pltpu.SemaphoreType.DMA((2,)),
                            pltpu.VMEM((MOE_TILE, d), BF16)],
        ),
        out_shape=jax.ShapeDtypeStruct((rows, d), F32),
        compiler_params=pltpu.CompilerParams(
            dimension_semantics=("arbitrary",), vmem_limit_bytes=VMEM_LIMIT),
        name="moe",
    )(item_tile, item_expert, item_first, n_items, token_of_slot, h2, w1, w3, w2)


def _invert_kernel(slots_ref, token_ref):
    def body(i, carry):
        for r in range(8):
            t = i * 8 + r
            token_ref[slots_ref[t]] = t
        return carry

    lax.fori_loop(0, slots_ref.shape[0] // 8, body, 0)


def _invert(slots):
    return pl.pallas_call(
        _invert_kernel,
        in_specs=[pl.BlockSpec(memory_space=pltpu.SMEM)],
        out_specs=pl.BlockSpec(memory_space=pltpu.SMEM),
        out_shape=jax.ShapeDtypeStruct(slots.shape, jnp.int32),
        name="invert",
    )(slots)


def _class_experts():
    member = np.zeros((N_CLASSES, N_EXPERTS), np.int32)
    for g in range(N_GROUPS):
        for p, bits in enumerate(PAIR_BITS):
            for e in range(EXPERTS_PER_GROUP):
                if bits >> e & 1:
                    member[g * len(PAIR_BITS) + p, g * EXPERTS_PER_GROUP + e] = 1
    return member


def _work_items(counts, n_tokens):
    n_tiles = n_tokens // MOE_TILE
    n_items_max = 2 * (n_tiles + 2 * N_CLASSES)
    ends = jnp.cumsum(counts)
    starts = ends - counts
    lo = jnp.arange(n_tiles, dtype=jnp.int32)[:, None] * MOE_TILE
    overlap = (counts[None, :] > 0) & (starts[None, :] < lo + MOE_TILE) & (ends[None, :] > lo)
    needed = (overlap.astype(jnp.int32) @ jnp.asarray(_class_experts())) > 0
    flip = (jnp.arange(n_tiles, dtype=jnp.int32) % 2 == 1)[:, None]
    needed = jnp.where(flip, needed[:, ::-1], needed)
    flat = needed.reshape(-1)
    n_items = jnp.sum(flat.astype(jnp.int32))
    place = jnp.where(flat, jnp.cumsum(flat.astype(jnp.int32)) - 1, n_items_max)
    ids = jnp.zeros((n_items_max,), jnp.int32).at[place].set(
        jnp.arange(flat.shape[0], dtype=jnp.int32), mode="drop")
    ids = ids[jnp.minimum(jnp.arange(n_items_max), n_items - 1)]
    item_tile = ids // N_EXPERTS
    item_expert = jnp.where(item_tile % 2 == 1, N_EXPERTS - 1 - ids % N_EXPERTS, ids % N_EXPERTS)
    item_first = jnp.concatenate([jnp.ones((1,), jnp.int32),
                                  (item_tile[1:] != item_tile[:-1]).astype(jnp.int32)])
    return starts.astype(jnp.int32), (item_tile, item_expert, item_first, n_items.reshape(1))


def _combine_kernel(alpha, slots_ref, ys_hbm, x1_ref, mod_ref, lng_ref, lnb_ref, o_ref, buf, sem):
    tm = o_ref.shape[0]
    i = pl.program_id(0)
    n = pl.num_programs(0)

    def gather(tile, b):
        for row in range(tm):
            s = slots_ref[tile * tm + row]
            pltpu.make_async_copy(ys_hbm.at[pl.ds(s, 1)], buf.at[b, pl.ds(row, 1)], sem.at[b]).start(
                priority=row % 2)

    def wait(b):
        pltpu.make_async_copy(ys_hbm.at[pl.ds(0, tm)], buf.at[b], sem.at[b]).wait()

    @pl.when(i == 0)
    def _():
        gather(0, 0)

    cur = i % 2
    nxt = (i + 1) % 2
    wait(cur)
    gather(jnp.minimum(i + 1, n - 1), nxt)
    g2 = mod_ref[0, 5:6, :]
    o_ref[...] = _layer_norm(alpha * x1_ref[...] + g2 * buf[cur], lng_ref[...], lnb_ref[...])

    @pl.when(i == n - 1)
    def _():
        wait(nxt)


def _combine(slots, ys, x1, modl, ln_g, ln_b, seq, alpha):
    t, d = x1.shape
    tm = 256
    per_batch = seq // tm
    row = lambda a: a.reshape(1, -1)
    tile = pl.BlockSpec((tm, d), lambda i, s: (i, 0))
    vec = pl.BlockSpec((1, d), lambda i, s: (0, 0))
    return pl.pallas_call(
        functools.partial(_combine_kernel, alpha),
        grid_spec=pltpu.PrefetchScalarGridSpec(
            num_scalar_prefetch=1,
            grid=(t // tm,),
            in_specs=[pl.BlockSpec(memory_space=pl.ANY), tile,
                      pl.BlockSpec((1, 6, d), lambda i, s: (i // per_batch, 0, 0)), vec, vec],
            out_specs=tile,
            scratch_shapes=[pltpu.VMEM((2, tm, d), F32), pltpu.SemaphoreType.DMA((2,))],
        ),
        out_shape=jax.ShapeDtypeStruct((t, d), F32),
        compiler_params=pltpu.CompilerParams(
            dimension_semantics=("arbitrary",), vmem_limit_bytes=VMEM_LIMIT),
        name="combine",
    )(slots, ys, x1, modl, row(ln_g), row(ln_b))


def _split_w_in(w):
    a = 8 * GW
    b = a + 2 * N_HEADS
    cc = b + 4 * GW
    dd = cc + GLA_RANK
    pad = jnp.zeros((w.shape[0], LANES - 2 * N_HEADS - GLA_RANK), w.dtype)
    small = jnp.concatenate([w[:, a:b], w[:, cc:dd], pad], axis=1)
    return _bf(w[:, :a]), _bf(w[:, b:cc]), _bf(w[:, dd:]), _bf(small)


def _rotary_tables(seq):
    inv = ROPE_BASE ** (-jnp.arange(0, HEAD_DIM, 2, dtype=F32) / HEAD_DIM)
    ang = jnp.arange(seq, dtype=F32)[:, None] * inv[None, :]
    cos = jnp.cos(ang)
    sin = jnp.sin(ang)
    cosf = jnp.tile(jnp.concatenate([cos, cos], axis=1), (1, N_HEADS))
    sins = jnp.tile(jnp.concatenate([-sin, sin], axis=1), (1, N_HEADS))
    return cosf, sins


def kernel(x, c, ada_w, ada_b, w_in, mlstm_conv, mlstm_gate_b, gla_w2, gla_b2, hgrn_lb, ret_norm, mlstm_norm, gla_norm, hgrn_norm, w_out, ln1_g, ln1_b, router_w, router_b, exp_w1, exp_w3, exp_w2, ln2_g, ln2_b):
    bsz, seq, d = x.shape
    depth = ada_w.shape[0]
    assert d == D_MODEL and seq % 512 == 0 and bsz <= 16
    t = bsz * seq
    alpha = (2.0 * depth) ** 0.25
    assert t % MOE_TILE == 0 and t < (1 << POS_BITS) and bsz % MIXER_BATCH == 0
    consts = _mixer_consts()
    cosf, sins = _rotary_tables(seq)
    c_rows = jnp.pad(c, ((0, 16 - bsz), (0, 0)))
    mod = _adaln(c_rows, ada_w, ada_b)[:, :bsz]
    rw_pad = jnp.pad(router_w, ((0, 0), (0, LANES - N_EXPERTS)))
    rb_pad = jnp.pad(router_b, (0, LANES - N_EXPERTS)).reshape(1, LANES)
    x_in = x.reshape(t, d)
    for l in range(depth):
        modl = mod[l].reshape(bsz, 6, d)
        gate_rows = jnp.broadcast_to(mlstm_gate_b[l][:, None], (2 * N_HEADS, CHUNK))
        w2_pad = jnp.zeros((LANES, GW), F32).at[2 * N_HEADS:2 * N_HEADS + GLA_RANK].set(gla_w2[l])
        y, xt = _mixer(x_in, modl, _split_w_in(w_in[l]), cosf, sins, mlstm_conv[l], gate_rows, w2_pad,
                       gla_b2[l], hgrn_lb, ret_norm[l], mlstm_norm[l], gla_norm[l], hgrn_norm[l], consts,
                       l, bsz, seq, alpha)
        x1, h2, codes, counts = _post(y, xt, modl, _bf(w_out[l]), ln1_g[l], ln1_b[l], rw_pad, rb_pad,
                                      seq, alpha)
        class_start, items = _work_items(counts[0, :N_CLASSES], t)
        codes = codes.reshape(t)
        in_class = (codes >> POS_BITS)[:, None] == jnp.arange(N_CLASSES, dtype=jnp.int32)[None, :]
        slots = jnp.sum(jnp.where(in_class, class_start[None, :], 0), axis=1) + (codes & ((1 << POS_BITS) - 1))
        ys = _moe(items, _invert(slots), h2, exp_w1, exp_w3, exp_w2, l)
        x_in = (slots, ys, x1, modl, ln2_g[l], ln2_b[l])
    return _combine(*x_in, seq, alpha).reshape(bsz, seq, d)
```

```python
import functools

import numpy as np
import jax
import jax.numpy as jnp
from jax import lax
from jax.experimental import pallas as pl
from jax.experimental.pallas import tpu as pltpu

F32 = jnp.float32
BF16 = jnp.bfloat16

D_MODEL = 1024
N_MIXERS = 4
GW = D_MODEL // N_MIXERS
N_HEADS = 4
HEAD_DIM = GW // N_HEADS
CONV_WIDTH = 4
GLA_RANK = 16
GLA_TAU = 16.0
ROPE_BASE = 10000.0
N_GROUPS = 4
EXPERTS_PER_GROUP = 4
N_EXPERTS = 16
D_EXPERT = D_MODEL // 2
LN_EPS = 1e-5
NORM_EPS = 1e-6
QK_SCALE = HEAD_DIM ** -0.5

LANES = 128
CHUNK = 128
LEVELS = 7
CONV_PAD = 8
VMEM_LIMIT = 56 * 1024 * 1024

(R_Q, R_K, R_V, R_G, M_Q, M_K, M_V, M_O, G_Q, G_K, G_V, G_G, H_Q, H_F, H_I, H_G) = (
    GW * i for i in range(16))


def _bf(x):
    return x.astype(BF16)


def _dot(a, b):
    return jnp.dot(a, b, preferred_element_type=F32)


def _dot_nt(a, b):
    return lax.dot_general(a, b, (((1,), (1,)), ((), ())), preferred_element_type=F32)


def _split2(x):
    hi = _bf(x)
    lo = _bf(x - hi.astype(F32))
    return hi, lo


def _dot_f32(a, b):
    ah, al = _split2(a)
    bh, bl = _split2(b)
    return _dot(ah, bh) + _dot(ah, bl) + _dot(al, bh)


def _sigmoid(x):
    return 0.5 * jnp.tanh(0.5 * x) + 0.5


def _silu(x):
    h = 0.5 * x
    return h + h * jnp.tanh(h)


def _log_sigmoid(x):
    return jnp.minimum(x, 0.0) - jnp.log1p(jnp.exp(-jnp.abs(x)))


def _layer_norm(v, g, b):
    mu = jnp.mean(v, axis=-1, keepdims=True)
    d = v - mu
    var = jnp.mean(d * d, axis=-1, keepdims=True)
    return d * lax.rsqrt(var + LN_EPS) * g + b


def _adaln_kernel(c_ref, w_ref, b_ref, o_ref):
    cond = _silu(c_ref[...])
    o_ref[0] = _dot_f32(cond, w_ref[0]) + b_ref[0]


def _adaln(c, ada_w, ada_b):
    depth, d, n = ada_w.shape
    bsz = c.shape[0]
    tn = 1536
    return pl.pallas_call(
        _adaln_kernel,
        grid=(depth, n // tn),
        in_specs=[
            pl.BlockSpec((bsz, d), lambda l, j: (0, 0)),
            pl.BlockSpec((1, d, tn), lambda l, j: (l, 0, j)),
            pl.BlockSpec((1, 1, tn), lambda l, j: (l, 0, j)),
        ],
        out_specs=pl.BlockSpec((1, bsz, tn), lambda l, j: (l, 0, j)),
        out_shape=jax.ShapeDtypeStruct((depth, bsz, n), F32),
        compiler_params=pltpu.CompilerParams(
            dimension_semantics=("arbitrary", "arbitrary"), vmem_limit_bytes=VMEM_LIMIT),
        name="adaln",
    )(c, ada_w, ada_b.reshape(depth, 1, n))


def _mixer_consts():
    c = CHUNK
    i = np.arange(c)[:, None]
    j = np.arange(c)[None, :]
    rel = i - j
    log_gamma = np.log1p(-np.exp2(-5.0 - np.arange(N_HEADS, dtype=np.float64)))
    dret = np.concatenate(
        [np.where(rel >= 0, np.exp(log_gamma[h] * np.maximum(rel, 0)), 0.0) for h in range(N_HEADS)], axis=1)
    lane_head = np.arange(GW) // HEAD_DIM
    dq = np.exp(log_gamma[lane_head][None, :] * (np.arange(c)[:, None] + 1.0))
    dk = np.exp(log_gamma[lane_head][None, :] * (c - 1.0 - np.arange(c)[:, None]))
    gc = np.exp(log_gamma[lane_head] * c)[None, :]
    lv = np.zeros((c, c), np.int32)
    lv[rel == 0] = 1
    spans = [(rel >= 0), (rel < 0)]
    code, b = 2, 1
    while b < c:
        lv[(i // (2 * b) == j // (2 * b)) & ((i // b) % 2 == 1) & ((j // b) % 2 == 0)] = code
        ref = (i // (2 * b)) * 2 * b + b - 1
        right = (i // b) % 2 == 1
        spans.append(np.where(right, (j > ref) & (j <= i), (j > i) & (j <= ref)))
        code, b = code + 1, b * 2
    lv = np.tile(lv, (1, N_HEADS))
    span = np.concatenate(spans, axis=0).astype(np.float32)
    span = np.tile(span, (1, 2))
    hh = (lane_head[:, None] == lane_head[None, :]).astype(np.float32)
    ones_bd = np.repeat(np.eye(N_HEADS, dtype=np.float32), c, axis=0)
    ones_bd = np.repeat(ones_bd, HEAD_DIM, axis=1)
    return dict(
        dret=jnp.asarray(dret, F32), dq=jnp.asarray(dq, F32), dk=jnp.asarray(dk, F32),
        gc=jnp.asarray(gc, F32), lv=jnp.asarray(lv), hh=jnp.asarray(hh, BF16),
        bdf=jnp.asarray(hh, F32), ones_bd=jnp.asarray(ones_bd, BF16), span=jnp.asarray(span, BF16))


def _lane_cumsum(x):
    lane = lax.broadcasted_iota(jnp.int32, x.shape, 1)
    s = 1
    while s < x.shape[1]:
        x = x + jnp.where(lane >= s, pltpu.roll(x, s, 1), 0.0)
        s *= 2
    return x


def _lane_cummax(x):
    lane = lax.broadcasted_iota(jnp.int32, x.shape, 1)
    s = 1
    while s < x.shape[1]:
        x = jnp.maximum(x, jnp.where(lane >= s, pltpu.roll(x, s, 1), -jnp.inf))
        s *= 2
    return x


N_MIXER_PARAMS = 25


def _mixer_kernel(layer, nb, alpha, seq_len, fused_in, *refs):
    if fused_in:
        slots_ref, x1p_ref, ys_hbm, modp_ref, lng2_ref, lnb2_ref = refs[:6]
        refs = refs[6:]
    else:
        x_ref = refs[0]
        refs = refs[1:]
    (mod_ref, wa_ref, wg_ref, wh_ref, wx_ref, cos_ref, sin_ref, conv_ref, gb_ref, w2_ref, b2_ref, lb_ref,
     retg_ref, mlg_ref, glag_ref, hgg_ref,
     dret_ref, dq_ref, dk_ref, gc_ref, lv_ref, hh_ref, bdf_ref, onesbd_ref, span_ref) = refs[:N_MIXER_PARAMS]
    refs = refs[N_MIXER_PARAMS:]
    if fused_in:
        y_ref, xout_ref, s_ret, s_cn, s_gla, s_hg, m_ml, conv_buf, gbuf, gsem = refs
    else:
        y_ref, s_ret, s_cn, s_gla, s_hg, m_ml, conv_buf = refs

    c = CHUNK
    batches = range(nb)
    step = pl.program_id(1)
    n_steps = pl.num_programs(1)

    def gather(chunk, slot):
        for b in batches:
            for i in range(c):
                s = slots_ref[b * seq_len + chunk * c + i]
                pltpu.make_async_copy(ys_hbm.at[pl.ds(s, 1)], gbuf.at[slot, pl.ds(b * c + i, 1)],
                                      gsem.at[slot]).start(priority=i % 2)

    def gather_wait(slot):
        pltpu.make_async_copy(ys_hbm.at[pl.ds(0, nb * c)], gbuf.at[slot], gsem.at[slot]).wait()

    @pl.when(step == 0)
    def _():
        for s in (s_ret, s_cn, s_gla, s_hg, m_ml):
            s[...] = jnp.zeros_like(s)
        conv_buf[:, 0:CONV_PAD, :] = jnp.zeros((nb, CONV_PAD, 2 * GW), F32)
        if fused_in:
            gather(0, 0)

    lane_head = lax.broadcasted_iota(jnp.int32, (1, GW), 1) // HEAD_DIM
    head_rows = [(lane_head == h).astype(BF16) for h in range(N_HEADS)]
    hh = hh_ref[...]
    bdf = bdf_ref[...]
    lv = lv_ref[...]
    causal = lv >= 1

    def stack(xs):
        return jnp.concatenate(xs, axis=0)

    def tiled(x):
        return stack([x] * nb)

    def seq(x, b):
        return x[b * c:(b + 1) * c]

    if fused_in:
        cur = step % 2
        gather_wait(cur)
        ffn = gbuf[cur]
        xs = _layer_norm(stack([alpha * x1p_ref[b] + modp_ref[b, 5:6, :] * seq(ffn, b) for b in batches]),
                         lng2_ref[...], lnb2_ref[...])
        x_rows = [seq(xs, b) for b in batches]
        for b in batches:
            xout_ref[b] = x_rows[b]
    else:
        x_rows = [x_ref[b] for b in batches]
    hb = _bf(stack([x_rows[b] * (1.0 + mod_ref[b, 1:2, :]) + mod_ref[b, 0:1, :] for b in batches]))

    issued = {}

    def issue(first):
        block = N_MIXERS * GW
        ref, start = {R_Q: (wa_ref, 0), M_Q: (wa_ref, block), G_Q: (wg_ref, 0), H_Q: (wh_ref, 0)}[first]
        issued[first] = _dot(hb, ref[:, start:start + block])

    def col(off, width=GW):
        first = off // (N_MIXERS * GW) * (N_MIXERS * GW)
        return issued[first][:, off - first:off - first + width]

    issue(R_Q)
    issue(M_Q)
    ext = _dot(hb, wx_ref[...])

    def put(off, val):
        for b in batches:
            y_ref[b, :, off:off + GW] = seq(val, b).astype(y_ref.dtype)

    def stackmask(xb):
        return jnp.concatenate([xb * head_rows[h] for h in range(N_HEADS)], axis=0)

    def expand_heads(cols, first):
        out = jnp.zeros((cols.shape[0], GW), F32)
        for h in range(N_HEADS):
            out = jnp.where(lane_head == h, cols[:, first + h:first + h + 1], out)
        return out

    def head_mean(v):
        return _dot(_bf(v), hh) * (1.0 / HEAD_DIM)

    def head_norm(v, gain, center):
        if center:
            v = v - head_mean(v)
        return v * lax.rsqrt(head_mean(v * v) + NORM_EPS) * gain

    cosf = tiled(cos_ref[...])
    sins = tiled(sin_ref[...])
    half = lax.broadcasted_iota(jnp.int32, (1, GW), 1) % HEAD_DIM < HEAD_DIM // 2

    def rotary(v):
        swapped = jnp.where(half, pltpu.roll(v, GW - HEAD_DIM // 2, 1), pltpu.roll(v, HEAD_DIM // 2, 1))
        return v * cosf + swapped * sins

    rq = _bf(rotary(col(R_Q)) * QK_SCALE)
    rk = rotary(col(R_K))
    rkb = _bf(rk)
    rvb = _bf(col(R_V))
    rkd = rk * tiled(dk_ref[...])
    dret = dret_ref[...]
    dq = dq_ref[...]
    gc = gc_ref[...]
    sc = [_dot_nt(seq(rq, b), stackmask(seq(rkb, b))) for b in batches]
    inter = [_dot(seq(rq, b), _bf(s_ret[b])) for b in batches]
    scb = [_bf(s * dret) for s in sc]
    ro = stack([_dot(scb[b], stackmask(seq(rvb, b))) + inter[b] * dq for b in batches])
    upd = [_dot(_bf(seq(rkd, b).T), seq(rvb, b)) for b in batches]
    for b in batches:
        s_ret[b] = s_ret[b] * gc + upd[b] * bdf
    put(0, head_norm(ro, retg_ref[...], True) * _silu(col(R_G)))

    if fused_in:
        gather(jnp.minimum(step + 1, n_steps - 1), (step + 1) % 2)

    issue(G_Q)
    mqk = col(M_Q, 2 * GW)
    for b in batches:
        conv_buf[b, CONV_PAD:CONV_PAD + c, :] = seq(mqk, b)
    qk = jnp.zeros((nb * c, 2 * GW), F32)
    for t in range(CONV_WIDTH):
        s = CONV_WIDTH - 1 - t
        qk = qk + stack([conv_buf[b, CONV_PAD - s:CONV_PAD - s + c, :] for b in batches]) * conv_ref[t:t + 1, :]
    for b in batches:
        conv_buf[b, 0:CONV_PAD, :] = conv_buf[b, c:c + CONV_PAD, :]
    qk = _silu(qk)
    mq = _bf(qk[:, 0:GW])
    mk = qk[:, GW:2 * GW] * QK_SCALE
    mkb = _bf(mk)
    mvb = _bf(col(M_V))

    pre = stack([seq(ext, b).T[0:8, :] for b in batches]) + tiled(gb_ref[...])
    row8 = lax.broadcasted_iota(jnp.int32, (8 * nb, c), 0)
    valid = row8 % 8 >= N_HEADS
    lf = jnp.where(valid, _log_sigmoid(pre), 0.0)
    gi = jnp.where(valid, pltpu.roll(pre, N_HEADS, 0), 0.0)
    bcum = _lane_cumsum(lf)
    a = gi - bcum
    m_prev = m_ml[...]
    mrow = jnp.maximum(m_prev, _lane_cummax(a))
    m_last = jnp.broadcast_to(mrow[:, c - 1:c], (8 * nb, c))
    b_last = jnp.broadcast_to(bcum[:, c - 1:c], (8 * nb, c))
    wi = jnp.exp(m_prev - mrow)
    em = jnp.exp(-(bcum + mrow))
    wk = jnp.exp(a - m_last)
    dec = jnp.exp(m_prev - m_last)
    m_ml[...] = jnp.where(valid, b_last + m_last, 0.0)
    kinds = [mrow, wi, em, wk]
    if 32 * nb < c:
        kinds.append(jnp.zeros((c - 32 * nb, c), F32))
    cols = stack(kinds).T

    def first_col(kind, b):
        return kind * 8 * nb + 8 * b + N_HEADS

    mexp = [jnp.concatenate([jnp.broadcast_to(cols[:, first_col(0, b) + h:first_col(0, b) + h + 1], (c, c))
                             for h in range(N_HEADS)], axis=1) for b in batches]
    a_row = [jnp.concatenate([a[8 * b + N_HEADS + h:8 * b + N_HEADS + h + 1, :] for h in range(N_HEADS)], axis=1)
             for b in batches]
    qkm = [_dot_nt(seq(mq, b), stackmask(seq(mkb, b))) for b in batches]
    inter = [_dot(seq(mq, b), _bf(s_cn[b])) for b in batches]
    sm = [_bf(qkm[b] * jnp.exp(jnp.where(causal, a_row[b] - mexp[b], -jnp.inf))) for b in batches]
    onesbd = onesbd_ref[...]
    wi_l = stack([expand_heads(cols, first_col(1, b)) for b in batches])
    em_l = stack([expand_heads(cols, first_col(2, b)) for b in batches])
    wk_l = stack([expand_heads(cols, first_col(3, b)) for b in batches])
    nd = (stack([_dot(sm[b], jnp.concatenate([stackmask(seq(mvb, b)), onesbd], axis=1)) for b in batches])
          + jnp.concatenate([wi_l, wi_l], axis=1) * stack(inter))
    mh = nd[:, 0:GW] / jnp.maximum(jnp.abs(nd[:, GW:2 * GW]), em_l)
    kt = mk * wk_l
    ones = jnp.ones((c, GW), BF16)
    upd = [_dot(_bf(seq(kt, b).T), jnp.concatenate([seq(mvb, b), ones], axis=1)) for b in batches]
    dec2 = jnp.concatenate([dec, dec], axis=1)
    bdf2 = jnp.concatenate([bdf, bdf], axis=1)
    for b in batches:
        dec_l = jnp.zeros((1, GW), F32)
        for h in range(N_HEADS):
            dec_l = jnp.where(lane_head == h, dec2[8 * b + N_HEADS + h:8 * b + N_HEADS + h + 1, :], dec_l)
        s_cn[b] = s_cn[b] * jnp.concatenate([dec_l, dec_l], axis=1) + upd[b] * bdf2
    put(GW, head_norm(mh * _sigmoid(col(M_O)), mlg_ref[...], True))

    span = span_ref[...]
    level_masks = [lv == code for code in range(2, 2 + LEVELS)]

    def decay_attention(q, k, v, g, st_ref):
        g2 = _split2(g)
        dec = [jnp.exp(_dot(span, stack([seq(part, b) for part in g2]))) for b in batches]
        e_cum = stack([d[0:c] for d in dec])
        e_end = stack([d[c:2 * c] for d in dec])
        qh = _bf(q)
        kh = _bf(k)
        qe = qh * _bf(e_cum)
        out = stack([_dot_nt(seq(qe, b), _bf(st_ref[b])) for b in batches])
        out = out + _dot(_bf(q * k), hh) * v
        scores = [jnp.zeros((c, N_HEADS * c), F32) for _ in batches]
        for lvl in range(LEVELS):
            e = _bf(stack([d[(2 + lvl) * c:(3 + lvl) * c] for d in dec]))
            qb = qh * e
            kb = kh * e
            level = [_dot_nt(seq(qb, b), stackmask(seq(kb, b))) for b in batches]
            scores = [jnp.where(level_masks[lvl], level[b], scores[b]) for b in batches]
        vb = _bf(v)
        out = out + stack([_dot(_bf(scores[b]), stackmask(seq(vb, b))) for b in batches])
        ke = kh * _bf(e_end)
        upd = [_dot(_bf(seq(v, b).T), seq(ke, b)) for b in batches]
        for b in batches:
            st_ref[b] = st_ref[b] * seq(e_cum, b)[c - 1:c, :] + upd[b] * bdf
        return out

    issue(H_Q)
    x_lr = _dot_f32(ext, w2_ref[...]) + b2_ref[...]
    log_a = _log_sigmoid(x_lr) * (1.0 / GLA_TAU)
    go = decay_attention(col(G_Q), col(G_K) * QK_SCALE, col(G_V), log_a, s_gla)
    put(2 * GW, head_norm(go, glag_ref[...], False) * _silu(col(G_G)))

    lb_all = lb_ref[...]
    lb_e = jnp.exp(lb_all - jnp.max(lb_all, axis=0, keepdims=True))
    lb_p = lb_e / jnp.sum(lb_e, axis=0, keepdims=True)
    lb = jnp.zeros((1, GW), F32)
    for l in range(1, layer + 1):
        lb = lb + lb_p[l:l + 1, :]
    z = col(H_F)
    x1 = jnp.log(lb)
    x2 = jnp.log1p(-lb) + _log_sigmoid(z)
    log_f = jnp.maximum(x1, x2) + jnp.log1p(jnp.exp(-jnp.abs(x1 - x2)))
    k_h = (1.0 - lb) * _sigmoid(-z)
    ho = decay_attention(_silu(col(H_Q)), k_h, col(H_I), log_f, s_hg)
    put(3 * GW, head_norm(ho * _sigmoid(col(H_G)), hgg_ref[...], False))

    if fused_in:
        @pl.when(step == n_steps - 1)
        def _():
            gather_wait((step + 1) % 2)


MIXER_BATCH = 4


def _mixer(x_in, modl, w_in_parts, cosf, sins, conv_w, gate_rows, w2_pad, b2, lb, ret_g, ml_g, gla_g, hg_g, consts,
           layer, bsz, seq, alpha):
    c = CHUNK
    nc = seq // c
    nb = MIXER_BATCH
    assert bsz == nb
    d = D_MODEL
    t = bsz * seq
    fused_in = isinstance(x_in, tuple)
    row = lambda a: a.reshape(1, -1)
    full = lambda a: pl.BlockSpec(a.shape, lambda b, n, *_: (0,) * a.ndim)
    tile = lambda width: pl.BlockSpec((nb, c, width), lambda b, n, *_: (b, n, 0))
    mod_spec = pl.BlockSpec((nb, 6, d), lambda b, n, *_: (b, 0, 0))
    cs = consts
    params = [modl, *w_in_parts, cosf, sins, conv_w, gate_rows, w2_pad, row(b2), lb,
              row(ret_g), row(ml_g), row(gla_g), row(hg_g),
              cs["dret"], cs["dq"], cs["dk"], cs["gc"], cs["lv"], cs["hh"], cs["bdf"], cs["ones_bd"], cs["span"]]
    assert len(params) == N_MIXER_PARAMS
    n_w = len(w_in_parts)
    param_specs = [mod_spec] + [full(a) for a in w_in_parts] + [
        pl.BlockSpec((c, GW), lambda b, n, *_: (n, 0)),
        pl.BlockSpec((c, GW), lambda b, n, *_: (n, 0)),
    ] + [full(a) for a in params[3 + n_w:]]
    scratch = [
        pltpu.VMEM((nb, GW, GW), F32),
        pltpu.VMEM((nb, GW, 2 * GW), F32),
        pltpu.VMEM((nb, GW, GW), F32),
        pltpu.VMEM((nb, GW, GW), F32),
        pltpu.VMEM((nb * 8, c), F32),
        pltpu.VMEM((nb, c + CONV_PAD, 2 * GW), F32),
    ]
    y_shape = jax.ShapeDtypeStruct((bsz, seq, d), BF16)
    if fused_in:
        slots, ys, x1, modp, ln2_g, ln2_b = x_in
        args = [slots, x1.reshape(bsz, seq, d), ys, modp, row(ln2_g), row(ln2_b)] + params
        in_specs = [tile(d), pl.BlockSpec(memory_space=pl.ANY), mod_spec, full(row(ln2_g)), full(row(ln2_b))]
        out_specs = [tile(d), tile(d)]
        out_shape = [y_shape, jax.ShapeDtypeStruct((bsz, seq, d), F32)]
        scratch = scratch + [pltpu.VMEM((2, nb * c, d), F32), pltpu.SemaphoreType.DMA((2,))]
        n_prefetch = 1
    else:
        args = [x_in.reshape(bsz, seq, d)] + params
        in_specs = [tile(d)]
        out_specs = tile(d)
        out_shape = y_shape
        n_prefetch = 0
    out = pl.pallas_call(
        functools.partial(_mixer_kernel, layer, nb, alpha, seq, fused_in),
        grid_spec=pltpu.PrefetchScalarGridSpec(
            num_scalar_prefetch=n_prefetch,
            grid=(bsz // nb, nc),
            in_specs=in_specs + param_specs,
            out_specs=out_specs,
            scratch_shapes=scratch,
        ),
        out_shape=out_shape,
        compiler_params=pltpu.CompilerParams(
            dimension_semantics=("arbitrary", "arbitrary"), vmem_limit_bytes=VMEM_LIMIT),
        name="mixer",
    )(*args)
    if fused_in:
        return out[0].reshape(t, d), out[1].reshape(t, d)
    return out.reshape(t, d), x_in


def _group_shift(v, pos, k):
    return jnp.where(pos < EXPERTS_PER_GROUP - k,
                     pltpu.roll(v, LANES - k, 1), pltpu.roll(v, EXPERTS_PER_GROUP - k, 1))


def _router(h, rw, rb):
    lane = lax.broadcasted_iota(jnp.int32, (1, LANES), 1)
    real = lane < N_EXPERTS
    logits = jnp.where(real, _dot_f32(h, rw), -jnp.inf)
    ex = jnp.exp(logits - jnp.max(logits, axis=-1, keepdims=True))
    probs = ex / jnp.sum(ex, axis=-1, keepdims=True)
    sel = jnp.where(real, probs + rb, -jnp.inf)
    pos = lane % EXPERTS_PER_GROUP
    gid = lane // EXPERTS_PER_GROUP
    r1 = _group_shift(sel, pos, 1)
    r2 = _group_shift(sel, pos, 2)
    r3 = _group_shift(sel, pos, 3)
    pair = jnp.maximum(jnp.maximum(jnp.maximum(sel + r1, sel + r2), jnp.maximum(sel + r3, r1 + r2)),
                       jnp.maximum(r1 + r3, r2 + r3))
    pair = jnp.where(real, pair, -jnp.inf)
    best = jnp.max(pair, axis=-1, keepdims=True)
    first = jnp.min(jnp.where(pair == best, gid, N_GROUPS), axis=-1, keepdims=True)
    rank = jnp.zeros(sel.shape, jnp.int32)
    for k, r in ((1, r1), (2, r2), (3, r3)):
        beats = (r > sel) | ((r == sel) & (pos + k >= EXPERTS_PER_GROUP))
        rank = rank + beats.astype(jnp.int32)
    return probs, first, rank


def _post_kernel(alpha, y_ref, x_ref, mod_ref, wout_ref, lng_ref, lnb_ref, rw_ref, rb_ref,
                 stri_ref, x1_ref, h2_ref, slots_ref, counts_ref, carry_ref):
    tm = y_ref.shape[0]

    @pl.when(pl.program_id(0) == 0)
    def _():
        carry_ref[...] = jnp.zeros_like(carry_ref)

    g1 = mod_ref[0, 2:3, :]
    sh2 = mod_ref[0, 3:4, :]
    sc2 = mod_ref[0, 4:5, :]
    mix = _dot(_bf(y_ref[...]), wout_ref[...])
    x1 = _layer_norm(alpha * x_ref[...] + g1 * mix, lng_ref[...], lnb_ref[...])
    x1_ref[...] = x1
    h = x1 * (1.0 + sc2) + sh2
    d = h.shape[1]
    h2_ref[:, 0:d] = h

    probs, first, rank = _router(h, rw_ref[...], rb_ref[...])
    lane = lax.broadcasted_iota(jnp.int32, (1, LANES), 1)
    chosen = (lane // EXPERTS_PER_GROUP == first) & (rank < 2)
    w = jnp.where(chosen, probs, 0.0)
    h2_ref[:, d:d + LANES] = w / jnp.sum(w, axis=-1, keepdims=True)

    bit = jnp.left_shift(1, lane % EXPERTS_PER_GROUP).astype(F32)
    bits = jnp.sum(jnp.where(chosen, bit, 0.0), axis=-1, keepdims=True)
    pair = jnp.full(bits.shape, len(PAIR_BITS) - 1, jnp.int32)
    for index in range(len(PAIR_BITS) - 2, -1, -1):
        pair = jnp.where(bits == float(PAIR_BITS[index]), index, pair)
    cls = first * len(PAIR_BITS) + pair
    onehot = jnp.where(lane == cls, 1.0, 0.0)
    before = _dot(stri_ref[...], _bf(onehot)) + carry_ref[...]
    code = cls.astype(F32) * float(1 << POS_BITS) + jnp.sum(onehot * before, axis=-1, keepdims=True)
    slots_ref[0] = jnp.broadcast_to(code, (tm, LANES)).T[0:1, :].astype(jnp.int32)
    carry_ref[...] += jnp.sum(onehot, axis=0, keepdims=True)
    counts_ref[...] = carry_ref[...].astype(jnp.int32)


def _post(y, xt, modl, w_out, ln_g, ln_b, rw_pad, rb_pad, seq, alpha):
    t, d = xt.shape
    tm = 256
    per_batch = seq // tm
    row = lambda a: a.reshape(1, -1)
    tile = pl.BlockSpec((tm, d), lambda i: (i, 0))
    full = lambda a: pl.BlockSpec(a.shape, lambda i: (0,) * a.ndim)
    stri = jnp.asarray(np.tril(np.ones((tm, tm), np.float32), -1), BF16)
    args = [y, xt, modl, w_out, row(ln_g), row(ln_b), rw_pad, rb_pad, stri]
    return pl.pallas_call(
        functools.partial(_post_kernel, alpha),
        grid=(t // tm,),
        in_specs=[tile, tile, pl.BlockSpec((1, 6, d), lambda i: (i // per_batch, 0, 0))]
        + [full(a) for a in args[3:]],
        out_specs=[tile, pl.BlockSpec((tm, d + LANES), lambda i: (i, 0)),
                   pl.BlockSpec((1, 1, tm), lambda i: (i, 0, 0)),
                   pl.BlockSpec((1, LANES), lambda i: (0, 0))],
        out_shape=[jax.ShapeDtypeStruct((t, d), F32), jax.ShapeDtypeStruct((t, d + LANES), F32),
                   jax.ShapeDtypeStruct((t // tm, 1, tm), jnp.int32),
                   jax.ShapeDtypeStruct((1, LANES), jnp.int32)],
        scratch_shapes=[pltpu.VMEM((1, LANES), F32)],
        compiler_params=pltpu.CompilerParams(
            dimension_semantics=("arbitrary",), vmem_limit_bytes=VMEM_LIMIT),
        name="post",
    )(*args)


MOE_TILE = 512
POS_BITS = 15
PAIR_BITS = (0b0011, 0b0101, 0b1001, 0b1010, 0b0110, 0b1100)
N_CLASSES = N_GROUPS * len(PAIR_BITS)


def _moe_kernel(n_tiles, item_tile_ref, item_expert_ref, item_first_ref, n_items_ref, token_ref,
                h_hbm, w1_ref, w3_ref, w2_ref, ys_ref, rows_buf, sem, hb_ref):
    s = pl.program_id(0)
    d = ys_ref.shape[1]
    tile = item_tile_ref[s]
    cur = tile % 2
    lane = lax.broadcasted_iota(jnp.int32, (1, LANES), 1)

    def gather(t, slot):
        for row in range(MOE_TILE):
            token = token_ref[t * MOE_TILE + row]
            pltpu.make_async_copy(h_hbm.at[pl.ds(token, 1)], rows_buf.at[slot, pl.ds(row, 1)],
                                  sem.at[slot]).start(priority=row % 2)

    def gather_wait(slot):
        pltpu.make_async_copy(h_hbm.at[pl.ds(0, MOE_TILE)], rows_buf.at[slot], sem.at[slot]).wait()

    def expert(hb, between=lambda: None):
        gates = rows_buf[cur, :, d:d + LANES]
        ge = jnp.sum(jnp.where(lane == item_expert_ref[s], gates, 0.0), axis=-1, keepdims=True)
        up = _dot(hb, _bf(w1_ref[0, 0]))
        between()
        w3 = _bf(w3_ref[0, 0])
        lin = _dot(hb, w3)
        w2 = _bf(w2_ref[0, 0])
        return _dot(_bf(_silu(up) * lin * ge), w2)

    @pl.when(s == 0)
    def _():
        gather(0, 0)

    valid = s < n_items_ref[0]
    first_of_tile = item_first_ref[s] == 1

    @pl.when(valid & first_of_tile)
    def _():
        gather_wait(cur)
        hb = _bf(rows_buf[cur, :, 0:d])
        hb_ref[...] = hb
        ys_ref[...] = expert(hb, lambda: gather(jnp.minimum(tile + 1, n_tiles - 1), 1 - cur))

    @pl.when(valid & jnp.logical_not(first_of_tile))
    def _():
        ys_ref[...] += expert(hb_ref[...])

    @pl.when(s == n_items_ref[0] - 1)
    def _():
        gather_wait(1 - cur)


def _moe(items, token_of_slot, h2, w1, w3, w2, layer):
    item_tile, item_expert, item_first, n_items = items
    rows, width = h2.shape
    d = w1.shape[2]
    up = pl.BlockSpec((1, 1, d, D_EXPERT), lambda s, it, ie, fi, n, tk: (layer, ie[s], 0, 0))
    down = pl.BlockSpec((1, 1, D_EXPERT, d), lambda s, it, ie, fi, n, tk: (layer, ie[s], 0, 0))
    return pl.pallas_call(
        functools.partial(_moe_kernel, rows // MOE_TILE),
        grid_spec=pltpu.PrefetchScalarGridSpec(
            num_scalar_prefetch=5,
            grid=(item_tile.shape[0],),
            in_specs=[pl.BlockSpec(memory_space=pl.ANY), up, up, down],
            out_specs=pl.BlockSpec((MOE_TILE, d), lambda s, it, ie, fi, n, tk: (it[s], 0)),
            scratch_shapes=[pltpu.VMEM((2, MOE_TILE, width), F32), pltpu.SemaphoreType.DMA((2,)),
                            pltpu.VMEM((MOE_TILE, d), BF16)],
        ),
        out_shape=jax.ShapeDtypeStruct((rows, d), F32),
        compiler_params=pltpu.CompilerParams(
            dimension_semantics=("arbitrary",), vmem_limit_bytes=VMEM_LIMIT),
        name="moe",
    )(item_tile, item_expert, item_first, n_items, token_of_slot, h2, w1, w3, w2)


def _invert_kernel(slots_ref, token_ref):
    def body(i, carry):
        for r in range(8):
            t = i * 8 + r
            token_ref[slots_ref[t]] = t
        return carry

    lax.fori_loop(0, slots_ref.shape[0] // 8, body, 0)


def _invert(slots):
    return pl.pallas_call(
        _invert_kernel,
        in_specs=[pl.BlockSpec(memory_space=pltpu.SMEM)],
        out_specs=pl.BlockSpec(memory_space=pltpu.SMEM),
        out_shape=jax.ShapeDtypeStruct(slots.shape, jnp.int32),
        name="invert",
    )(slots)


def _class_experts():
    member = np.zeros((N_CLASSES, N_EXPERTS), np.int32)
    for g in range(N_GROUPS):
        for p, bits in enumerate(PAIR_BITS):
            for e in range(EXPERTS_PER_GROUP):
                if bits >> e & 1:
                    member[g * len(PAIR_BITS) + p, g * EXPERTS_PER_GROUP + e] = 1
    return member


def _work_items(counts, n_tokens):
    n_tiles = n_tokens // MOE_TILE
    n_items_max = 2 * (n_tiles + 2 * N_CLASSES)
    ends = jnp.cumsum(counts)
    starts = ends - counts
    lo = jnp.arange(n_tiles, dtype=jnp.int32)[:, None] * MOE_TILE
    overlap = (counts[None, :] > 0) & (starts[None, :] < lo + MOE_TILE) & (ends[None, :] > lo)
    needed = (overlap.astype(jnp.int32) @ jnp.asarray(_class_experts())) > 0
    flip = (jnp.arange(n_tiles, dtype=jnp.int32) % 2 == 1)[:, None]
    needed = jnp.where(flip, needed[:, ::-1], needed)
    flat = needed.reshape(-1)
    n_items = jnp.sum(flat.astype(jnp.int32))
    place = jnp.where(flat, jnp.cumsum(flat.astype(jnp.int32)) - 1, n_items_max)
    ids = jnp.zeros((n_items_max,), jnp.int32).at[place].set(
        jnp.arange(flat.shape[0], dtype=jnp.int32), mode="drop")
    ids = ids[jnp.minimum(jnp.arange(n_items_max), n_items - 1)]
    item_tile = ids // N_EXPERTS
    item_expert = jnp.where(item_tile % 2 == 1, N_EXPERTS - 1 - ids % N_EXPERTS, ids % N_EXPERTS)
    item_first = jnp.concatenate([jnp.ones((1,), jnp.int32),
                                  (item_tile[1:] != item_tile[:-1]).astype(jnp.int32)])
    return starts.astype(jnp.int32), (item_tile, item_expert, item_first, n_items.reshape(1))


def _combine_kernel(alpha, slots_ref, ys_hbm, x1_ref, mod_ref, lng_ref, lnb_ref, o_ref, buf, sem):
    tm = o_ref.shape[0]
    i = pl.program_id(0)
    n = pl.num_programs(0)

    def gather(tile, b):
        for row in range(tm):
            s = slots_ref[tile * tm + row]
            pltpu.make_async_copy(ys_hbm.at[pl.ds(s, 1)], buf.at[b, pl.ds(row, 1)], sem.at[b]).start(
                priority=row % 2)

    def wait(b):
        pltpu.make_async_copy(ys_hbm.at[pl.ds(0, tm)], buf.at[b], sem.at[b]).wait()

    @pl.when(i == 0)
    def _():
        gather(0, 0)

    cur = i % 2
    nxt = (i + 1) % 2
    wait(cur)
    gather(jnp.minimum(i + 1, n - 1), nxt)
    g2 = mod_ref[0, 5:6, :]
    o_ref[...] = _layer_norm(alpha * x1_ref[...] + g2 * buf[cur], lng_ref[...], lnb_ref[...])

    @pl.when(i == n - 1)
    def _():
        wait(nxt)


def _combine(slots, ys, x1, modl, ln_g, ln_b, seq, alpha):
    t, d = x1.shape
    tm = 256
    per_batch = seq // tm
    row = lambda a: a.reshape(1, -1)
    tile = pl.BlockSpec((tm, d), lambda i, s: (i, 0))
    vec = pl.BlockSpec((1, d), lambda i, s: (0, 0))
    return pl.pallas_call(
        functools.partial(_combine_kernel, alpha),
        grid_spec=pltpu.PrefetchScalarGridSpec(
            num_scalar_prefetch=1,
            grid=(t // tm,),
            in_specs=[pl.BlockSpec(memory_space=pl.ANY), tile,
                      pl.BlockSpec((1, 6, d), lambda i, s: (i // per_batch, 0, 0)), vec, vec],
            out_specs=tile,
            scratch_shapes=[pltpu.VMEM((2, tm, d), F32), pltpu.SemaphoreType.DMA((2,))],
        ),
        out_shape=jax.ShapeDtypeStruct((t, d), F32),
        compiler_params=pltpu.CompilerParams(
            dimension_semantics=("arbitrary",), vmem_limit_bytes=VMEM_LIMIT),
        name="combine",
    )(slots, ys, x1, modl, row(ln_g), row(ln_b))


def _split_w_in(w):
    a = 8 * GW
    b = a + 2 * N_HEADS
    cc = b + 4 * GW
    dd = cc + GLA_RANK
    pad = jnp.zeros((w.shape[0], LANES - 2 * N_HEADS - GLA_RANK), w.dtype)
    small = jnp.concatenate([w[:, a:b], w[:, cc:dd], pad], axis=1)
    return _bf(w[:, :a]), _bf(w[:, b:cc]), _bf(w[:, dd:]), _bf(small)


def _rotary_tables(seq):
    inv = ROPE_BASE ** (-np.arange(0, HEAD_DIM, 2, dtype=np.float64) / HEAD_DIM)
    ang = np.arange(seq, dtype=np.float64)[:, None] * inv[None, :]
    cos = np.cos(ang).astype(np.float32)
    sin = np.sin(ang).astype(np.float32)
    cosf = np.tile(np.concatenate([cos, cos], axis=1), (1, N_HEADS))
    sins = np.tile(np.concatenate([-sin, sin], axis=1), (1, N_HEADS))
    return jnp.asarray(cosf), jnp.asarray(sins)


def kernel(x, c, ada_w, ada_b, w_in, mlstm_conv, mlstm_gate_b, gla_w2, gla_b2, hgrn_lb, ret_norm, mlstm_norm, gla_norm, hgrn_norm, w_out, ln1_g, ln1_b, router_w, router_b, exp_w1, exp_w3, exp_w2, ln2_g, ln2_b):
    bsz, seq, d = x.shape
    depth = ada_w.shape[0]
    assert d == D_MODEL and seq % 512 == 0 and bsz <= 16
    t = bsz * seq
    alpha = (2.0 * depth) ** 0.25
    assert t % MOE_TILE == 0 and t < (1 << POS_BITS) and bsz % MIXER_BATCH == 0
    consts = _mixer_consts()
    cosf, sins = _rotary_tables(seq)
    c_rows = jnp.pad(c, ((0, 16 - bsz), (0, 0)))
    mod = _adaln(c_rows, ada_w, ada_b)[:, :bsz]
    rw_pad = jnp.pad(router_w, ((0, 0), (0, LANES - N_EXPERTS)))
    rb_pad = jnp.pad(router_b, (0, LANES - N_EXPERTS)).reshape(1, LANES)
    x_in = x.reshape(t, d)
    for l in range(depth):
        modl = mod[l].reshape(bsz, 6, d)
        gate_rows = jnp.broadcast_to(mlstm_gate_b[l][:, None], (2 * N_HEADS, CHUNK))
        w2_pad = jnp.zeros((LANES, GW), F32).at[2 * N_HEADS:2 * N_HEADS + GLA_RANK].set(gla_w2[l])
        y, xt = _mixer(x_in, modl, _split_w_in(w_in[l]), cosf, sins, mlstm_conv[l], gate_rows, w2_pad,
                       gla_b2[l], hgrn_lb, ret_norm[l], mlstm_norm[l], gla_norm[l], hgrn_norm[l], consts,
                       l, bsz, seq, alpha)
        x1, h2, codes, counts = _post(y, xt, modl, _bf(w_out[l]), ln1_g[l], ln1_b[l], rw_pad, rb_pad,
                                      seq, alpha)
        class_start, items = _work_items(counts[0, :N_CLASSES], t)
        codes = codes.reshape(t)
        in_class = (codes >> POS_BITS)[:, None] == jnp.arange(N_CLASSES, dtype=jnp.int32)[None, :]
        slots = jnp.sum(jnp.where(in_class, class_start[None, :], 0), axis=1) + (codes & ((1 << POS_BITS) - 1))
        ys = _moe(items, _invert(slots), h2, exp_w1, exp_w3, exp_w2, l)
        x_in = (slots, ys, x1, modl, ln2_g[l], ln2_b[l])
    return _combine(*x_in, seq, alpha).reshape(bsz, seq, d)
```

```python
import functools

import numpy as np
import jax
import jax.numpy as jnp
from jax import lax
from jax.experimental import pallas as pl
from jax.experimental.pallas import tpu as pltpu

F32 = jnp.float32
BF16 = jnp.bfloat16

D_MODEL = 1024
N_MIXERS = 4
GW = D_MODEL // N_MIXERS
N_HEADS = 4
HEAD_DIM = GW // N_HEADS
CONV_WIDTH = 4
GLA_RANK = 16
GLA_TAU = 16.0
ROPE_BASE = 10000.0
N_GROUPS = 4
EXPERTS_PER_GROUP = 4
N_EXPERTS = 16
D_EXPERT = D_MODEL // 2
LN_EPS = 1e-5
NORM_EPS = 1e-6
QK_SCALE = HEAD_DIM ** -0.5

LANES = 128
CHUNK = 128
LEVELS = 7
CONV_PAD = 8
VMEM_LIMIT = 56 * 1024 * 1024

(R_Q, R_K, R_V, R_G, M_Q, M_K, M_V, M_O, G_Q, G_K, G_V, G_G, H_Q, H_F, H_I, H_G) = (
    GW * i for i in range(16))


def _bf(x):
    return x.astype(BF16)


def _dot(a, b):
    return jnp.dot(a, b, preferred_element_type=F32)


def _dot_nt(a, b):
    return lax.dot_general(a, b, (((1,), (1,)), ((), ())), preferred_element_type=F32)


def _split2(x):
    hi = _bf(x)
    lo = _bf(x - hi.astype(F32))
    return hi, lo


def _dot_f32(a, b):
    ah, al = _split2(a)
    bh, bl = _split2(b)
    return _dot(ah, bh) + _dot(ah, bl) + _dot(al, bh)


def _sigmoid(x):
    return 0.5 * jnp.tanh(0.5 * x) + 0.5


def _silu(x):
    h = 0.5 * x
    return h + h * jnp.tanh(h)


def _log_sigmoid(x):
    return jnp.minimum(x, 0.0) - jnp.log1p(jnp.exp(-jnp.abs(x)))


def _layer_norm(v, g, b):
    mu = jnp.mean(v, axis=-1, keepdims=True)
    d = v - mu
    var = jnp.mean(d * d, axis=-1, keepdims=True)
    return d * lax.rsqrt(var + LN_EPS) * g + b


def _adaln_kernel(c_ref, w_ref, b_ref, o_ref):
    cond = _silu(c_ref[...])
    o_ref[0] = _dot_f32(cond, w_ref[0]) + b_ref[0]


def _adaln(c, ada_w, ada_b):
    depth, d, n = ada_w.shape
    bsz = c.shape[0]
    tn = 1536
    return pl.pallas_call(
        _adaln_kernel,
        grid=(depth, n // tn),
        in_specs=[
            pl.BlockSpec((bsz, d), lambda l, j: (0, 0)),
            pl.BlockSpec((1, d, tn), lambda l, j: (l, 0, j)),
            pl.BlockSpec((1, 1, tn), lambda l, j: (l, 0, j)),
        ],
        out_specs=pl.BlockSpec((1, bsz, tn), lambda l, j: (l, 0, j)),
        out_shape=jax.ShapeDtypeStruct((depth, bsz, n), F32),
        compiler_params=pltpu.CompilerParams(
            dimension_semantics=("arbitrary", "arbitrary"), vmem_limit_bytes=VMEM_LIMIT),
        name="adaln",
    )(c, ada_w, ada_b.reshape(depth, 1, n))


def _mixer_consts():
    c = CHUNK
    i = np.arange(c)[:, None]
    j = np.arange(c)[None, :]
    rel = i - j
    log_gamma = np.log1p(-np.exp2(-5.0 - np.arange(N_HEADS, dtype=np.float64)))
    dret = np.concatenate(
        [np.where(rel >= 0, np.exp(log_gamma[h] * np.maximum(rel, 0)), 0.0) for h in range(N_HEADS)], axis=1)
    lane_head = np.arange(GW) // HEAD_DIM
    dq = np.exp(log_gamma[lane_head][None, :] * (np.arange(c)[:, None] + 1.0))
    dk = np.exp(log_gamma[lane_head][None, :] * (c - 1.0 - np.arange(c)[:, None]))
    gc = np.exp(log_gamma[lane_head] * c)[None, :]
    lv = np.zeros((c, c), np.int32)
    lv[rel == 0] = 1
    spans = [(rel >= 0), (rel < 0)]
    code, b = 2, 1
    while b < c:
        lv[(i // (2 * b) == j // (2 * b)) & ((i // b) % 2 == 1) & ((j // b) % 2 == 0)] = code
        ref = (i // (2 * b)) * 2 * b + b - 1
        right = (i // b) % 2 == 1
        spans.append(np.where(right, (j > ref) & (j <= i), (j > i) & (j <= ref)))
        code, b = code + 1, b * 2
    lv = np.tile(lv, (1, N_HEADS))
    span = np.concatenate(spans, axis=0).astype(np.float32)
    span = np.tile(span, (1, 2))
    hh = (lane_head[:, None] == lane_head[None, :]).astype(np.float32)
    ones_bd = np.repeat(np.eye(N_HEADS, dtype=np.float32), c, axis=0)
    ones_bd = np.repeat(ones_bd, HEAD_DIM, axis=1)
    return dict(
        dret=jnp.asarray(dret, F32), dq=jnp.asarray(dq, F32), dk=jnp.asarray(dk, F32),
        gc=jnp.asarray(gc, F32), lv=jnp.asarray(lv), hh=jnp.asarray(hh, BF16),
        bdf=jnp.asarray(hh, F32), ones_bd=jnp.asarray(ones_bd, BF16), span=jnp.asarray(span, BF16))


def _lane_cumsum(x):
    lane = lax.broadcasted_iota(jnp.int32, x.shape, 1)
    s = 1
    while s < x.shape[1]:
        x = x + jnp.where(lane >= s, pltpu.roll(x, s, 1), 0.0)
        s *= 2
    return x


def _lane_cummax(x):
    lane = lax.broadcasted_iota(jnp.int32, x.shape, 1)
    s = 1
    while s < x.shape[1]:
        x = jnp.maximum(x, jnp.where(lane >= s, pltpu.roll(x, s, 1), -jnp.inf))
        s *= 2
    return x


N_MIXER_PARAMS = 25


def _mixer_kernel(layer, nb, alpha, seq_len, fused_in, *refs):
    if fused_in:
        slots_ref, x1p_ref, ys_hbm, modp_ref, lng2_ref, lnb2_ref = refs[:6]
        refs = refs[6:]
    else:
        x_ref = refs[0]
        refs = refs[1:]
    (mod_ref, wa_ref, wg_ref, wh_ref, wx_ref, cos_ref, sin_ref, conv_ref, gb_ref, w2_ref, b2_ref, lb_ref,
     retg_ref, mlg_ref, glag_ref, hgg_ref,
     dret_ref, dq_ref, dk_ref, gc_ref, lv_ref, hh_ref, bdf_ref, onesbd_ref, span_ref) = refs[:N_MIXER_PARAMS]
    refs = refs[N_MIXER_PARAMS:]
    if fused_in:
        y_ref, xout_ref, s_ret, s_cn, s_gla, s_hg, m_ml, conv_buf, gbuf, gsem = refs
    else:
        y_ref, s_ret, s_cn, s_gla, s_hg, m_ml, conv_buf = refs

    c = CHUNK
    batches = range(nb)
    step = pl.program_id(1)
    n_steps = pl.num_programs(1)

    def gather(chunk, slot):
        for b in batches:
            for i in range(c):
                s = slots_ref[b * seq_len + chunk * c + i]
                pltpu.make_async_copy(ys_hbm.at[pl.ds(s, 1)], gbuf.at[slot, pl.ds(b * c + i, 1)],
                                      gsem.at[slot]).start(priority=i % 2)

    def gather_wait(slot):
        pltpu.make_async_copy(ys_hbm.at[pl.ds(0, nb * c)], gbuf.at[slot], gsem.at[slot]).wait()

    @pl.when(step == 0)
    def _():
        for s in (s_ret, s_cn, s_gla, s_hg, m_ml):
            s[...] = jnp.zeros_like(s)
        conv_buf[:, 0:CONV_PAD, :] = jnp.zeros((nb, CONV_PAD, 2 * GW), F32)
        if fused_in:
            gather(0, 0)

    lane_head = lax.broadcasted_iota(jnp.int32, (1, GW), 1) // HEAD_DIM
    head_rows = [(lane_head == h).astype(BF16) for h in range(N_HEADS)]
    hh = hh_ref[...]
    bdf = bdf_ref[...]
    lv = lv_ref[...]
    causal = lv >= 1

    def stack(xs):
        return jnp.concatenate(xs, axis=0)

    def tiled(x):
        return stack([x] * nb)

    def seq(x, b):
        return x[b * c:(b + 1) * c]

    if fused_in:
        cur = step % 2
        gather_wait(cur)
        ffn = gbuf[cur]
        xs = _layer_norm(stack([alpha * x1p_ref[b] + modp_ref[b, 5:6, :] * seq(ffn, b) for b in batches]),
                         lng2_ref[...], lnb2_ref[...])
        x_rows = [seq(xs, b) for b in batches]
        for b in batches:
            xout_ref[b] = x_rows[b]
    else:
        x_rows = [x_ref[b] for b in batches]
    hb = _bf(stack([x_rows[b] * (1.0 + mod_ref[b, 1:2, :]) + mod_ref[b, 0:1, :] for b in batches]))

    issued = {}

    def issue(first):
        block = N_MIXERS * GW
        ref, start = {R_Q: (wa_ref, 0), M_Q: (wa_ref, block), G_Q: (wg_ref, 0), H_Q: (wh_ref, 0)}[first]
        issued[first] = _dot(hb, ref[:, start:start + block])

    def col(off, width=GW):
        first = off // (N_MIXERS * GW) * (N_MIXERS * GW)
        return issued[first][:, off - first:off - first + width]

    issue(R_Q)
    issue(M_Q)
    ext = _dot(hb, wx_ref[...])

    def put(off, val):
        for b in batches:
            y_ref[b, :, off:off + GW] = seq(val, b).astype(y_ref.dtype)

    def stackmask(xb):
        return jnp.concatenate([xb * head_rows[h] for h in range(N_HEADS)], axis=0)

    def expand_heads(cols, first):
        out = jnp.zeros((cols.shape[0], GW), F32)
        for h in range(N_HEADS):
            out = jnp.where(lane_head == h, cols[:, first + h:first + h + 1], out)
        return out

    def head_mean(v):
        return _dot(_bf(v), hh) * (1.0 / HEAD_DIM)

    def head_norm(v, gain, center):
        if center:
            v = v - head_mean(v)
        return v * lax.rsqrt(head_mean(v * v) + NORM_EPS) * gain

    cosf = tiled(cos_ref[...])
    sins = tiled(sin_ref[...])
    half = lax.broadcasted_iota(jnp.int32, (1, GW), 1) % HEAD_DIM < HEAD_DIM // 2

    def rotary(v):
        swapped = jnp.where(half, pltpu.roll(v, GW - HEAD_DIM // 2, 1), pltpu.roll(v, HEAD_DIM // 2, 1))
        return v * cosf + swapped * sins

    rq = _bf(rotary(col(R_Q)) * QK_SCALE)
    rk = rotary(col(R_K))
    rkb = _bf(rk)
    rvb = _bf(col(R_V))
    rkd = rk * tiled(dk_ref[...])
    dret = dret_ref[...]
    dq = dq_ref[...]
    gc = gc_ref[...]
    sc = [_dot_nt(seq(rq, b), stackmask(seq(rkb, b))) for b in batches]
    inter = [_dot(seq(rq, b), _bf(s_ret[b])) for b in batches]
    scb = [_bf(s * dret) for s in sc]
    ro = stack([_dot(scb[b], stackmask(seq(rvb, b))) + inter[b] * dq for b in batches])
    upd = [_dot(_bf(seq(rkd, b).T), seq(rvb, b)) for b in batches]
    for b in batches:
        s_ret[b] = s_ret[b] * gc + upd[b] * bdf
    put(0, head_norm(ro, retg_ref[...], True) * _silu(col(R_G)))

    if fused_in:
        gather(jnp.minimum(step + 1, n_steps - 1), (step + 1) % 2)

    issue(G_Q)
    mqk = col(M_Q, 2 * GW)
    for b in batches:
        conv_buf[b, CONV_PAD:CONV_PAD + c, :] = seq(mqk, b)
    qk = jnp.zeros((nb * c, 2 * GW), F32)
    for t in range(CONV_WIDTH):
        s = CONV_WIDTH - 1 - t
        qk = qk + stack([conv_buf[b, CONV_PAD - s:CONV_PAD - s + c, :] for b in batches]) * conv_ref[t:t + 1, :]
    for b in batches:
        conv_buf[b, 0:CONV_PAD, :] = conv_buf[b, c:c + CONV_PAD, :]
    qk = _silu(qk)
    mq = _bf(qk[:, 0:GW])
    mk = qk[:, GW:2 * GW] * QK_SCALE
    mkb = _bf(mk)
    mvb = _bf(col(M_V))

    pre = stack([seq(ext, b).T[0:8, :] for b in batches]) + tiled(gb_ref[...])
    row8 = lax.broadcasted_iota(jnp.int32, (8 * nb, c), 0)
    valid = row8 % 8 >= N_HEADS
    lf = jnp.where(valid, _log_sigmoid(pre), 0.0)
    gi = jnp.where(valid, pltpu.roll(pre, N_HEADS, 0), 0.0)
    bcum = _lane_cumsum(lf)
    a = gi - bcum
    m_prev = m_ml[...]
    mrow = jnp.maximum(m_prev, _lane_cummax(a))
    m_last = jnp.broadcast_to(mrow[:, c - 1:c], (8 * nb, c))
    b_last = jnp.broadcast_to(bcum[:, c - 1:c], (8 * nb, c))
    wi = jnp.exp(m_prev - mrow)
    em = jnp.exp(-(bcum + mrow))
    wk = jnp.exp(a - m_last)
    dec = jnp.exp(m_prev - m_last)
    m_ml[...] = jnp.where(valid, b_last + m_last, 0.0)
    kinds = [mrow, wi, em, wk]
    if 32 * nb < c:
        kinds.append(jnp.zeros((c - 32 * nb, c), F32))
    cols = stack(kinds).T

    def first_col(kind, b):
        return kind * 8 * nb + 8 * b + N_HEADS

    mexp = [jnp.concatenate([jnp.broadcast_to(cols[:, first_col(0, b) + h:first_col(0, b) + h + 1], (c, c))
                             for h in range(N_HEADS)], axis=1) for b in batches]
    a_row = [jnp.concatenate([a[8 * b + N_HEADS + h:8 * b + N_HEADS + h + 1, :] for h in range(N_HEADS)], axis=1)
             for b in batches]
    qkm = [_dot_nt(seq(mq, b), stackmask(seq(mkb, b))) for b in batches]
    inter = [_dot(seq(mq, b), _bf(s_cn[b])) for b in batches]
    sm = [_bf(qkm[b] * jnp.exp(jnp.where(causal, a_row[b] - mexp[b], -jnp.inf))) for b in batches]
    onesbd = onesbd_ref[...]
    wi_l = stack([expand_heads(cols, first_col(1, b)) for b in batches])
    em_l = stack([expand_heads(cols, first_col(2, b)) for b in batches])
    wk_l = stack([expand_heads(cols, first_col(3, b)) for b in batches])
    nd = (stack([_dot(sm[b], jnp.concatenate([stackmask(seq(mvb, b)), onesbd], axis=1)) for b in batches])
          + jnp.concatenate([wi_l, wi_l], axis=1) * stack(inter))
    mh = nd[:, 0:GW] / jnp.maximum(jnp.abs(nd[:, GW:2 * GW]), em_l)
    kt = mk * wk_l
    ones = jnp.ones((c, GW), BF16)
    upd = [_dot(_bf(seq(kt, b).T), jnp.concatenate([seq(mvb, b), ones], axis=1)) for b in batches]
    dec2 = jnp.concatenate([dec, dec], axis=1)
    bdf2 = jnp.concatenate([bdf, bdf], axis=1)
    for b in batches:
        dec_l = jnp.zeros((1, GW), F32)
        for h in range(N_HEADS):
            dec_l = jnp.where(lane_head == h, dec2[8 * b + N_HEADS + h:8 * b + N_HEADS + h + 1, :], dec_l)
        s_cn[b] = s_cn[b] * jnp.concatenate([dec_l, dec_l], axis=1) + upd[b] * bdf2
    put(GW, head_norm(mh * _sigmoid(col(M_O)), mlg_ref[...], True))

    span = span_ref[...]
    level_masks = [lv == code for code in range(2, 2 + LEVELS)]

    def decay_attention(q, k, v, g, st_ref):
        g2 = _split2(g)
        dec = [jnp.exp(_dot(span, stack([seq(part, b) for part in g2]))) for b in batches]
        e_cum = stack([d[0:c] for d in dec])
        e_end = stack([d[c:2 * c] for d in dec])
        qh = _bf(q)
        kh = _bf(k)
        qe = qh * _bf(e_cum)
        out = stack([_dot_nt(seq(qe, b), _bf(st_ref[b])) for b in batches])
        out = out + _dot(_bf(q * k), hh) * v
        scores = [jnp.zeros((c, N_HEADS * c), F32) for _ in batches]
        for lvl in range(LEVELS):
            e = _bf(stack([d[(2 + lvl) * c:(3 + lvl) * c] for d in dec]))
            qb = qh * e
            kb = kh * e
            level = [_dot_nt(seq(qb, b), stackmask(seq(kb, b))) for b in batches]
            scores = [jnp.where(level_masks[lvl], level[b], scores[b]) for b in batches]
        vb = _bf(v)
        out = out + stack([_dot(_bf(scores[b]), stackmask(seq(vb, b))) for b in batches])
        ke = kh * _bf(e_end)
        upd = [_dot(_bf(seq(v, b).T), seq(ke, b)) for b in batches]
        for b in batches:
            st_ref[b] = st_ref[b] * seq(e_cum, b)[c - 1:c, :] + upd[b] * bdf
        return out

    issue(H_Q)
    x_lr = _dot_f32(ext, w2_ref[...]) + b2_ref[...]
    log_a = _log_sigmoid(x_lr) * (1.0 / GLA_TAU)
    go = decay_attention(col(G_Q), col(G_K) * QK_SCALE, col(G_V), log_a, s_gla)
    put(2 * GW, head_norm(go, glag_ref[...], False) * _silu(col(G_G)))

    lb_all = lb_ref[...]
    lb_e = jnp.exp(lb_all - jnp.max(lb_all, axis=0, keepdims=True))
    lb_p = lb_e / jnp.sum(lb_e, axis=0, keepdims=True)
    lb = jnp.zeros((1, GW), F32)
    for l in range(1, layer + 1):
        lb = lb + lb_p[l:l + 1, :]
    z = col(H_F)
    x1 = jnp.log(lb)
    x2 = jnp.log1p(-lb) + _log_sigmoid(z)
    log_f = jnp.maximum(x1, x2) + jnp.log1p(jnp.exp(-jnp.abs(x1 - x2)))
    k_h = (1.0 - lb) * _sigmoid(-z)
    ho = decay_attention(_silu(col(H_Q)), k_h, col(H_I), log_f, s_hg)
    put(3 * GW, head_norm(ho * _sigmoid(col(H_G)), hgg_ref[...], False))

    if fused_in:
        @pl.when(step == n_steps - 1)
        def _():
            gather_wait((step + 1) % 2)


MIXER_BATCH = 4


def _mixer(x_in, modl, w_in_parts, cosf, sins, conv_w, gate_rows, w2_pad, b2, lb, ret_g, ml_g, gla_g, hg_g, consts,
           layer, bsz, seq, alpha):
    c = CHUNK
    nc = seq // c
    nb = MIXER_BATCH
    assert bsz == nb
    d = D_MODEL
    t = bsz * seq
    fused_in = isinstance(x_in, tuple)
    row = lambda a: a.reshape(1, -1)
    full = lambda a: pl.BlockSpec(a.shape, lambda b, n, *_: (0,) * a.ndim)
    tile = lambda width: pl.BlockSpec((nb, c, width), lambda b, n, *_: (b, n, 0))
    mod_spec = pl.BlockSpec((nb, 6, d), lambda b, n, *_: (b, 0, 0))
    cs = consts
    params = [modl, *w_in_parts, cosf, sins, conv_w, gate_rows, w2_pad, row(b2), lb,
              row(ret_g), row(ml_g), row(gla_g), row(hg_g),
              cs["dret"], cs["dq"], cs["dk"], cs["gc"], cs["lv"], cs["hh"], cs["bdf"], cs["ones_bd"], cs["span"]]
    assert len(params) == N_MIXER_PARAMS
    n_w = len(w_in_parts)
    param_specs = [mod_spec] + [full(a) for a in w_in_parts] + [
        pl.BlockSpec((c, GW), lambda b, n, *_: (n, 0)),
        pl.BlockSpec((c, GW), lambda b, n, *_: (n, 0)),
    ] + [full(a) for a in params[3 + n_w:]]
    scratch = [
        pltpu.VMEM((nb, GW, GW), F32),
        pltpu.VMEM((nb, GW, 2 * GW), F32),
        pltpu.VMEM((nb, GW, GW), F32),
        pltpu.VMEM((nb, GW, GW), F32),
        pltpu.VMEM((nb * 8, c), F32),
        pltpu.VMEM((nb, c + CONV_PAD, 2 * GW), F32),
    ]
    y_shape = jax.ShapeDtypeStruct((bsz, seq, d), BF16)
    if fused_in:
        slots, ys, x1, modp, ln2_g, ln2_b = x_in
        args = [slots, x1.reshape(bsz, seq, d), ys, modp, row(ln2_g), row(ln2_b)] + params
        in_specs = [tile(d), pl.BlockSpec(memory_space=pl.ANY), mod_spec, full(row(ln2_g)), full(row(ln2_b))]
        out_specs = [tile(d), tile(d)]
        out_shape = [y_shape, jax.ShapeDtypeStruct((bsz, seq, d), F32)]
        scratch = scratch + [pltpu.VMEM((2, nb * c, d), F32), pltpu.SemaphoreType.DMA((2,))]
        n_prefetch = 1
    else:
        args = [x_in.reshape(bsz, seq, d)] + params
        in_specs = [tile(d)]
        out_specs = tile(d)
        out_shape = y_shape
        n_prefetch = 0
    out = pl.pallas_call(
        functools.partial(_mixer_kernel, layer, nb, alpha, seq, fused_in),
        grid_spec=pltpu.PrefetchScalarGridSpec(
            num_scalar_prefetch=n_prefetch,
            grid=(bsz // nb, nc),
            in_specs=in_specs + param_specs,
            out_specs=out_specs,
            scratch_shapes=scratch,
        ),
        out_shape=out_shape,
        compiler_params=pltpu.CompilerParams(
            dimension_semantics=("arbitrary", "arbitrary"), vmem_limit_bytes=VMEM_LIMIT),
        name="mixer",
    )(*args)
    if fused_in:
        return out[0].reshape(t, d), out[1].reshape(t, d)
    return out.reshape(t, d), x_in


def _group_shift(v, pos, k):
    return jnp.where(pos < EXPERTS_PER_GROUP - k,
                     pltpu.roll(v, LANES - k, 1), pltpu.roll(v, EXPERTS_PER_GROUP - k, 1))


def _router(h, rw, rb):
    lane = lax.broadcasted_iota(jnp.int32, (1, LANES), 1)
    real = lane < N_EXPERTS
    logits = jnp.where(real, _dot_f32(h, rw), -jnp.inf)
    ex = jnp.exp(logits - jnp.max(logits, axis=-1, keepdims=True))
    probs = ex / jnp.sum(ex, axis=-1, keepdims=True)
    sel = jnp.where(real, probs + rb, -jnp.inf)
    pos = lane % EXPERTS_PER_GROUP
    gid = lane // EXPERTS_PER_GROUP
    r1 = _group_shift(sel, pos, 1)
    r2 = _group_shift(sel, pos, 2)
    r3 = _group_shift(sel, pos, 3)
    pair = jnp.maximum(jnp.maximum(jnp.maximum(sel + r1, sel + r2), jnp.maximum(sel + r3, r1 + r2)),
                       jnp.maximum(r1 + r3, r2 + r3))
    pair = jnp.where(real, pair, -jnp.inf)
    best = jnp.max(pair, axis=-1, keepdims=True)
    first = jnp.min(jnp.where(pair == best, gid, N_GROUPS), axis=-1, keepdims=True)
    rank = jnp.zeros(sel.shape, jnp.int32)
    for k, r in ((1, r1), (2, r2), (3, r3)):
        beats = (r > sel) | ((r == sel) & (pos + k >= EXPERTS_PER_GROUP))
        rank = rank + beats.astype(jnp.int32)
    return probs, first, rank


def _post_kernel(alpha, y_ref, x_ref, mod_ref, wout_ref, lng_ref, lnb_ref, rw_ref, rb_ref,
                 stri_ref, x1_ref, h2_ref, slots_ref, counts_ref, carry_ref):
    tm = y_ref.shape[0]

    @pl.when(pl.program_id(0) == 0)
    def _():
        carry_ref[...] = jnp.zeros_like(carry_ref)

    g1 = mod_ref[0, 2:3, :]
    sh2 = mod_ref[0, 3:4, :]
    sc2 = mod_ref[0, 4:5, :]
    mix = _dot(_bf(y_ref[...]), wout_ref[...])
    x1 = _layer_norm(alpha * x_ref[...] + g1 * mix, lng_ref[...], lnb_ref[...])
    x1_ref[...] = x1
    h = x1 * (1.0 + sc2) + sh2
    d = h.shape[1]
    h2_ref[:, 0:d] = h

    probs, first, rank = _router(h, rw_ref[...], rb_ref[...])
    lane = lax.broadcasted_iota(jnp.int32, (1, LANES), 1)
    chosen = (lane // EXPERTS_PER_GROUP == first) & (rank < 2)
    w = jnp.where(chosen, probs, 0.0)
    h2_ref[:, d:d + LANES] = w / jnp.sum(w, axis=-1, keepdims=True)

    bit = jnp.left_shift(1, lane % EXPERTS_PER_GROUP).astype(F32)
    bits = jnp.sum(jnp.where(chosen, bit, 0.0), axis=-1, keepdims=True)
    pair = jnp.full(bits.shape, len(PAIR_BITS) - 1, jnp.int32)
    for index in range(len(PAIR_BITS) - 2, -1, -1):
        pair = jnp.where(bits == float(PAIR_BITS[index]), index, pair)
    cls = first * len(PAIR_BITS) + pair
    onehot = jnp.where(lane == cls, 1.0, 0.0)
    before = _dot(stri_ref[...], _bf(onehot)) + carry_ref[...]
    code = cls.astype(F32) * float(1 << POS_BITS) + jnp.sum(onehot * before, axis=-1, keepdims=True)
    slots_ref[0] = jnp.broadcast_to(code, (tm, LANES)).T[0:1, :].astype(jnp.int32)
    carry_ref[...] += jnp.sum(onehot, axis=0, keepdims=True)
    counts_ref[...] = carry_ref[...].astype(jnp.int32)


def _post(y, xt, modl, w_out, ln_g, ln_b, rw_pad, rb_pad, seq, alpha):
    t, d = xt.shape
    tm = 256
    per_batch = seq // tm
    row = lambda a: a.reshape(1, -1)
    tile = pl.BlockSpec((tm, d), lambda i: (i, 0))
    full = lambda a: pl.BlockSpec(a.shape, lambda i: (0,) * a.ndim)
    stri = jnp.asarray(np.tril(np.ones((tm, tm), np.float32), -1), BF16)
    args = [y, xt, modl, w_out, row(ln_g), row(ln_b), rw_pad, rb_pad, stri]
    return pl.pallas_call(
        functools.partial(_post_kernel, alpha),
        grid=(t // tm,),
        in_specs=[tile, tile, pl.BlockSpec((1, 6, d), lambda i: (i // per_batch, 0, 0))]
        + [full(a) for a in args[3:]],
        out_specs=[tile, pl.BlockSpec((tm, d + LANES), lambda i: (i, 0)),
                   pl.BlockSpec((1, 1, tm), lambda i: (i, 0, 0)),
                   pl.BlockSpec((1, LANES), lambda i: (0, 0))],
        out_shape=[jax.ShapeDtypeStruct((t, d), F32), jax.ShapeDtypeStruct((t, d + LANES), F32),
                   jax.ShapeDtypeStruct((t // tm, 1, tm), jnp.int32),
                   jax.ShapeDtypeStruct((1, LANES), jnp.int32)],
        scratch_shapes=[pltpu.VMEM((1, LANES), F32)],
        compiler_params=pltpu.CompilerParams(
            dimension_semantics=("arbitrary",), vmem_limit_bytes=VMEM_LIMIT),
        name="post",
    )(*args)


MOE_TILE = 512
POS_BITS = 15
PAIR_BITS = (0b0011, 0b0101, 0b1001, 0b1010, 0b0110, 0b1100)
N_CLASSES = N_GROUPS * len(PAIR_BITS)


def _moe_kernel(n_tiles, item_tile_ref, item_expert_ref, item_first_ref, n_items_ref, token_ref,
                h_hbm, w1_ref, w3_ref, w2_ref, ys_ref, rows_buf, sem, hb_ref):
    s = pl.program_id(0)
    d = ys_ref.shape[1]
    tile = item_tile_ref[s]
    cur = tile % 2
    lane = lax.broadcasted_iota(jnp.int32, (1, LANES), 1)

    def gather(t, slot):
        for row in range(MOE_TILE):
            token = token_ref[t * MOE_TILE + row]
            pltpu.make_async_copy(h_hbm.at[pl.ds(token, 1)], rows_buf.at[slot, pl.ds(row, 1)],
                                  sem.at[slot]).start(priority=row % 2)

    def gather_wait(slot):
        pltpu.make_async_copy(h_hbm.at[pl.ds(0, MOE_TILE)], rows_buf.at[slot], sem.at[slot]).wait()

    def expert(hb, between=lambda: None):
        gates = rows_buf[cur, :, d:d + LANES]
        ge = jnp.sum(jnp.where(lane == item_expert_ref[s], gates, 0.0), axis=-1, keepdims=True)
        up = _dot(hb, _bf(w1_ref[0, 0]))
        between()
        w3 = _bf(w3_ref[0, 0])
        lin = _dot(hb, w3)
        w2 = _bf(w2_ref[0, 0])
        return _dot(_bf(_silu(up) * lin * ge), w2)

    @pl.when(s == 0)
    def _():
        gather(0, 0)

    valid = s < n_items_ref[0]
    first_of_tile = item_first_ref[s] == 1

    @pl.when(valid & first_of_tile)
    def _():
        gather_wait(cur)
        hb = _bf(rows_buf[cur, :, 0:d])
        hb_ref[...] = hb
        ys_ref[...] = expert(hb, lambda: gather(jnp.minimum(tile + 1, n_tiles - 1), 1 - cur))

    @pl.when(valid & jnp.logical_not(first_of_tile))
    def _():
        ys_ref[...] += expert(hb_ref[...])

    @pl.when(s == n_items_ref[0] - 1)
    def _():
        gather_wait(1 - cur)


def _moe(items, token_of_slot, h2, w1, w3, w2, layer):
    item_tile, item_expert, item_first, n_items = items
    rows, width = h2.shape
    d = w1.shape[2]
    up = pl.BlockSpec((1, 1, d, D_EXPERT), lambda s, it, ie, fi, n, tk: (layer, ie[s], 0, 0))
    down = pl.BlockSpec((1, 1, D_EXPERT, d), lambda s, it, ie, fi, n, tk: (layer, ie[s], 0, 0))
    return pl.pallas_call(
        functools.partial(_moe_kernel, rows // MOE_TILE),
        grid_spec=pltpu.PrefetchScalarGridSpec(
            num_scalar_prefetch=5,
            grid=(item_tile.shape[0],),
            in_specs=[pl.BlockSpec(memory_space=pl.ANY), up, up, down],
            out_specs=pl.BlockSpec((MOE_TILE, d), lambda s, it, ie, fi, n, tk: (it[s], 0)),
            scratch_shapes=[pltpu.VMEM((2, MOE_TILE, width), F32), pltpu.SemaphoreType.DMA((2,)),
                            pltpu.VMEM((MOE_TILE, d), BF16)],
        ),
        out_shape=jax.ShapeDtypeStruct((rows, d), F32),
        compiler_params=pltpu.CompilerParams(
            dimension_semantics=("arbitrary",), vmem_limit_bytes=VMEM_LIMIT),
        name="moe",
    )(item_tile, item_expert, item_first, n_items, token_of_slot, h2, w1, w3, w2)


def _invert_kernel(slots_ref, token_ref):
    def body(i, carry):
        for r in range(8):
            t = i * 8 + r
            token_ref[slots_ref[t]] = t
        return carry

    lax.fori_loop(0, slots_ref.shape[0] // 8, body, 0)


def _invert(slots):
    return pl.pallas_call(
        _invert_kernel,
        in_specs=[pl.BlockSpec(memory_space=pltpu.SMEM)],
        out_specs=pl.BlockSpec(memory_space=pltpu.SMEM),
        out_shape=jax.ShapeDtypeStruct(slots.shape, jnp.int32),
        name="invert",
    )(slots)


def _class_experts():
    member = np.zeros((N_CLASSES, N_EXPERTS), np.int32)
    for g in range(N_GROUPS):
        for p, bits in enumerate(PAIR_BITS):
            for e in range(EXPERTS_PER_GROUP):
                if bits >> e & 1:
                    member[g * len(PAIR_BITS) + p, g * EXPERTS_PER_GROUP + e] = 1
    return member


def _work_items(counts, n_tokens):
    n_tiles = n_tokens // MOE_TILE
    n_items_max = 2 * (n_tiles + N_CLASSES - 1)
    ends = jnp.cumsum(counts)
    starts = ends - counts
    lo = jnp.arange(n_tiles, dtype=jnp.int32)[:, None] * MOE_TILE
    overlap = (counts[None, :] > 0) & (starts[None, :] < lo + MOE_TILE) & (ends[None, :] > lo)
    needed = (overlap.astype(jnp.int32) @ jnp.asarray(_class_experts())) > 0
    flip = (jnp.arange(n_tiles, dtype=jnp.int32) % 2 == 1)[:, None]
    needed = jnp.where(flip, needed[:, ::-1], needed)
    flat = needed.reshape(-1)
    n_items = jnp.sum(flat.astype(jnp.int32))
    place = jnp.where(flat, jnp.cumsum(flat.astype(jnp.int32)) - 1, n_items_max)
    ids = jnp.zeros((n_items_max,), jnp.int32).at[place].set(
        jnp.arange(flat.shape[0], dtype=jnp.int32), mode="drop")
    ids = ids[jnp.minimum(jnp.arange(n_items_max), n_items - 1)]
    item_tile = ids // N_EXPERTS
    item_expert = jnp.where(item_tile % 2 == 1, N_EXPERTS - 1 - ids % N_EXPERTS, ids % N_EXPERTS)
    item_first = jnp.concatenate([jnp.ones((1,), jnp.int32),
                                  (item_tile[1:] != item_tile[:-1]).astype(jnp.int32)])
    return starts.astype(jnp.int32), (item_tile, item_expert, item_first, n_items.reshape(1))


def _combine_kernel(alpha, slots_ref, ys_hbm, x1_ref, mod_ref, lng_ref, lnb_ref, o_ref, buf, sem):
    tm = o_ref.shape[0]
    i = pl.program_id(0)
    n = pl.num_programs(0)

    def gather(tile, b):
        for row in range(tm):
            s = slots_ref[tile * tm + row]
            pltpu.make_async_copy(ys_hbm.at[pl.ds(s, 1)], buf.at[b, pl.ds(row, 1)], sem.at[b]).start(
                priority=row % 2)

    def wait(b):
        pltpu.make_async_copy(ys_hbm.at[pl.ds(0, tm)], buf.at[b], sem.at[b]).wait()

    @pl.when(i == 0)
    def _():
        gather(0, 0)

    cur = i % 2
    nxt = (i + 1) % 2
    wait(cur)
    gather(jnp.minimum(i + 1, n - 1), nxt)
    g2 = mod_ref[0, 5:6, :]
    o_ref[...] = _layer_norm(alpha * x1_ref[...] + g2 * buf[cur], lng_ref[...], lnb_ref[...])

    @pl.when(i == n - 1)
    def _():
        wait(nxt)


def _combine(slots, ys, x1, modl, ln_g, ln_b, seq, alpha):
    t, d = x1.shape
    tm = 256
    per_batch = seq // tm
    row = lambda a: a.reshape(1, -1)
    tile = pl.BlockSpec((tm, d), lambda i, s: (i, 0))
    vec = pl.BlockSpec((1, d), lambda i, s: (0, 0))
    return pl.pallas_call(
        functools.partial(_combine_kernel, alpha),
        grid_spec=pltpu.PrefetchScalarGridSpec(
            num_scalar_prefetch=1,
            grid=(t // tm,),
            in_specs=[pl.BlockSpec(memory_space=pl.ANY), tile,
                      pl.BlockSpec((1, 6, d), lambda i, s: (i // per_batch, 0, 0)), vec, vec],
            out_specs=tile,
            scratch_shapes=[pltpu.VMEM((2, tm, d), F32), pltpu.SemaphoreType.DMA((2,))],
        ),
        out_shape=jax.ShapeDtypeStruct((t, d), F32),
        compiler_params=pltpu.CompilerParams(
            dimension_semantics=("arbitrary",), vmem_limit_bytes=VMEM_LIMIT),
        name="combine",
    )(slots, ys, x1, modl, row(ln_g), row(ln_b))


def _split_w_in(w):
    a = 8 * GW
    b = a + 2 * N_HEADS
    cc = b + 4 * GW
    dd = cc + GLA_RANK
    pad = jnp.zeros((w.shape[0], LANES - 2 * N_HEADS - GLA_RANK), w.dtype)
    small = jnp.concatenate([w[:, a:b], w[:, cc:dd], pad], axis=1)
    return _bf(w[:, :a]), _bf(w[:, b:cc]), _bf(w[:, dd:]), _bf(small)


def _rotary_tables(seq):
    inv = ROPE_BASE ** (-np.arange(0, HEAD_DIM, 2, dtype=np.float64) / HEAD_DIM)
    ang = np.arange(seq, dtype=np.float64)[:, None] * inv[None, :]
    cos = np.cos(ang).astype(np.float32)
    sin = np.sin(ang).astype(np.float32)
    cosf = np.tile(np.concatenate([cos, cos], axis=1), (1, N_HEADS))
    sins = np.tile(np.concatenate([-sin, sin], axis=1), (1, N_HEADS))
    return jnp.asarray(cosf), jnp.asarray(sins)


def kernel(x, c, ada_w, ada_b, w_in, mlstm_conv, mlstm_gate_b, gla_w2, gla_b2, hgrn_lb, ret_norm, mlstm_norm, gla_norm, hgrn_norm, w_out, ln1_g, ln1_b, router_w, router_b, exp_w1, exp_w3, exp_w2, ln2_g, ln2_b):
    bsz, seq, d = x.shape
    depth = ada_w.shape[0]
    assert d == D_MODEL and seq % 512 == 0 and bsz <= 16
    t = bsz * seq
    alpha = (2.0 * depth) ** 0.25
    assert t % MOE_TILE == 0 and t < (1 << POS_BITS) and bsz % MIXER_BATCH == 0
    consts = _mixer_consts()
    cosf, sins = _rotary_tables(seq)
    c_rows = jnp.pad(c, ((0, 16 - bsz), (0, 0)))
    mod = _adaln(c_rows, ada_w, ada_b)[:, :bsz]
    rw_pad = jnp.pad(router_w, ((0, 0), (0, LANES - N_EXPERTS)))
    rb_pad = jnp.pad(router_b, (0, LANES - N_EXPERTS)).reshape(1, LANES)
    x_in = x.reshape(t, d)
    for l in range(depth):
        modl = mod[l].reshape(bsz, 6, d)
        gate_rows = jnp.broadcast_to(mlstm_gate_b[l][:, None], (2 * N_HEADS, CHUNK))
        w2_pad = jnp.zeros((LANES, GW), F32).at[2 * N_HEADS:2 * N_HEADS + GLA_RANK].set(gla_w2[l])
        y, xt = _mixer(x_in, modl, _split_w_in(w_in[l]), cosf, sins, mlstm_conv[l], gate_rows, w2_pad,
                       gla_b2[l], hgrn_lb, ret_norm[l], mlstm_norm[l], gla_norm[l], hgrn_norm[l], consts,
                       l, bsz, seq, alpha)
        x1, h2, codes, counts = _post(y, xt, modl, _bf(w_out[l]), ln1_g[l], ln1_b[l], rw_pad, rb_pad,
                                      seq, alpha)
        class_start, items = _work_items(counts[0, :N_CLASSES], t)
        codes = codes.reshape(t)
        in_class = (codes >> POS_BITS)[:, None] == jnp.arange(N_CLASSES, dtype=jnp.int32)[None, :]
        slots = jnp.sum(jnp.where(in_class, class_start[None, :], 0), axis=1) + (codes & ((1 << POS_BITS) - 1))
        ys = _moe(items, _invert(slots), h2, exp_w1, exp_w3, exp_w2, l)
        x_in = (slots, ys, x1, modl, ln2_g[l], ln2_b[l])
    return _combine(*x_in, seq, alpha).reshape(bsz, seq, d)
```
